```python
import math
import jax
import jax.numpy as jnp
from jax import lax
import numpy as np

D_MODEL = 1024
BATCH = 8
SEQ = 2048
DEPTH = 2

HEAD_DIM = 64
ROPE_THETA = 10000.0
Q_BLOCK = 128
NSA_HEADS = 8
NSA_KV_HEADS = 2
NSA_GROUP = NSA_HEADS // NSA_KV_HEADS
CMP_BLOCK = 32
CMP_STRIDE = 16
SEL_BLOCK = 64
SEL_TOPK = 8
WINDOW = 256
FOX_HEADS = 8
GDN_HEADS = 8
GDN_HEAD_DIM = 128
GDN_WIDTH = GDN_HEADS * GDN_HEAD_DIM
CONV_WIDTH = 4
GDN_CHUNK = 64
N_GROUPS = 4
EXPERTS_PER_GROUP = 4
N_EXPERTS = N_GROUPS * EXPERTS_PER_GROUP
EXPERT_TOPK = 2
EXPERT_FF = 256

NORM_EPS = 1e-6
NEG_INF = -1e30
FORCE_SCORE = 1e9

NSA_Q_W = NSA_HEADS * HEAD_DIM
NSA_KV_W = NSA_KV_HEADS * HEAD_DIM
NSA_GATE_W = 3 * NSA_HEADS
FOX_W = FOX_HEADS * HEAD_DIM
EVEN_IN_W = NSA_Q_W + 6 * NSA_KV_W + NSA_GATE_W + 3 * FOX_W + FOX_HEADS
EVEN_MIX_W = NSA_Q_W + FOX_W
ODD_IN_W = 4 * GDN_WIDTH + 2 * GDN_HEADS

kernel_name = "hybrid_nsa_fox_gdn_hmoe"


def rms_norm(x, gain):
    xf = x.astype(jnp.float32)
    y = xf * lax.rsqrt(jnp.mean(xf * xf, axis=-1, keepdims=True) + NORM_EPS)
    return (y * gain.astype(jnp.float32)).astype(x.dtype)


def l2_norm(x):
    xf = x.astype(jnp.float32)
    return xf * lax.rsqrt(jnp.sum(xf * xf, axis=-1, keepdims=True) + NORM_EPS)


def rope(x, pos):
    half = x.shape[-1] // 2
    inv_freq = ROPE_THETA ** (-jnp.arange(half, dtype=jnp.float32) / half)
    ang = pos.astype(jnp.float32)[:, None] * inv_freq
    cos = jnp.cos(ang)[:, None, :]
    sin = jnp.sin(ang)[:, None, :]
    xf = x.astype(jnp.float32)
    x1, x2 = xf[..., :half], xf[..., half:]
    return jnp.concatenate([x1 * cos - x2 * sin, x2 * cos + x1 * sin], axis=-1).astype(x.dtype)


def compress_blocks(blk, pe, w1, w2):
    h = blk + pe[None, None, :, None, :]
    h = jnp.einsum('bnlhd,lde->bnhe', h, w1)
    return jnp.einsum('bnhe,ef->bnhf', jax.nn.gelu(h), w2)


def nsa_attention(q, kc_raw, vc_raw, ks, vs, kw, vw, gates, cmp_pe, cmp_w1, cmp_w2, k_gains):
    B, T = q.shape[0], q.shape[1]
    f32 = jnp.float32
    scale = HEAD_DIM ** -0.5
    pos = jnp.arange(T)
    n_q = T // Q_BLOCK
    q_g = q.reshape(B, T, NSA_KV_HEADS, NSA_GROUP, HEAD_DIM)

    n_cmp = (T - CMP_BLOCK) // CMP_STRIDE + 1
    cmp_idx = np.arange(n_cmp)[:, None] * CMP_STRIDE + np.arange(CMP_BLOCK)[None, :]
    cmp_end = cmp_idx[:, -1]
    kc = compress_blocks(kc_raw[:, cmp_idx], cmp_pe[0], cmp_w1[0], cmp_w2[0])
    vc = compress_blocks(vc_raw[:, cmp_idx], cmp_pe[1], cmp_w1[1], cmp_w2[1])
    kc = rope(rms_norm(kc, k_gains[0]), jnp.asarray(cmp_end))
    s_c = jnp.einsum('bthgd,bnhd->bhgtn', q_g, kc).astype(f32) * scale
    valid_c = jnp.asarray(cmp_end[None, :] <= np.arange(T)[:, None])
    p_c = jax.nn.softmax(jnp.where(valid_c, s_c, NEG_INF), axis=-1) * valid_c.astype(f32)
    o_cmp = jnp.einsum('bhgtn,bnhd->bthgd', p_c.astype(vc.dtype), vc)

    n_sel = T // SEL_BLOCK
    k_top = min(SEL_TOPK, n_sel)
    cs = np.arange(n_cmp)[:, None] * CMP_STRIDE
    ss = np.arange(n_sel)[None, :] * SEL_BLOCK
    overlap = np.clip(np.minimum(cs + CMP_BLOCK, ss + SEL_BLOCK) - np.maximum(cs, ss), 0, None)
    overlap = jnp.asarray((overlap / CMP_BLOCK).astype(np.float32))
    imp = jnp.einsum('bhgtn,nj->bhtj', p_c, overlap)
    blk = np.arange(n_sel)[None, :]
    cur = (np.arange(T) // SEL_BLOCK)[:, None]
    forced = jnp.asarray((blk == 0) | (blk == cur) | (blk == cur - 1))
    future = jnp.asarray(blk * SEL_BLOCK > np.arange(T)[:, None])
    imp = jnp.where(forced, FORCE_SCORE, jnp.where(future, NEG_INF, imp))
    _, sel_idx = lax.top_k(imp, k_top)

    ks_blk = rope(rms_norm(ks, k_gains[1]), pos).reshape(B, n_sel, SEL_BLOCK, NSA_KV_HEADS, HEAD_DIM).transpose(0, 3, 1, 2, 4)
    vs_blk = vs.reshape(B, n_sel, SEL_BLOCK, NSA_KV_HEADS, HEAD_DIM).transpose(0, 3, 1, 2, 4)
    q_chunks = q_g.reshape(B, n_q, Q_BLOCK, NSA_KV_HEADS, NSA_GROUP, HEAD_DIM).swapaxes(0, 1)
    idx_chunks = sel_idx.reshape(B, NSA_KV_HEADS, n_q, Q_BLOCK, k_top).transpose(2, 0, 1, 3, 4)
    pos_chunks = pos.reshape(n_q, Q_BLOCK)
    b_ix = jnp.arange(B)[:, None, None, None]
    h_ix = jnp.arange(NSA_KV_HEADS)[None, :, None, None]

    def sel_block(args):
        q_c, idx_c, t_c = args
        k_g = ks_blk[b_ix, h_ix, idx_c]
        v_g = vs_blk[b_ix, h_ix, idx_c]
        s = jnp.einsum('bqhgd,bhqnld->bhgqnl', q_c, k_g).astype(f32) * scale
        key_pos = idx_c[..., None] * SEL_BLOCK + jnp.arange(SEL_BLOCK)
        valid = (key_pos <= t_c[None, None, :, None, None])[:, :, None]
        s = jnp.where(valid, s, NEG_INF).reshape(B, NSA_KV_HEADS, NSA_GROUP, Q_BLOCK, k_top * SEL_BLOCK)
        p = jax.nn.softmax(s, axis=-1).reshape(B, NSA_KV_HEADS, NSA_GROUP, Q_BLOCK, k_top, SEL_BLOCK)
        return jnp.einsum('bhgqnl,bhqnld->bqhgd', p.astype(v_g.dtype), v_g)

    o_slc = lax.map(sel_block, (q_chunks, idx_chunks, pos_chunks))
    o_slc = o_slc.swapaxes(0, 1).reshape(B, T, NSA_KV_HEADS, NSA_GROUP, HEAD_DIM)

    kw = rope(rms_norm(kw, k_gains[2]), pos)
    kpad = jnp.pad(kw, ((0, 0), (WINDOW, 0), (0, 0), (0, 0)))
    vpad = jnp.pad(vw, ((0, 0), (WINDOW, 0), (0, 0), (0, 0)))
    band_idx = np.arange(n_q)[:, None] * Q_BLOCK + np.arange(Q_BLOCK + WINDOW)[None, :]
    kb = kpad[:, band_idx]
    vb = vpad[:, band_idx]
    key_pos = (band_idx - WINDOW)[:, None, :]
    qry_pos = (np.arange(n_q)[:, None] * Q_BLOCK + np.arange(Q_BLOCK)[None, :])[:, :, None]
    valid_w = jnp.asarray((key_pos <= qry_pos) & (key_pos > qry_pos - WINDOW) & (key_pos >= 0))
    qb = q_g.reshape(B, n_q, Q_BLOCK, NSA_KV_HEADS, NSA_GROUP, HEAD_DIM)
    s_w = jnp.einsum('bcqhgd,bckhd->bhgcqk', qb, kb).astype(f32) * scale
    p_w = jax.nn.softmax(jnp.where(valid_w, s_w, NEG_INF), axis=-1)
    o_win = jnp.einsum('bhgcqk,bckhd->bcqhgd', p_w.astype(vb.dtype), vb).reshape(B, T, NSA_KV_HEADS, NSA_GROUP, HEAD_DIM)

    g = gates.reshape(B, T, NSA_KV_HEADS, NSA_GROUP, 3).astype(f32)
    o = g[..., 0:1] * o_cmp.astype(f32) + g[..., 1:2] * o_slc.astype(f32) + g[..., 2:3] * o_win.astype(f32)
    return o.reshape(B, T, NSA_Q_W)


def forgetting_attention(q, k, v, log_f):
    B, T = q.shape[0], q.shape[1]
    f32 = jnp.float32
    scale = HEAD_DIM ** -0.5
    cum = jnp.cumsum(log_f, axis=1).transpose(0, 2, 1)
    outs = []
    for i in range(T // Q_BLOCK):
        lo, hi = i * Q_BLOCK, (i + 1) * Q_BLOCK
        s = jnp.einsum('bqhd,bkhd->bhqk', q[:, lo:hi], k[:, :hi]).astype(f32) * scale
        s = s + cum[:, :, lo:hi, None] - cum[:, :, None, :hi]
        causal = jnp.asarray(np.arange(lo, hi)[:, None] >= np.arange(hi)[None, :])
        p = jax.nn.softmax(jnp.where(causal, s, NEG_INF), axis=-1)
        outs.append(jnp.einsum('bhqk,bkhd->bqhd', p.astype(v.dtype), v[:, :hi]))
    return jnp.concatenate(outs, axis=1).reshape(B, T, FOX_W)


def even_mixer(h, w_in, b_gate, b_forget, cmp_pe, cmp_w1, cmp_w2, nsa_gain, fox_gain, w_out):
    B, T, _ = h.shape
    pos = jnp.arange(T)
    proj = h @ w_in
    widths = [NSA_Q_W] + [NSA_KV_W] * 6 + [NSA_GATE_W, FOX_W, FOX_W, FOX_W]
    offs = [int(o) for o in np.cumsum(widths)]
    q_n, kc, vc, ks, vs, kw, vw, gate_l, q_f, k_f, v_f, f_l = jnp.split(proj, offs, axis=-1)
    heads = lambda t, n: t.reshape(B, T, n, -1)
    q_a = rope(rms_norm(heads(q_n, NSA_HEADS), nsa_gain[0]), pos)
    gates = jax.nn.sigmoid((gate_l + b_gate).astype(jnp.float32)).reshape(B, T, NSA_HEADS, 3)
    o_a = nsa_attention(q_a, heads(kc, NSA_KV_HEADS), heads(vc, NSA_KV_HEADS), heads(ks, NSA_KV_HEADS),
                        heads(vs, NSA_KV_HEADS), heads(kw, NSA_KV_HEADS), heads(vw, NSA_KV_HEADS),
                        gates, cmp_pe, cmp_w1, cmp_w2, nsa_gain[1:])
    q_b = rms_norm(heads(q_f, FOX_HEADS), fox_gain[0])
    k_b = rms_norm(heads(k_f, FOX_HEADS), fox_gain[1])
    log_f = jax.nn.log_sigmoid((f_l + b_forget).astype(jnp.float32))
    o_b = forgetting_attention(q_b, k_b, heads(v_f, FOX_HEADS), log_f)
    o = jnp.concatenate([o_a.astype(h.dtype), o_b.astype(h.dtype)], axis=-1)
    return (o @ w_out).astype(h.dtype)


def causal_depthwise_conv(x, w):
    xp = jnp.pad(x, ((0, 0), (CONV_WIDTH - 1, 0), (0, 0)))
    return lax.conv_general_dilated(xp, w.astype(x.dtype)[:, None, :], (1,), 'VALID',
                                    dimension_numbers=('NWC', 'WIO', 'NWC'), feature_group_count=x.shape[-1])


def gated_delta_chunked(q, k, v, g, beta):
    B, T, H, dk = q.shape
    dv = v.shape[-1]
    C = GDN_CHUNK
    N = T // C
    f32 = jnp.float32
    to_chunks = lambda t: jnp.moveaxis(t.astype(f32).reshape(B, N, C, H, *t.shape[3:]), 3, 1)
    q = to_chunks(q) * (dk ** -0.5)
    k = to_chunks(k)
    v = to_chunks(v)
    beta = to_chunks(beta)
    g = jnp.cumsum(to_chunks(g), axis=-1)
    k_beta = k * beta[..., None]
    v_beta = v * beta[..., None]
    tril = jnp.tril(jnp.ones((C, C), bool))
    strict = jnp.tril(jnp.ones((C, C), bool), -1)
    diff = g[..., :, None] - g[..., None, :]
    decay = jnp.where(tril, jnp.exp(jnp.where(tril, diff, 0.0)), 0.0)
    L = jnp.where(strict, jnp.einsum('bhncd,bhnsd->bhncs', k_beta, k) * decay, 0.0)
    eye = jnp.eye(C, dtype=f32)
    t_inv = lax.linalg.triangular_solve(eye + L, jnp.broadcast_to(eye, L.shape), left_side=True, lower=True)
    u = t_inv @ v_beta
    w = t_inv @ (k_beta * jnp.exp(g)[..., None])
    intra = jnp.where(tril, jnp.einsum('bhncd,bhnsd->bhncs', q, k) * decay, 0.0)

    def step(S, xs):
        q_i, k_i, u_i, w_i, g_i, a_i = xs
        v_new = u_i - w_i @ S
        o = (q_i * jnp.exp(g_i)[..., None]) @ S + a_i @ v_new
        k_dec = k_i * jnp.exp(g_i[..., -1:] - g_i)[..., None]
        S = S * jnp.exp(g_i[..., -1])[..., None, None] + jnp.swapaxes(k_dec, -1, -2) @ v_new
        return S, o

    xs = tuple(jnp.moveaxis(a, 2, 0) for a in (q, k, u, w, g, intra))
    S0 = jnp.zeros((B, H, dk, dv), f32)
    _, o = lax.scan(step, S0, xs)
    return jnp.moveaxis(o, 0, 2).reshape(B, H, T, dv).transpose(0, 2, 1, 3)


def odd_mixer(h, w_in, conv_w, a_log, dt_bias, norm_gain, w_out):
    B, T, _ = h.shape
    proj = h @ w_in
    qkv = jax.nn.silu(causal_depthwise_conv(proj[..., :3 * GDN_WIDTH], conv_w))
    z = proj[..., 3 * GDN_WIDTH:4 * GDN_WIDTH]
    a = proj[..., 4 * GDN_WIDTH:4 * GDN_WIDTH + GDN_HEADS].astype(jnp.float32)
    b = proj[..., 4 * GDN_WIDTH + GDN_HEADS:].astype(jnp.float32)
    heads = lambda t: t.reshape(B, T, GDN_HEADS, GDN_HEAD_DIM)
    q, k, v = jnp.split(qkv, 3, axis=-1)
    q = l2_norm(heads(q))
    k = l2_norm(heads(k))
    beta = jax.nn.sigmoid(b)
    g = -jnp.exp(a_log.astype(jnp.float32)) * jax.nn.softplus(a + dt_bias.astype(jnp.float32))
    o = gated_delta_chunked(q, k, heads(v), g, beta)
    o = rms_norm(o, norm_gain) * jax.nn.silu(heads(z).astype(jnp.float32))
    return (o.reshape(B, T, GDN_WIDTH).astype(h.dtype) @ w_out).astype(h.dtype)


def hierarchical_moe(h, w_rg, b_rg, w_re, b_re, w_ein, w_eout):
    B, T, D = h.shape
    xt = h.reshape(-1, D)
    n_tok = xt.shape[0]
    f32 = jnp.float32
    g_logits = (xt @ w_rg + b_rg).astype(f32)
    g_sel = jnp.argmax(g_logits, axis=-1)
    p_group = jnp.max(jax.nn.softmax(g_logits, axis=-1), axis=-1, keepdims=True)
    e_logits = (xt @ w_re + b_re).astype(f32).reshape(n_tok, N_GROUPS, EXPERTS_PER_GROUP)
    e_in = e_logits[jnp.arange(n_tok), g_sel]
    top_v, top_i = lax.top_k(e_in, EXPERT_TOPK)
    w_top = jax.nn.softmax(top_v, axis=-1) * p_group
    expert_id = g_sel[:, None] * EXPERTS_PER_GROUP + top_i
    gates = jnp.sum(jax.nn.one_hot(expert_id, N_EXPERTS, dtype=f32) * w_top[..., None], axis=1)
    gates = gates.astype(h.dtype)
    y = jnp.zeros((n_tok, D), h.dtype)
    for grp in range(N_GROUPS):
        sl = slice(grp * EXPERTS_PER_GROUP, (grp + 1) * EXPERTS_PER_GROUP)
        gu = jnp.einsum('nd,edf->nef', xt, w_ein[sl])
        act = jax.nn.silu(gu[..., :EXPERT_FF]) * gu[..., EXPERT_FF:] * gates[:, sl, None]
        y = y + jnp.einsum('nef,efd->nd', act, w_eout[sl]).astype(h.dtype)
    return y.reshape(B, T, D)


def setup_inputs(seed: int = 0) -> dict:
    key = jax.random.key(seed)
    ks = jax.random.split(key, 24)
    f32 = jnp.float32
    n_even = (DEPTH + 1) // 2
    n_odd = DEPTH // 2
    nrm = lambda k, shape, s: s * jax.random.normal(k, shape, f32)
    dt = jnp.exp(jax.random.uniform(ks[16], (n_odd, GDN_HEADS), f32, math.log(1e-3), math.log(1e-1)))
    return {
        "x": jax.random.normal(ks[0], (BATCH, SEQ, D_MODEL), f32),
        "norm_mix": 1.0 + nrm(ks[1], (DEPTH, D_MODEL), 0.02),
        "norm_ffn": 1.0 + nrm(ks[2], (DEPTH, D_MODEL), 0.02),
        "w_in_even": nrm(ks[3], (n_even, D_MODEL, EVEN_IN_W), D_MODEL ** -0.5),
        "b_nsa_gate": nrm(ks[4], (n_even, NSA_GATE_W), 0.1),
        "b_forget": jax.random.uniform(ks[5], (n_even, FOX_HEADS), f32, 1.0, 5.0),
        "cmp_pe": nrm(ks[6], (n_even, 2, CMP_BLOCK, HEAD_DIM), 0.02),
        "cmp_w1": nrm(ks[7], (n_even, 2, CMP_BLOCK, HEAD_DIM, HEAD_DIM), (CMP_BLOCK * HEAD_DIM) ** -0.5),
        "cmp_w2": nrm(ks[8], (n_even, 2, HEAD_DIM, HEAD_DIM), HEAD_DIM ** -0.5),
        "nsa_qk_gain": 1.0 + nrm(ks[9], (n_even, 4, HEAD_DIM), 0.02),
        "fox_qk_gain": 1.0 + nrm(ks[10], (n_even, 2, HEAD_DIM), 0.02),
        "w_out_even": nrm(ks[11], (n_even, EVEN_MIX_W, D_MODEL), EVEN_MIX_W ** -0.5),
        "w_in_odd": nrm(ks[12], (n_odd, D_MODEL, ODD_IN_W), D_MODEL ** -0.5),
        "conv_w": nrm(ks[13], (n_odd, CONV_WIDTH, 3 * GDN_WIDTH), CONV_WIDTH ** -0.5),
        "a_log": jnp.log(jax.random.uniform(ks[14], (n_odd, GDN_HEADS), f32, 1.0, 16.0)),
        "dt_bias": dt + jnp.log(-jnp.expm1(-dt)),
        "gdn_norm_gain": 1.0 + nrm(ks[15], (n_odd, GDN_HEAD_DIM), 0.02),
        "w_out_odd": nrm(ks[17], (n_odd, GDN_WIDTH, D_MODEL), GDN_WIDTH ** -0.5),
        "w_router_group": nrm(ks[18], (DEPTH, D_MODEL, N_GROUPS), D_MODEL ** -0.5),
        "b_router_group": nrm(ks[19], (DEPTH, N_GROUPS), 0.01),
        "w_router_expert": nrm(ks[20], (DEPTH, D_MODEL, N_EXPERTS), D_MODEL ** -0.5),
        "b_router_expert": nrm(ks[21], (DEPTH, N_EXPERTS), 0.01),
        "w_expert_in": nrm(ks[22], (DEPTH, N_EXPERTS, D_MODEL, 2 * EXPERT_FF), D_MODEL ** -0.5),
        "w_expert_out": nrm(ks[23], (DEPTH, N_EXPERTS, EXPERT_FF, D_MODEL), EXPERT_FF ** -0.5),
    }


def reference(x, norm_mix, norm_ffn, w_in_even, b_nsa_gate, b_forget, cmp_pe, cmp_w1, cmp_w2,
              nsa_qk_gain, fox_qk_gain, w_out_even, w_in_odd, conv_w, a_log, dt_bias, gdn_norm_gain,
              w_out_odd, w_router_group, b_router_group, w_router_expert, b_router_expert,
              w_expert_in, w_expert_out):
    for layer in range(DEPTH):
        i = layer // 2
        h = rms_norm(x, norm_mix[layer])
        if layer % 2 == 0:
            x = x + even_mixer(h, w_in_even[i], b_nsa_gate[i], b_forget[i], cmp_pe[i], cmp_w1[i], cmp_w2[i],
                               nsa_qk_gain[i], fox_qk_gain[i], w_out_even[i])
        else:
            x = x + odd_mixer(h, w_in_odd[i], conv_w[i], a_log[i], dt_bias[i], gdn_norm_gain[i], w_out_odd[i])
        h = rms_norm(x, norm_ffn[layer])
        x = x + hierarchical_moe(h, w_router_group[layer], b_router_group[layer], w_router_expert[layer],
                                 b_router_expert[layer], w_expert_in[layer], w_expert_out[layer])
    return x
```

```python
import functools

import numpy as np
import jax
import jax.numpy as jnp
from jax import lax
from jax.experimental import pallas as pl
from jax.experimental.pallas import tpu as pltpu

F32 = jnp.float32
BF16 = jnp.bfloat16

HEAD_DIM = 64
ROPE_THETA = 10000.0
NSA_HEADS = 8
NSA_KV_HEADS = 2
NSA_GROUP = NSA_HEADS // NSA_KV_HEADS
CMP_BLOCK = 32
CMP_STRIDE = 16
SEL_BLOCK = 64
SEL_TOPK = 8
WINDOW = 256
FOX_HEADS = 8
GDN_HEADS = 8
GDN_HEAD_DIM = 128
GDN_WIDTH = GDN_HEADS * GDN_HEAD_DIM
CONV_WIDTH = 4
GDN_CHUNK = 64
N_GROUPS = 4
EXPERTS_PER_GROUP = 4
N_EXPERTS = N_GROUPS * EXPERTS_PER_GROUP
EXPERT_FF = 256
NORM_EPS = 1e-6
NEG_INF = -1e30
FORCE_SCORE = 1e9

LANES = 128
NSA_Q_W = NSA_HEADS * HEAD_DIM
NSA_KV_W = NSA_KV_HEADS * HEAD_DIM
NSA_GATE_W = 3 * NSA_HEADS
FOX_W = FOX_HEADS * HEAD_DIM
C_QN = 0
C_KC, C_VC, C_KS, C_VS, C_KW, C_VW = (NSA_Q_W + i * NSA_KV_W for i in range(6))
C_QF = NSA_Q_W + 6 * NSA_KV_W
C_KF = C_QF + FOX_W
C_VF = C_KF + FOX_W
C_MISC = C_VF + FOX_W
EVEN_W = C_MISC + LANES
MISC_F = NSA_GATE_W
C_AB = 4 * GDN_WIDTH
ODD_W = C_AB + LANES

VMEM_LIMIT = 56 * 1024 * 1024


def _params(*sem):
    return pltpu.CompilerParams(dimension_semantics=sem, vmem_limit_bytes=VMEM_LIMIT)


def _dot(a, b):
    return jnp.dot(a, b, preferred_element_type=F32)


def _dot_nt(a, b):
    return lax.dot_general(a, b, (((1,), (1,)), ((), ())), preferred_element_type=F32)


def _dot_tn(a, b):
    return lax.dot_general(a, b, (((0,), (0,)), ((), ())), preferred_element_type=F32)


def _split2(x):
    hi = x.astype(BF16)
    return hi, (x - hi.astype(F32)).astype(BF16)


def _split3(x):
    hi = x.astype(BF16)
    r = x - hi.astype(F32)
    mid = r.astype(BF16)
    return hi, mid, (r - mid.astype(F32)).astype(BF16)


def _sigmoid(z):
    return 1.0 / (1.0 + jnp.exp(-z))


def _silu(z):
    return z * _sigmoid(z)


def _full(shape):
    nd = len(shape)
    return pl.BlockSpec(shape, lambda *_: (0,) * nd)


def _norm_matmul_body(x_ref, g_ref, w_ref, o_ref):
    x = x_ref[...]
    ms = jnp.mean(x * x, axis=-1, keepdims=True)
    h = (x * lax.rsqrt(ms + NORM_EPS) * g_ref[...]).astype(BF16)
    o_ref[...] = _dot(h, w_ref[...])


def norm_matmul(x2, gain, w_bf, tm=512):
    n, d = x2.shape
    wp = w_bf.shape[1]
    return pl.pallas_call(
        _norm_matmul_body,
        grid=(n // tm,),
        in_specs=[pl.BlockSpec((tm, d), lambda i: (i, 0)), _full((1, d)), _full((d, wp))],
        out_specs=pl.BlockSpec((tm, wp), lambda i: (i, 0)),
        out_shape=jax.ShapeDtypeStruct((n, wp), F32),
        compiler_params=_params("parallel"),
    )(x2, gain.reshape(1, d), w_bf)


def _head_rms(x, bd, gain):
    hi, lo = _split2(x * x)
    w = x.shape[1]
    ssum = _dot(hi, bd[:w, :w]) + _dot(lo, bd[:w, :w])
    return x * lax.rsqrt(ssum * (1.0 / HEAD_DIM) + NORM_EPS) * gain


def _rope(x, cos, sin_signed, first_half):
    fwd = pltpu.roll(x, LANES - HEAD_DIM // 2, 1)
    bwd = pltpu.roll(x, HEAD_DIM // 2, 1)
    return x * cos + jnp.where(first_half, fwd, bwd) * sin_signed


def _even_prep_body(p_ref, cos_ref, sin_ref, gq_ref, gks_ref, gkw_ref, gfq_ref, gfk_ref, bias_ref, bd_ref,
                    qa_ref, ks_ref, kw_ref, vs_ref, vw_ref, kc_ref, vc_ref, qb_ref, kb_ref, vf_ref,
                    gate_ref, cum_ref, carry_ref):
    tr = p_ref.shape[1]
    bd = bd_ref[...]
    cos = cos_ref[...]
    sin = sin_ref[...]
    lane = lax.broadcasted_iota(jnp.int32, (1, LANES), 1)
    first_half = (lane % HEAD_DIM) < (HEAD_DIM // 2)
    scale = HEAD_DIM ** -0.5

    qn = _head_rms(p_ref[0, :, C_QN:C_QN + NSA_Q_W], bd, gq_ref[...])
    for c in range(NSA_Q_W // LANES):
        sl = slice(c * LANES, (c + 1) * LANES)
        qa_ref[0, :, sl] = (_rope(qn[:, sl], cos, sin, first_half) * scale).astype(BF16)
    ks = _head_rms(p_ref[0, :, C_KS:C_KS + NSA_KV_W], bd, gks_ref[...])
    ks_ref[0] = _rope(ks, cos, sin, first_half).astype(BF16)
    kw = _head_rms(p_ref[0, :, C_KW:C_KW + NSA_KV_W], bd, gkw_ref[...])
    kw_ref[0] = _rope(kw, cos, sin, first_half).astype(BF16)
    vs_ref[0] = p_ref[0, :, C_VS:C_VS + NSA_KV_W].astype(BF16)
    vw_ref[0] = p_ref[0, :, C_VW:C_VW + NSA_KV_W].astype(BF16)
    kc_ref[0] = p_ref[0, :, C_KC:C_KC + NSA_KV_W]
    vc_ref[0] = p_ref[0, :, C_VC:C_VC + NSA_KV_W]

    qb_ref[0] = (_head_rms(p_ref[0, :, C_QF:C_QF + FOX_W], bd, gfq_ref[...]) * scale).astype(BF16)
    kb_ref[0] = _head_rms(p_ref[0, :, C_KF:C_KF + FOX_W], bd, gfk_ref[...]).astype(BF16)
    vf_ref[0] = p_ref[0, :, C_VF:C_VF + FOX_W].astype(BF16)

    z = p_ref[0, :, C_MISC:C_MISC + LANES] + bias_ref[...]
    gate_ref[0] = _sigmoid(z)
    logf = jnp.minimum(z, 0.0) - jnp.log1p(jnp.exp(-jnp.abs(z)))

    @pl.when(pl.program_id(1) == 0)
    def _():
        carry_ref[...] = jnp.zeros_like(carry_ref)

    row = lax.broadcasted_iota(jnp.int32, (tr, tr), 0)
    col = lax.broadcasted_iota(jnp.int32, (tr, tr), 1)
    tril = jnp.where(row >= col, 1.0, 0.0).astype(BF16)
    hi, mid, lo = _split3(logf)
    cum = _dot(tril, hi) + _dot(tril, mid) + _dot(tril, lo) + carry_ref[...]
    cum_ref[0] = cum
    carry_ref[...] = cum[tr - 1:tr, :]


def even_prep(proj, cos, sin, gq, gks, gkw, gfq, gfk, bias, bd, tr=256):
    b, t, _ = proj.shape
    row = lambda w: pl.BlockSpec((1, tr, w), lambda i, j: (i, j, 0))
    tab = pl.BlockSpec((tr, LANES), lambda i, j: (j, 0))
    shp = lambda w, dt: jax.ShapeDtypeStruct((b, t, w), dt)
    return pl.pallas_call(
        _even_prep_body,
        grid=(b, t // tr),
        in_specs=[row(EVEN_W), tab, tab, _full((1, NSA_Q_W)), _full((1, LANES)), _full((1, LANES)),
                  _full((1, FOX_W)), _full((1, FOX_W)), _full((1, LANES)), _full((FOX_W, FOX_W))],
        out_specs=[row(NSA_Q_W), row(LANES), row(LANES), row(LANES), row(LANES), row(LANES), row(LANES),
                   row(FOX_W), row(FOX_W), row(FOX_W), row(LANES), row(LANES)],
        out_shape=[shp(NSA_Q_W, BF16), shp(LANES, BF16), shp(LANES, BF16), shp(LANES, BF16), shp(LANES, BF16),
                   shp(LANES, F32), shp(LANES, F32), shp(FOX_W, BF16), shp(FOX_W, BF16), shp(FOX_W, BF16),
                   shp(LANES, F32), shp(LANES, F32)],
        scratch_shapes=[pltpu.VMEM((1, LANES), F32)],
        compiler_params=_params("parallel", "arbitrary"),
    )(proj, cos, sin, gq, gks, gkw, gfq, gfk, bias, bd)


def _gelu_tanh(x):
    return 0.5 * x * (1.0 + jnp.tanh(np.sqrt(2.0 / np.pi).astype(np.float32) * (x + 0.044715 * (x * x * x))))


def _compress_body(xk_ref, xv_ref, pe_ref, w1_ref, w2_ref, gk_ref, cos_ref, sin_ref, bd_ref, kc_ref, vc_ref):
    n = xk_ref.shape[1]
    lane = lax.broadcasted_iota(jnp.int32, (1, LANES), 1)
    first_half = (lane % HEAD_DIM) < (HEAD_DIM // 2)

    def mlp(x_ref, i):
        x = x_ref[0]
        nxt = pltpu.roll(x, n - 1, 0)
        xa = (x + pe_ref[i, 0]).astype(BF16)
        xb = (nxt + pe_ref[i, 1]).astype(BF16)
        h = _dot(xa, w1_ref[i, 0]) + _dot(xb, w1_ref[i, 1])
        return _dot(_gelu_tanh(h).astype(BF16), w2_ref[i])

    kc = _head_rms(mlp(xk_ref, 0), bd_ref[...], gk_ref[...])
    kc_ref[0] = _rope(kc, cos_ref[...], sin_ref[...], first_half).astype(BF16)
    vc_ref[0] = mlp(xv_ref, 1).astype(BF16)


def compress(xk, xv, pe, w1, w2, gk, cos_c, sin_c, bd):
    b, n, w = xk.shape
    blk = pl.BlockSpec((1, n, w), lambda i: (i, 0, 0))
    out = pl.BlockSpec((1, n, LANES), lambda i: (i, 0, 0))
    return pl.pallas_call(
        _compress_body,
        grid=(b,),
        in_specs=[blk, blk, _full(pe.shape), _full(w1.shape), _full(w2.shape), _full((1, LANES)),
                  _full((n, LANES)), _full((n, LANES)), _full((LANES, LANES))],
        out_specs=[out, out],
        out_shape=[jax.ShapeDtypeStruct((b, n, LANES), BF16)] * 2,
        compiler_params=_params("parallel"),
    )(xk, xv, pe, w1, w2, gk, cos_c, sin_c, bd)


def _nsa_body(q_ref, kc_ref, vc_ref, ks_ref, vs_ref, kw_ref, vw_ref, gate_ref, ovt_ref, exp_ref, o_ref, *, k_top):
    tq = q_ref.shape[1]
    t_all = ks_ref.shape[1]
    n_cmp = kc_ref.shape[1]
    n_sel = ovt_ref.shape[0]
    g_n = NSA_GROUP
    c = pl.program_id(1)
    t0 = c * tq
    lane = lax.broadcasted_iota(jnp.int32, (1, LANES), 1)
    q = q_ref[0]
    gates = gate_ref[0]
    tpos = t0 + lax.broadcasted_iota(jnp.int32, (tq, 1), 0)

    ncol = lax.broadcasted_iota(jnp.int32, (1, n_cmp), 1)
    valid_c = (ncol * CMP_STRIDE + (CMP_BLOCK - 1)) <= tpos
    jrow = lax.broadcasted_iota(jnp.int32, (n_sel, tq), 0)
    jrow_f = jrow.astype(F32)
    tq_row = t0 + lax.broadcasted_iota(jnp.int32, (n_sel, tq), 1)
    cur = tq_row // SEL_BLOCK
    forced = (jrow == 0) | (jrow == cur) | (jrow == cur - 1)
    future = jrow * SEL_BLOCK > tq_row
    kcol = lax.broadcasted_iota(jnp.int32, (1, tq), 1)
    w_len = tq + WINDOW
    w_start = pl.multiple_of(jnp.clip(t0 - WINDOW, 0, t_all - w_len), LANES)
    wpos = w_start + lax.broadcasted_iota(jnp.int32, (1, w_len), 1)
    valid_w = (wpos <= tpos) & (wpos > tpos - WINDOW)

    outs = []
    for kvh in range(NSA_KV_HEADS):
        mine = (lane // HEAD_DIM) == kvh
        qs = jnp.concatenate([jnp.where(mine, q[:, g * LANES:(g + 1) * LANES], 0) for g in range(g_n)], axis=0)

        s_c = _dot_nt(qs, kc_ref[0]).reshape(g_n, tq, n_cmp)
        s_c = jnp.where(valid_c[None], s_c, NEG_INF)
        e_c = jnp.where(valid_c[None], jnp.exp(s_c - jnp.max(s_c, axis=-1, keepdims=True)), 0.0)
        den = jnp.sum(e_c, axis=-1, keepdims=True)
        p_c = e_c / jnp.where(den > 0.0, den, 1.0)
        o_cmp = _dot(p_c.reshape(g_n * tq, n_cmp).astype(BF16), vc_ref[0])

        p_hi, p_lo = _split2(jnp.sum(p_c, axis=0))
        imp_t = _dot_nt(ovt_ref[...], p_hi) + _dot_nt(ovt_ref[...], p_lo)
        val = jnp.where(forced, FORCE_SCORE, jnp.where(future, NEG_INF, imp_t))
        sel_t = jnp.zeros((n_sel, tq), F32)
        for _ in range(k_top):
            m = jnp.max(val, axis=0, keepdims=True)
            first = jnp.min(jnp.where(val == m, jrow_f, float(n_sel)), axis=0, keepdims=True)
            pick = jrow_f == first
            sel_t = jnp.where(pick, 1.0, sel_t)
            val = jnp.where(pick, -jnp.inf, val)
        sel_pad = jnp.concatenate([sel_t, jnp.zeros((LANES - n_sel, tq), F32)], axis=0) if n_sel < LANES else sel_t
        sel = sel_pad.T.astype(BF16)

        def sel_step(kb, carry):
            m_i, l_i, acc = carry
            k0 = pl.multiple_of(kb * tq, tq)
            s = _dot_nt(qs, ks_ref[0, pl.ds(k0, tq), :]).reshape(g_n, tq, tq)
            chosen = _dot(sel, exp_ref[:, pl.ds(k0, tq)]) > 0.5
            ok = (chosen & ((k0 + kcol) <= tpos))[None]
            s = jnp.where(ok, s, NEG_INF)
            m_new = jnp.maximum(m_i, jnp.max(s, axis=-1, keepdims=True))
            p = jnp.where(ok, jnp.exp(s - m_new), 0.0)
            alpha = jnp.exp(m_i - m_new)
            l_new = alpha * l_i + jnp.sum(p, axis=-1, keepdims=True)
            pv = _dot(p.reshape(g_n * tq, tq).astype(BF16), vs_ref[0, pl.ds(k0, tq), :])
            acc = alpha.reshape(g_n * tq, 1) * acc + pv
            return m_new, l_new, acc

        init = (jnp.full((g_n, tq, 1), NEG_INF, F32), jnp.zeros((g_n, tq, 1), F32), jnp.zeros((g_n * tq, LANES), F32))
        _, l_s, acc_s = lax.fori_loop(0, c + 1, sel_step, init)
        o_slc = acc_s / l_s.reshape(g_n * tq, 1)

        s_w = _dot_nt(qs, kw_ref[0, pl.ds(w_start, w_len), :]).reshape(g_n, tq, w_len)
        s_w = jnp.where(valid_w[None], s_w, NEG_INF)
        e_w = jnp.where(valid_w[None], jnp.exp(s_w - jnp.max(s_w, axis=-1, keepdims=True)), 0.0)
        p_w = e_w / jnp.sum(e_w, axis=-1, keepdims=True)
        o_win = _dot(p_w.reshape(g_n * tq, w_len).astype(BF16), vw_ref[0, pl.ds(w_start, w_len), :])

        per_g = []
        for g in range(g_n):
            col = (kvh * g_n + g) * 3
            rows = slice(g * tq, (g + 1) * tq)
            per_g.append(gates[:, col:col + 1] * o_cmp[rows] + gates[:, col + 1:col + 2] * o_slc[rows]
                         + gates[:, col + 2:col + 3] * o_win[rows])
        outs.append(per_g)

    first_head = lane < HEAD_DIM
    for g in range(g_n):
        o_ref[0, :, g * LANES:(g + 1) * LANES] = jnp.where(first_head, outs[0][g], outs[1][g]).astype(BF16)


def nsa_attention(qa, kc, vc, ks, vs, kw, vw, gates, ovt, expand, tq=128):
    b, t, _ = qa.shape
    n_cmp = kc.shape[1]
    n_sel = ovt.shape[0]
    k_top = min(SEL_TOPK, n_sel)
    qblk = lambda w: pl.BlockSpec((1, tq, w), lambda i, j: (i, j, 0))
    whole = lambda n: pl.BlockSpec((1, n, LANES), lambda i, j: (i, 0, 0))
    return pl.pallas_call(
        functools.partial(_nsa_body, k_top=k_top),
        grid=(b, t // tq),
        in_specs=[qblk(NSA_Q_W), whole(n_cmp), whole(n_cmp), whole(t), whole(t), whole(t), whole(t), qblk(LANES),
                  _full(ovt.shape), _full(expand.shape)],
        out_specs=qblk(NSA_Q_W),
        out_shape=jax.ShapeDtypeStruct((b, t, NSA_Q_W), BF16),
        compiler_params=_params("parallel", "arbitrary"),
    )(qa, kc, vc, ks, vs, kw, vw, gates, ovt, expand)


def _fox_body(q_ref, k_ref, v_ref, cum_ref, cumt_ref, o_ref):
    tq = q_ref.shape[1]
    pair = pl.program_id(1)
    i = pl.program_id(2)
    lane = lax.broadcasted_iota(jnp.int32, (1, LANES), 1)
    first_head = lane < HEAD_DIM
    q = q_ref[0]
    qs = jnp.concatenate([jnp.where(first_head, q, 0), jnp.where(first_head, 0, q)], axis=0)
    cum = cum_ref[0]
    lanes = lax.broadcasted_iota(jnp.int32, cum.shape, 1)
    cq = jnp.stack([jnp.sum(jnp.where(lanes == MISC_F + 2 * pair + h, cum, 0.0), axis=-1, keepdims=True)
                    for h in range(2)], axis=0)
    tpos = i * tq + lax.broadcasted_iota(jnp.int32, (tq, 1), 0)
    kcol = lax.broadcasted_iota(jnp.int32, (1, tq), 1)

    def step(kb, carry):
        m_i, l_i, acc = carry
        k0 = pl.multiple_of(kb * tq, tq)
        s = _dot_nt(qs, k_ref[0, pl.ds(k0, tq), :]).reshape(2, tq, tq)
        ck = cumt_ref[0, 0, :, pl.ds(k0, tq)]
        s = s + cq - ck[:, None, :]
        ok = ((k0 + kcol) <= tpos)[None]
        s = jnp.where(ok, s, NEG_INF)
        m_new = jnp.maximum(m_i, jnp.max(s, axis=-1, keepdims=True))
        p = jnp.where(ok, jnp.exp(s - m_new), 0.0)
        alpha = jnp.exp(m_i - m_new)
        l_new = alpha * l_i + jnp.sum(p, axis=-1, keepdims=True)
        pv = _dot(p.reshape(2 * tq, tq).astype(BF16), v_ref[0, pl.ds(k0, tq), :])
        return m_new, l_new, alpha.reshape(2 * tq, 1) * acc + pv

    init = (jnp.full((2, tq, 1), NEG_INF, F32), jnp.zeros((2, tq, 1), F32), jnp.zeros((2 * tq, LANES), F32))
    _, l_f, acc = lax.fori_loop(0, i + 1, step, init)
    o = acc / l_f.reshape(2 * tq, 1)
    o_ref[0] = jnp.where(first_head, o[:tq], o[tq:]).astype(BF16)


def fox_attention(qb, kb, vf, cum, cum_t, tq=256):
    b, t, w = qb.shape
    pairs = w // LANES
    return pl.pallas_call(
        _fox_body,
        grid=(b, pairs, t // tq),
        in_specs=[pl.BlockSpec((1, tq, LANES), lambda i, p, j: (i, j, p)),
                  pl.BlockSpec((1, t, LANES), lambda i, p, j: (i, 0, p)),
                  pl.BlockSpec((1, t, LANES), lambda i, p, j: (i, 0, p)),
                  pl.BlockSpec((1, tq, LANES), lambda i, p, j: (i, j, 0)),
                  pl.BlockSpec((1, 1, 2, t), lambda i, p, j: (i, p, 0, 0))],
        out_specs=pl.BlockSpec((1, tq, LANES), lambda i, p, j: (i, j, p)),
        out_shape=jax.ShapeDtypeStruct((b, t, w), BF16),
        compiler_params=_params("parallel", "parallel", "arbitrary"),
    )(qb, kb, vf, cum, cum_t)


def _even_out_body(x_ref, oa_ref, ob_ref, wa_ref, wb_ref, o_ref):
    o_ref[...] = x_ref[...] + _dot(oa_ref[...], wa_ref[...]) + _dot(ob_ref[...], wb_ref[...])


def even_out(x2, oa, ob, wa, wb, tm=512):
    n, d = x2.shape
    row = lambda w: pl.BlockSpec((tm, w), lambda i: (i, 0))
    return pl.pallas_call(
        _even_out_body,
        grid=(n // tm,),
        in_specs=[row(d), row(oa.shape[1]), row(ob.shape[1]), _full(wa.shape), _full(wb.shape)],
        out_specs=row(d),
        out_shape=jax.ShapeDtypeStruct((n, d), F32),
        compiler_params=_params("parallel"),
    )(x2, oa, ob, wa, wb)


def _odd_out_body(x_ref, o_ref_in, z_ref, g_ref, w_ref, out_ref):
    gain = g_ref[...]
    parts = []
    for h in range(GDN_HEADS):
        sl = slice(h * GDN_HEAD_DIM, (h + 1) * GDN_HEAD_DIM)
        o = o_ref_in[:, sl]
        y = o * lax.rsqrt(jnp.mean(o * o, axis=-1, keepdims=True) + NORM_EPS) * gain
        parts.append((y * _silu(z_ref[:, sl])).astype(BF16))
    out_ref[...] = x_ref[...] + _dot(jnp.concatenate(parts, axis=1), w_ref[...])


def odd_out(x2, o2, proj, gain, w_bf, tm=512):
    n, d = x2.shape
    row = lambda w: pl.BlockSpec((tm, w), lambda i: (i, 0))
    z_col = 3 * GDN_WIDTH // GDN_WIDTH
    return pl.pallas_call(
        _odd_out_body,
        grid=(n // tm,),
        in_specs=[row(d), row(GDN_WIDTH), pl.BlockSpec((tm, GDN_WIDTH), lambda i: (i, z_col)),
                  _full((1, GDN_HEAD_DIM)), _full(w_bf.shape)],
        out_specs=row(d),
        out_shape=jax.ShapeDtypeStruct((n, d), F32),
        compiler_params=_params("parallel"),
    )(x2, o2, proj, gain, w_bf)


R_GROUP = 0
R_EXPERT = N_GROUPS


def _moe_body(x_ref, g_ref, wr_hi_ref, wr_lo_ref, br_ref, win_ref, wout_ref, o_ref, h_ref, gate_ref):
    e = pl.program_id(1)

    @pl.when(e == 0)
    def _():
        x = x_ref[...]
        h = x * lax.rsqrt(jnp.mean(x * x, axis=-1, keepdims=True) + NORM_EPS) * g_ref[...]
        h_ref[...] = h.astype(BF16)
        h_hi, h_lo = _split2(h)
        logit = _dot(h_hi, wr_hi_ref[...]) + _dot(h_lo, wr_hi_ref[...]) + _dot(h_hi, wr_lo_ref[...]) + br_ref[...]
        lane_i = lax.broadcasted_iota(jnp.int32, logit.shape, 1)
        lane = lane_i.astype(F32)
        is_g = lane_i < N_GROUPS
        g_max = jnp.max(jnp.where(is_g, logit, -jnp.inf), axis=-1, keepdims=True)
        g_sel = jnp.min(jnp.where(is_g & (logit == g_max), lane, float(LANES)), axis=-1, keepdims=True)
        p_group = 1.0 / jnp.sum(jnp.where(is_g, jnp.exp(logit - g_max), 0.0), axis=-1, keepdims=True)
        group_of = ((lane_i - R_EXPERT) // EXPERTS_PER_GROUP).astype(F32)
        mine = (lane_i >= R_EXPERT) & (lane_i < R_EXPERT + N_EXPERTS) & (group_of == g_sel)
        v1 = jnp.max(jnp.where(mine, logit, -jnp.inf), axis=-1, keepdims=True)
        i1 = jnp.min(jnp.where(mine & (logit == v1), lane, float(LANES)), axis=-1, keepdims=True)
        rest = mine & (lane != i1)
        v2 = jnp.max(jnp.where(rest, logit, -jnp.inf), axis=-1, keepdims=True)
        i2 = jnp.min(jnp.where(rest & (logit == v2), lane, float(LANES)), axis=-1, keepdims=True)
        e2 = jnp.exp(v2 - v1)
        w1 = p_group / (1.0 + e2)
        w2 = p_group * e2 / (1.0 + e2)
        gate_ref[...] = jnp.where(lane == i1, w1, 0.0) + jnp.where(lane == i2, w2, 0.0)
        o_ref[...] = x

    gates = gate_ref[...]
    lane = lax.broadcasted_iota(jnp.int32, gates.shape, 1)
    gate_e = jnp.sum(jnp.where(lane == R_EXPERT + e, gates, 0.0), axis=-1, keepdims=True)
    gu = _dot(h_ref[...], win_ref[0])
    act = _silu(gu[:, :EXPERT_FF]) * gu[:, EXPERT_FF:] * gate_e
    o_ref[...] += _dot(act.astype(BF16), wout_ref[0])


def moe(x2, gain, wr_hi, wr_lo, br, win_bf, wout_bf, tm=1024):
    n, d = x2.shape
    row = pl.BlockSpec((tm, d), lambda i, e: (i, 0))
    return pl.pallas_call(
        _moe_body,
        grid=(n // tm, N_EXPERTS),
        in_specs=[row, _full((1, d)), _full((d, LANES)), _full((d, LANES)), _full((1, LANES)),
                  pl.BlockSpec((1, d, 2 * EXPERT_FF), lambda i, e: (e, 0, 0)),
                  pl.BlockSpec((1, EXPERT_FF, d), lambda i, e: (e, 0, 0))],
        out_specs=row,
        out_shape=jax.ShapeDtypeStruct((n, d), F32),
        scratch_shapes=[pltpu.VMEM((tm, d), BF16), pltpu.VMEM((tm, LANES), F32)],
        compiler_params=_params("parallel", "arbitrary"),
    )(x2, gain.reshape(1, d), wr_hi, wr_lo, br, win_bf, wout_bf)


def _gdn_prep_body(p_ref, ab_ref, cw_ref, alog_ref, dtb_ref, qkv_ref, gb_ref):
    t = p_ref.shape[1]
    j = pl.program_id(1)
    x = p_ref[0]
    rows = lax.broadcasted_iota(jnp.int32, (t, 1), 0)
    y = x * cw_ref[CONV_WIDTH - 1:CONV_WIDTH, :]
    for d in range(1, CONV_WIDTH):
        shifted = jnp.where(rows >= d, pltpu.roll(x, d, 0), 0.0)
        y = y + shifted * cw_ref[CONV_WIDTH - 1 - d:CONV_WIDTH - d, :]
    y = _silu(y)
    n_qk = 2 * GDN_WIDTH // y.shape[1]

    @pl.when(j < n_qk)
    def _():
        for h in range(y.shape[1] // GDN_HEAD_DIM):
            sl = slice(h * GDN_HEAD_DIM, (h + 1) * GDN_HEAD_DIM)
            yh = y[:, sl]
            qkv_ref[0, :, sl] = yh * lax.rsqrt(jnp.sum(yh * yh, axis=-1, keepdims=True) + NORM_EPS)

    @pl.when(j >= n_qk)
    def _():
        qkv_ref[0] = y

    @pl.when(j == 0)
    def _():
        ab = ab_ref[0]
        sp_in = ab + dtb_ref[...]
        softplus = jnp.maximum(sp_in, 0.0) + jnp.log1p(jnp.exp(-jnp.abs(sp_in)))
        g = -jnp.exp(alog_ref[...]) * softplus
        lane = lax.broadcasted_iota(jnp.int32, ab.shape, 1)
        g = jnp.where(lane < GDN_HEADS, g, 0.0)
        blk = 4 * GDN_CHUNK
        r = lax.broadcasted_iota(jnp.int32, (blk, blk), 0)
        c = lax.broadcasted_iota(jnp.int32, (blk, blk), 1)
        tri = jnp.where((r >= c) & (r // GDN_CHUNK == c // GDN_CHUNK), 1.0, 0.0).astype(BF16)
        is_g = lax.broadcasted_iota(jnp.int32, (blk, LANES), 1) < GDN_HEADS
        for s in range(t // blk):
            rs = slice(s * blk, (s + 1) * blk)
            hi, mid, lo = _split3(g[rs])
            gc = _dot(tri, hi) + _dot(tri, mid) + _dot(tri, lo)
            gb_ref[0, rs, :] = jnp.where(is_g, gc, _sigmoid(ab[rs]))


def gdn_prep(proj, conv_w, alog_row, dtb_row, tc=512):
    b, t, _ = proj.shape
    n_col = 3 * GDN_WIDTH // tc
    return pl.pallas_call(
        _gdn_prep_body,
        grid=(b, n_col),
        in_specs=[pl.BlockSpec((1, t, tc), lambda i, j: (i, 0, j)),
                  pl.BlockSpec((1, t, LANES), lambda i, j: (i, 0, C_AB // LANES)),
                  pl.BlockSpec((CONV_WIDTH, tc), lambda i, j: (0, j)),
                  _full((1, LANES)), _full((1, LANES))],
        out_specs=[pl.BlockSpec((1, t, tc), lambda i, j: (i, 0, j)),
                   pl.BlockSpec((1, t, LANES), lambda i, j: (i, 0, 0))],
        out_shape=[jax.ShapeDtypeStruct((b, t, 3 * GDN_WIDTH), F32), jax.ShapeDtypeStruct((b, t, LANES), F32)],
        compiler_params=_params("parallel", "arbitrary"),
    )(proj, proj, conv_w, alog_row, dtb_row)


def _dot_hp(a, b):
    return jnp.dot(a, b, preferred_element_type=F32, precision=lax.Precision.HIGHEST)


def _gdn_body(q_ref, k_ref, v_ref, gb_ref, grow_ref, o_ref, gcol_ref, bcol_ref, u_ref, w_ref, a_ref):
    t = q_ref.shape[1]
    cs = GDN_CHUNK
    dk = GDN_HEAD_DIM
    h = pl.program_id(1)
    gb = gb_ref[0]

    lanes = lax.broadcasted_iota(jnp.int32, gb.shape, 1)
    gcol_ref[...] = jnp.broadcast_to(jnp.sum(jnp.where(lanes == h, gb, 0.0), axis=-1, keepdims=True), gb.shape)
    bcol_ref[...] = jnp.broadcast_to(jnp.sum(jnp.where(lanes == GDN_HEADS + h, gb, 0.0), axis=-1, keepdims=True),
                                     gb.shape)
    r = lax.broadcasted_iota(jnp.int32, (cs, cs), 0)
    c = lax.broadcasted_iota(jnp.int32, (cs, cs), 1)
    tril = r >= c
    strict = r > c
    eye = jnp.where(r == c, 1.0, 0.0)

    def prep(n, _):
        r0 = pl.multiple_of(n * cs, cs)
        rows = pl.ds(r0, cs)
        k = k_ref[0, rows, :]
        q = q_ref[0, rows, :] * (dk ** -0.5)
        gc = gcol_ref[rows, :]
        bc = bcol_ref[rows, :]
        gr = grow_ref[0, 0, pl.ds(n, 1), :]
        decay = jnp.where(tril, jnp.exp(jnp.where(tril, gc[:, :cs] - gr, 0.0)), 0.0)
        kb = k * bc
        kbf = k.astype(BF16)
        lmat = jnp.where(strict, _dot_nt(kb.astype(BF16), kbf) * decay, 0.0)
        inv = eye - lmat
        pw = _dot_hp(lmat, lmat)
        span = 2
        while span < cs:
            inv = inv + _dot_hp(inv, pw)
            span *= 2
            if span < cs:
                pw = _dot_hp(pw, pw)
        inv_bf = inv.astype(BF16)
        u_ref[rows, :] = _dot(inv_bf, (v_ref[0, rows, :] * bc).astype(BF16))
        w_ref[rows, :] = _dot(inv_bf, (kb * jnp.exp(gc)).astype(BF16))
        a_ref[rows, :] = jnp.where(tril, _dot_nt(q.astype(BF16), kbf) * decay, 0.0)
        return 0

    lax.fori_loop(0, t // cs, prep, 0)

    def scan(n, state):
        r0 = pl.multiple_of(n * cs, cs)
        rows = pl.ds(r0, cs)
        k = k_ref[0, rows, :]
        q = q_ref[0, rows, :] * (dk ** -0.5)
        gc = gcol_ref[rows, :]
        g_last = gcol_ref[pl.ds(r0 + cs - 1, 1), :]
        s_bf = state.astype(BF16)
        v_new = u_ref[rows, :] - _dot(w_ref[rows, :].astype(BF16), s_bf)
        v_bf = v_new.astype(BF16)
        o_ref[0, rows, :] = _dot((q * jnp.exp(gc)).astype(BF16), s_bf) + _dot(a_ref[rows, :].astype(BF16), v_bf)
        k_dec = (k * jnp.exp(g_last - gc)).astype(BF16)
        return state * jnp.exp(g_last) + _dot_tn(k_dec, v_bf)

    lax.fori_loop(0, t // cs, scan, jnp.zeros((dk, dk), F32))


def gdn_core(qkv, gb):
    b, t, _ = qkv.shape
    n_chunks = t // GDN_CHUNK
    g_rows = jnp.swapaxes(gb[:, :, :GDN_HEADS], 1, 2).reshape(b, GDN_HEADS, n_chunks, GDN_CHUNK)
    head = lambda off: pl.BlockSpec((1, t, GDN_HEAD_DIM), lambda i, h: (i, 0, off + h))
    return pl.pallas_call(
        _gdn_body,
        grid=(b, GDN_HEADS),
        in_specs=[head(0), head(GDN_HEADS), head(2 * GDN_HEADS), pl.BlockSpec((1, t, LANES), lambda i, h: (i, 0, 0)),
                  pl.BlockSpec((1, 1, n_chunks, GDN_CHUNK), lambda i, h: (i, h, 0, 0))],
        out_specs=head(0),
        out_shape=jax.ShapeDtypeStruct((b, t, GDN_WIDTH), F32),
        scratch_shapes=[pltpu.VMEM((t, LANES), F32), pltpu.VMEM((t, LANES), F32),
                        pltpu.VMEM((t, GDN_HEAD_DIM), F32), pltpu.VMEM((t, GDN_HEAD_DIM), F32),
                        pltpu.VMEM((t, GDN_CHUNK), F32)],
        compiler_params=_params("parallel", "arbitrary"),
    )(qkv, qkv, qkv, gb, g_rows)


def _rope_tables(pos):
    half = HEAD_DIM // 2
    inv_freq = ROPE_THETA ** (-jnp.arange(half, dtype=F32) / half)
    ang = pos.astype(F32)[:, None] * inv_freq
    cos = jnp.cos(ang)
    sin = jnp.sin(ang)
    cos_t = jnp.tile(jnp.concatenate([cos, cos], axis=-1), (1, LANES // HEAD_DIM))
    sin_t = jnp.tile(jnp.concatenate([-sin, sin], axis=-1), (1, LANES // HEAD_DIM))
    return cos_t, sin_t


def _block_diag_ones(width, seg):
    idx = np.arange(width) // seg
    return jnp.asarray((idx[:, None] == idx[None, :]).astype(np.float32), dtype=BF16)


def _pad_cols(w, width):
    return jnp.pad(w, ((0, 0), (0, width - w.shape[1])))


def _even_layer(x2, b, t, norm_gain, w_in, b_gate, b_forget, cmp_pe, cmp_w1, cmp_w2, nsa_gain, fox_gain, w_out):
    d = x2.shape[1]
    q_perm = np.concatenate([np.arange(HEAD_DIM) + (kvh * NSA_GROUP + g) * HEAD_DIM
                             for g in range(NSA_GROUP) for kvh in range(NSA_KV_HEADS)])
    o_gate = NSA_Q_W + 6 * NSA_KV_W
    o_fox = o_gate + NSA_GATE_W
    w_re = jnp.concatenate([w_in[:, q_perm], w_in[:, NSA_Q_W:o_gate], w_in[:, o_fox:o_fox + 3 * FOX_W],
                            w_in[:, o_gate:o_fox], w_in[:, o_fox + 3 * FOX_W:]], axis=1)
    proj = norm_matmul(x2, norm_gain, _pad_cols(w_re, EVEN_W).astype(BF16)).reshape(b, t, EVEN_W)

    cos, sin = _rope_tables(jnp.arange(t))
    tile = lambda g, n: jnp.tile(g, n).reshape(1, -1)
    bias = jnp.pad(jnp.concatenate([b_gate, b_forget]), (0, LANES - NSA_GATE_W - FOX_HEADS)).reshape(1, LANES)
    bd = _block_diag_ones(FOX_W, HEAD_DIM)
    (qa, ks, kw, vs, vw, kc_raw, vc_raw, qb, kb, vf, gates, cum) = even_prep(
        proj, cos, sin, tile(nsa_gain[0], NSA_HEADS), tile(nsa_gain[2], NSA_KV_HEADS), tile(nsa_gain[3], NSA_KV_HEADS),
        tile(fox_gain[0], FOX_HEADS), tile(fox_gain[1], FOX_HEADS), bias, bd)

    n_str = t // CMP_STRIDE
    half = CMP_BLOCK // 2
    eye2 = jnp.eye(NSA_KV_HEADS, dtype=F32)
    pe = jnp.tile(cmp_pe[:, :, None, :], (1, 1, NSA_KV_HEADS, 1)).reshape(2, 2, 1, half * NSA_KV_W)
    w1 = jnp.einsum('ilde,hg->ilhdge', cmp_w1, eye2).reshape(2, 2, half * NSA_KV_W, NSA_KV_W).astype(BF16)
    w2 = jnp.einsum('ide,hg->ihdge', cmp_w2, eye2).reshape(2, NSA_KV_W, NSA_KV_W).astype(BF16)
    cos_c, sin_c = _rope_tables(jnp.arange(n_str) * CMP_STRIDE + (CMP_BLOCK - 1))
    kc, vc = compress(kc_raw.reshape(b, n_str, CMP_STRIDE * NSA_KV_W), vc_raw.reshape(b, n_str, CMP_STRIDE * NSA_KV_W),
                      pe, w1, w2, tile(nsa_gain[1], NSA_KV_HEADS), cos_c, sin_c, _block_diag_ones(LANES, HEAD_DIM))

    n_sel = t // SEL_BLOCK
    cs = np.arange(n_str)[:, None] * CMP_STRIDE
    ss = np.arange(n_sel)[None, :] * SEL_BLOCK
    overlap = np.clip(np.minimum(cs + CMP_BLOCK, ss + SEL_BLOCK) - np.maximum(cs, ss), 0, None) / CMP_BLOCK
    overlap[(t - CMP_BLOCK) // CMP_STRIDE + 1:] = 0.0
    ovt = jnp.asarray(overlap.T.astype(np.float32), dtype=BF16)
    expand = np.zeros((LANES, t), np.float32)
    expand[np.arange(t) // SEL_BLOCK, np.arange(t)] = 1.0
    o_a = nsa_attention(qa, kc, vc, ks, vs, kw, vw, gates, ovt, jnp.asarray(expand, dtype=BF16))

    cum_t = jnp.swapaxes(cum[:, :, MISC_F:MISC_F + FOX_HEADS], 1, 2).reshape(b, FOX_HEADS // 2, 2, t)
    o_b = fox_attention(qb, kb, vf, cum, cum_t)

    wa = w_out[:NSA_Q_W][q_perm].astype(BF16)
    wb = w_out[NSA_Q_W:].astype(BF16)
    return even_out(x2, o_a.reshape(b * t, NSA_Q_W), o_b.reshape(b * t, FOX_W), wa, wb)


def _odd_layer(x2, b, t, norm_gain, w_in, conv_w, a_log, dt_bias, gdn_gain, w_out):
    proj = norm_matmul(x2, norm_gain, _pad_cols(w_in, ODD_W).astype(BF16)).reshape(b, t, ODD_W)
    pad8 = lambda v: jnp.pad(v, (0, LANES - GDN_HEADS)).reshape(1, LANES)
    qkv, gb = gdn_prep(proj, conv_w, pad8(a_log), pad8(dt_bias))
    o = gdn_core(qkv, gb)
    return odd_out(x2, o.reshape(b * t, GDN_WIDTH), proj.reshape(b * t, ODD_W), gdn_gain.reshape(1, GDN_HEAD_DIM),
                   w_out.astype(BF16))


def _moe_layer(x2, gain, w_rg, b_rg, w_re, b_re, w_ein, w_eout):
    d = x2.shape[1]
    wr = _pad_cols(jnp.concatenate([w_rg, w_re], axis=1), LANES)
    wr_hi = wr.astype(BF16)
    wr_lo = (wr - wr_hi.astype(F32)).astype(BF16)
    br = jnp.pad(jnp.concatenate([b_rg, b_re]), (0, LANES - N_GROUPS - N_EXPERTS)).reshape(1, LANES)
    return moe(x2, gain, wr_hi, wr_lo, br, w_ein.astype(BF16), w_eout.astype(BF16))


def kernel(x, norm_mix, norm_ffn, w_in_even, b_nsa_gate, b_forget, cmp_pe, cmp_w1, cmp_w2, nsa_qk_gain, fox_qk_gain,
           w_out_even, w_in_odd, conv_w, a_log, dt_bias, gdn_norm_gain, w_out_odd, w_router_group, b_router_group,
           w_router_expert, b_router_expert, w_expert_in, w_expert_out):
    b, t, d = x.shape
    x2 = x.reshape(b * t, d)
    for layer in range(norm_mix.shape[0]):
        i = layer // 2
        if layer % 2 == 0:
            x2 = _even_layer(x2, b, t, norm_mix[layer], w_in_even[i], b_nsa_gate[i], b_forget[i], cmp_pe[i], cmp_w1[i],
                             cmp_w2[i], nsa_qk_gain[i], fox_qk_gain[i], w_out_even[i])
        else:
            x2 = _odd_layer(x2, b, t, norm_mix[layer], w_in_odd[i], conv_w[i], a_log[i], dt_bias[i], gdn_norm_gain[i],
                            w_out_odd[i])
        x2 = _moe_layer(x2, norm_ffn[layer], w_router_group[layer], b_router_group[layer], w_router_expert[layer],
                        b_router_expert[layer], w_expert_in[layer], w_expert_out[layer])
    return x2.reshape(b, t, d)
```

```python
import functools

import numpy as np
import jax
import jax.numpy as jnp
from jax import lax
from jax.experimental import pallas as pl
from jax.experimental.pallas import tpu as pltpu

F32 = jnp.float32
BF16 = jnp.bfloat16

HEAD_DIM = 64
ROPE_THETA = 10000.0
NSA_HEADS = 8
NSA_KV_HEADS = 2
NSA_GROUP = NSA_HEADS // NSA_KV_HEADS
CMP_BLOCK = 32
CMP_STRIDE = 16
SEL_BLOCK = 64
SEL_TOPK = 8
WINDOW = 256
FOX_HEADS = 8
GDN_HEADS = 8
GDN_HEAD_DIM = 128
GDN_WIDTH = GDN_HEADS * GDN_HEAD_DIM
CONV_WIDTH = 4
GDN_CHUNK = 64
N_GROUPS = 4
EXPERTS_PER_GROUP = 4
N_EXPERTS = N_GROUPS * EXPERTS_PER_GROUP
EXPERT_FF = 256
NORM_EPS = 1e-6
NEG_INF = -1e30
FORCE_SCORE = 1e9

LANES = 128
NSA_Q_W = NSA_HEADS * HEAD_DIM
NSA_KV_W = NSA_KV_HEADS * HEAD_DIM
NSA_GATE_W = 3 * NSA_HEADS
FOX_W = FOX_HEADS * HEAD_DIM
C_QN = 0
C_KC, C_VC, C_KS, C_VS, C_KW, C_VW = (NSA_Q_W + i * NSA_KV_W for i in range(6))
C_QF = NSA_Q_W + 6 * NSA_KV_W
C_KF = C_QF + FOX_W
C_VF = C_KF + FOX_W
C_MISC = C_VF + FOX_W
EVEN_W = C_MISC + LANES
MISC_F = NSA_GATE_W
C_AB = 4 * GDN_WIDTH
ODD_W = C_AB + LANES

VMEM_LIMIT = 56 * 1024 * 1024


def _params(*sem):
    return pltpu.CompilerParams(dimension_semantics=sem, vmem_limit_bytes=VMEM_LIMIT)


def _dot(a, b):
    return jnp.dot(a, b, preferred_element_type=F32)


def _dot_nt(a, b):
    return lax.dot_general(a, b, (((1,), (1,)), ((), ())), preferred_element_type=F32)


def _dot_tn(a, b):
    return lax.dot_general(a, b, (((0,), (0,)), ((), ())), preferred_element_type=F32)


def _split2(x):
    hi = x.astype(BF16)
    return hi, (x - hi.astype(F32)).astype(BF16)


def _split3(x):
    hi = x.astype(BF16)
    r = x - hi.astype(F32)
    mid = r.astype(BF16)
    return hi, mid, (r - mid.astype(F32)).astype(BF16)


def _sigmoid(z):
    return 1.0 / (1.0 + jnp.exp(-z))


def _silu(z):
    return z * _sigmoid(z)


def _full(shape):
    nd = len(shape)
    return pl.BlockSpec(shape, lambda *_: (0,) * nd)


def _norm_matmul_body(x_ref, g_ref, w_ref, o_ref):
    x = x_ref[...]
    ms = jnp.mean(x * x, axis=-1, keepdims=True)
    h = (x * lax.rsqrt(ms + NORM_EPS) * g_ref[...]).astype(BF16)
    o_ref[...] = _dot(h, w_ref[...])


def norm_matmul(x2, gain, w_bf, tm=512):
    n, d = x2.shape
    wp = w_bf.shape[1]
    return pl.pallas_call(
        _norm_matmul_body,
        grid=(n // tm,),
        in_specs=[pl.BlockSpec((tm, d), lambda i: (i, 0)), _full((1, d)), _full((d, wp))],
        out_specs=pl.BlockSpec((tm, wp), lambda i: (i, 0)),
        out_shape=jax.ShapeDtypeStruct((n, wp), F32),
        compiler_params=_params("parallel"),
    )(x2, gain.reshape(1, d), w_bf)


def _head_rms(x, bd, gain):
    hi, lo = _split2(x * x)
    w = x.shape[1]
    ssum = _dot(hi, bd[:w, :w]) + _dot(lo, bd[:w, :w])
    return x * lax.rsqrt(ssum * (1.0 / HEAD_DIM) + NORM_EPS) * gain


def _rope(x, cos, sin_signed, first_half):
    fwd = pltpu.roll(x, LANES - HEAD_DIM // 2, 1)
    bwd = pltpu.roll(x, HEAD_DIM // 2, 1)
    return x * cos + jnp.where(first_half, fwd, bwd) * sin_signed


def _even_prep_body(p_ref, cos_ref, sin_ref, gq_ref, gks_ref, gkw_ref, gfq_ref, gfk_ref, bias_ref, bd_ref,
                    qa_ref, ks_ref, kw_ref, vs_ref, vw_ref, kc_ref, vc_ref, qb_ref, kb_ref, vf_ref,
                    gate_ref, cum_ref, carry_ref):
    tr = p_ref.shape[1]
    bd = bd_ref[...]
    cos = cos_ref[...]
    sin = sin_ref[...]
    lane = lax.broadcasted_iota(jnp.int32, (1, LANES), 1)
    first_half = (lane % HEAD_DIM) < (HEAD_DIM // 2)
    scale = HEAD_DIM ** -0.5

    qn = _head_rms(p_ref[0, :, C_QN:C_QN + NSA_Q_W], bd, gq_ref[...])
    for c in range(NSA_Q_W // LANES):
        sl = slice(c * LANES, (c + 1) * LANES)
        qa_ref[0, :, sl] = (_rope(qn[:, sl], cos, sin, first_half) * scale).astype(BF16)
    ks = _head_rms(p_ref[0, :, C_KS:C_KS + NSA_KV_W], bd, gks_ref[...])
    ks_ref[0] = _rope(ks, cos, sin, first_half).astype(BF16)
    kw = _head_rms(p_ref[0, :, C_KW:C_KW + NSA_KV_W], bd, gkw_ref[...])
    kw_ref[0] = _rope(kw, cos, sin, first_half).astype(BF16)
    vs_ref[0] = p_ref[0, :, C_VS:C_VS + NSA_KV_W].astype(BF16)
    vw_ref[0] = p_ref[0, :, C_VW:C_VW + NSA_KV_W].astype(BF16)
    kc_ref[0] = p_ref[0, :, C_KC:C_KC + NSA_KV_W]
    vc_ref[0] = p_ref[0, :, C_VC:C_VC + NSA_KV_W]

    qb_ref[0] = (_head_rms(p_ref[0, :, C_QF:C_QF + FOX_W], bd, gfq_ref[...]) * scale).astype(BF16)
    kb_ref[0] = _head_rms(p_ref[0, :, C_KF:C_KF + FOX_W], bd, gfk_ref[...]).astype(BF16)
    vf_ref[0] = p_ref[0, :, C_VF:C_VF + FOX_W].astype(BF16)

    z = p_ref[0, :, C_MISC:C_MISC + LANES] + bias_ref[...]
    gate_ref[0] = _sigmoid(z)
    logf = jnp.minimum(z, 0.0) - jnp.log1p(jnp.exp(-jnp.abs(z)))

    @pl.when(pl.program_id(1) == 0)
    def _():
        carry_ref[...] = jnp.zeros_like(carry_ref)

    row = lax.broadcasted_iota(jnp.int32, (tr, tr), 0)
    col = lax.broadcasted_iota(jnp.int32, (tr, tr), 1)
    tril = jnp.where(row >= col, 1.0, 0.0).astype(BF16)
    hi, mid, lo = _split3(logf)
    cum = _dot(tril, hi) + _dot(tril, mid) + _dot(tril, lo) + carry_ref[...]
    cum_ref[0] = cum
    carry_ref[...] = cum[tr - 1:tr, :]


def even_prep(proj, cos, sin, gq, gks, gkw, gfq, gfk, bias, bd, tr=256):
    b, t, _ = proj.shape
    row = lambda w: pl.BlockSpec((1, tr, w), lambda i, j: (i, j, 0))
    tab = pl.BlockSpec((tr, LANES), lambda i, j: (j, 0))
    shp = lambda w, dt: jax.ShapeDtypeStruct((b, t, w), dt)
    return pl.pallas_call(
        _even_prep_body,
        grid=(b, t // tr),
        in_specs=[row(EVEN_W), tab, tab, _full((1, NSA_Q_W)), _full((1, LANES)), _full((1, LANES)),
                  _full((1, FOX_W)), _full((1, FOX_W)), _full((1, LANES)), _full((FOX_W, FOX_W))],
        out_specs=[row(NSA_Q_W), row(LANES), row(LANES), row(LANES), row(LANES), row(LANES), row(LANES),
                   row(FOX_W), row(FOX_W), row(FOX_W), row(LANES), row(LANES)],
        out_shape=[shp(NSA_Q_W, BF16), shp(LANES, BF16), shp(LANES, BF16), shp(LANES, BF16), shp(LANES, BF16),
                   shp(LANES, F32), shp(LANES, F32), shp(FOX_W, BF16), shp(FOX_W, BF16), shp(FOX_W, BF16),
                   shp(LANES, F32), shp(LANES, F32)],
        scratch_shapes=[pltpu.VMEM((1, LANES), F32)],
        compiler_params=_params("parallel", "arbitrary"),
    )(proj, cos, sin, gq, gks, gkw, gfq, gfk, bias, bd)


def _gelu_tanh(x):
    return 0.5 * x * (1.0 + jnp.tanh(np.sqrt(2.0 / np.pi).astype(np.float32) * (x + 0.044715 * (x * x * x))))


def _compress_body(xk_ref, xv_ref, pe_ref, w1_ref, w2_ref, gk_ref, cos_ref, sin_ref, bd_ref, kc_ref, vc_ref):
    n = xk_ref.shape[1]
    lane = lax.broadcasted_iota(jnp.int32, (1, LANES), 1)
    first_half = (lane % HEAD_DIM) < (HEAD_DIM // 2)

    def mlp(x_ref, i):
        x = x_ref[0]
        nxt = pltpu.roll(x, n - 1, 0)
        xa = (x + pe_ref[i, 0]).astype(BF16)
        xb = (nxt + pe_ref[i, 1]).astype(BF16)
        h = _dot(xa, w1_ref[i, 0]) + _dot(xb, w1_ref[i, 1])
        return _dot(_gelu_tanh(h).astype(BF16), w2_ref[i])

    kc = _head_rms(mlp(xk_ref, 0), bd_ref[...], gk_ref[...])
    kc_ref[0] = _rope(kc, cos_ref[...], sin_ref[...], first_half).astype(BF16)
    vc_ref[0] = mlp(xv_ref, 1).astype(BF16)


def compress(xk, xv, pe, w1, w2, gk, cos_c, sin_c, bd):
    b, n, w = xk.shape
    blk = pl.BlockSpec((1, n, w), lambda i: (i, 0, 0))
    out = pl.BlockSpec((1, n, LANES), lambda i: (i, 0, 0))
    return pl.pallas_call(
        _compress_body,
        grid=(b,),
        in_specs=[blk, blk, _full(pe.shape), _full(w1.shape), _full(w2.shape), _full((1, LANES)),
                  _full((n, LANES)), _full((n, LANES)), _full((LANES, LANES))],
        out_specs=[out, out],
        out_shape=[jax.ShapeDtypeStruct((b, n, LANES), BF16)] * 2,
        compiler_params=_params("parallel"),
    )(xk, xv, pe, w1, w2, gk, cos_c, sin_c, bd)


def _nsa_body(q_ref, kc_ref, vc_ref, ks_ref, vs_ref, kw_ref, vw_ref, gate_ref, ovt_ref, exp_ref, o_ref, *, k_top):
    tq = q_ref.shape[1]
    t_all = ks_ref.shape[1]
    n_cmp = kc_ref.shape[1]
    n_sel = ovt_ref.shape[0]
    g_n = NSA_GROUP
    c = pl.program_id(1)
    t0 = c * tq
    lane = lax.broadcasted_iota(jnp.int32, (1, LANES), 1)
    q = q_ref[0]
    gates = gate_ref[0]
    tpos = t0 + lax.broadcasted_iota(jnp.int32, (tq, 1), 0)

    ncol = lax.broadcasted_iota(jnp.int32, (1, n_cmp), 1)
    valid_c = (ncol * CMP_STRIDE + (CMP_BLOCK - 1)) <= tpos
    jrow = lax.broadcasted_iota(jnp.int32, (n_sel, tq), 0)
    jrow_f = jrow.astype(F32)
    tq_row = t0 + lax.broadcasted_iota(jnp.int32, (n_sel, tq), 1)
    cur = tq_row // SEL_BLOCK
    forced = (jrow == 0) | (jrow == cur) | (jrow == cur - 1)
    future = jrow * SEL_BLOCK > tq_row
    kcol = lax.broadcasted_iota(jnp.int32, (1, tq), 1)
    w_len = tq + WINDOW
    w_start = pl.multiple_of(jnp.clip(t0 - WINDOW, 0, t_all - w_len), LANES)
    wpos = w_start + lax.broadcasted_iota(jnp.int32, (1, w_len), 1)
    valid_w = (wpos <= tpos) & (wpos > tpos - WINDOW)

    outs = []
    for kvh in range(NSA_KV_HEADS):
        mine = (lane // HEAD_DIM) == kvh
        qs = jnp.concatenate([jnp.where(mine, q[:, g * LANES:(g + 1) * LANES], 0) for g in range(g_n)], axis=0)

        s_c = _dot_nt(qs, kc_ref[0]).reshape(g_n, tq, n_cmp)
        s_c = jnp.where(valid_c[None], s_c, NEG_INF)
        e_c = jnp.where(valid_c[None], jnp.exp(s_c - jnp.max(s_c, axis=-1, keepdims=True)), 0.0)
        den = jnp.sum(e_c, axis=-1, keepdims=True)
        p_c = e_c / jnp.where(den > 0.0, den, 1.0)
        o_cmp = _dot(p_c.reshape(g_n * tq, n_cmp).astype(BF16), vc_ref[0])

        p_hi, p_lo = _split2(jnp.sum(p_c, axis=0))
        imp_t = _dot_nt(ovt_ref[...], p_hi) + _dot_nt(ovt_ref[...], p_lo)
        val = jnp.where(forced, FORCE_SCORE, jnp.where(future, NEG_INF, imp_t))
        sel_t = jnp.zeros((n_sel, tq), F32)
        for _ in range(k_top):
            m = jnp.max(val, axis=0, keepdims=True)
            first = jnp.min(jnp.where(val == m, jrow_f, float(n_sel)), axis=0, keepdims=True)
            pick = jrow_f == first
            sel_t = jnp.where(pick, 1.0, sel_t)
            val = jnp.where(pick, -jnp.inf, val)
        sel_pad = jnp.concatenate([sel_t, jnp.zeros((LANES - n_sel, tq), F32)], axis=0) if n_sel < LANES else sel_t
        sel = sel_pad.T.astype(BF16)

        def sel_step(kb, carry):
            m_i, l_i, acc = carry
            k0 = pl.multiple_of(kb * tq, tq)
            s = _dot_nt(qs, ks_ref[0, pl.ds(k0, tq), :]).reshape(g_n, tq, tq)
            chosen = _dot(sel, exp_ref[:, pl.ds(k0, tq)]) > 0.5
            ok = (chosen & ((k0 + kcol) <= tpos))[None]
            s = jnp.where(ok, s, NEG_INF)
            m_new = jnp.maximum(m_i, jnp.max(s, axis=-1, keepdims=True))
            p = jnp.where(ok, jnp.exp(s - m_new), 0.0)
            alpha = jnp.exp(m_i - m_new)
            l_new = alpha * l_i + jnp.sum(p, axis=-1, keepdims=True)
            pv = _dot(p.reshape(g_n * tq, tq).astype(BF16), vs_ref[0, pl.ds(k0, tq), :])
            acc = alpha.reshape(g_n * tq, 1) * acc + pv
            return m_new, l_new, acc

        init = (jnp.full((g_n, tq, 1), NEG_INF, F32), jnp.zeros((g_n, tq, 1), F32), jnp.zeros((g_n * tq, LANES), F32))
        _, l_s, acc_s = lax.fori_loop(0, c + 1, sel_step, init)
        o_slc = acc_s / l_s.reshape(g_n * tq, 1)

        s_w = _dot_nt(qs, kw_ref[0, pl.ds(w_start, w_len), :]).reshape(g_n, tq, w_len)
        s_w = jnp.where(valid_w[None], s_w, NEG_INF)
        e_w = jnp.where(valid_w[None], jnp.exp(s_w - jnp.max(s_w, axis=-1, keepdims=True)), 0.0)
        p_w = e_w / jnp.sum(e_w, axis=-1, keepdims=True)
        o_win = _dot(p_w.reshape(g_n * tq, w_len).astype(BF16), vw_ref[0, pl.ds(w_start, w_len), :])

        per_g = []
        for g in range(g_n):
            col = (kvh * g_n + g) * 3
            rows = slice(g * tq, (g + 1) * tq)
            per_g.append(gates[:, col:col + 1] * o_cmp[rows] + gates[:, col + 1:col + 2] * o_slc[rows]
                         + gates[:, col + 2:col + 3] * o_win[rows])
        outs.append(per_g)

    first_head = lane < HEAD_DIM
    for g in range(g_n):
        o_ref[0, :, g * LANES:(g + 1) * LANES] = jnp.where(first_head, outs[0][g], outs[1][g]).astype(BF16)


def nsa_attention(qa, kc, vc, ks, vs, kw, vw, gates, ovt, expand, tq=128):
    b, t, _ = qa.shape
    n_cmp = kc.shape[1]
    n_sel = ovt.shape[0]
    k_top = min(SEL_TOPK, n_sel)
    qblk = lambda w: pl.BlockSpec((1, tq, w), lambda i, j: (i, j, 0))
    whole = lambda n: pl.BlockSpec((1, n, LANES), lambda i, j: (i, 0, 0))
    return pl.pallas_call(
        functools.partial(_nsa_body, k_top=k_top),
        grid=(b, t // tq),
        in_specs=[qblk(NSA_Q_W), whole(n_cmp), whole(n_cmp), whole(t), whole(t), whole(t), whole(t), qblk(LANES),
                  _full(ovt.shape), _full(expand.shape)],
        out_specs=qblk(NSA_Q_W),
        out_shape=jax.ShapeDtypeStruct((b, t, NSA_Q_W), BF16),
        compiler_params=_params("parallel", "arbitrary"),
    )(qa, kc, vc, ks, vs, kw, vw, gates, ovt, expand)


def _fox_body(q_ref, k_ref, v_ref, cum_ref, cumt_ref, o_ref):
    tq = q_ref.shape[1]
    pair = pl.program_id(1)
    i = pl.program_id(2)
    lane = lax.broadcasted_iota(jnp.int32, (1, LANES), 1)
    first_head = lane < HEAD_DIM
    q = q_ref[0]
    qs = jnp.concatenate([jnp.where(first_head, q, 0), jnp.where(first_head, 0, q)], axis=0)
    cum = cum_ref[0]
    lanes = lax.broadcasted_iota(jnp.int32, cum.shape, 1)
    cq = jnp.stack([jnp.sum(jnp.where(lanes == MISC_F + 2 * pair + h, cum, 0.0), axis=-1, keepdims=True)
                    for h in range(2)], axis=0)
    tpos = i * tq + lax.broadcasted_iota(jnp.int32, (tq, 1), 0)
    kcol = lax.broadcasted_iota(jnp.int32, (1, tq), 1)

    def step(kb, carry):
        m_i, l_i, acc = carry
        k0 = pl.multiple_of(kb * tq, tq)
        s = _dot_nt(qs, k_ref[0, pl.ds(k0, tq), :]).reshape(2, tq, tq)
        ck = cumt_ref[0, 0, :, pl.ds(k0, tq)]
        s = s + cq - ck[:, None, :]
        ok = ((k0 + kcol) <= tpos)[None]
        s = jnp.where(ok, s, NEG_INF)
        m_new = jnp.maximum(m_i, jnp.max(s, axis=-1, keepdims=True))
        p = jnp.where(ok, jnp.exp(s - m_new), 0.0)
        alpha = jnp.exp(m_i - m_new)
        l_new = alpha * l_i + jnp.sum(p, axis=-1, keepdims=True)
        pv = _dot(p.reshape(2 * tq, tq).astype(BF16), v_ref[0, pl.ds(k0, tq), :])
        return m_new, l_new, alpha.reshape(2 * tq, 1) * acc + pv

    init = (jnp.full((2, tq, 1), NEG_INF, F32), jnp.zeros((2, tq, 1), F32), jnp.zeros((2 * tq, LANES), F32))
    _, l_f, acc = lax.fori_loop(0, i + 1, step, init)
    o = acc / l_f.reshape(2 * tq, 1)
    o_ref[0] = jnp.where(first_head, o[:tq], o[tq:]).astype(BF16)


def fox_attention(qb, kb, vf, cum, cum_t, tq=256):
    b, t, w = qb.shape
    pairs = w // LANES
    return pl.pallas_call(
        _fox_body,
        grid=(b, pairs, t // tq),
        in_specs=[pl.BlockSpec((1, tq, LANES), lambda i, p, j: (i, j, p)),
                  pl.BlockSpec((1, t, LANES), lambda i, p, j: (i, 0, p)),
                  pl.BlockSpec((1, t, LANES), lambda i, p, j: (i, 0, p)),
                  pl.BlockSpec((1, tq, LANES), lambda i, p, j: (i, j, 0)),
                  pl.BlockSpec((1, 1, 2, t), lambda i, p, j: (i, p, 0, 0))],
        out_specs=pl.BlockSpec((1, tq, LANES), lambda i, p, j: (i, j, p)),
        out_shape=jax.ShapeDtypeStruct((b, t, w), BF16),
        compiler_params=_params("parallel", "parallel", "arbitrary"),
    )(qb, kb, vf, cum, cum_t)


def _even_out_body(x_ref, oa_ref, ob_ref, wa_ref, wb_ref, o_ref):
    o_ref[...] = x_ref[...] + _dot(oa_ref[...], wa_ref[...]) + _dot(ob_ref[...], wb_ref[...])


def even_out(x2, oa, ob, wa, wb, tm=512):
    n, d = x2.shape
    row = lambda w: pl.BlockSpec((tm, w), lambda i: (i, 0))
    return pl.pallas_call(
        _even_out_body,
        grid=(n // tm,),
        in_specs=[row(d), row(oa.shape[1]), row(ob.shape[1]), _full(wa.shape), _full(wb.shape)],
        out_specs=row(d),
        out_shape=jax.ShapeDtypeStruct((n, d), F32),
        compiler_params=_params("parallel"),
    )(x2, oa, ob, wa, wb)


def _odd_out_body(x_ref, o_ref_in, z_ref, g_ref, w_ref, out_ref):
    gain = g_ref[...]
    parts = []
    for h in range(GDN_HEADS):
        sl = slice(h * GDN_HEAD_DIM, (h + 1) * GDN_HEAD_DIM)
        o = o_ref_in[:, sl]
        y = o * lax.rsqrt(jnp.mean(o * o, axis=-1, keepdims=True) + NORM_EPS) * gain
        parts.append((y * _silu(z_ref[:, sl])).astype(BF16))
    out_ref[...] = x_ref[...] + _dot(jnp.concatenate(parts, axis=1), w_ref[...])


def odd_out(x2, o2, proj, gain, w_bf, tm=512):
    n, d = x2.shape
    row = lambda w: pl.BlockSpec((tm, w), lambda i: (i, 0))
    z_col = 3 * GDN_WIDTH // GDN_WIDTH
    return pl.pallas_call(
        _odd_out_body,
        grid=(n // tm,),
        in_specs=[row(d), row(GDN_WIDTH), pl.BlockSpec((tm, GDN_WIDTH), lambda i: (i, z_col)),
                  _full((1, GDN_HEAD_DIM)), _full(w_bf.shape)],
        out_specs=row(d),
        out_shape=jax.ShapeDtypeStruct((n, d), F32),
        compiler_params=_params("parallel"),
    )(x2, o2, proj, gain, w_bf)


R_GROUP = 0
R_EXPERT = N_GROUPS


def _moe_body(x_ref, g_ref, wr_hi_ref, wr_lo_ref, br_ref, win_ref, wout_ref, o_ref, h_ref, gate_ref):
    e = pl.program_id(1)

    @pl.when(e == 0)
    def _():
        x = x_ref[...]
        h = x * lax.rsqrt(jnp.mean(x * x, axis=-1, keepdims=True) + NORM_EPS) * g_ref[...]
        h_ref[...] = h.astype(BF16)
        h_hi, h_lo = _split2(h)
        logit = _dot(h_hi, wr_hi_ref[...]) + _dot(h_lo, wr_hi_ref[...]) + _dot(h_hi, wr_lo_ref[...]) + br_ref[...]
        lane_i = lax.broadcasted_iota(jnp.int32, logit.shape, 1)
        lane = lane_i.astype(F32)
        is_g = lane_i < N_GROUPS
        g_max = jnp.max(jnp.where(is_g, logit, -jnp.inf), axis=-1, keepdims=True)
        g_sel = jnp.min(jnp.where(is_g & (logit == g_max), lane, float(LANES)), axis=-1, keepdims=True)
        p_group = 1.0 / jnp.sum(jnp.where(is_g, jnp.exp(logit - g_max), 0.0), axis=-1, keepdims=True)
        group_of = ((lane_i - R_EXPERT) // EXPERTS_PER_GROUP).astype(F32)
        mine = (lane_i >= R_EXPERT) & (lane_i < R_EXPERT + N_EXPERTS) & (group_of == g_sel)
        v1 = jnp.max(jnp.where(mine, logit, -jnp.inf), axis=-1, keepdims=True)
        i1 = jnp.min(jnp.where(mine & (logit == v1), lane, float(LANES)), axis=-1, keepdims=True)
        rest = mine & (lane != i1)
        v2 = jnp.max(jnp.where(rest, logit, -jnp.inf), axis=-1, keepdims=True)
        i2 = jnp.min(jnp.where(rest & (logit == v2), lane, float(LANES)), axis=-1, keepdims=True)
        e2 = jnp.exp(v2 - v1)
        w1 = p_group / (1.0 + e2)
        w2 = p_group * e2 / (1.0 + e2)
        gate_ref[...] = jnp.where(lane == i1, w1, 0.0) + jnp.where(lane == i2, w2, 0.0)
        o_ref[...] = x

    gates = gate_ref[...]
    lane = lax.broadcasted_iota(jnp.int32, gates.shape, 1)
    gate_e = jnp.sum(jnp.where(lane == R_EXPERT + e, gates, 0.0), axis=-1, keepdims=True)
    gu = _dot(h_ref[...], win_ref[0])
    act = _silu(gu[:, :EXPERT_FF]) * gu[:, EXPERT_FF:] * gate_e
    o_ref[...] += _dot(act.astype(BF16), wout_ref[0])


def moe(x2, gain, wr_hi, wr_lo, br, win_bf, wout_bf, tm=1024):
    n, d = x2.shape
    row = pl.BlockSpec((tm, d), lambda i, e: (i, 0))
    return pl.pallas_call(
        _moe_body,
        grid=(n // tm, N_EXPERTS),
        in_specs=[row, _full((1, d)), _full((d, LANES)), _full((d, LANES)), _full((1, LANES)),
                  pl.BlockSpec((1, d, 2 * EXPERT_FF), lambda i, e: (e, 0, 0)),
                  pl.BlockSpec((1, EXPERT_FF, d), lambda i, e: (e, 0, 0))],
        out_specs=row,
        out_shape=jax.ShapeDtypeStruct((n, d), F32),
        scratch_shapes=[pltpu.VMEM((tm, d), BF16), pltpu.VMEM((tm, LANES), F32)],
        compiler_params=_params("parallel", "arbitrary"),
    )(x2, gain.reshape(1, d), wr_hi, wr_lo, br, win_bf, wout_bf)


G_CUM, G_BETA, G_LAST = 0, GDN_HEADS, 2 * GDN_HEADS


def _gdn_gates_body(ab_ref, alog_ref, dtb_ref, gb_ref):
    t = ab_ref.shape[1]
    ab = ab_ref[0]
    sp_in = ab + dtb_ref[...]
    softplus = jnp.maximum(sp_in, 0.0) + jnp.log1p(jnp.exp(-jnp.abs(sp_in)))
    lane_row = lax.broadcasted_iota(jnp.int32, (1, LANES), 1)
    g = jnp.where(lane_row < GDN_HEADS, -jnp.exp(alog_ref[...]) * softplus, 0.0)
    blk = 4 * GDN_CHUNK
    r = lax.broadcasted_iota(jnp.int32, (blk, blk), 0)
    c = lax.broadcasted_iota(jnp.int32, (blk, blk), 1)
    same = r // GDN_CHUNK == c // GDN_CHUNK
    tri = jnp.where(same & (r >= c), 1.0, 0.0).astype(BF16)
    tot = jnp.where(same, 1.0, 0.0).astype(BF16)
    lane = lax.broadcasted_iota(jnp.int32, (blk, LANES), 1)
    for s in range(t // blk):
        rs = slice(s * blk, (s + 1) * blk)
        hi, mid, lo = _split3(g[rs])
        gc = _dot(tri, hi) + _dot(tri, mid) + _dot(tri, lo)
        gl = _dot(tot, hi) + _dot(tot, mid) + _dot(tot, lo)
        gl = pltpu.roll(gl, G_LAST, 1)
        gb_ref[0, rs, :] = jnp.where(lane < G_BETA, gc, jnp.where(lane < G_LAST, _sigmoid(ab[rs]), gl))


def gdn_gates(proj, alog_row, dtb_row):
    b, t, _ = proj.shape
    return pl.pallas_call(
        _gdn_gates_body,
        grid=(b,),
        in_specs=[pl.BlockSpec((1, t, LANES), lambda i: (i, 0, C_AB // LANES)), _full((1, LANES)), _full((1, LANES))],
        out_specs=pl.BlockSpec((1, t, LANES), lambda i: (i, 0, 0)),
        out_shape=jax.ShapeDtypeStruct((b, t, LANES), F32),
        compiler_params=_params("parallel"),
    )(proj, alog_row, dtb_row)


GDN_HEADS_PER_STEP = 2
GDN_CHUNKS_PER_TRIP = 4


def _dot3(a, b):
    a_hi, a_lo = _split2(a)
    b_hi, b_lo = _split2(b)
    return _dot(a_hi, b_hi) + _dot(a_hi, b_lo) + _dot(a_lo, b_hi)


def _conv_silu(x, cw_ref, lanes):
    rows = lax.broadcasted_iota(jnp.int32, (x.shape[0], 1), 0)
    y = x * cw_ref[CONV_WIDTH - 1:CONV_WIDTH, lanes]
    for d in range(1, CONV_WIDTH):
        shifted = jnp.where(rows >= d, pltpu.roll(x, d, 0), 0.0)
        y = y + shifted * cw_ref[CONV_WIDTH - 1 - d:CONV_WIDTH - d, lanes]
    return _silu(y)


def _l2norm(y):
    return y * lax.rsqrt(jnp.sum(y * y, axis=-1, keepdims=True) + NORM_EPS)


def _gdn_body(q_ref, k_ref, v_ref, cq_ref, ck_ref, cv_ref, gb_ref, grow_ref, o_ref,
              gl_ref, gc_ref, kb_ref, k_ref_s, kbg_ref, vb_ref, qs_ref, qg_ref, kd_ref, u_ref, w_ref, a_ref):
    t = q_ref.shape[1]
    cs = GDN_CHUNK
    dk = GDN_HEAD_DIM
    nh = GDN_HEADS_PER_STEP
    pair = pl.program_id(1)
    gb_hi, gb_mid, gb_lo = _split3(gb_ref[0])
    pick_row = lax.broadcasted_iota(jnp.int32, (LANES, LANES), 0)

    def column(idx):
        sel = jnp.where(pick_row == idx, 1.0, 0.0).astype(BF16)
        return _dot(gb_hi, sel) + _dot(gb_mid, sel) + _dot(gb_lo, sel)

    for s in range(nh):
        lanes = slice(s * dk, (s + 1) * dk)
        head = nh * pair + s
        gcol = column(G_CUM + head)
        bcol = column(G_BETA + head)
        glast = column(G_LAST + head)
        eg = jnp.exp(gcol)
        k = _l2norm(_conv_silu(k_ref[0, :, lanes], ck_ref, lanes))
        kb = k * bcol
        k_ref_s[s] = k.astype(BF16)
        kb_ref[s] = kb.astype(BF16)
        kbg_ref[s] = (kb * eg).astype(BF16)
        kd_ref[s] = (k * jnp.exp(glast - gcol)).astype(BF16)
        q = _l2norm(_conv_silu(q_ref[0, :, lanes], cq_ref, lanes)) * (dk ** -0.5)
        qs_ref[s] = q.astype(BF16)
        qg_ref[s] = (q * eg).astype(BF16)
        vb_ref[s] = (_conv_silu(v_ref[0, :, lanes], cv_ref, lanes) * bcol).astype(BF16)
        gl_ref[s] = glast
        gc_ref[s] = gcol

    r = lax.broadcasted_iota(jnp.int32, (cs, cs), 0)
    c = lax.broadcasted_iota(jnp.int32, (cs, cs), 1)
    tril = r >= c
    strict = r > c
    eye = jnp.where(r == c, 1.0, 0.0)

    def prep(trip, _):
        probs = [(s, trip * GDN_CHUNKS_PER_TRIP + j) for j in range(GDN_CHUNKS_PER_TRIP) for s in range(nh)]
        rows = [pl.ds(pl.multiple_of(n * cs, cs), cs) for _, n in probs]
        decay, lmat = [], []
        for (s, n), rw in zip(probs, rows):
            gr = grow_ref[0, s, pl.ds(n, 1), :]
            gc = gc_ref[s, rw, :cs]
            decay.append(jnp.where(tril, jnp.exp(jnp.where(tril, gc - gr, 0.0)), 0.0))
        for i, ((s, _), rw) in enumerate(zip(probs, rows)):
            lmat.append(jnp.where(strict, _dot_nt(kb_ref[s, rw, :], k_ref_s[s, rw, :]) * decay[i], 0.0))
        inv = [eye - m for m in lmat]
        pw = [_dot3(m, m) for m in lmat]
        span = 2
        while span < cs:
            inv = [x + _dot3(x, p) for x, p in zip(inv, pw)]
            span *= 2
            if span < cs:
                pw = [_dot3(p, p) for p in pw]
        inv_bf = [x.astype(BF16) for x in inv]
        for i, ((s, _), rw) in enumerate(zip(probs, rows)):
            u_ref[s, rw, :] = _dot(inv_bf[i], vb_ref[s, rw, :])
            w_ref[s, rw, :] = _dot(inv_bf[i], kbg_ref[s, rw, :]).astype(BF16)
            a_ref[s, rw, :] = jnp.where(tril, _dot_nt(qs_ref[s, rw, :], k_ref_s[s, rw, :]) * decay[i], 0.0).astype(BF16)
        return 0

    lax.fori_loop(0, t // (cs * GDN_CHUNKS_PER_TRIP), prep, 0)

    def scan(n, states):
        r0 = pl.multiple_of(n * cs, cs)
        rows = pl.ds(r0, cs)
        s_bf = [st.astype(BF16) for st in states]
        v_bf = [(u_ref[s, rows, :] - _dot(w_ref[s, rows, :], s_bf[s])).astype(BF16) for s in range(nh)]
        new = [states[s] * jnp.exp(gl_ref[s, pl.ds(r0, 1), :]) + _dot_tn(kd_ref[s, rows, :], v_bf[s])
               for s in range(nh)]
        for s in range(nh):
            o_ref[0, rows, s * dk:(s + 1) * dk] = _dot(qg_ref[s, rows, :], s_bf[s]) + _dot(a_ref[s, rows, :], v_bf[s])
        return tuple(new)

    lax.fori_loop(0, t // cs, scan, tuple(jnp.zeros((dk, dk), F32) for _ in range(nh)))


def gdn_core(proj, conv_w, gb):
    b, t, _ = proj.shape
    n_chunks = t // GDN_CHUNK
    nh = GDN_HEADS_PER_STEP
    wide = nh * GDN_HEAD_DIM
    per = GDN_WIDTH // wide
    g_rows = jnp.swapaxes(gb[:, :, G_CUM:G_CUM + GDN_HEADS], 1, 2).reshape(b, GDN_HEADS, n_chunks, GDN_CHUNK)
    sect = lambda k: pl.BlockSpec((1, t, wide), lambda i, h: (i, 0, k * per + h))
    taps = lambda k: pl.BlockSpec((CONV_WIDTH, wide), lambda i, h: (0, k * per + h))
    bf = lambda w: pltpu.VMEM((nh, t, w), BF16)
    return pl.pallas_call(
        _gdn_body,
        grid=(b, per),
        in_specs=[sect(0), sect(1), sect(2), taps(0), taps(1), taps(2),
                  pl.BlockSpec((1, t, LANES), lambda i, h: (i, 0, 0)),
                  pl.BlockSpec((1, nh, n_chunks, GDN_CHUNK), lambda i, h: (i, h, 0, 0))],
        out_specs=pl.BlockSpec((1, t, wide), lambda i, h: (i, 0, h)),
        out_shape=jax.ShapeDtypeStruct((b, t, GDN_WIDTH), F32),
        scratch_shapes=[pltpu.VMEM((nh, t, LANES), F32)] * 2 + [bf(GDN_HEAD_DIM)] * 7
                       + [pltpu.VMEM((nh, t, GDN_HEAD_DIM), F32), bf(GDN_HEAD_DIM), bf(GDN_CHUNK)],
        compiler_params=_params("parallel", "arbitrary"),
    )(proj, proj, proj, conv_w, conv_w, conv_w, gb, g_rows)


def _rope_tables(pos):
    half = HEAD_DIM // 2
    inv_freq = ROPE_THETA ** (-jnp.arange(half, dtype=F32) / half)
    ang = pos.astype(F32)[:, None] * inv_freq
    cos = jnp.cos(ang)
    sin = jnp.sin(ang)
    cos_t = jnp.tile(jnp.concatenate([cos, cos], axis=-1), (1, LANES // HEAD_DIM))
    sin_t = jnp.tile(jnp.concatenate([-sin, sin], axis=-1), (1, LANES // HEAD_DIM))
    return cos_t, sin_t


def _block_diag_ones(width, seg):
    idx = np.arange(width) // seg
    return jnp.asarray((idx[:, None] == idx[None, :]).astype(np.float32), dtype=BF16)


def _pad_cols(w, width):
    return jnp.pad(w, ((0, 0), (0, width - w.shape[1])))


def _even_layer(x2, b, t, norm_gain, w_in, b_gate, b_forget, cmp_pe, cmp_w1, cmp_w2, nsa_gain, fox_gain, w_out):
    d = x2.shape[1]
    q_perm = np.concatenate([np.arange(HEAD_DIM) + (kvh * NSA_GROUP + g) * HEAD_DIM
                             for g in range(NSA_GROUP) for kvh in range(NSA_KV_HEADS)])
    o_gate = NSA_Q_W + 6 * NSA_KV_W
    o_fox = o_gate + NSA_GATE_W
    w_re = jnp.concatenate([w_in[:, q_perm], w_in[:, NSA_Q_W:o_gate], w_in[:, o_fox:o_fox + 3 * FOX_W],
                            w_in[:, o_gate:o_fox], w_in[:, o_fox + 3 * FOX_W:]], axis=1)
    proj = norm_matmul(x2, norm_gain, _pad_cols(w_re, EVEN_W).astype(BF16)).reshape(b, t, EVEN_W)

    cos, sin = _rope_tables(jnp.arange(t))
    tile = lambda g, n: jnp.tile(g, n).reshape(1, -1)
    bias = jnp.pad(jnp.concatenate([b_gate, b_forget]), (0, LANES - NSA_GATE_W - FOX_HEADS)).reshape(1, LANES)
    bd = _block_diag_ones(FOX_W, HEAD_DIM)
    (qa, ks, kw, vs, vw, kc_raw, vc_raw, qb, kb, vf, gates, cum) = even_prep(
        proj, cos, sin, tile(nsa_gain[0], NSA_HEADS), tile(nsa_gain[2], NSA_KV_HEADS), tile(nsa_gain[3], NSA_KV_HEADS),
        tile(fox_gain[0], FOX_HEADS), tile(fox_gain[1], FOX_HEADS), bias, bd)

    n_str = t // CMP_STRIDE
    half = CMP_BLOCK // 2
    eye2 = jnp.eye(NSA_KV_HEADS, dtype=F32)
    pe = jnp.tile(cmp_pe[:, :, None, :], (1, 1, NSA_KV_HEADS, 1)).reshape(2, 2, 1, half * NSA_KV_W)
    w1 = jnp.einsum('ilde,hg->ilhdge', cmp_w1, eye2).reshape(2, 2, half * NSA_KV_W, NSA_KV_W).astype(BF16)
    w2 = jnp.einsum('ide,hg->ihdge', cmp_w2, eye2).reshape(2, NSA_KV_W, NSA_KV_W).astype(BF16)
    cos_c, sin_c = _rope_tables(jnp.arange(n_str) * CMP_STRIDE + (CMP_BLOCK - 1))
    kc, vc = compress(kc_raw.reshape(b, n_str, CMP_STRIDE * NSA_KV_W), vc_raw.reshape(b, n_str, CMP_STRIDE * NSA_KV_W),
                      pe, w1, w2, tile(nsa_gain[1], NSA_KV_HEADS), cos_c, sin_c, _block_diag_ones(LANES, HEAD_DIM))

    n_sel = t // SEL_BLOCK
    cs = np.arange(n_str)[:, None] * CMP_STRIDE
    ss = np.arange(n_sel)[None, :] * SEL_BLOCK
    overlap = np.clip(np.minimum(cs + CMP_BLOCK, ss + SEL_BLOCK) - np.maximum(cs, ss), 0, None) / CMP_BLOCK
    overlap[(t - CMP_BLOCK) // CMP_STRIDE + 1:] = 0.0
    ovt = jnp.asarray(overlap.T.astype(np.float32), dtype=BF16)
    expand = np.zeros((LANES, t), np.float32)
    expand[np.arange(t) // SEL_BLOCK, np.arange(t)] = 1.0
    o_a = nsa_attention(qa, kc, vc, ks, vs, kw, vw, gates, ovt, jnp.asarray(expand, dtype=BF16))

    cum_t = jnp.swapaxes(cum[:, :, MISC_F:MISC_F + FOX_HEADS], 1, 2).reshape(b, FOX_HEADS // 2, 2, t)
    o_b = fox_attention(qb, kb, vf, cum, cum_t)

    wa = w_out[:NSA_Q_W][q_perm].astype(BF16)
    wb = w_out[NSA_Q_W:].astype(BF16)
    return even_out(x2, o_a.reshape(b * t, NSA_Q_W), o_b.reshape(b * t, FOX_W), wa, wb)


def _odd_layer(x2, b, t, norm_gain, w_in, conv_w, a_log, dt_bias, gdn_gain, w_out):
    proj = norm_matmul(x2, norm_gain, _pad_cols(w_in, ODD_W).astype(BF16)).reshape(b, t, ODD_W)
    pad8 = lambda v: jnp.pad(v, (0, LANES - GDN_HEADS)).reshape(1, LANES)
    gb = gdn_gates(proj, pad8(a_log), pad8(dt_bias))
    o = gdn_core(proj, conv_w, gb)
    return odd_out(x2, o.reshape(b * t, GDN_WIDTH), proj.reshape(b * t, ODD_W), gdn_gain.reshape(1, GDN_HEAD_DIM),
                   w_out.astype(BF16))


def _moe_layer(x2, gain, w_rg, b_rg, w_re, b_re, w_ein, w_eout):
    d = x2.shape[1]
    wr = _pad_cols(jnp.concatenate([w_rg, w_re], axis=1), LANES)
    wr_hi = wr.astype(BF16)
    wr_lo = (wr - wr_hi.astype(F32)).astype(BF16)
    br = jnp.pad(jnp.concatenate([b_rg, b_re]), (0, LANES - N_GROUPS - N_EXPERTS)).reshape(1, LANES)
    return moe(x2, gain, wr_hi, wr_lo, br, w_ein.astype(BF16), w_eout.astype(BF16))


def kernel(x, norm_mix, norm_ffn, w_in_even, b_nsa_gate, b_forget, cmp_pe, cmp_w1, cmp_w2, nsa_qk_gain, fox_qk_gain,
           w_out_even, w_in_odd, conv_w, a_log, dt_bias, gdn_norm_gain, w_out_odd, w_router_group, b_router_group,
           w_router_expert, b_router_expert, w_expert_in, w_expert_out):
    b, t, d = x.shape
    x2 = x.reshape(b * t, d)
    for layer in range(norm_mix.shape[0]):
        i = layer // 2
        if layer % 2 == 0:
            x2 = _even_layer(x2, b, t, norm_mix[layer], w_in_even[i], b_nsa_gate[i], b_forget[i], cmp_pe[i], cmp_w1[i],
                             cmp_w2[i], nsa_qk_gain[i], fox_qk_gain[i], w_out_even[i])
        else:
            x2 = _odd_layer(x2, b, t, norm_mix[layer], w_in_odd[i], conv_w[i], a_log[i], dt_bias[i], gdn_norm_gain[i],
                            w_out_odd[i])
        x2 = _moe_layer(x2, norm_ffn[layer], w_router_group[layer], b_router_group[layer], w_router_expert[layer],
                        b_router_expert[layer], w_expert_in[layer], w_expert_out[layer])
    return x2.reshape(b, t, d)
```

```python
import functools

import numpy as np
import jax
import jax.numpy as jnp
from jax import lax
from jax.experimental import pallas as pl
from jax.experimental.pallas import tpu as pltpu

F32 = jnp.float32
BF16 = jnp.bfloat16

HEAD_DIM = 64
ROPE_THETA = 10000.0
NSA_HEADS = 8
NSA_KV_HEADS = 2
NSA_GROUP = NSA_HEADS // NSA_KV_HEADS
CMP_BLOCK = 32
CMP_STRIDE = 16
SEL_BLOCK = 64
SEL_TOPK = 8
WINDOW = 256
FOX_HEADS = 8
GDN_HEADS = 8
GDN_HEAD_DIM = 128
GDN_WIDTH = GDN_HEADS * GDN_HEAD_DIM
CONV_WIDTH = 4
GDN_CHUNK = 64
N_GROUPS = 4
EXPERTS_PER_GROUP = 4
N_EXPERTS = N_GROUPS * EXPERTS_PER_GROUP
EXPERT_FF = 256
NORM_EPS = 1e-6
NEG_INF = -1e30
FORCE_SCORE = 1e9

LANES = 128
NSA_Q_W = NSA_HEADS * HEAD_DIM
NSA_KV_W = NSA_KV_HEADS * HEAD_DIM
NSA_GATE_W = 3 * NSA_HEADS
FOX_W = FOX_HEADS * HEAD_DIM
C_QN = 0
C_KC, C_VC, C_KS, C_VS, C_KW, C_VW = (NSA_Q_W + i * NSA_KV_W for i in range(6))
C_QF = NSA_Q_W + 6 * NSA_KV_W
C_KF = C_QF + FOX_W
C_VF = C_KF + FOX_W
C_MISC = C_VF + FOX_W
EVEN_W = C_MISC + LANES
MISC_F = NSA_GATE_W
C_AB = 4 * GDN_WIDTH
ODD_W = C_AB + LANES

VMEM_LIMIT = 56 * 1024 * 1024


def _params(*sem):
    return pltpu.CompilerParams(dimension_semantics=sem, vmem_limit_bytes=VMEM_LIMIT)


def _dot(a, b):
    return jnp.dot(a, b, preferred_element_type=F32)


def _dot_nt(a, b):
    return lax.dot_general(a, b, (((1,), (1,)), ((), ())), preferred_element_type=F32)


def _dot_tn(a, b):
    return lax.dot_general(a, b, (((0,), (0,)), ((), ())), preferred_element_type=F32)


def _split2(x):
    hi = x.astype(BF16)
    return hi, (x - hi.astype(F32)).astype(BF16)


def _split3(x):
    hi = x.astype(BF16)
    r = x - hi.astype(F32)
    mid = r.astype(BF16)
    return hi, mid, (r - mid.astype(F32)).astype(BF16)


def _sigmoid(z):
    return 1.0 / (1.0 + jnp.exp(-z))


def _silu(z):
    return z * _sigmoid(z)


def _full(shape):
    nd = len(shape)
    return pl.BlockSpec(shape, lambda *_: (0,) * nd)


def _norm_matmul_body(x_ref, g_ref, w_ref, o_ref):
    x = x_ref[...]
    ms = jnp.mean(x * x, axis=-1, keepdims=True)
    h = (x * lax.rsqrt(ms + NORM_EPS) * g_ref[...]).astype(BF16)
    o_ref[...] = _dot(h, w_ref[...])


def norm_matmul(x2, gain, w_bf, tm=512):
    n, d = x2.shape
    wp = w_bf.shape[1]
    return pl.pallas_call(
        _norm_matmul_body,
        grid=(n // tm,),
        in_specs=[pl.BlockSpec((tm, d), lambda i: (i, 0)), _full((1, d)), _full((d, wp))],
        out_specs=pl.BlockSpec((tm, wp), lambda i: (i, 0)),
        out_shape=jax.ShapeDtypeStruct((n, wp), F32),
        compiler_params=_params("parallel"),
    )(x2, gain.reshape(1, d), w_bf)


def _head_rms(x, bd, gain):
    hi, lo = _split2(x * x)
    w = x.shape[1]
    ssum = _dot(hi, bd[:w, :w]) + _dot(lo, bd[:w, :w])
    return x * lax.rsqrt(ssum * (1.0 / HEAD_DIM) + NORM_EPS) * gain


def _rope(x, cos, sin_signed, first_half):
    fwd = pltpu.roll(x, LANES - HEAD_DIM // 2, 1)
    bwd = pltpu.roll(x, HEAD_DIM // 2, 1)
    return x * cos + jnp.where(first_half, fwd, bwd) * sin_signed


def _even_prep_body(p_ref, cos_ref, sin_ref, gq_ref, gks_ref, gkw_ref, gfq_ref, gfk_ref, bias_ref, bd_ref,
                    qa_ref, ks_ref, kw_ref, vs_ref, vw_ref, kc_ref, vc_ref, qb_ref, kb_ref, vf_ref,
                    gate_ref, cum_ref, carry_ref):
    tr = p_ref.shape[1]
    bd = bd_ref[...]
    cos = cos_ref[...]
    sin = sin_ref[...]
    lane = lax.broadcasted_iota(jnp.int32, (1, LANES), 1)
    first_half = (lane % HEAD_DIM) < (HEAD_DIM // 2)
    scale = HEAD_DIM ** -0.5

    qn = _head_rms(p_ref[0, :, C_QN:C_QN + NSA_Q_W], bd, gq_ref[...])
    for c in range(NSA_Q_W // LANES):
        sl = slice(c * LANES, (c + 1) * LANES)
        qa_ref[0, sl, :] = (_rope(qn[:, sl], cos, sin, first_half) * scale).T.astype(BF16)
    ks = _head_rms(p_ref[0, :, C_KS:C_KS + NSA_KV_W], bd, gks_ref[...])
    ks_ref[0] = _rope(ks, cos, sin, first_half).astype(BF16)
    kw = _head_rms(p_ref[0, :, C_KW:C_KW + NSA_KV_W], bd, gkw_ref[...])
    kw_ref[0] = _rope(kw, cos, sin, first_half).astype(BF16)
    vs_ref[0] = p_ref[0, :, C_VS:C_VS + NSA_KV_W].T.astype(BF16)
    vw_ref[0] = p_ref[0, :, C_VW:C_VW + NSA_KV_W].T.astype(BF16)
    kc_ref[0] = p_ref[0, :, C_KC:C_KC + NSA_KV_W]
    vc_ref[0] = p_ref[0, :, C_VC:C_VC + NSA_KV_W]

    qb = _head_rms(p_ref[0, :, C_QF:C_QF + FOX_W], bd, gfq_ref[...]) * scale
    kb_ref[0] = _head_rms(p_ref[0, :, C_KF:C_KF + FOX_W], bd, gfk_ref[...]).astype(BF16)
    for c in range(FOX_W // LANES):
        sl = slice(c * LANES, (c + 1) * LANES)
        qb_ref[0, sl, :] = qb[:, sl].T.astype(BF16)
        vf_ref[0, sl, :] = p_ref[0, :, C_VF + c * LANES:C_VF + (c + 1) * LANES].T.astype(BF16)

    z = p_ref[0, :, C_MISC:C_MISC + LANES] + bias_ref[...]
    gate_ref[0] = _sigmoid(z).T
    logf = jnp.minimum(z, 0.0) - jnp.log1p(jnp.exp(-jnp.abs(z)))

    @pl.when(pl.program_id(1) == 0)
    def _():
        carry_ref[...] = jnp.zeros_like(carry_ref)

    row = lax.broadcasted_iota(jnp.int32, (tr, tr), 0)
    col = lax.broadcasted_iota(jnp.int32, (tr, tr), 1)
    tril = jnp.where(row >= col, 1.0, 0.0).astype(BF16)
    hi, mid, lo = _split3(logf)
    cum = _dot(tril, hi) + _dot(tril, mid) + _dot(tril, lo) + carry_ref[...]
    cum_ref[0] = cum
    carry_ref[...] = cum[tr - 1:tr, :]


def even_prep(proj, cos, sin, gq, gks, gkw, gfq, gfk, bias, bd, tr=256):
    b, t, _ = proj.shape
    row = lambda w: pl.BlockSpec((1, tr, w), lambda i, j: (i, j, 0))
    tab = pl.BlockSpec((tr, LANES), lambda i, j: (j, 0))
    shp = lambda w, dt: jax.ShapeDtypeStruct((b, t, w), dt)
    col = lambda w: pl.BlockSpec((1, w, tr), lambda i, j: (i, 0, j))
    shp_t = lambda w, dt: jax.ShapeDtypeStruct((b, w, t), dt)
    return pl.pallas_call(
        _even_prep_body,
        grid=(b, t // tr),
        in_specs=[row(EVEN_W), tab, tab, _full((1, NSA_Q_W)), _full((1, LANES)), _full((1, LANES)),
                  _full((1, FOX_W)), _full((1, FOX_W)), _full((1, LANES)), _full((FOX_W, FOX_W))],
        out_specs=[col(NSA_Q_W), row(LANES), row(LANES), col(LANES), col(LANES), row(LANES), row(LANES),
                   col(FOX_W), row(FOX_W), col(FOX_W), col(LANES), row(LANES)],
        out_shape=[shp_t(NSA_Q_W, BF16), shp(LANES, BF16), shp(LANES, BF16), shp_t(LANES, BF16), shp_t(LANES, BF16),
                   shp(LANES, F32), shp(LANES, F32), shp_t(FOX_W, BF16), shp(FOX_W, BF16), shp_t(FOX_W, BF16),
                   shp_t(LANES, F32), shp(LANES, F32)],
        scratch_shapes=[pltpu.VMEM((1, LANES), F32)],
        compiler_params=_params("parallel", "arbitrary"),
    )(proj, cos, sin, gq, gks, gkw, gfq, gfk, bias, bd)


def _gelu_tanh(x):
    return 0.5 * x * (1.0 + jnp.tanh(np.sqrt(2.0 / np.pi).astype(np.float32) * (x + 0.044715 * (x * x * x))))


def _compress_body(xk_ref, xv_ref, pe_ref, w1_ref, w2_ref, gk_ref, cos_ref, sin_ref, bd_ref, kc_ref, vc_ref):
    n = xk_ref.shape[1]
    lane = lax.broadcasted_iota(jnp.int32, (1, LANES), 1)
    first_half = (lane % HEAD_DIM) < (HEAD_DIM // 2)

    def mlp(x_ref, i):
        x = x_ref[0]
        nxt = pltpu.roll(x, n - 1, 0)
        xa = (x + pe_ref[i, 0]).astype(BF16)
        xb = (nxt + pe_ref[i, 1]).astype(BF16)
        h = _dot(xa, w1_ref[i, 0]) + _dot(xb, w1_ref[i, 1])
        return _dot(_gelu_tanh(h).astype(BF16), w2_ref[i])

    kc = _head_rms(mlp(xk_ref, 0), bd_ref[...], gk_ref[...])
    kc_ref[0] = _rope(kc, cos_ref[...], sin_ref[...], first_half).astype(BF16)
    vc_ref[0] = mlp(xv_ref, 1).T.astype(BF16)


def compress(xk, xv, pe, w1, w2, gk, cos_c, sin_c, bd):
    b, n, w = xk.shape
    blk = pl.BlockSpec((1, n, w), lambda i: (i, 0, 0))
    out = pl.BlockSpec((1, n, LANES), lambda i: (i, 0, 0))
    return pl.pallas_call(
        _compress_body,
        grid=(b,),
        in_specs=[blk, blk, _full(pe.shape), _full(w1.shape), _full(w2.shape), _full((1, LANES)),
                  _full((n, LANES)), _full((n, LANES)), _full((LANES, LANES))],
        out_specs=[out, pl.BlockSpec((1, LANES, n), lambda i: (i, 0, 0))],
        out_shape=[jax.ShapeDtypeStruct((b, n, LANES), BF16), jax.ShapeDtypeStruct((b, LANES, n), BF16)],
        compiler_params=_params("parallel"),
    )(xk, xv, pe, w1, w2, gk, cos_c, sin_c, bd)


def _flash_step(s_ref, p_ref, acc_ref, v_blk, m_i, l_i, adjust, first=False):
    al, ms, ls = [], [], []
    for cg in range(s_ref.shape[1] // LANES):
        sl = slice(cg * LANES, (cg + 1) * LANES)
        s = adjust(s_ref[:, sl], cg)
        m_new = jnp.maximum(m_i[:, sl], jnp.max(s, axis=0, keepdims=True))
        p = jnp.exp(s - m_new)
        alpha = jnp.exp(m_i[:, sl] - m_new)
        p_ref[:, sl] = p.astype(BF16)
        al.append(alpha)
        ms.append(m_new)
        ls.append(alpha * l_i[:, sl] + jnp.sum(p, axis=0, keepdims=True))
    cat = lambda xs: jnp.concatenate(xs, axis=1)
    pv = _dot(v_blk, p_ref[...])
    acc_ref[...] = pv if first else cat(al) * acc_ref[...] + pv
    return cat(ms), cat(ls)


NSA_KEYS_PER_QUERY_BLOCK = 2


def _nsa_body(q_ref, kc_ref, vc_ref, ks_ref, vs_ref, kw_ref, vw_ref, gate_ref, ovt_ref, o_ref,
              sel_ref, s0_ref, s1_ref, p_ref, acc_ref, *, k_top):
    tq = q_ref.shape[2]
    t_all = ks_ref.shape[1]
    n_cmp = kc_ref.shape[1]
    n_sel = ovt_ref.shape[0]
    g_n = NSA_GROUP
    c = pl.program_id(1)
    t0 = c * tq
    chan = lax.broadcasted_iota(jnp.int32, (LANES, 1), 0)
    tlane = t0 + lax.broadcasted_iota(jnp.int32, (1, tq), 1)
    tile_g = lambda m: jnp.concatenate([m] * g_n, axis=1)
    gates = gate_ref[0]

    nrow = lax.broadcasted_iota(jnp.int32, (n_cmp, 1), 0)
    valid_c = tile_g((nrow * CMP_STRIDE + (CMP_BLOCK - 1)) <= tlane)
    jrow = lax.broadcasted_iota(jnp.int32, (n_sel, tq), 0)
    jrow_f = jrow.astype(F32)
    cur = tlane // SEL_BLOCK
    forced = (jrow == 0) | (jrow == cur) | (jrow == cur - 1)
    future = jrow * SEL_BLOCK > tlane
    tk = NSA_KEYS_PER_QUERY_BLOCK * tq
    krow = lax.broadcasted_iota(jnp.int32, (tk, 1), 0)
    per_blk = tk // SEL_BLOCK
    w_len = tq + WINDOW
    w_start = pl.multiple_of(jnp.clip(t0 - WINDOW, 0, t_all - w_len), LANES)
    wrow = w_start + lax.broadcasted_iota(jnp.int32, (w_len, 1), 0)
    valid_w = tile_g((wrow <= tlane) & (wrow > tlane - WINDOW))

    outs = []
    for kvh in range(NSA_KV_HEADS):
        mine = (chan // HEAD_DIM) == kvh
        qst = jnp.concatenate([jnp.where(mine, q_ref[0, g * LANES:(g + 1) * LANES, :], 0) for g in range(g_n)], axis=1)

        s_c = jnp.where(valid_c, _dot(kc_ref[0], qst), NEG_INF)
        e_c = jnp.where(valid_c, jnp.exp(s_c - jnp.max(s_c, axis=0, keepdims=True)), 0.0)
        den = jnp.sum(e_c, axis=0, keepdims=True)
        p_c = e_c * (1.0 / jnp.where(den > 0.0, den, 1.0))
        o_cmp = _dot(vc_ref[0], p_c.astype(BF16))

        p_sum = p_c[:, :tq]
        for g in range(1, g_n):
            p_sum = p_sum + p_c[:, g * tq:(g + 1) * tq]
        p_hi, p_lo = _split2(p_sum)
        imp_t = _dot(ovt_ref[...], p_hi) + _dot(ovt_ref[...], p_lo)
        val = jnp.where(forced, FORCE_SCORE, jnp.where(future, NEG_INF, imp_t))
        sel_t = jnp.zeros((n_sel, tq), F32)
        for _ in range(k_top):
            m = jnp.max(val, axis=0, keepdims=True)
            first = jnp.min(jnp.where(val == m, jrow_f, float(n_sel)), axis=0, keepdims=True)
            pick = jrow_f == first
            sel_t = jnp.where(pick, 1.0, sel_t)
            val = jnp.where(pick, -jnp.inf, val)
        sel_ref[...] = sel_t

        def put_scores(buf, kb):
            k0 = pl.multiple_of(jnp.minimum(kb * tk, t_all - tk), tk)
            buf[...] = _dot(ks_ref[0, pl.ds(k0, tk), :], qst)

        def half_step(buf, kb, m_i, l_i):
            k0 = pl.multiple_of(jnp.minimum(kb * tk, t_all - tk), tk)
            chosen = jnp.concatenate([jnp.broadcast_to(sel_ref[pl.ds(k0 // SEL_BLOCK + r, 1), :], (SEL_BLOCK, tq))
                                      for r in range(per_blk)], axis=0) > 0.5
            ok = chosen & ((kb * tk + krow) <= tlane)
            adjust = lambda s_cols, cg: jnp.where(ok, s_cols, NEG_INF)
            return _flash_step(buf, p_ref, acc_ref, vs_ref[0, :, pl.ds(k0, tk)], m_i, l_i, adjust)

        def sel_trip(j, carry):
            put_scores(s1_ref, 2 * j + 1)
            carry = half_step(s0_ref, 2 * j, *carry)
            put_scores(s0_ref, 2 * j + 2)
            return half_step(s1_ref, 2 * j + 1, *carry)

        put_scores(s0_ref, 0)
        acc_ref[...] = jnp.zeros_like(acc_ref)
        init = (jnp.full((1, g_n * tq), NEG_INF, F32), jnp.zeros((1, g_n * tq), F32))
        n_blocks = (t0 + tq + tk - 1) // tk
        _, l_s = lax.fori_loop(0, (n_blocks + 1) // 2, sel_trip, init)
        o_slc = acc_ref[...] * (1.0 / l_s)

        s_w = jnp.where(valid_w, _dot(kw_ref[0, pl.ds(w_start, w_len), :], qst), NEG_INF)
        e_w = jnp.exp(s_w - jnp.max(s_w, axis=0, keepdims=True))
        p_w = e_w * (1.0 / jnp.sum(e_w, axis=0, keepdims=True))
        o_win = _dot(vw_ref[0, :, pl.ds(w_start, w_len)], p_w.astype(BF16))

        per_g = []
        for g in range(g_n):
            row = (kvh * g_n + g) * 3
            cols = slice(g * tq, (g + 1) * tq)
            per_g.append(gates[row:row + 1] * o_cmp[:, cols] + gates[row + 1:row + 2] * o_slc[:, cols]
                         + gates[row + 2:row + 3] * o_win[:, cols])
        outs.append(per_g)

    first_head = chan < HEAD_DIM
    for g in range(g_n):
        o_ref[0, :, g * LANES:(g + 1) * LANES] = jnp.where(first_head, outs[0][g], outs[1][g]).T.astype(BF16)


def nsa_attention(qa_t, kc, vc_t, ks, vs_t, kw, vw_t, gates_t, ovt, tq=LANES):
    b, _, t = qa_t.shape
    n_cmp = kc.shape[1]
    n_sel = ovt.shape[0]
    k_top = min(SEL_TOPK, n_sel)
    tk = NSA_KEYS_PER_QUERY_BLOCK * tq
    tok = lambda n: pl.BlockSpec((1, n, LANES), lambda i, j: (i, 0, 0))
    chn = lambda n: pl.BlockSpec((1, LANES, n), lambda i, j: (i, 0, 0))
    return pl.pallas_call(
        functools.partial(_nsa_body, k_top=k_top),
        grid=(b, t // tq),
        in_specs=[pl.BlockSpec((1, NSA_Q_W, tq), lambda i, j: (i, 0, j)), tok(n_cmp), chn(n_cmp), tok(t), chn(t),
                  tok(t), chn(t), pl.BlockSpec((1, LANES, tq), lambda i, j: (i, 0, j)), _full(ovt.shape)],
        out_specs=pl.BlockSpec((1, tq, NSA_Q_W), lambda i, j: (i, j, 0)),
        out_shape=jax.ShapeDtypeStruct((b, t, NSA_Q_W), BF16),
        scratch_shapes=[pltpu.VMEM((n_sel, tq), F32), pltpu.VMEM((tk, NSA_GROUP * tq), F32),
                        pltpu.VMEM((tk, NSA_GROUP * tq), F32), pltpu.VMEM((tk, NSA_GROUP * tq), BF16),
                        pltpu.VMEM((LANES, NSA_GROUP * tq), F32)],
        compiler_params=_params("parallel", "arbitrary"),
    )(qa_t, kc, vc_t, ks, vs_t, kw, vw_t, gates_t, ovt)


FOX_KEYS_PER_QUERY_BLOCK = 2


def _fox_body(q_ref, k_ref, v_ref, cum_ref, o_ref, ck_ref, s0_ref, s1_ref, p_ref, acc_ref, *, tq):
    t = k_ref.shape[1]
    tk = FOX_KEYS_PER_QUERY_BLOCK * tq
    pair = pl.program_id(1)

    hi, mid, lo = _split3(cum_ref[0])
    pick_row = lax.broadcasted_iota(jnp.int32, (LANES, LANES), 0)
    for h in range(2):
        sel = jnp.where(pick_row == MISC_F + 2 * pair + h, 1.0, 0.0).astype(BF16)
        ck_ref[h] = _dot(hi, sel) + _dot(mid, sel) + _dot(lo, sel)

    chan = lax.broadcasted_iota(jnp.int32, (LANES, 1), 0)
    first_head = chan < HEAD_DIM
    krow = lax.broadcasted_iota(jnp.int32, (tk, 1), 0)
    qlane = lax.broadcasted_iota(jnp.int32, (1, tq), 1)
    reps = tq // LANES
    bufs = (s0_ref, s1_ref)
    blocks = [(i, kb) for i in range(t // tq) for kb in range((i * tq) // tk + 1)]
    q_cache = {}

    def q_pair(i):
        if i not in q_cache:
            q = q_ref[0, :, i * tq:(i + 1) * tq]
            q_cache[i] = jnp.concatenate([jnp.where(first_head, q, 0), jnp.where(first_head, 0, q)], axis=1)
        return q_cache[i]

    def put_scores(n):
        i, kb = blocks[n]
        bufs[n % 2][...] = _dot(k_ref[0, kb * tk:(kb + 1) * tk, :], q_pair(i))

    put_scores(0)
    m_i = l_i = None
    for n, (i, kb) in enumerate(blocks):
        if n + 1 < len(blocks):
            put_scores(n + 1)
        last = kb == (i * tq) // tk
        ok = ((kb * tk + krow) <= (i * tq + qlane)) if last else None

        def adjust(s_cols, cg, kb=kb, last=last, ok=ok):
            s_cols = s_cols - ck_ref[cg // reps, kb * tk:(kb + 1) * tk, :]
            return jnp.where(ok[:, (cg % reps) * LANES:(cg % reps + 1) * LANES], s_cols, NEG_INF) if last else s_cols

        if kb == 0:
            m_i = jnp.full((1, 2 * tq), NEG_INF, F32)
            l_i = jnp.zeros((1, 2 * tq), F32)
        m_i, l_i = _flash_step(bufs[n % 2], p_ref, acc_ref, v_ref[0, :, kb * tk:(kb + 1) * tk], m_i, l_i, adjust,
                               first=kb == 0)
        if last:
            o = acc_ref[...] * (1.0 / l_i)
            o_ref[0, i * tq:(i + 1) * tq, :] = jnp.where(first_head, o[:, :tq], o[:, tq:]).T.astype(BF16)


def fox_attention(qb_t, kb, vf_t, cum, tq=256):
    b, w, t = qb_t.shape
    pairs = w // LANES
    tk = FOX_KEYS_PER_QUERY_BLOCK * tq
    return pl.pallas_call(
        functools.partial(_fox_body, tq=tq),
        grid=(b, pairs),
        in_specs=[pl.BlockSpec((1, LANES, t), lambda i, p: (i, p, 0)),
                  pl.BlockSpec((1, t, LANES), lambda i, p: (i, 0, p)),
                  pl.BlockSpec((1, LANES, t), lambda i, p: (i, p, 0)),
                  pl.BlockSpec((1, t, LANES), lambda i, p: (i, 0, 0))],
        out_specs=pl.BlockSpec((1, t, LANES), lambda i, p: (i, 0, p)),
        out_shape=jax.ShapeDtypeStruct((b, t, w), BF16),
        scratch_shapes=[pltpu.VMEM((2, t, LANES), F32), pltpu.VMEM((tk, 2 * tq), F32), pltpu.VMEM((tk, 2 * tq), F32),
                        pltpu.VMEM((tk, 2 * tq), BF16), pltpu.VMEM((LANES, 2 * tq), F32)],
        compiler_params=_params("parallel", "arbitrary"),
    )(qb_t, kb, vf_t, cum)


def _even_out_body(x_ref, oa_ref, ob_ref, wa_ref, wb_ref, o_ref):
    o_ref[...] = x_ref[...] + _dot(oa_ref[...], wa_ref[...]) + _dot(ob_ref[...], wb_ref[...])


def even_out(x2, oa, ob, wa, wb, tm=512):
    n, d = x2.shape
    row = lambda w: pl.BlockSpec((tm, w), lambda i: (i, 0))
    return pl.pallas_call(
        _even_out_body,
        grid=(n // tm,),
        in_specs=[row(d), row(oa.shape[1]), row(ob.shape[1]), _full(wa.shape), _full(wb.shape)],
        out_specs=row(d),
        out_shape=jax.ShapeDtypeStruct((n, d), F32),
        compiler_params=_params("parallel"),
    )(x2, oa, ob, wa, wb)


def _odd_out_body(x_ref, o_ref_in, z_ref, g_ref, w_ref, out_ref):
    gain = g_ref[...]
    parts = []
    for h in range(GDN_HEADS):
        sl = slice(h * GDN_HEAD_DIM, (h + 1) * GDN_HEAD_DIM)
        o = o_ref_in[:, sl]
        y = o * lax.rsqrt(jnp.mean(o * o, axis=-1, keepdims=True) + NORM_EPS) * gain
        parts.append((y * _silu(z_ref[:, sl])).astype(BF16))
    out_ref[...] = x_ref[...] + _dot(jnp.concatenate(parts, axis=1), w_ref[...])


def odd_out(x2, o2, proj, gain, w_bf, tm=512):
    n, d = x2.shape
    row = lambda w: pl.BlockSpec((tm, w), lambda i: (i, 0))
    z_col = 3 * GDN_WIDTH // GDN_WIDTH
    return pl.pallas_call(
        _odd_out_body,
        grid=(n // tm,),
        in_specs=[row(d), row(GDN_WIDTH), pl.BlockSpec((tm, GDN_WIDTH), lambda i: (i, z_col)),
                  _full((1, GDN_HEAD_DIM)), _full(w_bf.shape)],
        out_specs=row(d),
        out_shape=jax.ShapeDtypeStruct((n, d), F32),
        compiler_params=_params("parallel"),
    )(x2, o2, proj, gain, w_bf)


R_GROUP = 0
R_EXPERT = N_GROUPS


def _moe_body(x_ref, g_ref, wr_hi_ref, wr_lo_ref, br_ref, win_ref, wout_ref, o_ref, h_ref, gate_ref):
    e = pl.program_id(1)

    @pl.when(e == 0)
    def _():
        x = x_ref[...]
        h = x * lax.rsqrt(jnp.mean(x * x, axis=-1, keepdims=True) + NORM_EPS) * g_ref[...]
        h_ref[...] = h.astype(BF16)
        h_hi, h_lo = _split2(h)
        logit = _dot(h_hi, wr_hi_ref[...]) + _dot(h_lo, wr_hi_ref[...]) + _dot(h_hi, wr_lo_ref[...]) + br_ref[...]
        lane_i = lax.broadcasted_iota(jnp.int32, logit.shape, 1)
        lane = lane_i.astype(F32)
        is_g = lane_i < N_GROUPS
        g_max = jnp.max(jnp.where(is_g, logit, -jnp.inf), axis=-1, keepdims=True)
        g_sel = jnp.min(jnp.where(is_g & (logit == g_max), lane, float(LANES)), axis=-1, keepdims=True)
        p_group = 1.0 / jnp.sum(jnp.where(is_g, jnp.exp(logit - g_max), 0.0), axis=-1, keepdims=True)
        group_of = ((lane_i - R_EXPERT) // EXPERTS_PER_GROUP).astype(F32)
        mine = (lane_i >= R_EXPERT) & (lane_i < R_EXPERT + N_EXPERTS) & (group_of == g_sel)
        v1 = jnp.max(jnp.where(mine, logit, -jnp.inf), axis=-1, keepdims=True)
        i1 = jnp.min(jnp.where(mine & (logit == v1), lane, float(LANES)), axis=-1, keepdims=True)
        rest = mine & (lane != i1)
        v2 = jnp.max(jnp.where(rest, logit, -jnp.inf), axis=-1, keepdims=True)
        i2 = jnp.min(jnp.where(rest & (logit == v2), lane, float(LANES)), axis=-1, keepdims=True)
        e2 = jnp.exp(v2 - v1)
        w1 = p_group / (1.0 + e2)
        w2 = p_group * e2 / (1.0 + e2)
        gate_ref[...] = jnp.where(lane == i1, w1, 0.0) + jnp.where(lane == i2, w2, 0.0)
        o_ref[...] = x

    gates = gate_ref[...]
    lane = lax.broadcasted_iota(jnp.int32, gates.shape, 1)
    gate_e = jnp.sum(jnp.where(lane == R_EXPERT + e, gates, 0.0), axis=-1, keepdims=True)
    gu = _dot(h_ref[...], win_ref[0])
    act = _silu(gu[:, :EXPERT_FF]) * gu[:, EXPERT_FF:] * gate_e
    o_ref[...] += _dot(act.astype(BF16), wout_ref[0])


def moe(x2, gain, wr_hi, wr_lo, br, win_bf, wout_bf, tm=1024):
    n, d = x2.shape
    row = pl.BlockSpec((tm, d), lambda i, e: (i, 0))
    return pl.pallas_call(
        _moe_body,
        grid=(n // tm, N_EXPERTS),
        in_specs=[row, _full((1, d)), _full((d, LANES)), _full((d, LANES)), _full((1, LANES)),
                  pl.BlockSpec((1, d, 2 * EXPERT_FF), lambda i, e: (e, 0, 0)),
                  pl.BlockSpec((1, EXPERT_FF, d), lambda i, e: (e, 0, 0))],
        out_specs=row,
        out_shape=jax.ShapeDtypeStruct((n, d), F32),
        scratch_shapes=[pltpu.VMEM((tm, d), BF16), pltpu.VMEM((tm, LANES), F32)],
        compiler_params=_params("parallel", "arbitrary"),
    )(x2, gain.reshape(1, d), wr_hi, wr_lo, br, win_bf, wout_bf)


G_CUM, G_BETA, G_LAST = 0, GDN_HEADS, 2 * GDN_HEADS


def _gdn_gates_body(ab_ref, alog_ref, dtb_ref, gb_ref):
    t = ab_ref.shape[1]
    ab = ab_ref[0]
    sp_in = ab + dtb_ref[...]
    softplus = jnp.maximum(sp_in, 0.0) + jnp.log1p(jnp.exp(-jnp.abs(sp_in)))
    lane_row = lax.broadcasted_iota(jnp.int32, (1, LANES), 1)
    g = jnp.where(lane_row < GDN_HEADS, -jnp.exp(alog_ref[...]) * softplus, 0.0)
    blk = 4 * GDN_CHUNK
    r = lax.broadcasted_iota(jnp.int32, (blk, blk), 0)
    c = lax.broadcasted_iota(jnp.int32, (blk, blk), 1)
    same = r // GDN_CHUNK == c // GDN_CHUNK
    tri = jnp.where(same & (r >= c), 1.0, 0.0).astype(BF16)
    tot = jnp.where(same, 1.0, 0.0).astype(BF16)
    lane = lax.broadcasted_iota(jnp.int32, (blk, LANES), 1)
    for s in range(t // blk):
        rs = slice(s * blk, (s + 1) * blk)
        hi, mid, lo = _split3(g[rs])
        gc = _dot(tri, hi) + _dot(tri, mid) + _dot(tri, lo)
        gl = _dot(tot, hi) + _dot(tot, mid) + _dot(tot, lo)
        gl = pltpu.roll(gl, G_LAST, 1)
        gb_ref[0, rs, :] = jnp.where(lane < G_BETA, gc, jnp.where(lane < G_LAST, _sigmoid(ab[rs]), gl))


def gdn_gates(proj, alog_row, dtb_row):
    b, t, _ = proj.shape
    return pl.pallas_call(
        _gdn_gates_body,
        grid=(b,),
        in_specs=[pl.BlockSpec((1, t, LANES), lambda i: (i, 0, C_AB // LANES)), _full((1, LANES)), _full((1, LANES))],
        out_specs=pl.BlockSpec((1, t, LANES), lambda i: (i, 0, 0)),
        out_shape=jax.ShapeDtypeStruct((b, t, LANES), F32),
        compiler_params=_params("parallel"),
    )(proj, alog_row, dtb_row)


GDN_HEADS_PER_STEP = 2
GDN_CHUNKS_PER_TRIP = 4


def _dot3(a, b):
    a_hi, a_lo = _split2(a)
    b_hi, b_lo = _split2(b)
    return _dot(a_hi, b_hi) + _dot(a_hi, b_lo) + _dot(a_lo, b_hi)


def _conv_silu(x, cw_ref, lanes):
    rows = lax.broadcasted_iota(jnp.int32, (x.shape[0], 1), 0)
    y = x * cw_ref[CONV_WIDTH - 1:CONV_WIDTH, lanes]
    for d in range(1, CONV_WIDTH):
        shifted = jnp.where(rows >= d, pltpu.roll(x, d, 0), 0.0)
        y = y + shifted * cw_ref[CONV_WIDTH - 1 - d:CONV_WIDTH - d, lanes]
    return _silu(y)


def _l2norm(y):
    return y * lax.rsqrt(jnp.sum(y * y, axis=-1, keepdims=True) + NORM_EPS)


def _gdn_body(q_ref, k_ref, v_ref, cq_ref, ck_ref, cv_ref, gb_ref, grow_ref, o_ref,
              gl_ref, gc_ref, kb_ref, k_ref_s, kbg_ref, vb_ref, qs_ref, qg_ref, kd_ref, u_ref, w_ref, a_ref):
    t = q_ref.shape[1]
    cs = GDN_CHUNK
    dk = GDN_HEAD_DIM
    nh = GDN_HEADS_PER_STEP
    pair = pl.program_id(1)
    gb_hi, gb_mid, gb_lo = _split3(gb_ref[0])
    pick_row = lax.broadcasted_iota(jnp.int32, (LANES, LANES), 0)

    def column(idx):
        sel = jnp.where(pick_row == idx, 1.0, 0.0).astype(BF16)
        return _dot(gb_hi, sel) + _dot(gb_mid, sel) + _dot(gb_lo, sel)

    for s in range(nh):
        lanes = slice(s * dk, (s + 1) * dk)
        head = nh * pair + s
        gcol = column(G_CUM + head)
        bcol = column(G_BETA + head)
        glast = column(G_LAST + head)
        eg = jnp.exp(gcol)
        k = _l2norm(_conv_silu(k_ref[0, :, lanes], ck_ref, lanes))
        kb = k * bcol
        k_ref_s[s] = k.astype(BF16)
        kb_ref[s] = kb.astype(BF16)
        kbg_ref[s] = (kb * eg).astype(BF16)
        kd_ref[s] = (k * jnp.exp(glast - gcol)).astype(BF16)
        q = _l2norm(_conv_silu(q_ref[0, :, lanes], cq_ref, lanes)) * (dk ** -0.5)
        qs_ref[s] = q.astype(BF16)
        qg_ref[s] = (q * eg).astype(BF16)
        vb_ref[s] = (_conv_silu(v_ref[0, :, lanes], cv_ref, lanes) * bcol).astype(BF16)
        gl_ref[s] = glast
        gc_ref[s] = gcol

    r = lax.broadcasted_iota(jnp.int32, (cs, cs), 0)
    c = lax.broadcasted_iota(jnp.int32, (cs, cs), 1)
    tril = r >= c
    strict = r > c
    eye = jnp.where(r == c, 1.0, 0.0)

    def prep(trip, _):
        probs = [(s, trip * GDN_CHUNKS_PER_TRIP + j) for j in range(GDN_CHUNKS_PER_TRIP) for s in range(nh)]
        rows = [pl.ds(pl.multiple_of(n * cs, cs), cs) for _, n in probs]
        decay, lmat = [], []
        for (s, n), rw in zip(probs, rows):
            gr = grow_ref[0, s, pl.ds(n, 1), :]
            gc = gc_ref[s, rw, :cs]
            decay.append(jnp.where(tril, jnp.exp(jnp.where(tril, gc - gr, 0.0)), 0.0))
        for i, ((s, _), rw) in enumerate(zip(probs, rows)):
            lmat.append(jnp.where(strict, _dot_nt(kb_ref[s, rw, :], k_ref_s[s, rw, :]) * decay[i], 0.0))
        inv = [eye - m for m in lmat]
        pw = [_dot3(m, m) for m in lmat]
        span = 2
        while span < cs:
            inv = [x + _dot3(x, p) for x, p in zip(inv, pw)]
            span *= 2
            if span < cs:
                pw = [_dot3(p, p) for p in pw]
        inv_bf = [x.astype(BF16) for x in inv]
        for i, ((s, _), rw) in enumerate(zip(probs, rows)):
            u_ref[s, rw, :] = _dot(inv_bf[i], vb_ref[s, rw, :])
            w_ref[s, rw, :] = _dot(inv_bf[i], kbg_ref[s, rw, :]).astype(BF16)
            a_ref[s, rw, :] = jnp.where(tril, _dot_nt(qs_ref[s, rw, :], k_ref_s[s, rw, :]) * decay[i], 0.0).astype(BF16)
        return 0

    lax.fori_loop(0, t // (cs * GDN_CHUNKS_PER_TRIP), prep, 0)

    def scan(n, states):
        r0 = pl.multiple_of(n * cs, cs)
        rows = pl.ds(r0, cs)
        s_bf = [st.astype(BF16) for st in states]
        v_bf = [(u_ref[s, rows, :] - _dot(w_ref[s, rows, :], s_bf[s])).astype(BF16) for s in range(nh)]
        new = [states[s] * jnp.exp(gl_ref[s, pl.ds(r0, 1), :]) + _dot_tn(kd_ref[s, rows, :], v_bf[s])
               for s in range(nh)]
        for s in range(nh):
            o_ref[0, rows, s * dk:(s + 1) * dk] = _dot(qg_ref[s, rows, :], s_bf[s]) + _dot(a_ref[s, rows, :], v_bf[s])
        return tuple(new)

    lax.fori_loop(0, t // cs, scan, tuple(jnp.zeros((dk, dk), F32) for _ in range(nh)))


def gdn_core(proj, conv_w, gb):
    b, t, _ = proj.shape
    n_chunks = t // GDN_CHUNK
    nh = GDN_HEADS_PER_STEP
    wide = nh * GDN_HEAD_DIM
    per = GDN_WIDTH // wide
    g_rows = jnp.swapaxes(gb[:, :, G_CUM:G_CUM + GDN_HEADS], 1, 2).reshape(b, GDN_HEADS, n_chunks, GDN_CHUNK)
    sect = lambda k: pl.BlockSpec((1, t, wide), lambda i, h: (i, 0, k * per + h))
    taps = lambda k: pl.BlockSpec((CONV_WIDTH, wide), lambda i, h: (0, k * per + h))
    bf = lambda w: pltpu.VMEM((nh, t, w), BF16)
    return pl.pallas_call(
        _gdn_body,
        grid=(b, per),
        in_specs=[sect(0), sect(1), sect(2), taps(0), taps(1), taps(2),
                  pl.BlockSpec((1, t, LANES), lambda i, h: (i, 0, 0)),
                  pl.BlockSpec((1, nh, n_chunks, GDN_CHUNK), lambda i, h: (i, h, 0, 0))],
        out_specs=pl.BlockSpec((1, t, wide), lambda i, h: (i, 0, h)),
        out_shape=jax.ShapeDtypeStruct((b, t, GDN_WIDTH), F32),
        scratch_shapes=[pltpu.VMEM((nh, t, LANES), F32)] * 2 + [bf(GDN_HEAD_DIM)] * 7
                       + [pltpu.VMEM((nh, t, GDN_HEAD_DIM), F32), bf(GDN_HEAD_DIM), bf(GDN_CHUNK)],
        compiler_params=_params("parallel", "arbitrary"),
    )(proj, proj, proj, conv_w, conv_w, conv_w, gb, g_rows)


def _rope_tables(pos):
    half = HEAD_DIM // 2
    inv_freq = ROPE_THETA ** (-jnp.arange(half, dtype=F32) / half)
    ang = pos.astype(F32)[:, None] * inv_freq
    cos = jnp.cos(ang)
    sin = jnp.sin(ang)
    cos_t = jnp.tile(jnp.concatenate([cos, cos], axis=-1), (1, LANES // HEAD_DIM))
    sin_t = jnp.tile(jnp.concatenate([-sin, sin], axis=-1), (1, LANES // HEAD_DIM))
    return cos_t, sin_t


def _block_diag_ones(width, seg):
    idx = np.arange(width) // seg
    return jnp.asarray((idx[:, None] == idx[None, :]).astype(np.float32), dtype=BF16)


def _pad_cols(w, width):
    return jnp.pad(w, ((0, 0), (0, width - w.shape[1])))


def _even_layer(x2, b, t, norm_gain, w_in, b_gate, b_forget, cmp_pe, cmp_w1, cmp_w2, nsa_gain, fox_gain, w_out):
    d = x2.shape[1]
    q_perm = np.concatenate([np.arange(HEAD_DIM) + (kvh * NSA_GROUP + g) * HEAD_DIM
                             for g in range(NSA_GROUP) for kvh in range(NSA_KV_HEADS)])
    o_gate = NSA_Q_W + 6 * NSA_KV_W
    o_fox = o_gate + NSA_GATE_W
    w_re = jnp.concatenate([w_in[:, q_perm], w_in[:, NSA_Q_W:o_gate], w_in[:, o_fox:o_fox + 3 * FOX_W],
                            w_in[:, o_gate:o_fox], w_in[:, o_fox + 3 * FOX_W:]], axis=1)
    proj = norm_matmul(x2, norm_gain, _pad_cols(w_re, EVEN_W).astype(BF16)).reshape(b, t, EVEN_W)

    cos, sin = _rope_tables(jnp.arange(t))
    tile = lambda g, n: jnp.tile(g, n).reshape(1, -1)
    bias = jnp.pad(jnp.concatenate([b_gate, b_forget]), (0, LANES - NSA_GATE_W - FOX_HEADS)).reshape(1, LANES)
    bd = _block_diag_ones(FOX_W, HEAD_DIM)
    (qa, ks, kw, vs, vw, kc_raw, vc_raw, qb, kb, vf, gates, cum) = even_prep(
        proj, cos, sin, tile(nsa_gain[0], NSA_HEADS), tile(nsa_gain[2], NSA_KV_HEADS), tile(nsa_gain[3], NSA_KV_HEADS),
        tile(fox_gain[0], FOX_HEADS), tile(fox_gain[1], FOX_HEADS), bias, bd)

    n_str = t // CMP_STRIDE
    half = CMP_BLOCK // 2
    eye2 = jnp.eye(NSA_KV_HEADS, dtype=F32)
    pe = jnp.tile(cmp_pe[:, :, None, :], (1, 1, NSA_KV_HEADS, 1)).reshape(2, 2, 1, half * NSA_KV_W)
    w1 = jnp.einsum('ilde,hg->ilhdge', cmp_w1, eye2).reshape(2, 2, half * NSA_KV_W, NSA_KV_W).astype(BF16)
    w2 = jnp.einsum('ide,hg->ihdge', cmp_w2, eye2).reshape(2, NSA_KV_W, NSA_KV_W).astype(BF16)
    cos_c, sin_c = _rope_tables(jnp.arange(n_str) * CMP_STRIDE + (CMP_BLOCK - 1))
    kc, vc = compress(kc_raw.reshape(b, n_str, CMP_STRIDE * NSA_KV_W), vc_raw.reshape(b, n_str, CMP_STRIDE * NSA_KV_W),
                      pe, w1, w2, tile(nsa_gain[1], NSA_KV_HEADS), cos_c, sin_c, _block_diag_ones(LANES, HEAD_DIM))

    n_sel = t // SEL_BLOCK
    cs = np.arange(n_str)[:, None] * CMP_STRIDE
    ss = np.arange(n_sel)[None, :] * SEL_BLOCK
    overlap = np.clip(np.minimum(cs + CMP_BLOCK, ss + SEL_BLOCK) - np.maximum(cs, ss), 0, None) / CMP_BLOCK
    overlap[(t - CMP_BLOCK) // CMP_STRIDE + 1:] = 0.0
    ovt = jnp.asarray(overlap.T.astype(np.float32), dtype=BF16)
    o_a = nsa_attention(qa, kc, vc, ks, vs, kw, vw, gates, ovt)
    o_b = fox_attention(qb, kb, vf, cum)

    wa = w_out[:NSA_Q_W][q_perm].astype(BF16)
    wb = w_out[NSA_Q_W:].astype(BF16)
    return even_out(x2, o_a.reshape(b * t, NSA_Q_W), o_b.reshape(b * t, FOX_W), wa, wb)


def _odd_layer(x2, b, t, norm_gain, w_in, conv_w, a_log, dt_bias, gdn_gain, w_out):
    proj = norm_matmul(x2, norm_gain, _pad_cols(w_in, ODD_W).astype(BF16)).reshape(b, t, ODD_W)
    pad8 = lambda v: jnp.pad(v, (0, LANES - GDN_HEADS)).reshape(1, LANES)
    gb = gdn_gates(proj, pad8(a_log), pad8(dt_bias))
    o = gdn_core(proj, conv_w, gb)
    return odd_out(x2, o.reshape(b * t, GDN_WIDTH), proj.reshape(b * t, ODD_W), gdn_gain.reshape(1, GDN_HEAD_DIM),
                   w_out.astype(BF16))


def _moe_layer(x2, gain, w_rg, b_rg, w_re, b_re, w_ein, w_eout):
    d = x2.shape[1]
    wr = _pad_cols(jnp.concatenate([w_rg, w_re], axis=1), LANES)
    wr_hi = wr.astype(BF16)
    wr_lo = (wr - wr_hi.astype(F32)).astype(BF16)
    br = jnp.pad(jnp.concatenate([b_rg, b_re]), (0, LANES - N_GROUPS - N_EXPERTS)).reshape(1, LANES)
    return moe(x2, gain, wr_hi, wr_lo, br, w_ein.astype(BF16), w_eout.astype(BF16))


def kernel(x, norm_mix, norm_ffn, w_in_even, b_nsa_gate, b_forget, cmp_pe, cmp_w1, cmp_w2, nsa_qk_gain, fox_qk_gain,
           w_out_even, w_in_odd, conv_w, a_log, dt_bias, gdn_norm_gain, w_out_odd, w_router_group, b_router_group,
           w_router_expert, b_router_expert, w_expert_in, w_expert_out):
    b, t, d = x.shape
    x2 = x.reshape(b * t, d)
    for layer in range(norm_mix.shape[0]):
        i = layer // 2
        if layer % 2 == 0:
            x2 = _even_layer(x2, b, t, norm_mix[layer], w_in_even[i], b_nsa_gate[i], b_forget[i], cmp_pe[i], cmp_w1[i],
                             cmp_w2[i], nsa_qk_gain[i], fox_qk_gain[i], w_out_even[i])
        else:
            x2 = _odd_layer(x2, b, t, norm_mix[layer], w_in_odd[i], conv_w[i], a_log[i], dt_bias[i], gdn_norm_gain[i],
                            w_out_odd[i])
        x2 = _moe_layer(x2, norm_ffn[layer], w_router_group[layer], b_router_group[layer], w_router_expert[layer],
                        b_router_expert[layer], w_expert_in[layer], w_expert_out[layer])
    return x2.reshape(b, t, d)
```

```python
import functools

import numpy as np
import jax
import jax.numpy as jnp
from jax import lax
from jax.experimental import pallas as pl
from jax.experimental.pallas import tpu as pltpu

F32 = jnp.float32
BF16 = jnp.bfloat16

HEAD_DIM = 64
ROPE_THETA = 10000.0
NSA_HEADS = 8
NSA_KV_HEADS = 2
NSA_GROUP = NSA_HEADS // NSA_KV_HEADS
CMP_BLOCK = 32
CMP_STRIDE = 16
SEL_BLOCK = 64
SEL_TOPK = 8
WINDOW = 256
FOX_HEADS = 8
GDN_HEADS = 8
GDN_HEAD_DIM = 128
GDN_WIDTH = GDN_HEADS * GDN_HEAD_DIM
CONV_WIDTH = 4
GDN_CHUNK = 64
N_GROUPS = 4
EXPERTS_PER_GROUP = 4
N_EXPERTS = N_GROUPS * EXPERTS_PER_GROUP
EXPERT_FF = 256
NORM_EPS = 1e-6
NEG_INF = -1e30
FORCE_SCORE = 1e9

LANES = 128
NSA_Q_W = NSA_HEADS * HEAD_DIM
NSA_KV_W = NSA_KV_HEADS * HEAD_DIM
NSA_GATE_W = 3 * NSA_HEADS
FOX_W = FOX_HEADS * HEAD_DIM
C_QN = 0
C_KC, C_VC, C_KS, C_VS, C_KW, C_VW = (NSA_Q_W + i * NSA_KV_W for i in range(6))
C_QF = NSA_Q_W + 6 * NSA_KV_W
C_KF = C_QF + FOX_W
C_VF = C_KF + FOX_W
C_MISC = C_VF + FOX_W
EVEN_W = C_MISC + LANES
MISC_F = NSA_GATE_W
C_AB = 4 * GDN_WIDTH
ODD_W = C_AB + LANES

VMEM_LIMIT = 56 * 1024 * 1024


def _params(*sem):
    return pltpu.CompilerParams(dimension_semantics=sem, vmem_limit_bytes=VMEM_LIMIT)


def _dot(a, b):
    return jnp.dot(a, b, preferred_element_type=F32)


def _dot_nt(a, b):
    return lax.dot_general(a, b, (((1,), (1,)), ((), ())), preferred_element_type=F32)


def _dot_tn(a, b):
    return lax.dot_general(a, b, (((0,), (0,)), ((), ())), preferred_element_type=F32)


def _split2(x):
    hi = x.astype(BF16)
    return hi, (x - hi.astype(F32)).astype(BF16)


def _split3(x):
    hi = x.astype(BF16)
    r = x - hi.astype(F32)
    mid = r.astype(BF16)
    return hi, mid, (r - mid.astype(F32)).astype(BF16)


def _sigmoid(z):
    return 1.0 / (1.0 + jnp.exp(-z))


def _silu(z):
    return z * _sigmoid(z)


def _full(shape):
    nd = len(shape)
    return pl.BlockSpec(shape, lambda *_: (0,) * nd)


def _norm_matmul_body(x_ref, g_ref, w_ref, o_ref):
    x = x_ref[...]
    ms = jnp.mean(x * x, axis=-1, keepdims=True)
    h = (x * lax.rsqrt(ms + NORM_EPS) * g_ref[...]).astype(BF16)
    o_ref[...] = _dot(h, w_ref[...])


def norm_matmul(x2, gain, w_bf, tm=512):
    n, d = x2.shape
    wp = w_bf.shape[1]
    return pl.pallas_call(
        _norm_matmul_body,
        grid=(n // tm,),
        in_specs=[pl.BlockSpec((tm, d), lambda i: (i, 0)), _full((1, d)), _full((d, wp))],
        out_specs=pl.BlockSpec((tm, wp), lambda i: (i, 0)),
        out_shape=jax.ShapeDtypeStruct((n, wp), F32),
        compiler_params=_params("parallel"),
    )(x2, gain.reshape(1, d), w_bf)


def _head_rms(x, bd, gain):
    hi, lo = _split2(x * x)
    w = x.shape[1]
    ssum = _dot(hi, bd[:w, :w]) + _dot(lo, bd[:w, :w])
    return x * lax.rsqrt(ssum * (1.0 / HEAD_DIM) + NORM_EPS) * gain


def _rope(x, cos, sin_signed, first_half):
    fwd = pltpu.roll(x, LANES - HEAD_DIM // 2, 1)
    bwd = pltpu.roll(x, HEAD_DIM // 2, 1)
    return x * cos + jnp.where(first_half, fwd, bwd) * sin_signed


def _even_prep_body(p_ref, cos_ref, sin_ref, gq_ref, gks_ref, gkw_ref, gfq_ref, gfk_ref, bias_ref, bd_ref,
                    qa_ref, ks_ref, kw_ref, vs_ref, vw_ref, kc_ref, vc_ref, qb_ref, kb_ref, vf_ref,
                    gate_ref, cum_ref, carry_ref):
    tr = p_ref.shape[1]
    bd = bd_ref[...]
    cos = cos_ref[...]
    sin = sin_ref[...]
    lane = lax.broadcasted_iota(jnp.int32, (1, LANES), 1)
    first_half = (lane % HEAD_DIM) < (HEAD_DIM // 2)
    scale = HEAD_DIM ** -0.5

    qn = _head_rms(p_ref[0, :, C_QN:C_QN + NSA_Q_W], bd, gq_ref[...])
    for c in range(NSA_Q_W // LANES):
        sl = slice(c * LANES, (c + 1) * LANES)
        qa_ref[0, sl, :] = (_rope(qn[:, sl], cos, sin, first_half) * scale).T.astype(BF16)
    ks = _head_rms(p_ref[0, :, C_KS:C_KS + NSA_KV_W], bd, gks_ref[...])
    ks_ref[0] = _rope(ks, cos, sin, first_half).astype(BF16)
    kw = _head_rms(p_ref[0, :, C_KW:C_KW + NSA_KV_W], bd, gkw_ref[...])
    kw_ref[0] = _rope(kw, cos, sin, first_half).astype(BF16)
    vs_ref[0] = p_ref[0, :, C_VS:C_VS + NSA_KV_W].T.astype(BF16)
    vw_ref[0] = p_ref[0, :, C_VW:C_VW + NSA_KV_W].T.astype(BF16)
    kc_ref[0] = p_ref[0, :, C_KC:C_KC + NSA_KV_W]
    vc_ref[0] = p_ref[0, :, C_VC:C_VC + NSA_KV_W]

    qb = _head_rms(p_ref[0, :, C_QF:C_QF + FOX_W], bd, gfq_ref[...]) * scale
    kb_ref[0] = _head_rms(p_ref[0, :, C_KF:C_KF + FOX_W], bd, gfk_ref[...]).astype(BF16)
    for c in range(FOX_W // LANES):
        sl = slice(c * LANES, (c + 1) * LANES)
        qb_ref[0, sl, :] = qb[:, sl].T.astype(BF16)
        vf_ref[0, sl, :] = p_ref[0, :, C_VF + c * LANES:C_VF + (c + 1) * LANES].T.astype(BF16)

    z = p_ref[0, :, C_MISC:C_MISC + LANES] + bias_ref[...]
    gate_ref[0] = _sigmoid(z).T
    logf = jnp.minimum(z, 0.0) - jnp.log1p(jnp.exp(-jnp.abs(z)))

    @pl.when(pl.program_id(1) == 0)
    def _():
        carry_ref[...] = jnp.zeros_like(carry_ref)

    row = lax.broadcasted_iota(jnp.int32, (tr, tr), 0)
    col = lax.broadcasted_iota(jnp.int32, (tr, tr), 1)
    tril = jnp.where(row >= col, 1.0, 0.0).astype(BF16)
    hi, mid, lo = _split3(logf)
    cum = _dot(tril, hi) + _dot(tril, mid) + _dot(tril, lo) + carry_ref[...]
    cum_ref[0] = cum
    carry_ref[...] = cum[tr - 1:tr, :]


def even_prep(proj, cos, sin, gq, gks, gkw, gfq, gfk, bias, bd, tr=256):
    b, t, _ = proj.shape
    row = lambda w: pl.BlockSpec((1, tr, w), lambda i, j: (i, j, 0))
    tab = pl.BlockSpec((tr, LANES), lambda i, j: (j, 0))
    shp = lambda w, dt: jax.ShapeDtypeStruct((b, t, w), dt)
    col = lambda w: pl.BlockSpec((1, w, tr), lambda i, j: (i, 0, j))
    shp_t = lambda w, dt: jax.ShapeDtypeStruct((b, w, t), dt)
    return pl.pallas_call(
        _even_prep_body,
        grid=(b, t // tr),
        in_specs=[row(EVEN_W), tab, tab, _full((1, NSA_Q_W)), _full((1, LANES)), _full((1, LANES)),
                  _full((1, FOX_W)), _full((1, FOX_W)), _full((1, LANES)), _full((FOX_W, FOX_W))],
        out_specs=[col(NSA_Q_W), row(LANES), row(LANES), col(LANES), col(LANES), row(LANES), row(LANES),
                   col(FOX_W), row(FOX_W), col(FOX_W), col(LANES), row(LANES)],
        out_shape=[shp_t(NSA_Q_W, BF16), shp(LANES, BF16), shp(LANES, BF16), shp_t(LANES, BF16), shp_t(LANES, BF16),
                   shp(LANES, F32), shp(LANES, F32), shp_t(FOX_W, BF16), shp(FOX_W, BF16), shp_t(FOX_W, BF16),
                   shp_t(LANES, F32), shp(LANES, F32)],
        scratch_shapes=[pltpu.VMEM((1, LANES), F32)],
        compiler_params=_params("parallel", "arbitrary"),
    )(proj, cos, sin, gq, gks, gkw, gfq, gfk, bias, bd)


def _gelu_tanh(x):
    return 0.5 * x * (1.0 + jnp.tanh(np.sqrt(2.0 / np.pi).astype(np.float32) * (x + 0.044715 * (x * x * x))))


def _compress_body(xk_ref, xv_ref, pe_ref, w1_ref, w2_ref, gk_ref, cos_ref, sin_ref, bd_ref, kc_ref, vc_ref):
    n = xk_ref.shape[1]
    lane = lax.broadcasted_iota(jnp.int32, (1, LANES), 1)
    first_half = (lane % HEAD_DIM) < (HEAD_DIM // 2)

    def mlp(x_ref, i):
        x = x_ref[0]
        nxt = pltpu.roll(x, n - 1, 0)
        xa = (x + pe_ref[i, 0]).astype(BF16)
        xb = (nxt + pe_ref[i, 1]).astype(BF16)
        h = _dot(xa, w1_ref[i, 0]) + _dot(xb, w1_ref[i, 1])
        return _dot(_gelu_tanh(h).astype(BF16), w2_ref[i])

    kc = _head_rms(mlp(xk_ref, 0), bd_ref[...], gk_ref[...])
    kc_ref[0] = _rope(kc, cos_ref[...], sin_ref[...], first_half).astype(BF16)
    vc_ref[0] = mlp(xv_ref, 1).T.astype(BF16)


def compress(xk, xv, pe, w1, w2, gk, cos_c, sin_c, bd):
    b, n, w = xk.shape
    blk = pl.BlockSpec((1, n, w), lambda i: (i, 0, 0))
    out = pl.BlockSpec((1, n, LANES), lambda i: (i, 0, 0))
    return pl.pallas_call(
        _compress_body,
        grid=(b,),
        in_specs=[blk, blk, _full(pe.shape), _full(w1.shape), _full(w2.shape), _full((1, LANES)),
                  _full((n, LANES)), _full((n, LANES)), _full((LANES, LANES))],
        out_specs=[out, pl.BlockSpec((1, LANES, n), lambda i: (i, 0, 0))],
        out_shape=[jax.ShapeDtypeStruct((b, n, LANES), BF16), jax.ShapeDtypeStruct((b, LANES, n), BF16)],
        compiler_params=_params("parallel"),
    )(xk, xv, pe, w1, w2, gk, cos_c, sin_c, bd)


def _flash_step(s_ref, p_ref, acc_ref, v_blk, m_i, l_i, adjust, first=False):
    al, ms, ls = [], [], []
    for cg in range(s_ref.shape[1] // LANES):
        sl = slice(cg * LANES, (cg + 1) * LANES)
        s = adjust(s_ref[:, sl], cg)
        m_new = jnp.maximum(m_i[:, sl], jnp.max(s, axis=0, keepdims=True))
        p = jnp.exp(s - m_new)
        alpha = jnp.exp(m_i[:, sl] - m_new)
        p_ref[:, sl] = p.astype(BF16)
        al.append(alpha)
        ms.append(m_new)
        ls.append(alpha * l_i[:, sl] + jnp.sum(p, axis=0, keepdims=True))
    cat = lambda xs: jnp.concatenate(xs, axis=1)
    pv = _dot(v_blk, p_ref[...])
    acc_ref[...] = pv if first else cat(al) * acc_ref[...] + pv
    return cat(ms), cat(ls)


NSA_KEYS_PER_QUERY_BLOCK = 2


def _nsa_body(q_ref, kc_ref, vc_ref, ks_ref, vs_ref, kw_ref, vw_ref, gate_ref, ovt_ref, o_ref,
              sel_ref, s0_ref, s1_ref, p_ref, acc_ref, *, k_top):
    tq = q_ref.shape[2]
    t_all = ks_ref.shape[1]
    n_cmp = kc_ref.shape[1]
    n_sel = ovt_ref.shape[0]
    g_n = NSA_GROUP
    c = pl.program_id(1)
    t0 = c * tq
    chan = lax.broadcasted_iota(jnp.int32, (LANES, 1), 0)
    tlane = t0 + lax.broadcasted_iota(jnp.int32, (1, tq), 1)
    gates = gate_ref[0]

    nrow = lax.broadcasted_iota(jnp.int32, (n_cmp, 1), 0)
    valid_c = (nrow * CMP_STRIDE + (CMP_BLOCK - 1)) <= tlane
    jrow = lax.broadcasted_iota(jnp.int32, (n_sel, tq), 0)
    jrow_f = jrow.astype(F32)
    cur = tlane // SEL_BLOCK
    forced = (jrow == 0) | (jrow == cur) | (jrow == cur - 1)
    future = jrow * SEL_BLOCK > tlane
    tk = NSA_KEYS_PER_QUERY_BLOCK * tq
    krow = lax.broadcasted_iota(jnp.int32, (tk, 1), 0)
    per_blk = tk // SEL_BLOCK
    w_len = tq + WINDOW
    w_start = pl.multiple_of(jnp.clip(t0 - WINDOW, 0, t_all - w_len), LANES)
    wrow = w_start + lax.broadcasted_iota(jnp.int32, (w_len, 1), 0)
    valid_w = (wrow <= tlane) & (wrow > tlane - WINDOW)

    heads = [(kvh, g) for kvh in range(NSA_KV_HEADS) for g in range(g_n)]
    qst = jnp.concatenate([jnp.where((chan // HEAD_DIM) == kvh, q_ref[0, g * LANES:(g + 1) * LANES, :], 0)
                           for kvh, g in heads], axis=1)
    n_col = len(heads) * tq

    def softmax_cols(s, ok, guard):
        outs = []
        for cg in range(len(heads)):
            sc = jnp.where(ok, s[:, cg * tq:(cg + 1) * tq], NEG_INF)
            e = jnp.exp(sc - jnp.max(sc, axis=0, keepdims=True))
            if guard:
                e = jnp.where(ok, e, 0.0)
            den = jnp.sum(e, axis=0, keepdims=True)
            outs.append(e * (1.0 / (jnp.where(den > 0.0, den, 1.0) if guard else den)))
        return outs

    p_c = softmax_cols(_dot(kc_ref[0], qst), valid_c, guard=True)
    o_cmp = _dot(vc_ref[0], jnp.concatenate(p_c, axis=1).astype(BF16))

    for kvh in range(NSA_KV_HEADS):
        p_sum = p_c[kvh * g_n]
        for g in range(1, g_n):
            p_sum = p_sum + p_c[kvh * g_n + g]
        p_hi, p_lo = _split2(p_sum)
        imp_t = _dot(ovt_ref[...], p_hi) + _dot(ovt_ref[...], p_lo)
        val = jnp.where(forced, FORCE_SCORE, jnp.where(future, NEG_INF, imp_t))
        sel_t = jnp.zeros((n_sel, tq), F32)
        for _ in range(k_top):
            m = jnp.max(val, axis=0, keepdims=True)
            first = jnp.min(jnp.where(val == m, jrow_f, float(n_sel)), axis=0, keepdims=True)
            pick = jrow_f == first
            sel_t = jnp.where(pick, 1.0, sel_t)
            val = jnp.where(pick, -jnp.inf, val)
        sel_ref[kvh] = sel_t

    p_w = softmax_cols(_dot(kw_ref[0, pl.ds(w_start, w_len), :], qst), valid_w, guard=False)
    o_win = _dot(vw_ref[0, :, pl.ds(w_start, w_len)], jnp.concatenate(p_w, axis=1).astype(BF16))

    def put_scores(buf, kb):
        k0 = pl.multiple_of(jnp.minimum(kb * tk, t_all - tk), tk)
        buf[...] = _dot(ks_ref[0, pl.ds(k0, tk), :], qst)

    def half_step(buf, kb, m_i, l_i):
        k0 = pl.multiple_of(jnp.minimum(kb * tk, t_all - tk), tk)
        causal = (kb * tk + krow) <= tlane
        ok = [causal & (jnp.concatenate([jnp.broadcast_to(sel_ref[kvh, pl.ds(k0 // SEL_BLOCK + r, 1), :],
                                                          (SEL_BLOCK, tq)) for r in range(per_blk)], axis=0) > 0.5)
              for kvh in range(NSA_KV_HEADS)]
        adjust = lambda s_cols, cg: jnp.where(ok[cg // g_n], s_cols, NEG_INF)
        return _flash_step(buf, p_ref, acc_ref, vs_ref[0, :, pl.ds(k0, tk)], m_i, l_i, adjust)

    def sel_trip(j, carry):
        put_scores(s1_ref, 2 * j + 1)
        carry = half_step(s0_ref, 2 * j, *carry)
        put_scores(s0_ref, 2 * j + 2)
        return half_step(s1_ref, 2 * j + 1, *carry)

    put_scores(s0_ref, 0)
    acc_ref[...] = jnp.zeros_like(acc_ref)
    init = (jnp.full((1, n_col), NEG_INF, F32), jnp.zeros((1, n_col), F32))
    n_blocks = (t0 + tq + tk - 1) // tk
    _, l_s = lax.fori_loop(0, (n_blocks + 1) // 2, sel_trip, init)
    o_slc = acc_ref[...] * (1.0 / l_s)

    first_head = chan < HEAD_DIM
    for g in range(g_n):
        per_kv = []
        for kvh in range(NSA_KV_HEADS):
            row = (kvh * g_n + g) * 3
            cols = slice((kvh * g_n + g) * tq, (kvh * g_n + g + 1) * tq)
            per_kv.append(gates[row:row + 1] * o_cmp[:, cols] + gates[row + 1:row + 2] * o_slc[:, cols]
                          + gates[row + 2:row + 3] * o_win[:, cols])
        o_ref[0, :, g * LANES:(g + 1) * LANES] = jnp.where(first_head, per_kv[0], per_kv[1]).T.astype(BF16)


def nsa_attention(qa_t, kc, vc_t, ks, vs_t, kw, vw_t, gates_t, ovt, tq=LANES):
    b, _, t = qa_t.shape
    n_cmp = kc.shape[1]
    n_sel = ovt.shape[0]
    k_top = min(SEL_TOPK, n_sel)
    tk = NSA_KEYS_PER_QUERY_BLOCK * tq
    tok = lambda n: pl.BlockSpec((1, n, LANES), lambda i, j: (i, 0, 0))
    chn = lambda n: pl.BlockSpec((1, LANES, n), lambda i, j: (i, 0, 0))
    return pl.pallas_call(
        functools.partial(_nsa_body, k_top=k_top),
        grid=(b, t // tq),
        in_specs=[pl.BlockSpec((1, NSA_Q_W, tq), lambda i, j: (i, 0, j)), tok(n_cmp), chn(n_cmp), tok(t), chn(t),
                  tok(t), chn(t), pl.BlockSpec((1, LANES, tq), lambda i, j: (i, 0, j)), _full(ovt.shape)],
        out_specs=pl.BlockSpec((1, tq, NSA_Q_W), lambda i, j: (i, j, 0)),
        out_shape=jax.ShapeDtypeStruct((b, t, NSA_Q_W), BF16),
        scratch_shapes=[pltpu.VMEM((NSA_KV_HEADS, n_sel, tq), F32), pltpu.VMEM((tk, NSA_HEADS * tq), F32),
                        pltpu.VMEM((tk, NSA_HEADS * tq), F32), pltpu.VMEM((tk, NSA_HEADS * tq), BF16),
                        pltpu.VMEM((LANES, NSA_HEADS * tq), F32)],
        compiler_params=_params("parallel", "arbitrary"),
    )(qa_t, kc, vc_t, ks, vs_t, kw, vw_t, gates_t, ovt)


FOX_KEYS_PER_QUERY_BLOCK = 2


def _fox_body(q_ref, k_ref, v_ref, cum_ref, o_ref, ck_ref, s0_ref, s1_ref, p_ref, acc_ref, *, tq):
    t = k_ref.shape[1]
    tk = FOX_KEYS_PER_QUERY_BLOCK * tq
    pair = pl.program_id(1)

    hi, mid, lo = _split3(cum_ref[0])
    pick_row = lax.broadcasted_iota(jnp.int32, (LANES, LANES), 0)
    for h in range(2):
        sel = jnp.where(pick_row == MISC_F + 2 * pair + h, 1.0, 0.0).astype(BF16)
        ck_ref[h] = _dot(hi, sel) + _dot(mid, sel) + _dot(lo, sel)

    chan = lax.broadcasted_iota(jnp.int32, (LANES, 1), 0)
    first_head = chan < HEAD_DIM
    krow = lax.broadcasted_iota(jnp.int32, (tk, 1), 0)
    qlane = lax.broadcasted_iota(jnp.int32, (1, tq), 1)
    reps = tq // LANES
    bufs = (s0_ref, s1_ref)
    blocks = [(i, kb) for i in range(t // tq) for kb in range((i * tq) // tk + 1)]
    q_cache = {}

    def q_pair(i):
        if i not in q_cache:
            q = q_ref[0, :, i * tq:(i + 1) * tq]
            q_cache[i] = jnp.concatenate([jnp.where(first_head, q, 0), jnp.where(first_head, 0, q)], axis=1)
        return q_cache[i]

    def put_scores(n):
        i, kb = blocks[n]
        bufs[n % 2][...] = _dot(k_ref[0, kb * tk:(kb + 1) * tk, :], q_pair(i))

    put_scores(0)
    m_i = l_i = None
    for n, (i, kb) in enumerate(blocks):
        if n + 1 < len(blocks):
            put_scores(n + 1)
        last = kb == (i * tq) // tk
        ok = ((kb * tk + krow) <= (i * tq + qlane)) if last else None

        def adjust(s_cols, cg, kb=kb, last=last, ok=ok):
            s_cols = s_cols - ck_ref[cg // reps, kb * tk:(kb + 1) * tk, :]
            return jnp.where(ok[:, (cg % reps) * LANES:(cg % reps + 1) * LANES], s_cols, NEG_INF) if last else s_cols

        if kb == 0:
            m_i = jnp.full((1, 2 * tq), NEG_INF, F32)
            l_i = jnp.zeros((1, 2 * tq), F32)
        m_i, l_i = _flash_step(bufs[n % 2], p_ref, acc_ref, v_ref[0, :, kb * tk:(kb + 1) * tk], m_i, l_i, adjust,
                               first=kb == 0)
        if last:
            o = acc_ref[...] * (1.0 / l_i)
            o_ref[0, i * tq:(i + 1) * tq, :] = jnp.where(first_head, o[:, :tq], o[:, tq:]).T.astype(BF16)


def fox_attention(qb_t, kb, vf_t, cum, tq=256):
    b, w, t = qb_t.shape
    pairs = w // LANES
    tk = FOX_KEYS_PER_QUERY_BLOCK * tq
    return pl.pallas_call(
        functools.partial(_fox_body, tq=tq),
        grid=(b, pairs),
        in_specs=[pl.BlockSpec((1, LANES, t), lambda i, p: (i, p, 0)),
                  pl.BlockSpec((1, t, LANES), lambda i, p: (i, 0, p)),
                  pl.BlockSpec((1, LANES, t), lambda i, p: (i, p, 0)),
                  pl.BlockSpec((1, t, LANES), lambda i, p: (i, 0, 0))],
        out_specs=pl.BlockSpec((1, t, LANES), lambda i, p: (i, 0, p)),
        out_shape=jax.ShapeDtypeStruct((b, t, w), BF16),
        scratch_shapes=[pltpu.VMEM((2, t, LANES), F32), pltpu.VMEM((tk, 2 * tq), F32), pltpu.VMEM((tk, 2 * tq), F32),
                        pltpu.VMEM((tk, 2 * tq), BF16), pltpu.VMEM((LANES, 2 * tq), F32)],
        compiler_params=_params("parallel", "arbitrary"),
    )(qb_t, kb, vf_t, cum)


def _even_out_body(x_ref, oa_ref, ob_ref, wa_ref, wb_ref, o_ref):
    o_ref[...] = x_ref[...] + _dot(oa_ref[...], wa_ref[...]) + _dot(ob_ref[...], wb_ref[...])


def even_out(x2, oa, ob, wa, wb, tm=512):
    n, d = x2.shape
    row = lambda w: pl.BlockSpec((tm, w), lambda i: (i, 0))
    return pl.pallas_call(
        _even_out_body,
        grid=(n // tm,),
        in_specs=[row(d), row(oa.shape[1]), row(ob.shape[1]), _full(wa.shape), _full(wb.shape)],
        out_specs=row(d),
        out_shape=jax.ShapeDtypeStruct((n, d), F32),
        compiler_params=_params("parallel"),
    )(x2, oa, ob, wa, wb)


def _odd_out_body(x_ref, o_ref_in, z_ref, g_ref, w_ref, out_ref):
    gain = g_ref[...]
    parts = []
    for h in range(GDN_HEADS):
        sl = slice(h * GDN_HEAD_DIM, (h + 1) * GDN_HEAD_DIM)
        o = o_ref_in[:, sl]
        y = o * lax.rsqrt(jnp.mean(o * o, axis=-1, keepdims=True) + NORM_EPS) * gain
        parts.append((y * _silu(z_ref[:, sl])).astype(BF16))
    out_ref[...] = x_ref[...] + _dot(jnp.concatenate(parts, axis=1), w_ref[...])


def odd_out(x2, o2, proj, gain, w_bf, tm=512):
    n, d = x2.shape
    row = lambda w: pl.BlockSpec((tm, w), lambda i: (i, 0))
    z_col = 3 * GDN_WIDTH // GDN_WIDTH
    return pl.pallas_call(
        _odd_out_body,
        grid=(n // tm,),
        in_specs=[row(d), row(GDN_WIDTH), pl.BlockSpec((tm, GDN_WIDTH), lambda i: (i, z_col)),
                  _full((1, GDN_HEAD_DIM)), _full(w_bf.shape)],
        out_specs=row(d),
        out_shape=jax.ShapeDtypeStruct((n, d), F32),
        compiler_params=_params("parallel"),
    )(x2, o2, proj, gain, w_bf)


R_GROUP = 0
R_EXPERT = N_GROUPS


def _moe_body(x_ref, g_ref, wr_hi_ref, wr_lo_ref, br_ref, win_ref, wout_ref, o_ref, h_ref, gate_ref):
    e = pl.program_id(1)

    @pl.when(e == 0)
    def _():
        x = x_ref[...]
        h = x * lax.rsqrt(jnp.mean(x * x, axis=-1, keepdims=True) + NORM_EPS) * g_ref[...]
        h_ref[...] = h.astype(BF16)
        h_hi, h_lo = _split2(h)
        logit = _dot(h_hi, wr_hi_ref[...]) + _dot(h_lo, wr_hi_ref[...]) + _dot(h_hi, wr_lo_ref[...]) + br_ref[...]
        lane_i = lax.broadcasted_iota(jnp.int32, logit.shape, 1)
        lane = lane_i.astype(F32)
        is_g = lane_i < N_GROUPS
        g_max = jnp.max(jnp.where(is_g, logit, -jnp.inf), axis=-1, keepdims=True)
        g_sel = jnp.min(jnp.where(is_g & (logit == g_max), lane, float(LANES)), axis=-1, keepdims=True)
        p_group = 1.0 / jnp.sum(jnp.where(is_g, jnp.exp(logit - g_max), 0.0), axis=-1, keepdims=True)
        group_of = ((lane_i - R_EXPERT) // EXPERTS_PER_GROUP).astype(F32)
        mine = (lane_i >= R_EXPERT) & (lane_i < R_EXPERT + N_EXPERTS) & (group_of == g_sel)
        v1 = jnp.max(jnp.where(mine, logit, -jnp.inf), axis=-1, keepdims=True)
        i1 = jnp.min(jnp.where(mine & (logit == v1), lane, float(LANES)), axis=-1, keepdims=True)
        rest = mine & (lane != i1)
        v2 = jnp.max(jnp.where(rest, logit, -jnp.inf), axis=-1, keepdims=True)
        i2 = jnp.min(jnp.where(rest & (logit == v2), lane, float(LANES)), axis=-1, keepdims=True)
        e2 = jnp.exp(v2 - v1)
        w1 = p_group / (1.0 + e2)
        w2 = p_group * e2 / (1.0 + e2)
        gate_ref[...] = jnp.where(lane == i1, w1, 0.0) + jnp.where(lane == i2, w2, 0.0)
        o_ref[...] = x

    gates = gate_ref[...]
    lane = lax.broadcasted_iota(jnp.int32, gates.shape, 1)
    acts = []
    for j in range(EXPERTS_PER_GROUP):
        gate_e = jnp.sum(jnp.where(lane == R_EXPERT + e * EXPERTS_PER_GROUP + j, gates, 0.0), axis=-1, keepdims=True)
        gu = _dot(h_ref[...], win_ref[j])
        acts.append((_silu(gu[:, :EXPERT_FF]) * gu[:, EXPERT_FF:] * gate_e).astype(BF16))
    o_ref[...] += _dot(jnp.concatenate(acts, axis=1), wout_ref[0])


def moe(x2, gain, wr_hi, wr_lo, br, win_bf, wout_bf, tm=1024):
    n, d = x2.shape
    row = pl.BlockSpec((tm, d), lambda i, e: (i, 0))
    wout_g = wout_bf.reshape(N_GROUPS, EXPERTS_PER_GROUP * EXPERT_FF, d)
    return pl.pallas_call(
        _moe_body,
        grid=(n // tm, N_GROUPS),
        in_specs=[row, _full((1, d)), _full((d, LANES)), _full((d, LANES)), _full((1, LANES)),
                  pl.BlockSpec((EXPERTS_PER_GROUP, d, 2 * EXPERT_FF), lambda i, e: (e, 0, 0)),
                  pl.BlockSpec((1, EXPERTS_PER_GROUP * EXPERT_FF, d), lambda i, e: (e, 0, 0))],
        out_specs=row,
        out_shape=jax.ShapeDtypeStruct((n, d), F32),
        scratch_shapes=[pltpu.VMEM((tm, d), BF16), pltpu.VMEM((tm, LANES), F32)],
        compiler_params=_params("parallel", "arbitrary"),
    )(x2, gain.reshape(1, d), wr_hi, wr_lo, br, win_bf, wout_g)


G_CUM, G_BETA, G_LAST = 0, GDN_HEADS, 2 * GDN_HEADS


def _gdn_gates_body(ab_ref, alog_ref, dtb_ref, gb_ref):
    t = ab_ref.shape[1]
    ab = ab_ref[0]
    sp_in = ab + dtb_ref[...]
    softplus = jnp.maximum(sp_in, 0.0) + jnp.log1p(jnp.exp(-jnp.abs(sp_in)))
    lane_row = lax.broadcasted_iota(jnp.int32, (1, LANES), 1)
    g = jnp.where(lane_row < GDN_HEADS, -jnp.exp(alog_ref[...]) * softplus, 0.0)
    blk = 4 * GDN_CHUNK
    r = lax.broadcasted_iota(jnp.int32, (blk, blk), 0)
    c = lax.broadcasted_iota(jnp.int32, (blk, blk), 1)
    same = r // GDN_CHUNK == c // GDN_CHUNK
    tri = jnp.where(same & (r >= c), 1.0, 0.0).astype(BF16)
    tot = jnp.where(same, 1.0, 0.0).astype(BF16)
    lane = lax.broadcasted_iota(jnp.int32, (blk, LANES), 1)
    for s in range(t // blk):
        rs = slice(s * blk, (s + 1) * blk)
        hi, mid, lo = _split3(g[rs])
        gc = _dot(tri, hi) + _dot(tri, mid) + _dot(tri, lo)
        gl = _dot(tot, hi) + _dot(tot, mid) + _dot(tot, lo)
        gl = pltpu.roll(gl, G_LAST, 1)
        gb_ref[0, rs, :] = jnp.where(lane < G_BETA, gc, jnp.where(lane < G_LAST, _sigmoid(ab[rs]), gl))


def gdn_gates(proj, alog_row, dtb_row):
    b, t, _ = proj.shape
    return pl.pallas_call(
        _gdn_gates_body,
        grid=(b,),
        in_specs=[pl.BlockSpec((1, t, LANES), lambda i: (i, 0, C_AB // LANES)), _full((1, LANES)), _full((1, LANES))],
        out_specs=pl.BlockSpec((1, t, LANES), lambda i: (i, 0, 0)),
        out_shape=jax.ShapeDtypeStruct((b, t, LANES), F32),
        compiler_params=_params("parallel"),
    )(proj, alog_row, dtb_row)


GDN_HEADS_PER_STEP = 2
GDN_CHUNKS_PER_TRIP = 4


def _dot3(a, b):
    a_hi, a_lo = _split2(a)
    b_hi, b_lo = _split2(b)
    return _dot(a_hi, b_hi) + _dot(a_hi, b_lo) + _dot(a_lo, b_hi)


def _dot1(a, b):
    return _dot(a.astype(BF16), b.astype(BF16))


def _conv_silu(x, cw_ref, lanes):
    rows = lax.broadcasted_iota(jnp.int32, (x.shape[0], 1), 0)
    y = x * cw_ref[CONV_WIDTH - 1:CONV_WIDTH, lanes]
    for d in range(1, CONV_WIDTH):
        shifted = jnp.where(rows >= d, pltpu.roll(x, d, 0), 0.0)
        y = y + shifted * cw_ref[CONV_WIDTH - 1 - d:CONV_WIDTH - d, lanes]
    return _silu(y)


def _l2norm(y):
    return y * lax.rsqrt(jnp.sum(y * y, axis=-1, keepdims=True) + NORM_EPS)


def _gdn_body(q_ref, k_ref, v_ref, cq_ref, ck_ref, cv_ref, gb_ref, grow_ref, o_ref,
              gl_ref, gc_ref, kb_ref, k_ref_s, kbg_ref, vb_ref, qs_ref, qg_ref, kd_ref, u_ref, w_ref, a_ref):
    t = q_ref.shape[1]
    cs = GDN_CHUNK
    dk = GDN_HEAD_DIM
    nh = GDN_HEADS_PER_STEP
    pair = pl.program_id(1)
    gb_hi, gb_mid, gb_lo = _split3(gb_ref[0])
    pick_row = lax.broadcasted_iota(jnp.int32, (LANES, LANES), 0)

    def column(idx):
        sel = jnp.where(pick_row == idx, 1.0, 0.0).astype(BF16)
        return _dot(gb_hi, sel) + _dot(gb_mid, sel) + _dot(gb_lo, sel)

    for s in range(nh):
        lanes = slice(s * dk, (s + 1) * dk)
        head = nh * pair + s
        gcol = column(G_CUM + head)
        bcol = column(G_BETA + head)
        glast = column(G_LAST + head)
        eg = jnp.exp(gcol)
        k = _l2norm(_conv_silu(k_ref[0, :, lanes], ck_ref, lanes))
        kb = k * bcol
        k_ref_s[s] = k.astype(BF16)
        kb_ref[s] = kb.astype(BF16)
        kbg_ref[s] = (kb * eg).astype(BF16)
        kd_ref[s] = (k * jnp.exp(glast - gcol)).astype(BF16)
        q = _l2norm(_conv_silu(q_ref[0, :, lanes], cq_ref, lanes)) * (dk ** -0.5)
        qs_ref[s] = q.astype(BF16)
        qg_ref[s] = (q * eg).astype(BF16)
        vb_ref[s] = (_conv_silu(v_ref[0, :, lanes], cv_ref, lanes) * bcol).astype(BF16)
        gl_ref[s] = glast
        gc_ref[s] = gcol

    r = lax.broadcasted_iota(jnp.int32, (cs, cs), 0)
    c = lax.broadcasted_iota(jnp.int32, (cs, cs), 1)
    tril = r >= c
    strict = r > c
    eye = jnp.where(r == c, 1.0, 0.0)

    def prep(trip, _):
        probs = [(s, trip * GDN_CHUNKS_PER_TRIP + j) for j in range(GDN_CHUNKS_PER_TRIP) for s in range(nh)]
        rows = [pl.ds(pl.multiple_of(n * cs, cs), cs) for _, n in probs]
        decay, lmat = [], []
        for (s, n), rw in zip(probs, rows):
            gr = grow_ref[0, s, pl.ds(n, 1), :]
            gc = gc_ref[s, rw, :cs]
            decay.append(jnp.where(tril, jnp.exp(jnp.where(tril, gc - gr, 0.0)), 0.0))
        for i, ((s, _), rw) in enumerate(zip(probs, rows)):
            lmat.append(jnp.where(strict, _dot_nt(kb_ref[s, rw, :], k_ref_s[s, rw, :]) * decay[i], 0.0))
        inv = [eye - m for m in lmat]
        pw = [_dot3(m, m) for m in lmat]
        span = 2
        while span < cs:
            mm = _dot3 if span == 2 else _dot1
            inv = [x + mm(x, p) for x, p in zip(inv, pw)]
            span *= 2
            if span < cs:
                pw = [_dot1(p, p) for p in pw]
        inv_bf = [x.astype(BF16) for x in inv]
        for i, ((s, _), rw) in enumerate(zip(probs, rows)):
            u_ref[s, rw, :] = _dot(inv_bf[i], vb_ref[s, rw, :])
            w_ref[s, rw, :] = _dot(inv_bf[i], kbg_ref[s, rw, :]).astype(BF16)
            a_ref[s, rw, :] = jnp.where(tril, _dot_nt(qs_ref[s, rw, :], k_ref_s[s, rw, :]) * decay[i], 0.0).astype(BF16)
        return 0

    lax.fori_loop(0, t // (cs * GDN_CHUNKS_PER_TRIP), prep, 0)

    def scan(n, states):
        r0 = pl.multiple_of(n * cs, cs)
        rows = pl.ds(r0, cs)
        s_bf = [st.astype(BF16) for st in states]
        v_bf = [(u_ref[s, rows, :] - _dot(w_ref[s, rows, :], s_bf[s])).astype(BF16) for s in range(nh)]
        new = [states[s] * jnp.exp(gl_ref[s, pl.ds(r0, 1), :]) + _dot_tn(kd_ref[s, rows, :], v_bf[s])
               for s in range(nh)]
        for s in range(nh):
            o_ref[0, rows, s * dk:(s + 1) * dk] = _dot(qg_ref[s, rows, :], s_bf[s]) + _dot(a_ref[s, rows, :], v_bf[s])
        return tuple(new)

    lax.fori_loop(0, t // cs, scan, tuple(jnp.zeros((dk, dk), F32) for _ in range(nh)))


def gdn_core(proj, conv_w, gb):
    b, t, _ = proj.shape
    n_chunks = t // GDN_CHUNK
    nh = GDN_HEADS_PER_STEP
    wide = nh * GDN_HEAD_DIM
    per = GDN_WIDTH // wide
    g_rows = jnp.swapaxes(gb[:, :, G_CUM:G_CUM + GDN_HEADS], 1, 2).reshape(b, GDN_HEADS, n_chunks, GDN_CHUNK)
    sect = lambda k: pl.BlockSpec((1, t, wide), lambda i, h: (i, 0, k * per + h))
    taps = lambda k: pl.BlockSpec((CONV_WIDTH, wide), lambda i, h: (0, k * per + h))
    bf = lambda w: pltpu.VMEM((nh, t, w), BF16)
    return pl.pallas_call(
        _gdn_body,
        grid=(b, per),
        in_specs=[sect(0), sect(1), sect(2), taps(0), taps(1), taps(2),
                  pl.BlockSpec((1, t, LANES), lambda i, h: (i, 0, 0)),
                  pl.BlockSpec((1, nh, n_chunks, GDN_CHUNK), lambda i, h: (i, h, 0, 0))],
        out_specs=pl.BlockSpec((1, t, wide), lambda i, h: (i, 0, h)),
        out_shape=jax.ShapeDtypeStruct((b, t, GDN_WIDTH), F32),
        scratch_shapes=[pltpu.VMEM((nh, t, LANES), F32)] * 2 + [bf(GDN_HEAD_DIM)] * 7
                       + [pltpu.VMEM((nh, t, GDN_HEAD_DIM), F32), bf(GDN_HEAD_DIM), bf(GDN_CHUNK)],
        compiler_params=_params("parallel", "arbitrary"),
    )(proj, proj, proj, conv_w, conv_w, conv_w, gb, g_rows)


def _rope_tables(pos):
    half = HEAD_DIM // 2
    inv_freq = ROPE_THETA ** (-jnp.arange(half, dtype=F32) / half)
    ang = pos.astype(F32)[:, None] * inv_freq
    cos = jnp.cos(ang)
    sin = jnp.sin(ang)
    cos_t = jnp.tile(jnp.concatenate([cos, cos], axis=-1), (1, LANES // HEAD_DIM))
    sin_t = jnp.tile(jnp.concatenate([-sin, sin], axis=-1), (1, LANES // HEAD_DIM))
    return cos_t, sin_t


def _block_diag_ones(width, seg):
    idx = np.arange(width) // seg
    return jnp.asarray((idx[:, None] == idx[None, :]).astype(np.float32), dtype=BF16)


def _pad_cols(w, width):
    return jnp.pad(w, ((0, 0), (0, width - w.shape[1])))


def _even_layer(x2, b, t, norm_gain, w_in, b_gate, b_forget, cmp_pe, cmp_w1, cmp_w2, nsa_gain, fox_gain, w_out):
    d = x2.shape[1]
    q_perm = np.concatenate([np.arange(HEAD_DIM) + (kvh * NSA_GROUP + g) * HEAD_DIM
                             for g in range(NSA_GROUP) for kvh in range(NSA_KV_HEADS)])
    o_gate = NSA_Q_W + 6 * NSA_KV_W
    o_fox = o_gate + NSA_GATE_W
    w_re = jnp.concatenate([w_in[:, q_perm], w_in[:, NSA_Q_W:o_gate], w_in[:, o_fox:o_fox + 3 * FOX_W],
                            w_in[:, o_gate:o_fox], w_in[:, o_fox + 3 * FOX_W:]], axis=1)
    proj = norm_matmul(x2, norm_gain, _pad_cols(w_re, EVEN_W).astype(BF16)).reshape(b, t, EVEN_W)

    cos, sin = _rope_tables(jnp.arange(t))
    tile = lambda g, n: jnp.tile(g, n).reshape(1, -1)
    bias = jnp.pad(jnp.concatenate([b_gate, b_forget]), (0, LANES - NSA_GATE_W - FOX_HEADS)).reshape(1, LANES)
    bd = _block_diag_ones(FOX_W, HEAD_DIM)
    (qa, ks, kw, vs, vw, kc_raw, vc_raw, qb, kb, vf, gates, cum) = even_prep(
        proj, cos, sin, tile(nsa_gain[0], NSA_HEADS), tile(nsa_gain[2], NSA_KV_HEADS), tile(nsa_gain[3], NSA_KV_HEADS),
        tile(fox_gain[0], FOX_HEADS), tile(fox_gain[1], FOX_HEADS), bias, bd)

    n_str = t // CMP_STRIDE
    half = CMP_BLOCK // 2
    eye2 = jnp.eye(NSA_KV_HEADS, dtype=F32)
    pe = jnp.tile(cmp_pe[:, :, None, :], (1, 1, NSA_KV_HEADS, 1)).reshape(2, 2, 1, half * NSA_KV_W)
    w1 = jnp.einsum('ilde,hg->ilhdge', cmp_w1, eye2).reshape(2, 2, half * NSA_KV_W, NSA_KV_W).astype(BF16)
    w2 = jnp.einsum('ide,hg->ihdge', cmp_w2, eye2).reshape(2, NSA_KV_W, NSA_KV_W).astype(BF16)
    cos_c, sin_c = _rope_tables(jnp.arange(n_str) * CMP_STRIDE + (CMP_BLOCK - 1))
    kc, vc = compress(kc_raw.reshape(b, n_str, CMP_STRIDE * NSA_KV_W), vc_raw.reshape(b, n_str, CMP_STRIDE * NSA_KV_W),
                      pe, w1, w2, tile(nsa_gain[1], NSA_KV_HEADS), cos_c, sin_c, _block_diag_ones(LANES, HEAD_DIM))

    n_sel = t // SEL_BLOCK
    cs = np.arange(n_str)[:, None] * CMP_STRIDE
    ss = np.arange(n_sel)[None, :] * SEL_BLOCK
    overlap = np.clip(np.minimum(cs + CMP_BLOCK, ss + SEL_BLOCK) - np.maximum(cs, ss), 0, None) / CMP_BLOCK
    overlap[(t - CMP_BLOCK) // CMP_STRIDE + 1:] = 0.0
    ovt = jnp.asarray(overlap.T.astype(np.float32), dtype=BF16)
    o_a = nsa_attention(qa, kc, vc, ks, vs, kw, vw, gates, ovt)
    o_b = fox_attention(qb, kb, vf, cum)

    wa = w_out[:NSA_Q_W][q_perm].astype(BF16)
    wb = w_out[NSA_Q_W:].astype(BF16)
    return even_out(x2, o_a.reshape(b * t, NSA_Q_W), o_b.reshape(b * t, FOX_W), wa, wb)


def _odd_layer(x2, b, t, norm_gain, w_in, conv_w, a_log, dt_bias, gdn_gain, w_out):
    proj = norm_matmul(x2, norm_gain, _pad_cols(w_in, ODD_W).astype(BF16)).reshape(b, t, ODD_W)
    pad8 = lambda v: jnp.pad(v, (0, LANES - GDN_HEADS)).reshape(1, LANES)
    gb = gdn_gates(proj, pad8(a_log), pad8(dt_bias))
    o = gdn_core(proj, conv_w, gb)
    return odd_out(x2, o.reshape(b * t, GDN_WIDTH), proj.reshape(b * t, ODD_W), gdn_gain.reshape(1, GDN_HEAD_DIM),
                   w_out.astype(BF16))


def _moe_layer(x2, gain, w_rg, b_rg, w_re, b_re, w_ein, w_eout):
    d = x2.shape[1]
    wr = _pad_cols(jnp.concatenate([w_rg, w_re], axis=1), LANES)
    wr_hi = wr.astype(BF16)
    wr_lo = (wr - wr_hi.astype(F32)).astype(BF16)
    br = jnp.pad(jnp.concatenate([b_rg, b_re]), (0, LANES - N_GROUPS - N_EXPERTS)).reshape(1, LANES)
    return moe(x2, gain, wr_hi, wr_lo, br, w_ein.astype(BF16), w_eout.astype(BF16))


def kernel(x, norm_mix, norm_ffn, w_in_even, b_nsa_gate, b_forget, cmp_pe, cmp_w1, cmp_w2, nsa_qk_gain, fox_qk_gain,
           w_out_even, w_in_odd, conv_w, a_log, dt_bias, gdn_norm_gain, w_out_odd, w_router_group, b_router_group,
           w_router_expert, b_router_expert, w_expert_in, w_expert_out):
    b, t, d = x.shape
    x2 = x.reshape(b * t, d)
    for layer in range(norm_mix.shape[0]):
        i = layer // 2
        if layer % 2 == 0:
            x2 = _even_layer(x2, b, t, norm_mix[layer], w_in_even[i], b_nsa_gate[i], b_forget[i], cmp_pe[i], cmp_w1[i],
                             cmp_w2[i], nsa_qk_gain[i], fox_qk_gain[i], w_out_even[i])
        else:
            x2 = _odd_layer(x2, b, t, norm_mix[layer], w_in_odd[i], conv_w[i], a_log[i], dt_bias[i], gdn_norm_gain[i],
                            w_out_odd[i])
        x2 = _moe_layer(x2, norm_ffn[layer], w_router_group[layer], b_router_group[layer], w_router_expert[layer],
                        b_router_expert[layer], w_expert_in[layer], w_expert_out[layer])
    return x2.reshape(b, t, d)
```

```python
import functools

import numpy as np
import jax
import jax.numpy as jnp
from jax import lax
from jax.experimental import pallas as pl
from jax.experimental.pallas import tpu as pltpu

F32 = jnp.float32
BF16 = jnp.bfloat16

HEAD_DIM = 64
ROPE_THETA = 10000.0
NSA_HEADS = 8
NSA_KV_HEADS = 2
NSA_GROUP = NSA_HEADS // NSA_KV_HEADS
CMP_BLOCK = 32
CMP_STRIDE = 16
SEL_BLOCK = 64
SEL_TOPK = 8
WINDOW = 256
FOX_HEADS = 8
GDN_HEADS = 8
GDN_HEAD_DIM = 128
GDN_WIDTH = GDN_HEADS * GDN_HEAD_DIM
CONV_WIDTH = 4
GDN_CHUNK = 64
N_GROUPS = 4
EXPERTS_PER_GROUP = 4
N_EXPERTS = N_GROUPS * EXPERTS_PER_GROUP
EXPERT_FF = 256
NORM_EPS = 1e-6
NEG_INF = -1e30
FORCE_SCORE = 1e9

LANES = 128
NSA_Q_W = NSA_HEADS * HEAD_DIM
NSA_KV_W = NSA_KV_HEADS * HEAD_DIM
NSA_GATE_W = 3 * NSA_HEADS
FOX_W = FOX_HEADS * HEAD_DIM
C_QN = 0
C_KC, C_VC, C_KS, C_VS, C_KW, C_VW = (NSA_Q_W + i * NSA_KV_W for i in range(6))
C_QF = NSA_Q_W + 6 * NSA_KV_W
C_KF = C_QF + FOX_W
C_VF = C_KF + FOX_W
C_MISC = C_VF + FOX_W
EVEN_W = C_MISC + LANES
MISC_F = NSA_GATE_W
C_AB = 4 * GDN_WIDTH
ODD_W = C_AB + LANES

VMEM_LIMIT = 56 * 1024 * 1024


def _params(*sem):
    return pltpu.CompilerParams(dimension_semantics=sem, vmem_limit_bytes=VMEM_LIMIT)


def _dot(a, b):
    return jnp.dot(a, b, preferred_element_type=F32)


def _dot_nt(a, b):
    return lax.dot_general(a, b, (((1,), (1,)), ((), ())), preferred_element_type=F32)


def _dot_tn(a, b):
    return lax.dot_general(a, b, (((0,), (0,)), ((), ())), preferred_element_type=F32)


def _split2(x):
    hi = x.astype(BF16)
    return hi, (x - hi.astype(F32)).astype(BF16)


def _split3(x):
    hi = x.astype(BF16)
    r = x - hi.astype(F32)
    mid = r.astype(BF16)
    return hi, mid, (r - mid.astype(F32)).astype(BF16)


def _sigmoid(z):
    return 1.0 / (1.0 + jnp.exp(-z))


def _silu(z):
    return z * _sigmoid(z)


def _full(shape):
    nd = len(shape)
    return pl.BlockSpec(shape, lambda *_: (0,) * nd)


def _norm_matmul_body(x_ref, g_ref, w_ref, o_ref):
    x = x_ref[...]
    ms = jnp.mean(x * x, axis=-1, keepdims=True)
    h = (x * lax.rsqrt(ms + NORM_EPS) * g_ref[...]).astype(BF16)
    o_ref[...] = _dot(h, w_ref[...])


def norm_matmul(x2, gain, w_bf, tm=512):
    n, d = x2.shape
    wp = w_bf.shape[1]
    return pl.pallas_call(
        _norm_matmul_body,
        grid=(n // tm,),
        in_specs=[pl.BlockSpec((tm, d), lambda i: (i, 0)), _full((1, d)), _full((d, wp))],
        out_specs=pl.BlockSpec((tm, wp), lambda i: (i, 0)),
        out_shape=jax.ShapeDtypeStruct((n, wp), F32),
        compiler_params=_params("parallel"),
    )(x2, gain.reshape(1, d), w_bf)


def _head_rms(x, bd, gain):
    hi, lo = _split2(x * x)
    w = x.shape[1]
    ssum = _dot(hi, bd[:w, :w]) + _dot(lo, bd[:w, :w])
    return x * lax.rsqrt(ssum * (1.0 / HEAD_DIM) + NORM_EPS) * gain


def _rope(x, cos, sin_signed, first_half):
    fwd = pltpu.roll(x, LANES - HEAD_DIM // 2, 1)
    bwd = pltpu.roll(x, HEAD_DIM // 2, 1)
    return x * cos + jnp.where(first_half, fwd, bwd) * sin_signed


def _even_prep_body(p_ref, cos_ref, sin_ref, gq_ref, gks_ref, gkw_ref, gfq_ref, gfk_ref, bias_ref, bd_ref,
                    qa_ref, ks_ref, kw_ref, vs_ref, vw_ref, kc_ref, vc_ref, qb_ref, kb_ref, vf_ref,
                    gate_ref, cum_ref, carry_ref):
    tr = p_ref.shape[1]
    bd = bd_ref[...]
    cos = cos_ref[...]
    sin = sin_ref[...]
    lane = lax.broadcasted_iota(jnp.int32, (1, LANES), 1)
    first_half = (lane % HEAD_DIM) < (HEAD_DIM // 2)
    scale = HEAD_DIM ** -0.5

    qn = _head_rms(p_ref[0, :, C_QN:C_QN + NSA_Q_W], bd, gq_ref[...])
    for c in range(NSA_Q_W // LANES):
        sl = slice(c * LANES, (c + 1) * LANES)
        qa_ref[0, sl, :] = (_rope(qn[:, sl], cos, sin, first_half) * scale).T.astype(BF16)
    ks = _head_rms(p_ref[0, :, C_KS:C_KS + NSA_KV_W], bd, gks_ref[...])
    ks_ref[0] = _rope(ks, cos, sin, first_half).astype(BF16)
    kw = _head_rms(p_ref[0, :, C_KW:C_KW + NSA_KV_W], bd, gkw_ref[...])
    kw_ref[0] = _rope(kw, cos, sin, first_half).astype(BF16)
    vs_ref[0] = p_ref[0, :, C_VS:C_VS + NSA_KV_W].T.astype(BF16)
    vw_ref[0] = p_ref[0, :, C_VW:C_VW + NSA_KV_W].T.astype(BF16)
    kc_ref[0] = p_ref[0, :, C_KC:C_KC + NSA_KV_W]
    vc_ref[0] = p_ref[0, :, C_VC:C_VC + NSA_KV_W]

    qb = _head_rms(p_ref[0, :, C_QF:C_QF + FOX_W], bd, gfq_ref[...]) * scale
    kb_ref[0] = _head_rms(p_ref[0, :, C_KF:C_KF + FOX_W], bd, gfk_ref[...]).astype(BF16)
    for c in range(FOX_W // LANES):
        sl = slice(c * LANES, (c + 1) * LANES)
        qb_ref[0, sl, :] = qb[:, sl].T.astype(BF16)
        vf_ref[0, sl, :] = p_ref[0, :, C_VF + c * LANES:C_VF + (c + 1) * LANES].T.astype(BF16)

    z = p_ref[0, :, C_MISC:C_MISC + LANES] + bias_ref[...]
    gate_ref[0] = _sigmoid(z).T
    logf = jnp.minimum(z, 0.0) - jnp.log1p(jnp.exp(-jnp.abs(z)))

    @pl.when(pl.program_id(1) == 0)
    def _():
        carry_ref[...] = jnp.zeros_like(carry_ref)

    row = lax.broadcasted_iota(jnp.int32, (tr, tr), 0)
    col = lax.broadcasted_iota(jnp.int32, (tr, tr), 1)
    tril = jnp.where(row >= col, 1.0, 0.0).astype(BF16)
    hi, mid, lo = _split3(logf)
    cum = _dot(tril, hi) + _dot(tril, mid) + _dot(tril, lo) + carry_ref[...]
    cum_ref[0] = cum
    carry_ref[...] = cum[tr - 1:tr, :]


def even_prep(proj, cos, sin, gq, gks, gkw, gfq, gfk, bias, bd, tr=256):
    b, t, _ = proj.shape
    row = lambda w: pl.BlockSpec((1, tr, w), lambda i, j: (i, j, 0))
    tab = pl.BlockSpec((tr, LANES), lambda i, j: (j, 0))
    shp = lambda w, dt: jax.ShapeDtypeStruct((b, t, w), dt)
    col = lambda w: pl.BlockSpec((1, w, tr), lambda i, j: (i, 0, j))
    shp_t = lambda w, dt: jax.ShapeDtypeStruct((b, w, t), dt)
    return pl.pallas_call(
        _even_prep_body,
        grid=(b, t // tr),
        in_specs=[row(EVEN_W), tab, tab, _full((1, NSA_Q_W)), _full((1, LANES)), _full((1, LANES)),
                  _full((1, FOX_W)), _full((1, FOX_W)), _full((1, LANES)), _full((FOX_W, FOX_W))],
        out_specs=[col(NSA_Q_W), row(LANES), row(LANES), col(LANES), col(LANES), row(LANES), row(LANES),
                   col(FOX_W), row(FOX_W), col(FOX_W), col(LANES), row(LANES)],
        out_shape=[shp_t(NSA_Q_W, BF16), shp(LANES, BF16), shp(LANES, BF16), shp_t(LANES, BF16), shp_t(LANES, BF16),
                   shp(LANES, F32), shp(LANES, F32), shp_t(FOX_W, BF16), shp(FOX_W, BF16), shp_t(FOX_W, BF16),
                   shp_t(LANES, F32), shp(LANES, F32)],
        scratch_shapes=[pltpu.VMEM((1, LANES), F32)],
        compiler_params=_params("parallel", "arbitrary"),
    )(proj, cos, sin, gq, gks, gkw, gfq, gfk, bias, bd)


def _gelu_tanh(x):
    return 0.5 * x * (1.0 + jnp.tanh(np.sqrt(2.0 / np.pi).astype(np.float32) * (x + 0.044715 * (x * x * x))))


def _compress_body(xk_ref, xv_ref, pe_ref, w1_ref, w2_ref, gk_ref, cos_ref, sin_ref, bd_ref, kc_ref, vc_ref):
    n = xk_ref.shape[1]
    lane = lax.broadcasted_iota(jnp.int32, (1, LANES), 1)
    first_half = (lane % HEAD_DIM) < (HEAD_DIM // 2)

    def mlp(x_ref, i):
        x = x_ref[0]
        nxt = pltpu.roll(x, n - 1, 0)
        xa = (x + pe_ref[i, 0]).astype(BF16)
        xb = (nxt + pe_ref[i, 1]).astype(BF16)
        h = _dot(xa, w1_ref[i, 0]) + _dot(xb, w1_ref[i, 1])
        return _dot(_gelu_tanh(h).astype(BF16), w2_ref[i])

    kc = _head_rms(mlp(xk_ref, 0), bd_ref[...], gk_ref[...])
    kc_ref[0] = _rope(kc, cos_ref[...], sin_ref[...], first_half).astype(BF16)
    vc_ref[0] = mlp(xv_ref, 1).T.astype(BF16)


def compress(xk, xv, pe, w1, w2, gk, cos_c, sin_c, bd):
    b, n, w = xk.shape
    blk = pl.BlockSpec((1, n, w), lambda i: (i, 0, 0))
    out = pl.BlockSpec((1, n, LANES), lambda i: (i, 0, 0))
    return pl.pallas_call(
        _compress_body,
        grid=(b,),
        in_specs=[blk, blk, _full(pe.shape), _full(w1.shape), _full(w2.shape), _full((1, LANES)),
                  _full((n, LANES)), _full((n, LANES)), _full((LANES, LANES))],
        out_specs=[out, pl.BlockSpec((1, LANES, n), lambda i: (i, 0, 0))],
        out_shape=[jax.ShapeDtypeStruct((b, n, LANES), BF16), jax.ShapeDtypeStruct((b, LANES, n), BF16)],
        compiler_params=_params("parallel"),
    )(xk, xv, pe, w1, w2, gk, cos_c, sin_c, bd)


def _flash_step(s_ref, p_ref, acc_ref, v_blk, m_i, l_i, adjust, first=False):
    al, ms, ls = [], [], []
    for cg in range(s_ref.shape[1] // LANES):
        sl = slice(cg * LANES, (cg + 1) * LANES)
        s = adjust(s_ref[:, sl], cg)
        m_new = jnp.maximum(m_i[:, sl], jnp.max(s, axis=0, keepdims=True))
        p = jnp.exp(s - m_new)
        alpha = jnp.exp(m_i[:, sl] - m_new)
        p_ref[:, sl] = p.astype(BF16)
        al.append(alpha)
        ms.append(m_new)
        ls.append(alpha * l_i[:, sl] + jnp.sum(p, axis=0, keepdims=True))
    cat = lambda xs: jnp.concatenate(xs, axis=1)
    pv = _dot(v_blk, p_ref[...])
    acc_ref[...] = pv if first else cat(al) * acc_ref[...] + pv
    return cat(ms), cat(ls)


NSA_KEYS_PER_QUERY_BLOCK = 2


def _nsa_body(q_ref, kc_ref, vc_ref, ks_ref, vs_ref, kw_ref, vw_ref, gate_ref, ovt_ref, o_ref,
              sel_ref, s0_ref, s1_ref, p_ref, acc_ref, *, k_top):
    tq = q_ref.shape[2]
    t_all = ks_ref.shape[1]
    n_cmp = kc_ref.shape[1]
    n_sel = ovt_ref.shape[0]
    g_n = NSA_GROUP
    c = pl.program_id(1)
    t0 = c * tq
    chan = lax.broadcasted_iota(jnp.int32, (LANES, 1), 0)
    tlane = t0 + lax.broadcasted_iota(jnp.int32, (1, tq), 1)
    gates = gate_ref[0]

    nrow = lax.broadcasted_iota(jnp.int32, (n_cmp, 1), 0)
    valid_c = (nrow * CMP_STRIDE + (CMP_BLOCK - 1)) <= tlane
    jrow = lax.broadcasted_iota(jnp.int32, (n_sel, tq), 0)
    jrow_f = jrow.astype(F32)
    cur = tlane // SEL_BLOCK
    forced = (jrow == 0) | (jrow == cur) | (jrow == cur - 1)
    future = jrow * SEL_BLOCK > tlane
    tk = NSA_KEYS_PER_QUERY_BLOCK * tq
    krow = lax.broadcasted_iota(jnp.int32, (tk, 1), 0)
    per_blk = tk // SEL_BLOCK
    w_len = tq + WINDOW
    w_start = pl.multiple_of(jnp.clip(t0 - WINDOW, 0, t_all - w_len), LANES)
    wrow = w_start + lax.broadcasted_iota(jnp.int32, (w_len, 1), 0)
    valid_w = (wrow <= tlane) & (wrow > tlane - WINDOW)

    heads = [(kvh, g) for kvh in range(NSA_KV_HEADS) for g in range(g_n)]
    qst = jnp.concatenate([jnp.where((chan // HEAD_DIM) == kvh, q_ref[0, g * LANES:(g + 1) * LANES, :], 0)
                           for kvh, g in heads], axis=1)
    n_col = len(heads) * tq

    def softmax_cols(s, ok, guard):
        outs = []
        for cg in range(len(heads)):
            sc = jnp.where(ok, s[:, cg * tq:(cg + 1) * tq], NEG_INF)
            e = jnp.exp(sc - jnp.max(sc, axis=0, keepdims=True))
            if guard:
                e = jnp.where(ok, e, 0.0)
            den = jnp.sum(e, axis=0, keepdims=True)
            outs.append(e * (1.0 / (jnp.where(den > 0.0, den, 1.0) if guard else den)))
        return outs

    p_c = softmax_cols(_dot(kc_ref[0], qst), valid_c, guard=True)
    o_cmp = _dot(vc_ref[0], jnp.concatenate(p_c, axis=1).astype(BF16))

    for kvh in range(NSA_KV_HEADS):
        p_sum = p_c[kvh * g_n]
        for g in range(1, g_n):
            p_sum = p_sum + p_c[kvh * g_n + g]
        p_hi, p_lo = _split2(p_sum)
        imp_t = _dot(ovt_ref[...], p_hi) + _dot(ovt_ref[...], p_lo)
        val = jnp.where(forced, FORCE_SCORE, jnp.where(future, NEG_INF, imp_t))
        sel_t = jnp.zeros((n_sel, tq), F32)
        for _ in range(k_top):
            m = jnp.max(val, axis=0, keepdims=True)
            first = jnp.min(jnp.where(val == m, jrow_f, float(n_sel)), axis=0, keepdims=True)
            pick = jrow_f == first
            sel_t = jnp.where(pick, 1.0, sel_t)
            val = jnp.where(pick, -jnp.inf, val)
        sel_ref[kvh] = sel_t

    p_w = softmax_cols(_dot(kw_ref[0, pl.ds(w_start, w_len), :], qst), valid_w, guard=False)
    o_win = _dot(vw_ref[0, :, pl.ds(w_start, w_len)], jnp.concatenate(p_w, axis=1).astype(BF16))

    def put_scores(buf, kb):
        k0 = pl.multiple_of(jnp.minimum(kb * tk, t_all - tk), tk)
        buf[...] = _dot(ks_ref[0, pl.ds(k0, tk), :], qst)

    def half_step(buf, kb, m_i, l_i):
        k0 = pl.multiple_of(jnp.minimum(kb * tk, t_all - tk), tk)
        causal = (kb * tk + krow) <= tlane
        ok = [causal & (jnp.concatenate([jnp.broadcast_to(sel_ref[kvh, pl.ds(k0 // SEL_BLOCK + r, 1), :],
                                                          (SEL_BLOCK, tq)) for r in range(per_blk)], axis=0) > 0.5)
              for kvh in range(NSA_KV_HEADS)]
        adjust = lambda s_cols, cg: jnp.where(ok[cg // g_n], s_cols, NEG_INF)
        return _flash_step(buf, p_ref, acc_ref, vs_ref[0, :, pl.ds(k0, tk)], m_i, l_i, adjust)

    def sel_trip(j, carry):
        put_scores(s1_ref, 2 * j + 1)
        carry = half_step(s0_ref, 2 * j, *carry)
        put_scores(s0_ref, 2 * j + 2)
        return half_step(s1_ref, 2 * j + 1, *carry)

    put_scores(s0_ref, 0)
    acc_ref[...] = jnp.zeros_like(acc_ref)
    init = (jnp.full((1, n_col), NEG_INF, F32), jnp.zeros((1, n_col), F32))
    n_blocks = (t0 + tq + tk - 1) // tk
    _, l_s = lax.fori_loop(0, (n_blocks + 1) // 2, sel_trip, init)
    o_slc = acc_ref[...] * (1.0 / l_s)

    first_head = chan < HEAD_DIM
    for g in range(g_n):
        per_kv = []
        for kvh in range(NSA_KV_HEADS):
            row = (kvh * g_n + g) * 3
            cols = slice((kvh * g_n + g) * tq, (kvh * g_n + g + 1) * tq)
            per_kv.append(gates[row:row + 1] * o_cmp[:, cols] + gates[row + 1:row + 2] * o_slc[:, cols]
                          + gates[row + 2:row + 3] * o_win[:, cols])
        o_ref[0, :, g * LANES:(g + 1) * LANES] = jnp.where(first_head, per_kv[0], per_kv[1]).T.astype(BF16)


def nsa_attention(qa_t, kc, vc_t, ks, vs_t, kw, vw_t, gates_t, ovt, tq=LANES):
    b, _, t = qa_t.shape
    n_cmp = kc.shape[1]
    n_sel = ovt.shape[0]
    k_top = min(SEL_TOPK, n_sel)
    tk = NSA_KEYS_PER_QUERY_BLOCK * tq
    tok = lambda n: pl.BlockSpec((1, n, LANES), lambda i, j: (i, 0, 0))
    chn = lambda n: pl.BlockSpec((1, LANES, n), lambda i, j: (i, 0, 0))
    return pl.pallas_call(
        functools.partial(_nsa_body, k_top=k_top),
        grid=(b, t // tq),
        in_specs=[pl.BlockSpec((1, NSA_Q_W, tq), lambda i, j: (i, 0, j)), tok(n_cmp), chn(n_cmp), tok(t), chn(t),
                  tok(t), chn(t), pl.BlockSpec((1, LANES, tq), lambda i, j: (i, 0, j)), _full(ovt.shape)],
        out_specs=pl.BlockSpec((1, tq, NSA_Q_W), lambda i, j: (i, j, 0)),
        out_shape=jax.ShapeDtypeStruct((b, t, NSA_Q_W), BF16),
        scratch_shapes=[pltpu.VMEM((NSA_KV_HEADS, n_sel, tq), F32), pltpu.VMEM((tk, NSA_HEADS * tq), F32),
                        pltpu.VMEM((tk, NSA_HEADS * tq), F32), pltpu.VMEM((tk, NSA_HEADS * tq), BF16),
                        pltpu.VMEM((LANES, NSA_HEADS * tq), F32)],
        compiler_params=_params("parallel", "arbitrary"),
    )(qa_t, kc, vc_t, ks, vs_t, kw, vw_t, gates_t, ovt)


FOX_KEYS_PER_QUERY_BLOCK = 2


def _fox_body(q_ref, k_ref, v_ref, cum_ref, o_ref, ck_ref, s0_ref, s1_ref, p_ref, acc_ref, *, tq):
    t = k_ref.shape[1]
    tk = FOX_KEYS_PER_QUERY_BLOCK * tq
    pair = pl.program_id(1)

    hi, mid, lo = _split3(cum_ref[0])
    pick_row = lax.broadcasted_iota(jnp.int32, (LANES, LANES), 0)
    for h in range(2):
        sel = jnp.where(pick_row == MISC_F + 2 * pair + h, 1.0, 0.0).astype(BF16)
        ck_ref[h] = _dot(hi, sel) + _dot(mid, sel) + _dot(lo, sel)

    chan = lax.broadcasted_iota(jnp.int32, (LANES, 1), 0)
    first_head = chan < HEAD_DIM
    krow = lax.broadcasted_iota(jnp.int32, (tk, 1), 0)
    qlane = lax.broadcasted_iota(jnp.int32, (1, tq), 1)
    reps = tq // LANES
    bufs = (s0_ref, s1_ref)
    blocks = [(i, kb) for i in range(t // tq) for kb in range((i * tq) // tk + 1)]
    q_cache = {}

    def q_pair(i):
        if i not in q_cache:
            q = q_ref[0, :, i * tq:(i + 1) * tq]
            q_cache[i] = jnp.concatenate([jnp.where(first_head, q, 0), jnp.where(first_head, 0, q)], axis=1)
        return q_cache[i]

    def put_scores(n):
        i, kb = blocks[n]
        bufs[n % 2][...] = _dot(k_ref[0, kb * tk:(kb + 1) * tk, :], q_pair(i))

    put_scores(0)
    m_i = l_i = None
    for n, (i, kb) in enumerate(blocks):
        if n + 1 < len(blocks):
            put_scores(n + 1)
        last = kb == (i * tq) // tk
        ok = ((kb * tk + krow) <= (i * tq + qlane)) if last else None

        def adjust(s_cols, cg, kb=kb, last=last, ok=ok):
            s_cols = s_cols - ck_ref[cg // reps, kb * tk:(kb + 1) * tk, :]
            return jnp.where(ok[:, (cg % reps) * LANES:(cg % reps + 1) * LANES], s_cols, NEG_INF) if last else s_cols

        if kb == 0:
            m_i = jnp.full((1, 2 * tq), NEG_INF, F32)
            l_i = jnp.zeros((1, 2 * tq), F32)
        m_i, l_i = _flash_step(bufs[n % 2], p_ref, acc_ref, v_ref[0, :, kb * tk:(kb + 1) * tk], m_i, l_i, adjust,
                               first=kb == 0)
        if last:
            o = acc_ref[...] * (1.0 / l_i)
            o_ref[0, i * tq:(i + 1) * tq, :] = jnp.where(first_head, o[:, :tq], o[:, tq:]).T.astype(BF16)


def fox_attention(qb_t, kb, vf_t, cum, tq=256):
    b, w, t = qb_t.shape
    pairs = w // LANES
    tk = FOX_KEYS_PER_QUERY_BLOCK * tq
    return pl.pallas_call(
        functools.partial(_fox_body, tq=tq),
        grid=(b, pairs),
        in_specs=[pl.BlockSpec((1, LANES, t), lambda i, p: (i, p, 0)),
                  pl.BlockSpec((1, t, LANES), lambda i, p: (i, 0, p)),
                  pl.BlockSpec((1, LANES, t), lambda i, p: (i, p, 0)),
                  pl.BlockSpec((1, t, LANES), lambda i, p: (i, 0, 0))],
        out_specs=pl.BlockSpec((1, t, LANES), lambda i, p: (i, 0, p)),
        out_shape=jax.ShapeDtypeStruct((b, t, w), BF16),
        scratch_shapes=[pltpu.VMEM((2, t, LANES), F32), pltpu.VMEM((tk, 2 * tq), F32), pltpu.VMEM((tk, 2 * tq), F32),
                        pltpu.VMEM((tk, 2 * tq), BF16), pltpu.VMEM((LANES, 2 * tq), F32)],
        compiler_params=_params("parallel", "arbitrary"),
    )(qb_t, kb, vf_t, cum)


def _even_out_body(x_ref, oa_ref, ob_ref, wa_ref, wb_ref, o_ref):
    o_ref[...] = x_ref[...] + _dot(oa_ref[...], wa_ref[...]) + _dot(ob_ref[...], wb_ref[...])


def even_out(x2, oa, ob, wa, wb, tm=512):
    n, d = x2.shape
    row = lambda w: pl.BlockSpec((tm, w), lambda i: (i, 0))
    return pl.pallas_call(
        _even_out_body,
        grid=(n // tm,),
        in_specs=[row(d), row(oa.shape[1]), row(ob.shape[1]), _full(wa.shape), _full(wb.shape)],
        out_specs=row(d),
        out_shape=jax.ShapeDtypeStruct((n, d), F32),
        compiler_params=_params("parallel"),
    )(x2, oa, ob, wa, wb)


def _odd_out_body(x_ref, o_ref_in, z_ref, g_ref, w_ref, out_ref):
    gain = g_ref[...]
    parts = []
    for h in range(GDN_HEADS):
        sl = slice(h * GDN_HEAD_DIM, (h + 1) * GDN_HEAD_DIM)
        o = o_ref_in[:, sl]
        y = o * lax.rsqrt(jnp.mean(o * o, axis=-1, keepdims=True) + NORM_EPS) * gain
        parts.append((y * _silu(z_ref[:, sl])).astype(BF16))
    out_ref[...] = x_ref[...] + _dot(jnp.concatenate(parts, axis=1), w_ref[...])


def odd_out(x2, o2, proj, gain, w_bf, tm=512):
    n, d = x2.shape
    row = lambda w: pl.BlockSpec((tm, w), lambda i: (i, 0))
    z_col = 3 * GDN_WIDTH // GDN_WIDTH
    return pl.pallas_call(
        _odd_out_body,
        grid=(n // tm,),
        in_specs=[row(d), row(GDN_WIDTH), pl.BlockSpec((tm, GDN_WIDTH), lambda i: (i, z_col)),
                  _full((1, GDN_HEAD_DIM)), _full(w_bf.shape)],
        out_specs=row(d),
        out_shape=jax.ShapeDtypeStruct((n, d), F32),
        compiler_params=_params("parallel"),
    )(x2, o2, proj, gain, w_bf)


R_GROUP = 0
R_EXPERT = N_GROUPS


def _moe_body(x_ref, g_ref, wr_hi_ref, wr_lo_ref, br_ref, win_ref, wout_ref, o_ref, h_ref, gate_ref):
    e = pl.program_id(1)

    @pl.when(e == 0)
    def _():
        x = x_ref[...]
        h = x * lax.rsqrt(jnp.mean(x * x, axis=-1, keepdims=True) + NORM_EPS) * g_ref[...]
        h_ref[...] = h.astype(BF16)
        h_hi, h_lo = _split2(h)
        logit = _dot(h_hi, wr_hi_ref[...]) + _dot(h_lo, wr_hi_ref[...]) + _dot(h_hi, wr_lo_ref[...]) + br_ref[...]
        lane_i = lax.broadcasted_iota(jnp.int32, logit.shape, 1)
        lane = lane_i.astype(F32)
        is_g = lane_i < N_GROUPS
        g_max = jnp.max(jnp.where(is_g, logit, -jnp.inf), axis=-1, keepdims=True)
        g_sel = jnp.min(jnp.where(is_g & (logit == g_max), lane, float(LANES)), axis=-1, keepdims=True)
        p_group = 1.0 / jnp.sum(jnp.where(is_g, jnp.exp(logit - g_max), 0.0), axis=-1, keepdims=True)
        group_of = ((lane_i - R_EXPERT) // EXPERTS_PER_GROUP).astype(F32)
        mine = (lane_i >= R_EXPERT) & (lane_i < R_EXPERT + N_EXPERTS) & (group_of == g_sel)
        v1 = jnp.max(jnp.where(mine, logit, -jnp.inf), axis=-1, keepdims=True)
        i1 = jnp.min(jnp.where(mine & (logit == v1), lane, float(LANES)), axis=-1, keepdims=True)
        rest = mine & (lane != i1)
        v2 = jnp.max(jnp.where(rest, logit, -jnp.inf), axis=-1, keepdims=True)
        i2 = jnp.min(jnp.where(rest & (logit == v2), lane, float(LANES)), axis=-1, keepdims=True)
        e2 = jnp.exp(v2 - v1)
        w1 = p_group / (1.0 + e2)
        w2 = p_group * e2 / (1.0 + e2)
        gate_ref[...] = jnp.where(lane == i1, w1, 0.0) + jnp.where(lane == i2, w2, 0.0)
        o_ref[...] = x

    gates = gate_ref[...]
    lane = lax.broadcasted_iota(jnp.int32, gates.shape, 1)
    acts = []
    for j in range(EXPERTS_PER_GROUP):
        gate_e = jnp.sum(jnp.where(lane == R_EXPERT + e * EXPERTS_PER_GROUP + j, gates, 0.0), axis=-1, keepdims=True)
        gu = _dot(h_ref[...], win_ref[j])
        acts.append((_silu(gu[:, :EXPERT_FF]) * gu[:, EXPERT_FF:] * gate_e).astype(BF16))
    o_ref[...] += _dot(jnp.concatenate(acts, axis=1), wout_ref[0])


def moe(x2, gain, wr_hi, wr_lo, br, win_bf, wout_bf, layer, tm=1024):
    n, d = x2.shape
    row = pl.BlockSpec((tm, d), lambda i, e: (i, 0))
    win_g = win_bf.reshape(-1, d, 2 * EXPERT_FF)
    wout_g = wout_bf.reshape(-1, EXPERTS_PER_GROUP * EXPERT_FF, d)
    return pl.pallas_call(
        _moe_body,
        grid=(n // tm, N_GROUPS),
        in_specs=[row, _full((1, d)), _full((d, LANES)), _full((d, LANES)), _full((1, LANES)),
                  pl.BlockSpec((EXPERTS_PER_GROUP, d, 2 * EXPERT_FF), lambda i, e: (layer * N_GROUPS + e, 0, 0)),
                  pl.BlockSpec((1, EXPERTS_PER_GROUP * EXPERT_FF, d), lambda i, e: (layer * N_GROUPS + e, 0, 0))],
        out_specs=row,
        out_shape=jax.ShapeDtypeStruct((n, d), F32),
        scratch_shapes=[pltpu.VMEM((tm, d), BF16), pltpu.VMEM((tm, LANES), F32)],
        compiler_params=_params("parallel", "arbitrary"),
    )(x2, gain.reshape(1, d), wr_hi, wr_lo, br, win_g, wout_g)


G_CUM, G_BETA, G_LAST = 0, GDN_HEADS, 2 * GDN_HEADS


def _gdn_gates_body(ab_ref, alog_ref, dtb_ref, gb_ref):
    t = ab_ref.shape[1]
    ab = ab_ref[0]
    sp_in = ab + dtb_ref[...]
    softplus = jnp.maximum(sp_in, 0.0) + jnp.log1p(jnp.exp(-jnp.abs(sp_in)))
    lane_row = lax.broadcasted_iota(jnp.int32, (1, LANES), 1)
    g = jnp.where(lane_row < GDN_HEADS, -jnp.exp(alog_ref[...]) * softplus, 0.0)
    blk = 4 * GDN_CHUNK
    r = lax.broadcasted_iota(jnp.int32, (blk, blk), 0)
    c = lax.broadcasted_iota(jnp.int32, (blk, blk), 1)
    same = r // GDN_CHUNK == c // GDN_CHUNK
    tri = jnp.where(same & (r >= c), 1.0, 0.0).astype(BF16)
    tot = jnp.where(same, 1.0, 0.0).astype(BF16)
    lane = lax.broadcasted_iota(jnp.int32, (blk, LANES), 1)
    for s in range(t // blk):
        rs = slice(s * blk, (s + 1) * blk)
        hi, mid, lo = _split3(g[rs])
        gc = _dot(tri, hi) + _dot(tri, mid) + _dot(tri, lo)
        gl = _dot(tot, hi) + _dot(tot, mid) + _dot(tot, lo)
        gl = pltpu.roll(gl, G_LAST, 1)
        gb_ref[0, rs, :] = jnp.where(lane < G_BETA, gc, jnp.where(lane < G_LAST, _sigmoid(ab[rs]), gl))


def gdn_gates(proj, alog_row, dtb_row):
    b, t, _ = proj.shape
    return pl.pallas_call(
        _gdn_gates_body,
        grid=(b,),
        in_specs=[pl.BlockSpec((1, t, LANES), lambda i: (i, 0, C_AB // LANES)), _full((1, LANES)), _full((1, LANES))],
        out_specs=pl.BlockSpec((1, t, LANES), lambda i: (i, 0, 0)),
        out_shape=jax.ShapeDtypeStruct((b, t, LANES), F32),
        compiler_params=_params("parallel"),
    )(proj, alog_row, dtb_row)


GDN_HEADS_PER_STEP = 4
GDN_CHUNKS_PER_TRIP = 4
GDN_SEGMENTS = 2


def _dot3(a, b):
    a_hi, a_lo = _split2(a)
    b_hi, b_lo = _split2(b)
    return _dot(a_hi, b_hi) + _dot(a_hi, b_lo) + _dot(a_lo, b_hi)


def _dot1(a, b):
    return _dot(a.astype(BF16), b.astype(BF16))


CONV_HALO = 8


def _conv_silu(x, halo, cw_ref, lanes):
    t = x.shape[0]
    ext = jnp.concatenate([halo, x], axis=0)
    y = x * cw_ref[CONV_WIDTH - 1:CONV_WIDTH, lanes]
    for d in range(1, CONV_WIDTH):
        shifted = pltpu.roll(ext, d, 0)[CONV_HALO:CONV_HALO + t]
        y = y + shifted * cw_ref[CONV_WIDTH - 1 - d:CONV_WIDTH - d, lanes]
    return _silu(y)


def _l2norm(y):
    return y * lax.rsqrt(jnp.sum(y * y, axis=-1, keepdims=True) + NORM_EPS)


def _gdn_body(q_ref, k_ref, v_ref, cq_ref, ck_ref, cv_ref, gb_ref, grow_ref, o_ref,
              halo_ref, state_ref, gl_ref, gc_ref, kb_ref, k_ref_s, kbg_ref, vb_ref, qs_ref, qg_ref, kd_ref,
              u_ref, w_ref, a_ref):
    t = q_ref.shape[1]
    cs = GDN_CHUNK
    dk = GDN_HEAD_DIM
    nh = GDN_HEADS_PER_STEP
    hgrp = pl.program_id(1)
    seg = pl.program_id(2)
    gb_hi, gb_mid, gb_lo = _split3(gb_ref[0])
    pick_row = lax.broadcasted_iota(jnp.int32, (LANES, LANES), 0)

    @pl.when(seg == 0)
    def _():
        halo_ref[...] = jnp.zeros_like(halo_ref)
        state_ref[...] = jnp.zeros_like(state_ref)

    def column(idx):
        sel = jnp.where(pick_row == idx, 1.0, 0.0).astype(BF16)
        return _dot(gb_hi, sel) + _dot(gb_mid, sel) + _dot(gb_lo, sel)

    for s in range(nh):
        lanes = slice(s * dk, (s + 1) * dk)
        head = nh * hgrp + s
        gcol = column(G_CUM + head)
        bcol = column(G_BETA + head)
        glast = column(G_LAST + head)
        eg = jnp.exp(gcol)
        k = _l2norm(_conv_silu(k_ref[0, :, lanes], halo_ref[1, :, lanes], ck_ref, lanes))
        kb = k * bcol
        k_ref_s[s] = k.astype(BF16)
        kb_ref[s] = kb.astype(BF16)
        kbg_ref[s] = (kb * eg).astype(BF16)
        kd_ref[s] = (k * jnp.exp(glast - gcol)).astype(BF16)
        q = _l2norm(_conv_silu(q_ref[0, :, lanes], halo_ref[0, :, lanes], cq_ref, lanes)) * (dk ** -0.5)
        qs_ref[s] = q.astype(BF16)
        qg_ref[s] = (q * eg).astype(BF16)
        vb_ref[s] = (_conv_silu(v_ref[0, :, lanes], halo_ref[2, :, lanes], cv_ref, lanes) * bcol).astype(BF16)
        gl_ref[s] = glast
        gc_ref[s] = gcol
    for i, ref in enumerate((q_ref, k_ref, v_ref)):
        halo_ref[i] = ref[0, t - CONV_HALO:t, :]

    r = lax.broadcasted_iota(jnp.int32, (cs, cs), 0)
    c = lax.broadcasted_iota(jnp.int32, (cs, cs), 1)
    tril = r >= c
    strict = r > c
    eye = jnp.where(r == c, 1.0, 0.0)

    def prep(trip, _):
        probs = [(s, trip * GDN_CHUNKS_PER_TRIP + j) for j in range(GDN_CHUNKS_PER_TRIP) for s in range(nh)]
        rows = [pl.ds(pl.multiple_of(n * cs, cs), cs) for _, n in probs]
        decay, lmat = [], []
        for (s, n), rw in zip(probs, rows):
            gr = grow_ref[0, s, pl.ds(n, 1), :]
            gc = gc_ref[s, rw, :cs]
            decay.append(jnp.where(tril, jnp.exp(jnp.where(tril, gc - gr, 0.0)), 0.0))
        for i, ((s, _), rw) in enumerate(zip(probs, rows)):
            lmat.append(jnp.where(strict, _dot_nt(kb_ref[s, rw, :], k_ref_s[s, rw, :]) * decay[i], 0.0))
        inv = [eye - m for m in lmat]
        pw = [_dot3(m, m) for m in lmat]
        span = 2
        while span < cs:
            mm = _dot3 if span == 2 else _dot1
            inv = [x + mm(x, p) for x, p in zip(inv, pw)]
            span *= 2
            if span < cs:
                pw = [_dot1(p, p) for p in pw]
        inv_bf = [x.astype(BF16) for x in inv]
        for i, ((s, _), rw) in enumerate(zip(probs, rows)):
            u_ref[s, rw, :] = _dot(inv_bf[i], vb_ref[s, rw, :])
            w_ref[s, rw, :] = _dot(inv_bf[i], kbg_ref[s, rw, :]).astype(BF16)
            a_ref[s, rw, :] = jnp.where(tril, _dot_nt(qs_ref[s, rw, :], k_ref_s[s, rw, :]) * decay[i], 0.0).astype(BF16)
        return 0

    lax.fori_loop(0, t // (cs * GDN_CHUNKS_PER_TRIP), prep, 0)

    def scan(n, states):
        r0 = pl.multiple_of(n * cs, cs)
        rows = pl.ds(r0, cs)
        s_bf = [st.astype(BF16) for st in states]
        v_bf = [(u_ref[s, rows, :] - _dot(w_ref[s, rows, :], s_bf[s])).astype(BF16) for s in range(nh)]
        new = [states[s] * jnp.exp(gl_ref[s, pl.ds(r0, 1), :]) + _dot_tn(kd_ref[s, rows, :], v_bf[s])
               for s in range(nh)]
        for s in range(nh):
            o_ref[0, rows, s * dk:(s + 1) * dk] = _dot(qg_ref[s, rows, :], s_bf[s]) + _dot(a_ref[s, rows, :], v_bf[s])
        return tuple(new)

    final = lax.fori_loop(0, t // cs, scan, tuple(state_ref[s] for s in range(nh)))
    for s in range(nh):
        state_ref[s] = final[s]


def gdn_core(proj, conv_w, gb):
    b, t, _ = proj.shape
    ts = t // GDN_SEGMENTS if t % (GDN_SEGMENTS * GDN_CHUNK * GDN_CHUNKS_PER_TRIP) == 0 else t
    seg_chunks = ts // GDN_CHUNK
    nh = GDN_HEADS_PER_STEP
    wide = nh * GDN_HEAD_DIM
    per = GDN_WIDTH // wide
    g_rows = jnp.swapaxes(gb[:, :, G_CUM:G_CUM + GDN_HEADS], 1, 2).reshape(b, GDN_HEADS, t // GDN_CHUNK, GDN_CHUNK)
    sect = lambda k: pl.BlockSpec((1, ts, wide), lambda i, h, s: (i, s, k * per + h))
    taps = lambda k: pl.BlockSpec((CONV_WIDTH, wide), lambda i, h, s: (0, k * per + h))
    bf = lambda w: pltpu.VMEM((nh, ts, w), BF16)
    return pl.pallas_call(
        _gdn_body,
        grid=(b, per, t // ts),
        in_specs=[sect(0), sect(1), sect(2), taps(0), taps(1), taps(2),
                  pl.BlockSpec((1, ts, LANES), lambda i, h, s: (i, s, 0)),
                  pl.BlockSpec((1, nh, seg_chunks, GDN_CHUNK), lambda i, h, s: (i, h, s, 0))],
        out_specs=pl.BlockSpec((1, ts, wide), lambda i, h, s: (i, s, h)),
        out_shape=jax.ShapeDtypeStruct((b, t, GDN_WIDTH), F32),
        scratch_shapes=[pltpu.VMEM((3, CONV_HALO, wide), F32), pltpu.VMEM((nh, GDN_HEAD_DIM, GDN_HEAD_DIM), F32)]
                       + [pltpu.VMEM((nh, ts, LANES), F32)] * 2 + [bf(GDN_HEAD_DIM)] * 7
                       + [pltpu.VMEM((nh, ts, GDN_HEAD_DIM), F32), bf(GDN_HEAD_DIM), bf(GDN_CHUNK)],
        compiler_params=_params("parallel", "parallel", "arbitrary"),
    )(proj, proj, proj, conv_w, conv_w, conv_w, gb, g_rows)


def _rope_tables(pos):
    half = HEAD_DIM // 2
    inv_freq = ROPE_THETA ** (-jnp.arange(half, dtype=F32) / half)
    ang = pos.astype(F32)[:, None] * inv_freq
    cos = jnp.cos(ang)
    sin = jnp.sin(ang)
    cos_t = jnp.tile(jnp.concatenate([cos, cos], axis=-1), (1, LANES // HEAD_DIM))
    sin_t = jnp.tile(jnp.concatenate([-sin, sin], axis=-1), (1, LANES // HEAD_DIM))
    return cos_t, sin_t


def _block_diag_ones(width, seg):
    idx = np.arange(width) // seg
    return jnp.asarray((idx[:, None] == idx[None, :]).astype(np.float32), dtype=BF16)


def _pad_cols(w, width):
    return jnp.pad(w, ((0, 0), (0, width - w.shape[1])))


def _even_layer(x2, b, t, norm_gain, w_in, b_gate, b_forget, cmp_pe, cmp_w1, cmp_w2, nsa_gain, fox_gain, w_out):
    d = x2.shape[1]
    q_perm = np.concatenate([np.arange(HEAD_DIM) + (kvh * NSA_GROUP + g) * HEAD_DIM
                             for g in range(NSA_GROUP) for kvh in range(NSA_KV_HEADS)])
    o_gate = NSA_Q_W + 6 * NSA_KV_W
    o_fox = o_gate + NSA_GATE_W
    w_re = jnp.concatenate([w_in[:, q_perm], w_in[:, NSA_Q_W:o_gate], w_in[:, o_fox:o_fox + 3 * FOX_W],
                            w_in[:, o_gate:o_fox], w_in[:, o_fox + 3 * FOX_W:]], axis=1)
    proj = norm_matmul(x2, norm_gain, _pad_cols(w_re, EVEN_W).astype(BF16)).reshape(b, t, EVEN_W)

    cos, sin = _rope_tables(jnp.arange(t))
    tile = lambda g, n: jnp.tile(g, n).reshape(1, -1)
    bias = jnp.pad(jnp.concatenate([b_gate, b_forget]), (0, LANES - NSA_GATE_W - FOX_HEADS)).reshape(1, LANES)
    bd = _block_diag_ones(FOX_W, HEAD_DIM)
    (qa, ks, kw, vs, vw, kc_raw, vc_raw, qb, kb, vf, gates, cum) = even_prep(
        proj, cos, sin, tile(nsa_gain[0], NSA_HEADS), tile(nsa_gain[2], NSA_KV_HEADS), tile(nsa_gain[3], NSA_KV_HEADS),
        tile(fox_gain[0], FOX_HEADS), tile(fox_gain[1], FOX_HEADS), bias, bd)

    n_str = t // CMP_STRIDE
    half = CMP_BLOCK // 2
    eye2 = jnp.eye(NSA_KV_HEADS, dtype=F32)
    pe = jnp.tile(cmp_pe[:, :, None, :], (1, 1, NSA_KV_HEADS, 1)).reshape(2, 2, 1, half * NSA_KV_W)
    w1 = jnp.einsum('ilde,hg->ilhdge', cmp_w1, eye2).reshape(2, 2, half * NSA_KV_W, NSA_KV_W).astype(BF16)
    w2 = jnp.einsum('ide,hg->ihdge', cmp_w2, eye2).reshape(2, NSA_KV_W, NSA_KV_W).astype(BF16)
    cos_c, sin_c = _rope_tables(jnp.arange(n_str) * CMP_STRIDE + (CMP_BLOCK - 1))
    kc, vc = compress(kc_raw.reshape(b, n_str, CMP_STRIDE * NSA_KV_W), vc_raw.reshape(b, n_str, CMP_STRIDE * NSA_KV_W),
                      pe, w1, w2, tile(nsa_gain[1], NSA_KV_HEADS), cos_c, sin_c, _block_diag_ones(LANES, HEAD_DIM))

    n_sel = t // SEL_BLOCK
    cs = np.arange(n_str)[:, None] * CMP_STRIDE
    ss = np.arange(n_sel)[None, :] * SEL_BLOCK
    overlap = np.clip(np.minimum(cs + CMP_BLOCK, ss + SEL_BLOCK) - np.maximum(cs, ss), 0, None) / CMP_BLOCK
    overlap[(t - CMP_BLOCK) // CMP_STRIDE + 1:] = 0.0
    ovt = jnp.asarray(overlap.T.astype(np.float32), dtype=BF16)
    o_a = nsa_attention(qa, kc, vc, ks, vs, kw, vw, gates, ovt)
    o_b = fox_attention(qb, kb, vf, cum)

    wa = w_out[:NSA_Q_W][q_perm].astype(BF16)
    wb = w_out[NSA_Q_W:].astype(BF16)
    return even_out(x2, o_a.reshape(b * t, NSA_Q_W), o_b.reshape(b * t, FOX_W), wa, wb)


def _odd_layer(x2, b, t, norm_gain, w_in, conv_w, a_log, dt_bias, gdn_gain, w_out):
    proj = norm_matmul(x2, norm_gain, _pad_cols(w_in, ODD_W).astype(BF16)).reshape(b, t, ODD_W)
    pad8 = lambda v: jnp.pad(v, (0, LANES - GDN_HEADS)).reshape(1, LANES)
    gb = gdn_gates(proj, pad8(a_log), pad8(dt_bias))
    o = gdn_core(proj, conv_w, gb)
    return odd_out(x2, o.reshape(b * t, GDN_WIDTH), proj.reshape(b * t, ODD_W), gdn_gain.reshape(1, GDN_HEAD_DIM),
                   w_out.astype(BF16))


def _moe_layer(x2, gain, w_rg, b_rg, w_re, b_re, w_ein_bf, w_eout_bf, layer):
    d = x2.shape[1]
    wr = _pad_cols(jnp.concatenate([w_rg, w_re], axis=1), LANES)
    wr_hi = wr.astype(BF16)
    wr_lo = (wr - wr_hi.astype(F32)).astype(BF16)
    br = jnp.pad(jnp.concatenate([b_rg, b_re]), (0, LANES - N_GROUPS - N_EXPERTS)).reshape(1, LANES)
    return moe(x2, gain, wr_hi, wr_lo, br, w_ein_bf, w_eout_bf, layer)


def kernel(x, norm_mix, norm_ffn, w_in_even, b_nsa_gate, b_forget, cmp_pe, cmp_w1, cmp_w2, nsa_qk_gain, fox_qk_gain,
           w_out_even, w_in_odd, conv_w, a_log, dt_bias, gdn_norm_gain, w_out_odd, w_router_group, b_router_group,
           w_router_expert, b_router_expert, w_expert_in, w_expert_out):
    b, t, d = x.shape
    x2 = x.reshape(b * t, d)
    w_ein_bf = w_expert_in.astype(BF16)
    w_eout_bf = w_expert_out.astype(BF16)
    for layer in range(norm_mix.shape[0]):
        i = layer // 2
        if layer % 2 == 0:
            x2 = _even_layer(x2, b, t, norm_mix[layer], w_in_even[i], b_nsa_gate[i], b_forget[i], cmp_pe[i], cmp_w1[i],
                             cmp_w2[i], nsa_qk_gain[i], fox_qk_gain[i], w_out_even[i])
        else:
            x2 = _odd_layer(x2, b, t, norm_mix[layer], w_in_odd[i], conv_w[i], a_log[i], dt_bias[i], gdn_norm_gain[i],
                            w_out_odd[i])
        x2 = _moe_layer(x2, norm_ffn[layer], w_router_group[layer], b_router_group[layer], w_router_expert[layer],
                        b_router_expert[layer], w_ein_bf, w_eout_bf, layer)
    return x2.reshape(b, t, d)
```

```python
import functools

import numpy as np
import jax
import jax.numpy as jnp
from jax import lax
from jax.experimental import pallas as pl
from jax.experimental.pallas import tpu as pltpu

F32 = jnp.float32
BF16 = jnp.bfloat16

HEAD_DIM = 64
ROPE_THETA = 10000.0
NSA_HEADS = 8
NSA_KV_HEADS = 2
NSA_GROUP = NSA_HEADS // NSA_KV_HEADS
CMP_BLOCK = 32
CMP_STRIDE = 16
SEL_BLOCK = 64
SEL_TOPK = 8
WINDOW = 256
FOX_HEADS = 8
GDN_HEADS = 8
GDN_HEAD_DIM = 128
GDN_WIDTH = GDN_HEADS * GDN_HEAD_DIM
CONV_WIDTH = 4
GDN_CHUNK = 64
N_GROUPS = 4
EXPERTS_PER_GROUP = 4
N_EXPERTS = N_GROUPS * EXPERTS_PER_GROUP
EXPERT_FF = 256
NORM_EPS = 1e-6
NEG_INF = -1e30
FORCE_SCORE = 1e9

LANES = 128
LOG2E = 1.4426950408889634
SUM_ROWS = 16
NSA_Q_W = NSA_HEADS * HEAD_DIM
NSA_KV_W = NSA_KV_HEADS * HEAD_DIM
NSA_GATE_W = 3 * NSA_HEADS
FOX_W = FOX_HEADS * HEAD_DIM
C_QN = 0
C_KC, C_VC, C_KS, C_VS, C_KW, C_VW = (NSA_Q_W + i * NSA_KV_W for i in range(6))
C_QF = NSA_Q_W + 6 * NSA_KV_W
C_KF = C_QF + FOX_W
C_VF = C_KF + FOX_W
C_MISC = C_VF + FOX_W
EVEN_W = C_MISC + LANES
MISC_F = NSA_GATE_W
C_AB = 4 * GDN_WIDTH
ODD_W = C_AB + LANES

VMEM_LIMIT = 56 * 1024 * 1024


def _params(*sem):
    return pltpu.CompilerParams(dimension_semantics=sem, vmem_limit_bytes=VMEM_LIMIT)


def _dot(a, b):
    return jnp.dot(a, b, preferred_element_type=F32)


def _dot_nt(a, b):
    return lax.dot_general(a, b, (((1,), (1,)), ((), ())), preferred_element_type=F32)


def _dot_tn(a, b):
    return lax.dot_general(a, b, (((0,), (0,)), ((), ())), preferred_element_type=F32)


def _split2(x):
    hi = x.astype(BF16)
    return hi, (x - hi.astype(F32)).astype(BF16)


def _split3(x):
    hi = x.astype(BF16)
    r = x - hi.astype(F32)
    mid = r.astype(BF16)
    return hi, mid, (r - mid.astype(F32)).astype(BF16)


def _sigmoid(z):
    return 1.0 / (1.0 + jnp.exp(-z))


def _silu(z):
    return z * _sigmoid(z)


def _full(shape):
    nd = len(shape)
    return pl.BlockSpec(shape, lambda *_: (0,) * nd)


def _norm_matmul_body(x_ref, g_ref, w_ref, o_ref):
    x = x_ref[...]
    ms = jnp.mean(x * x, axis=-1, keepdims=True)
    h = (x * lax.rsqrt(ms + NORM_EPS) * g_ref[...]).astype(BF16)
    o_ref[...] = _dot(h, w_ref[...])


def norm_matmul(x2, gain, w_bf, tm=512):
    n, d = x2.shape
    wp = w_bf.shape[1]
    return pl.pallas_call(
        _norm_matmul_body,
        grid=(n // tm,),
        in_specs=[pl.BlockSpec((tm, d), lambda i: (i, 0)), _full((1, d)), _full((d, wp))],
        out_specs=pl.BlockSpec((tm, wp), lambda i: (i, 0)),
        out_shape=jax.ShapeDtypeStruct((n, wp), F32),
        compiler_params=_params("parallel"),
    )(x2, gain.reshape(1, d), w_bf)


def _head_rms(x, bd, gain):
    hi, lo = _split2(x * x)
    w = x.shape[1]
    ssum = _dot(hi, bd[:w, :w]) + _dot(lo, bd[:w, :w])
    return x * lax.rsqrt(ssum * (1.0 / HEAD_DIM) + NORM_EPS) * gain


def _rope(x, cos, sin_signed, first_half):
    fwd = pltpu.roll(x, LANES - HEAD_DIM // 2, 1)
    bwd = pltpu.roll(x, HEAD_DIM // 2, 1)
    return x * cos + jnp.where(first_half, fwd, bwd) * sin_signed


def _even_prep_body(p_ref, cos_ref, sin_ref, gq_ref, gks_ref, gkw_ref, gfq_ref, gfk_ref, bias_ref, bd_ref,
                    qa_ref, ks_ref, kw_ref, vs_ref, vw_ref, kc_ref, vc_ref, qb_ref, kb_ref, vf_ref,
                    gate_ref, cum_ref, carry_ref):
    tr = p_ref.shape[1]
    bd = bd_ref[...]
    cos = cos_ref[...]
    sin = sin_ref[...]
    lane = lax.broadcasted_iota(jnp.int32, (1, LANES), 1)
    first_half = (lane % HEAD_DIM) < (HEAD_DIM // 2)
    scale = HEAD_DIM ** -0.5 * LOG2E

    qn = _head_rms(p_ref[0, :, C_QN:C_QN + NSA_Q_W], bd, gq_ref[...])
    for c in range(NSA_Q_W // LANES):
        sl = slice(c * LANES, (c + 1) * LANES)
        qa_ref[0, sl, :] = (_rope(qn[:, sl], cos, sin, first_half) * scale).T.astype(BF16)
    ks = _head_rms(p_ref[0, :, C_KS:C_KS + NSA_KV_W], bd, gks_ref[...])
    ks_ref[0] = _rope(ks, cos, sin, first_half).astype(BF16)
    kw = _head_rms(p_ref[0, :, C_KW:C_KW + NSA_KV_W], bd, gkw_ref[...])
    kw_ref[0] = _rope(kw, cos, sin, first_half).astype(BF16)
    vs_ref[0] = p_ref[0, :, C_VS:C_VS + NSA_KV_W].T.astype(BF16)
    vw_ref[0] = p_ref[0, :, C_VW:C_VW + NSA_KV_W].T.astype(BF16)
    kc_ref[0] = p_ref[0, :, C_KC:C_KC + NSA_KV_W]
    vc_ref[0] = p_ref[0, :, C_VC:C_VC + NSA_KV_W]

    qb = _head_rms(p_ref[0, :, C_QF:C_QF + FOX_W], bd, gfq_ref[...]) * scale
    kb_ref[0] = _head_rms(p_ref[0, :, C_KF:C_KF + FOX_W], bd, gfk_ref[...]).astype(BF16)
    for c in range(FOX_W // LANES):
        sl = slice(c * LANES, (c + 1) * LANES)
        qb_ref[0, sl, :] = qb[:, sl].T.astype(BF16)
        vf_ref[0, sl, :] = p_ref[0, :, C_VF + c * LANES:C_VF + (c + 1) * LANES].T.astype(BF16)

    z = p_ref[0, :, C_MISC:C_MISC + LANES] + bias_ref[...]
    gate_ref[0] = _sigmoid(z).T
    logf = jnp.minimum(z, 0.0) - jnp.log1p(jnp.exp(-jnp.abs(z)))

    @pl.when(pl.program_id(1) == 0)
    def _():
        carry_ref[...] = jnp.zeros_like(carry_ref)

    row = lax.broadcasted_iota(jnp.int32, (tr, tr), 0)
    col = lax.broadcasted_iota(jnp.int32, (tr, tr), 1)
    tril = jnp.where(row >= col, 1.0, 0.0).astype(BF16)
    hi, mid, lo = _split3(logf)
    cum = _dot(tril, hi) + _dot(tril, mid) + _dot(tril, lo) + carry_ref[...]
    cum_ref[0] = cum
    carry_ref[...] = cum[tr - 1:tr, :]


def even_prep(proj, cos, sin, gq, gks, gkw, gfq, gfk, bias, bd, tr=256):
    b, t, _ = proj.shape
    row = lambda w: pl.BlockSpec((1, tr, w), lambda i, j: (i, j, 0))
    tab = pl.BlockSpec((tr, LANES), lambda i, j: (j, 0))
    shp = lambda w, dt: jax.ShapeDtypeStruct((b, t, w), dt)
    col = lambda w: pl.BlockSpec((1, w, tr), lambda i, j: (i, 0, j))
    shp_t = lambda w, dt: jax.ShapeDtypeStruct((b, w, t), dt)
    return pl.pallas_call(
        _even_prep_body,
        grid=(b, t // tr),
        in_specs=[row(EVEN_W), tab, tab, _full((1, NSA_Q_W)), _full((1, LANES)), _full((1, LANES)),
                  _full((1, FOX_W)), _full((1, FOX_W)), _full((1, LANES)), _full((FOX_W, FOX_W))],
        out_specs=[col(NSA_Q_W), row(LANES), row(LANES), col(LANES), col(LANES), row(LANES), row(LANES),
                   col(FOX_W), row(FOX_W), col(FOX_W), col(LANES), row(LANES)],
        out_shape=[shp_t(NSA_Q_W, BF16), shp(LANES, BF16), shp(LANES, BF16), shp_t(LANES, BF16), shp_t(LANES, BF16),
                   shp(LANES, F32), shp(LANES, F32), shp_t(FOX_W, BF16), shp(FOX_W, BF16), shp_t(FOX_W, BF16),
                   shp_t(LANES, F32), shp(LANES, F32)],
        scratch_shapes=[pltpu.VMEM((1, LANES), F32)],
        compiler_params=_params("parallel", "arbitrary"),
    )(proj, cos, sin, gq, gks, gkw, gfq, gfk, bias, bd)


def _gelu_tanh(x):
    return 0.5 * x * (1.0 + jnp.tanh(np.sqrt(2.0 / np.pi).astype(np.float32) * (x + 0.044715 * (x * x * x))))


def _compress_body(xk_ref, xv_ref, pe_ref, w1_ref, w2_ref, gk_ref, cos_ref, sin_ref, bd_ref, kc_ref, vc_ref):
    n = xk_ref.shape[1]
    lane = lax.broadcasted_iota(jnp.int32, (1, LANES), 1)
    first_half = (lane % HEAD_DIM) < (HEAD_DIM // 2)

    def mlp(x_ref, i):
        x = x_ref[0]
        nxt = pltpu.roll(x, n - 1, 0)
        xa = (x + pe_ref[i, 0]).astype(BF16)
        xb = (nxt + pe_ref[i, 1]).astype(BF16)
        h = _dot(xa, w1_ref[i, 0]) + _dot(xb, w1_ref[i, 1])
        return _dot(_gelu_tanh(h).astype(BF16), w2_ref[i])

    kc = _head_rms(mlp(xk_ref, 0), bd_ref[...], gk_ref[...])
    kc_ref[0] = _rope(kc, cos_ref[...], sin_ref[...], first_half).astype(BF16)
    vc_ref[0] = mlp(xv_ref, 1).T.astype(BF16)


def compress(xk, xv, pe, w1, w2, gk, cos_c, sin_c, bd):
    b, n, w = xk.shape
    blk = pl.BlockSpec((1, n, w), lambda i: (i, 0, 0))
    out = pl.BlockSpec((1, n, LANES), lambda i: (i, 0, 0))
    return pl.pallas_call(
        _compress_body,
        grid=(b,),
        in_specs=[blk, blk, _full(pe.shape), _full(w1.shape), _full(w2.shape), _full((1, LANES)),
                  _full((n, LANES)), _full((n, LANES)), _full((LANES, LANES))],
        out_specs=[out, pl.BlockSpec((1, LANES, n), lambda i: (i, 0, 0))],
        out_shape=[jax.ShapeDtypeStruct((b, n, LANES), BF16), jax.ShapeDtypeStruct((b, LANES, n), BF16)],
        compiler_params=_params("parallel"),
    )(xk, xv, pe, w1, w2, gk, cos_c, sin_c, bd)


def _flash_step(s_ref, p_ref, acc_ref, v_blk, m_i, l_i, adjust, first=False):
    n_ch = acc_ref.shape[0]
    al, ms = [], []
    for cg in range(s_ref.shape[1] // LANES):
        sl = slice(cg * LANES, (cg + 1) * LANES)
        s = adjust(s_ref[:, sl], cg)
        m_new = jnp.maximum(m_i[:, sl], jnp.max(s, axis=0, keepdims=True))
        p_ref[:, sl] = jnp.exp2(s - m_new).astype(BF16)
        al.append(jnp.exp2(m_i[:, sl] - m_new))
        ms.append(m_new)
    cat = lambda xs: jnp.concatenate(xs, axis=1)
    alpha = cat(al)
    pv = _dot(v_blk, p_ref[...])
    acc_ref[...] = pv[:n_ch] if first else alpha * acc_ref[...] + pv[:n_ch]
    return cat(ms), alpha * l_i + pv[n_ch:n_ch + 1]


NSA_KEYS_PER_QUERY_BLOCK = 2


def _nsa_body(q_ref, kc_ref, vc_ref, ks_ref, vs_ref, kw_ref, vw_ref, gate_ref, ovt_ref, o_ref,
              sel_ref, s0_ref, s1_ref, p_ref, acc_ref, *, k_top):
    tq = q_ref.shape[2]
    t_all = ks_ref.shape[1]
    n_cmp = kc_ref.shape[1]
    n_sel = ovt_ref.shape[0]
    g_n = NSA_GROUP
    c = pl.program_id(1)
    t0 = c * tq
    chan = lax.broadcasted_iota(jnp.int32, (LANES, 1), 0)
    tlane = t0 + lax.broadcasted_iota(jnp.int32, (1, tq), 1)
    gates = gate_ref[0]

    nrow = lax.broadcasted_iota(jnp.int32, (n_cmp, 1), 0)
    valid_c = (nrow * CMP_STRIDE + (CMP_BLOCK - 1)) <= tlane
    jrow = lax.broadcasted_iota(jnp.int32, (n_sel, tq), 0)
    jrow_f = jrow.astype(F32)
    cur = tlane // SEL_BLOCK
    forced = (jrow == 0) | (jrow == cur) | (jrow == cur - 1)
    future = jrow * SEL_BLOCK > tlane
    tk = NSA_KEYS_PER_QUERY_BLOCK * tq
    krow = lax.broadcasted_iota(jnp.int32, (tk, 1), 0)
    per_blk = tk // SEL_BLOCK
    w_len = tq + WINDOW
    w_start = pl.multiple_of(jnp.clip(t0 - WINDOW, 0, t_all - w_len), LANES)
    wrow = w_start + lax.broadcasted_iota(jnp.int32, (w_len, 1), 0)
    valid_w = (wrow <= tlane) & (wrow > tlane - WINDOW)

    heads = [(kvh, g) for kvh in range(NSA_KV_HEADS) for g in range(g_n)]
    qst = jnp.concatenate([jnp.where((chan // HEAD_DIM) == kvh, q_ref[0, g * LANES:(g + 1) * LANES, :], 0)
                           for kvh, g in heads], axis=1)
    n_col = len(heads) * tq

    def softmax_cols(s, ok, guard):
        outs = []
        for cg in range(len(heads)):
            sc = jnp.where(ok, s[:, cg * tq:(cg + 1) * tq], NEG_INF)
            e = jnp.exp2(sc - jnp.max(sc, axis=0, keepdims=True))
            if guard:
                e = jnp.where(ok, e, 0.0)
            den = jnp.sum(e, axis=0, keepdims=True)
            outs.append(e * (1.0 / (jnp.where(den > 0.0, den, 1.0) if guard else den)))
        return outs

    p_c = softmax_cols(_dot(kc_ref[0], qst), valid_c, guard=True)
    o_cmp = _dot(vc_ref[0], jnp.concatenate(p_c, axis=1).astype(BF16))

    for kvh in range(NSA_KV_HEADS):
        p_sum = p_c[kvh * g_n]
        for g in range(1, g_n):
            p_sum = p_sum + p_c[kvh * g_n + g]
        p_hi, p_lo = _split2(p_sum)
        imp_t = _dot(ovt_ref[...], p_hi) + _dot(ovt_ref[...], p_lo)
        val = jnp.where(forced, FORCE_SCORE, jnp.where(future, NEG_INF, imp_t))
        sel_t = jnp.zeros((n_sel, tq), F32)
        for _ in range(k_top):
            m = jnp.max(val, axis=0, keepdims=True)
            first = jnp.min(jnp.where(val == m, jrow_f, float(n_sel)), axis=0, keepdims=True)
            pick = jrow_f == first
            sel_t = jnp.where(pick, 1.0, sel_t)
            val = jnp.where(pick, -jnp.inf, val)
        sel_ref[kvh] = sel_t

    p_w = softmax_cols(_dot(kw_ref[0, pl.ds(w_start, w_len), :], qst), valid_w, guard=False)
    o_win = _dot(vw_ref[0, :, pl.ds(w_start, w_len)], jnp.concatenate(p_w, axis=1).astype(BF16))

    def put_scores(buf, kb):
        k0 = pl.multiple_of(jnp.minimum(kb * tk, t_all - tk), tk)
        buf[...] = _dot(ks_ref[0, pl.ds(k0, tk), :], qst)

    def half_step(buf, kb, m_i, l_i):
        k0 = pl.multiple_of(jnp.minimum(kb * tk, t_all - tk), tk)
        causal = (kb * tk + krow) <= tlane
        ok = [causal & (jnp.concatenate([jnp.broadcast_to(sel_ref[kvh, pl.ds(k0 // SEL_BLOCK + r, 1), :],
                                                          (SEL_BLOCK, tq)) for r in range(per_blk)], axis=0) > 0.5)
              for kvh in range(NSA_KV_HEADS)]
        adjust = lambda s_cols, cg: jnp.where(ok[cg // g_n], s_cols, NEG_INF)
        v_blk = jnp.concatenate([vs_ref[0, :, pl.ds(k0, tk)], jnp.ones((SUM_ROWS, tk), BF16)], axis=0)
        return _flash_step(buf, p_ref, acc_ref, v_blk, m_i, l_i, adjust)

    def sel_trip(j, carry):
        put_scores(s1_ref, 2 * j + 1)
        carry = half_step(s0_ref, 2 * j, *carry)
        put_scores(s0_ref, 2 * j + 2)
        return half_step(s1_ref, 2 * j + 1, *carry)

    put_scores(s0_ref, 0)
    acc_ref[...] = jnp.zeros_like(acc_ref)
    init = (jnp.full((1, n_col), NEG_INF, F32), jnp.zeros((1, n_col), F32))
    n_blocks = (t0 + tq + tk - 1) // tk
    _, l_s = lax.fori_loop(0, (n_blocks + 1) // 2, sel_trip, init)
    o_slc = acc_ref[...] * (1.0 / l_s)

    first_head = chan < HEAD_DIM
    for g in range(g_n):
        per_kv = []
        for kvh in range(NSA_KV_HEADS):
            row = (kvh * g_n + g) * 3
            cols = slice((kvh * g_n + g) * tq, (kvh * g_n + g + 1) * tq)
            per_kv.append(gates[row:row + 1] * o_cmp[:, cols] + gates[row + 1:row + 2] * o_slc[:, cols]
                          + gates[row + 2:row + 3] * o_win[:, cols])
        o_ref[0, :, g * LANES:(g + 1) * LANES] = jnp.where(first_head, per_kv[0], per_kv[1]).T.astype(BF16)


def nsa_attention(qa_t, kc, vc_t, ks, vs_t, kw, vw_t, gates_t, ovt, tq=LANES):
    b, _, t = qa_t.shape
    n_cmp = kc.shape[1]
    n_sel = ovt.shape[0]
    k_top = min(SEL_TOPK, n_sel)
    tk = NSA_KEYS_PER_QUERY_BLOCK * tq
    tok = lambda n: pl.BlockSpec((1, n, LANES), lambda i, j: (i, 0, 0))
    chn = lambda n: pl.BlockSpec((1, LANES, n), lambda i, j: (i, 0, 0))
    return pl.pallas_call(
        functools.partial(_nsa_body, k_top=k_top),
        grid=(b, t // tq),
        in_specs=[pl.BlockSpec((1, NSA_Q_W, tq), lambda i, j: (i, 0, j)), tok(n_cmp), chn(n_cmp), tok(t), chn(t),
                  tok(t), chn(t), pl.BlockSpec((1, LANES, tq), lambda i, j: (i, 0, j)), _full(ovt.shape)],
        out_specs=pl.BlockSpec((1, tq, NSA_Q_W), lambda i, j: (i, j, 0)),
        out_shape=jax.ShapeDtypeStruct((b, t, NSA_Q_W), BF16),
        scratch_shapes=[pltpu.VMEM((NSA_KV_HEADS, n_sel, tq), F32), pltpu.VMEM((tk, NSA_HEADS * tq), F32),
                        pltpu.VMEM((tk, NSA_HEADS * tq), F32), pltpu.VMEM((tk, NSA_HEADS * tq), BF16),
                        pltpu.VMEM((LANES, NSA_HEADS * tq), F32)],
        compiler_params=_params("parallel", "arbitrary"),
    )(qa_t, kc, vc_t, ks, vs_t, kw, vw_t, gates_t, ovt)


FOX_KEYS_PER_QUERY_BLOCK = 2


def _fox_body(q_ref, k_ref, v_ref, cum_ref, o_ref, ck_ref, s0_ref, s1_ref, p_ref, acc_ref, *, tq):
    t = k_ref.shape[1]
    tk = FOX_KEYS_PER_QUERY_BLOCK * tq
    pair = pl.program_id(1)

    hi, mid, lo = _split3(cum_ref[0])
    pick_row = lax.broadcasted_iota(jnp.int32, (LANES, LANES), 0)
    for h in range(2):
        sel = jnp.where(pick_row == MISC_F + 2 * pair + h, 1.0, 0.0).astype(BF16)
        ck_ref[h] = (_dot(hi, sel) + _dot(mid, sel) + _dot(lo, sel)) * LOG2E

    chan = lax.broadcasted_iota(jnp.int32, (LANES, 1), 0)
    first_head = chan < HEAD_DIM
    krow = lax.broadcasted_iota(jnp.int32, (tk, 1), 0)
    qlane = lax.broadcasted_iota(jnp.int32, (1, tq), 1)
    reps = tq // LANES
    bufs = (s0_ref, s1_ref)
    blocks = [(i, kb) for i in range(t // tq) for kb in range((i * tq) // tk + 1)]
    q_cache = {}

    def q_pair(i):
        if i not in q_cache:
            q = q_ref[0, :, i * tq:(i + 1) * tq]
            q_cache[i] = jnp.concatenate([jnp.where(first_head, q, 0), jnp.where(first_head, 0, q)], axis=1)
        return q_cache[i]

    def put_scores(n):
        i, kb = blocks[n]
        bufs[n % 2][...] = _dot(k_ref[0, kb * tk:(kb + 1) * tk, :], q_pair(i))

    put_scores(0)
    m_i = l_i = None
    for n, (i, kb) in enumerate(blocks):
        if n + 1 < len(blocks):
            put_scores(n + 1)
        last = kb == (i * tq) // tk
        ok = ((kb * tk + krow) <= (i * tq + qlane)) if last else None

        def adjust(s_cols, cg, kb=kb, last=last, ok=ok):
            s_cols = s_cols - ck_ref[cg // reps, kb * tk:(kb + 1) * tk, :]
            return jnp.where(ok[:, (cg % reps) * LANES:(cg % reps + 1) * LANES], s_cols, NEG_INF) if last else s_cols

        if kb == 0:
            m_i = jnp.full((1, 2 * tq), NEG_INF, F32)
            l_i = jnp.zeros((1, 2 * tq), F32)
        v_blk = jnp.concatenate([v_ref[0, :, kb * tk:(kb + 1) * tk], jnp.ones((SUM_ROWS, tk), BF16)], axis=0)
        m_i, l_i = _flash_step(bufs[n % 2], p_ref, acc_ref, v_blk, m_i, l_i, adjust, first=kb == 0)
        if last:
            o = acc_ref[...] * (1.0 / l_i)
            o_ref[0, i * tq:(i + 1) * tq, :] = jnp.where(first_head, o[:, :tq], o[:, tq:]).T.astype(BF16)


def fox_attention(qb_t, kb, vf_t, cum, tq=256):
    b, w, t = qb_t.shape
    pairs = w // LANES
    tk = FOX_KEYS_PER_QUERY_BLOCK * tq
    return pl.pallas_call(
        functools.partial(_fox_body, tq=tq),
        grid=(b, pairs),
        in_specs=[pl.BlockSpec((1, LANES, t), lambda i, p: (i, p, 0)),
                  pl.BlockSpec((1, t, LANES), lambda i, p: (i, 0, p)),
                  pl.BlockSpec((1, LANES, t), lambda i, p: (i, p, 0)),
                  pl.BlockSpec((1, t, LANES), lambda i, p: (i, 0, 0))],
        out_specs=pl.BlockSpec((1, t, LANES), lambda i, p: (i, 0, p)),
        out_shape=jax.ShapeDtypeStruct((b, t, w), BF16),
        scratch_shapes=[pltpu.VMEM((2, t, LANES), F32), pltpu.VMEM((tk, 2 * tq), F32), pltpu.VMEM((tk, 2 * tq), F32),
                        pltpu.VMEM((tk, 2 * tq), BF16), pltpu.VMEM((LANES, 2 * tq), F32)],
        compiler_params=_params("parallel", "arbitrary"),
    )(qb_t, kb, vf_t, cum)


def _even_out_body(x_ref, oa_ref, ob_ref, wa_ref, wb_ref, o_ref):
    o_ref[...] = x_ref[...] + _dot(oa_ref[...], wa_ref[...]) + _dot(ob_ref[...], wb_ref[...])


def even_out(x2, oa, ob, wa, wb, tm=512):
    n, d = x2.shape
    row = lambda w: pl.BlockSpec((tm, w), lambda i: (i, 0))
    return pl.pallas_call(
        _even_out_body,
        grid=(n // tm,),
        in_specs=[row(d), row(oa.shape[1]), row(ob.shape[1]), _full(wa.shape), _full(wb.shape)],
        out_specs=row(d),
        out_shape=jax.ShapeDtypeStruct((n, d), F32),
        compiler_params=_params("parallel"),
    )(x2, oa, ob, wa, wb)


def _odd_out_body(x_ref, o_ref_in, z_ref, g_ref, w_ref, out_ref):
    gain = g_ref[...]
    parts = []
    for h in range(GDN_HEADS):
        sl = slice(h * GDN_HEAD_DIM, (h + 1) * GDN_HEAD_DIM)
        o = o_ref_in[:, sl]
        y = o * lax.rsqrt(jnp.mean(o * o, axis=-1, keepdims=True) + NORM_EPS) * gain
        parts.append((y * _silu(z_ref[:, sl])).astype(BF16))
    out_ref[...] = x_ref[...] + _dot(jnp.concatenate(parts, axis=1), w_ref[...])


def odd_out(x2, o2, proj, gain, w_bf, tm=512):
    n, d = x2.shape
    row = lambda w: pl.BlockSpec((tm, w), lambda i: (i, 0))
    z_col = 3 * GDN_WIDTH // GDN_WIDTH
    return pl.pallas_call(
        _odd_out_body,
        grid=(n // tm,),
        in_specs=[row(d), row(GDN_WIDTH), pl.BlockSpec((tm, GDN_WIDTH), lambda i: (i, z_col)),
                  _full((1, GDN_HEAD_DIM)), _full(w_bf.shape)],
        out_specs=row(d),
        out_shape=jax.ShapeDtypeStruct((n, d), F32),
        compiler_params=_params("parallel"),
    )(x2, o2, proj, gain, w_bf)


R_GROUP = 0
R_EXPERT = N_GROUPS


def _moe_body(x_ref, g_ref, wr_hi_ref, wr_lo_ref, br_ref, win_ref, wout_ref, o_ref, h_ref, gate_ref):
    e = pl.program_id(1)

    @pl.when(e == 0)
    def _():
        x = x_ref[...]
        h = x * lax.rsqrt(jnp.mean(x * x, axis=-1, keepdims=True) + NORM_EPS) * g_ref[...]
        h_ref[...] = h.astype(BF16)
        h_hi, h_lo = _split2(h)
        logit = _dot(h_hi, wr_hi_ref[...]) + _dot(h_lo, wr_hi_ref[...]) + _dot(h_hi, wr_lo_ref[...]) + br_ref[...]
        lane_i = lax.broadcasted_iota(jnp.int32, logit.shape, 1)
        lane = lane_i.astype(F32)
        is_g = lane_i < N_GROUPS
        g_max = jnp.max(jnp.where(is_g, logit, -jnp.inf), axis=-1, keepdims=True)
        g_sel = jnp.min(jnp.where(is_g & (logit == g_max), lane, float(LANES)), axis=-1, keepdims=True)
        p_group = 1.0 / jnp.sum(jnp.where(is_g, jnp.exp(logit - g_max), 0.0), axis=-1, keepdims=True)
        group_of = ((lane_i - R_EXPERT) // EXPERTS_PER_GROUP).astype(F32)
        mine = (lane_i >= R_EXPERT) & (lane_i < R_EXPERT + N_EXPERTS) & (group_of == g_sel)
        v1 = jnp.max(jnp.where(mine, logit, -jnp.inf), axis=-1, keepdims=True)
        i1 = jnp.min(jnp.where(mine & (logit == v1), lane, float(LANES)), axis=-1, keepdims=True)
        rest = mine & (lane != i1)
        v2 = jnp.max(jnp.where(rest, logit, -jnp.inf), axis=-1, keepdims=True)
        i2 = jnp.min(jnp.where(rest & (logit == v2), lane, float(LANES)), axis=-1, keepdims=True)
        e2 = jnp.exp(v2 - v1)
        w1 = p_group / (1.0 + e2)
        w2 = p_group * e2 / (1.0 + e2)
        gate_ref[...] = jnp.where(lane == i1, w1, 0.0) + jnp.where(lane == i2, w2, 0.0)
        o_ref[...] = x

    gates = gate_ref[...]
    lane = lax.broadcasted_iota(jnp.int32, gates.shape, 1)
    acts = []
    for j in range(EXPERTS_PER_GROUP):
        gate_e = jnp.sum(jnp.where(lane == R_EXPERT + e * EXPERTS_PER_GROUP + j, gates, 0.0), axis=-1, keepdims=True)
        gu = _dot(h_ref[...], win_ref[j])
        acts.append((_silu(gu[:, :EXPERT_FF]) * gu[:, EXPERT_FF:] * gate_e).astype(BF16))
    o_ref[...] += _dot(jnp.concatenate(acts, axis=1), wout_ref[0])


def moe(x2, gain, wr_hi, wr_lo, br, win_bf, wout_bf, layer, tm=1024):
    n, d = x2.shape
    row = pl.BlockSpec((tm, d), lambda i, e: (i, 0))
    win_g = win_bf.reshape(-1, d, 2 * EXPERT_FF)
    wout_g = wout_bf.reshape(-1, EXPERTS_PER_GROUP * EXPERT_FF, d)
    return pl.pallas_call(
        _moe_body,
        grid=(n // tm, N_GROUPS),
        in_specs=[row, _full((1, d)), _full((d, LANES)), _full((d, LANES)), _full((1, LANES)),
                  pl.BlockSpec((EXPERTS_PER_GROUP, d, 2 * EXPERT_FF), lambda i, e: (layer * N_GROUPS + e, 0, 0)),
                  pl.BlockSpec((1, EXPERTS_PER_GROUP * EXPERT_FF, d), lambda i, e: (layer * N_GROUPS + e, 0, 0))],
        out_specs=row,
        out_shape=jax.ShapeDtypeStruct((n, d), F32),
        scratch_shapes=[pltpu.VMEM((tm, d), BF16), pltpu.VMEM((tm, LANES), F32)],
        compiler_params=_params("parallel", "arbitrary"),
    )(x2, gain.reshape(1, d), wr_hi, wr_lo, br, win_g, wout_g)


G_CUM, G_BETA, G_LAST = 0, GDN_HEADS, 2 * GDN_HEADS


def _gdn_gates_body(ab_ref, alog_ref, dtb_ref, gb_ref):
    t = ab_ref.shape[1]
    ab = ab_ref[0]
    sp_in = ab + dtb_ref[...]
    softplus = jnp.maximum(sp_in, 0.0) + jnp.log1p(jnp.exp(-jnp.abs(sp_in)))
    lane_row = lax.broadcasted_iota(jnp.int32, (1, LANES), 1)
    g = jnp.where(lane_row < GDN_HEADS, -jnp.exp(alog_ref[...]) * softplus, 0.0)
    blk = 4 * GDN_CHUNK
    r = lax.broadcasted_iota(jnp.int32, (blk, blk), 0)
    c = lax.broadcasted_iota(jnp.int32, (blk, blk), 1)
    same = r // GDN_CHUNK == c // GDN_CHUNK
    tri = jnp.where(same & (r >= c), 1.0, 0.0).astype(BF16)
    tot = jnp.where(same, 1.0, 0.0).astype(BF16)
    lane = lax.broadcasted_iota(jnp.int32, (blk, LANES), 1)
    for s in range(t // blk):
        rs = slice(s * blk, (s + 1) * blk)
        hi, mid, lo = _split3(g[rs])
        gc = _dot(tri, hi) + _dot(tri, mid) + _dot(tri, lo)
        gl = _dot(tot, hi) + _dot(tot, mid) + _dot(tot, lo)
        gl = pltpu.roll(gl, G_LAST, 1)
        gb_ref[0, rs, :] = jnp.where(lane < G_BETA, gc, jnp.where(lane < G_LAST, _sigmoid(ab[rs]), gl))


def gdn_gates(proj, alog_row, dtb_row):
    b, t, _ = proj.shape
    return pl.pallas_call(
        _gdn_gates_body,
        grid=(b,),
        in_specs=[pl.BlockSpec((1, t, LANES), lambda i: (i, 0, C_AB // LANES)), _full((1, LANES)), _full((1, LANES))],
        out_specs=pl.BlockSpec((1, t, LANES), lambda i: (i, 0, 0)),
        out_shape=jax.ShapeDtypeStruct((b, t, LANES), F32),
        compiler_params=_params("parallel"),
    )(proj, alog_row, dtb_row)


GDN_HEADS_PER_STEP = 8
GDN_CHUNKS_PER_TRIP = 2
GDN_SEGMENTS = 4


def _dot3(a, b):
    a_hi, a_lo = _split2(a)
    b_hi, b_lo = _split2(b)
    return _dot(a_hi, b_hi) + _dot(a_hi, b_lo) + _dot(a_lo, b_hi)


def _dot1(a, b):
    return _dot(a.astype(BF16), b.astype(BF16))


CONV_HALO = 8


def _conv_silu(ext_ref, cw_ref, lanes):
    t = ext_ref.shape[0] - CONV_HALO
    y = ext_ref[CONV_HALO:, lanes] * cw_ref[CONV_WIDTH - 1:CONV_WIDTH, lanes]
    for d in range(1, CONV_WIDTH):
        y = y + ext_ref[CONV_HALO - d:CONV_HALO - d + t, lanes] * cw_ref[CONV_WIDTH - 1 - d:CONV_WIDTH - d, lanes]
    return _silu(y)


def _l2norm(y):
    return y * lax.rsqrt(jnp.sum(y * y, axis=-1, keepdims=True) + NORM_EPS)


def _gdn_body(q_ref, k_ref, v_ref, cq_ref, ck_ref, cv_ref, gb_ref, grow_ref, o_ref,
              ext_ref, state_ref, gl_ref, gc_ref, kb_ref, k_ref_s, kbg_ref, vb_ref, qs_ref, qg_ref, kd_ref,
              u_ref, w_ref, a_ref):
    t = q_ref.shape[1]
    cs = GDN_CHUNK
    dk = GDN_HEAD_DIM
    nh = GDN_HEADS_PER_STEP
    hgrp = pl.program_id(1)
    seg = pl.program_id(2)
    gb_hi, gb_mid, gb_lo = _split3(gb_ref[0])
    pick_row = lax.broadcasted_iota(jnp.int32, (LANES, LANES), 0)

    @pl.when(seg == 0)
    def _():
        ext_ref[:, 0:CONV_HALO, :] = jnp.zeros((3, CONV_HALO, ext_ref.shape[2]), F32)
        state_ref[...] = jnp.zeros_like(state_ref)

    @pl.when(seg > 0)
    def _():
        ext_ref[:, 0:CONV_HALO, :] = ext_ref[:, t:t + CONV_HALO, :]

    for i, ref in enumerate((q_ref, k_ref, v_ref)):
        ext_ref[i, CONV_HALO:, :] = ref[0]

    def column(idx):
        sel = jnp.where(pick_row == idx, 1.0, 0.0).astype(BF16)
        return _dot(gb_hi, sel) + _dot(gb_mid, sel) + _dot(gb_lo, sel)

    for s in range(nh):
        lanes = slice(s * dk, (s + 1) * dk)
        head = nh * hgrp + s
        gcol = column(G_CUM + head)
        bcol = column(G_BETA + head)
        glast = column(G_LAST + head)
        eg = jnp.exp(gcol)
        k = _l2norm(_conv_silu(ext_ref.at[1], ck_ref, lanes))
        kb = k * bcol
        k_ref_s[s] = k.astype(BF16)
        kb_ref[s] = kb.astype(BF16)
        kbg_ref[s] = (kb * eg).astype(BF16)
        kd_ref[s] = (k * jnp.exp(glast - gcol)).astype(BF16)
        q = _l2norm(_conv_silu(ext_ref.at[0], cq_ref, lanes)) * (dk ** -0.5)
        qs_ref[s] = q.astype(BF16)
        qg_ref[s] = (q * eg).astype(BF16)
        vb_ref[s] = (_conv_silu(ext_ref.at[2], cv_ref, lanes) * bcol).astype(BF16)
        gl_ref[s] = glast
        gc_ref[s] = gcol

    r = lax.broadcasted_iota(jnp.int32, (cs, cs), 0)
    c = lax.broadcasted_iota(jnp.int32, (cs, cs), 1)
    tril = r >= c
    strict = r > c
    eye = jnp.where(r == c, 1.0, 0.0)

    def prep(trip, _):
        probs = [(s, trip * GDN_CHUNKS_PER_TRIP + j) for j in range(GDN_CHUNKS_PER_TRIP) for s in range(nh)]
        rows = [pl.ds(pl.multiple_of(n * cs, cs), cs) for _, n in probs]
        decay, lmat = [], []
        for (s, n), rw in zip(probs, rows):
            gr = grow_ref[0, s, pl.ds(n, 1), :]
            gc = gc_ref[s, rw, :cs]
            decay.append(jnp.where(tril, jnp.exp(jnp.where(tril, gc - gr, 0.0)), 0.0))
        for i, ((s, _), rw) in enumerate(zip(probs, rows)):
            lmat.append(jnp.where(strict, _dot_nt(kb_ref[s, rw, :], k_ref_s[s, rw, :]) * decay[i], 0.0))
        inv = [eye - m for m in lmat]
        pw = [_dot3(m, m) for m in lmat]
        span = 2
        while span < cs:
            mm = _dot3 if span == 2 else _dot1
            inv = [x + mm(x, p) for x, p in zip(inv, pw)]
            span *= 2
            if span < cs:
                pw = [_dot1(p, p) for p in pw]
        inv_bf = [x.astype(BF16) for x in inv]
        for i, ((s, _), rw) in enumerate(zip(probs, rows)):
            u_ref[s, rw, :] = _dot(inv_bf[i], vb_ref[s, rw, :])
            w_ref[s, rw, :] = _dot(inv_bf[i], kbg_ref[s, rw, :]).astype(BF16)
            a_ref[s, rw, :] = jnp.where(tril, _dot_nt(qs_ref[s, rw, :], k_ref_s[s, rw, :]) * decay[i], 0.0).astype(BF16)
        return 0

    lax.fori_loop(0, t // (cs * GDN_CHUNKS_PER_TRIP), prep, 0)

    def scan(n, states):
        r0 = pl.multiple_of(n * cs, cs)
        rows = pl.ds(r0, cs)
        s_bf = [st.astype(BF16) for st in states]
        v_bf = [(u_ref[s, rows, :] - _dot(w_ref[s, rows, :], s_bf[s])).astype(BF16) for s in range(nh)]
        new = [states[s] * jnp.exp(gl_ref[s, pl.ds(r0, 1), :]) + _dot_tn(kd_ref[s, rows, :], v_bf[s])
               for s in range(nh)]
        for s in range(nh):
            o_ref[0, rows, s * dk:(s + 1) * dk] = _dot(qg_ref[s, rows, :], s_bf[s]) + _dot(a_ref[s, rows, :], v_bf[s])
        return tuple(new)

    final = lax.fori_loop(0, t // cs, scan, tuple(state_ref[s] for s in range(nh)))
    for s in range(nh):
        state_ref[s] = final[s]


def gdn_core(proj, conv_w, gb):
    b, t, _ = proj.shape
    ts = t // GDN_SEGMENTS if t % (GDN_SEGMENTS * GDN_CHUNK * GDN_CHUNKS_PER_TRIP) == 0 else t
    seg_chunks = ts // GDN_CHUNK
    nh = GDN_HEADS_PER_STEP
    wide = nh * GDN_HEAD_DIM
    per = GDN_WIDTH // wide
    g_rows = jnp.swapaxes(gb[:, :, G_CUM:G_CUM + GDN_HEADS], 1, 2).reshape(b, GDN_HEADS, t // GDN_CHUNK, GDN_CHUNK)
    sect = lambda k: pl.BlockSpec((1, ts, wide), lambda i, h, s: (i, s, k * per + h))
    taps = lambda k: pl.BlockSpec((CONV_WIDTH, wide), lambda i, h, s: (0, k * per + h))
    bf = lambda w: pltpu.VMEM((nh, ts, w), BF16)
    return pl.pallas_call(
        _gdn_body,
        grid=(b, per, t // ts),
        in_specs=[sect(0), sect(1), sect(2), taps(0), taps(1), taps(2),
                  pl.BlockSpec((1, ts, LANES), lambda i, h, s: (i, s, 0)),
                  pl.BlockSpec((1, nh, seg_chunks, GDN_CHUNK), lambda i, h, s: (i, h, s, 0))],
        out_specs=pl.BlockSpec((1, ts, wide), lambda i, h, s: (i, s, h)),
        out_shape=jax.ShapeDtypeStruct((b, t, GDN_WIDTH), F32),
        scratch_shapes=[pltpu.VMEM((3, CONV_HALO + ts, wide), F32), pltpu.VMEM((nh, GDN_HEAD_DIM, GDN_HEAD_DIM), F32)]
                       + [pltpu.VMEM((nh, ts, LANES), F32)] * 2 + [bf(GDN_HEAD_DIM)] * 7
                       + [pltpu.VMEM((nh, ts, GDN_HEAD_DIM), F32), bf(GDN_HEAD_DIM), bf(GDN_CHUNK)],
        compiler_params=_params("parallel", "parallel", "arbitrary"),
    )(proj, proj, proj, conv_w, conv_w, conv_w, gb, g_rows)


def _rope_tables(pos):
    half = HEAD_DIM // 2
    inv_freq = ROPE_THETA ** (-jnp.arange(half, dtype=F32) / half)
    ang = pos.astype(F32)[:, None] * inv_freq
    cos = jnp.cos(ang)
    sin = jnp.sin(ang)
    cos_t = jnp.tile(jnp.concatenate([cos, cos], axis=-1), (1, LANES // HEAD_DIM))
    sin_t = jnp.tile(jnp.concatenate([-sin, sin], axis=-1), (1, LANES // HEAD_DIM))
    return cos_t, sin_t


def _block_diag_ones(width, seg):
    idx = np.arange(width) // seg
    return jnp.asarray((idx[:, None] == idx[None, :]).astype(np.float32), dtype=BF16)


def _pad_cols(w, width):
    return jnp.pad(w, ((0, 0), (0, width - w.shape[1])))


def _even_layer(x2, b, t, norm_gain, w_in, b_gate, b_forget, cmp_pe, cmp_w1, cmp_w2, nsa_gain, fox_gain, w_out):
    d = x2.shape[1]
    q_perm = np.concatenate([np.arange(HEAD_DIM) + (kvh * NSA_GROUP + g) * HEAD_DIM
                             for g in range(NSA_GROUP) for kvh in range(NSA_KV_HEADS)])
    o_gate = NSA_Q_W + 6 * NSA_KV_W
    o_fox = o_gate + NSA_GATE_W
    w_re = jnp.concatenate([w_in[:, q_perm], w_in[:, NSA_Q_W:o_gate], w_in[:, o_fox:o_fox + 3 * FOX_W],
                            w_in[:, o_gate:o_fox], w_in[:, o_fox + 3 * FOX_W:]], axis=1)
    proj = norm_matmul(x2, norm_gain, _pad_cols(w_re, EVEN_W).astype(BF16)).reshape(b, t, EVEN_W)

    cos, sin = _rope_tables(jnp.arange(t))
    tile = lambda g, n: jnp.tile(g, n).reshape(1, -1)
    bias = jnp.pad(jnp.concatenate([b_gate, b_forget]), (0, LANES - NSA_GATE_W - FOX_HEADS)).reshape(1, LANES)
    bd = _block_diag_ones(FOX_W, HEAD_DIM)
    (qa, ks, kw, vs, vw, kc_raw, vc_raw, qb, kb, vf, gates, cum) = even_prep(
        proj, cos, sin, tile(nsa_gain[0], NSA_HEADS), tile(nsa_gain[2], NSA_KV_HEADS), tile(nsa_gain[3], NSA_KV_HEADS),
        tile(fox_gain[0], FOX_HEADS), tile(fox_gain[1], FOX_HEADS), bias, bd)

    n_str = t // CMP_STRIDE
    half = CMP_BLOCK // 2
    eye2 = jnp.eye(NSA_KV_HEADS, dtype=F32)
    pe = jnp.tile(cmp_pe[:, :, None, :], (1, 1, NSA_KV_HEADS, 1)).reshape(2, 2, 1, half * NSA_KV_W)
    w1 = jnp.einsum('ilde,hg->ilhdge', cmp_w1, eye2).reshape(2, 2, half * NSA_KV_W, NSA_KV_W).astype(BF16)
    w2 = jnp.einsum('ide,hg->ihdge', cmp_w2, eye2).reshape(2, NSA_KV_W, NSA_KV_W).astype(BF16)
    cos_c, sin_c = _rope_tables(jnp.arange(n_str) * CMP_STRIDE + (CMP_BLOCK - 1))
    kc, vc = compress(kc_raw.reshape(b, n_str, CMP_STRIDE * NSA_KV_W), vc_raw.reshape(b, n_str, CMP_STRIDE * NSA_KV_W),
                      pe, w1, w2, tile(nsa_gain[1], NSA_KV_HEADS), cos_c, sin_c, _block_diag_ones(LANES, HEAD_DIM))

    n_sel = t // SEL_BLOCK
    cs = np.arange(n_str)[:, None] * CMP_STRIDE
    ss = np.arange(n_sel)[None, :] * SEL_BLOCK
    overlap = np.clip(np.minimum(cs + CMP_BLOCK, ss + SEL_BLOCK) - np.maximum(cs, ss), 0, None) / CMP_BLOCK
    overlap[(t - CMP_BLOCK) // CMP_STRIDE + 1:] = 0.0
    ovt = jnp.asarray(overlap.T.astype(np.float32), dtype=BF16)
    o_a = nsa_attention(qa, kc, vc, ks, vs, kw, vw, gates, ovt)
    o_b = fox_attention(qb, kb, vf, cum)

    wa = w_out[:NSA_Q_W][q_perm].astype(BF16)
    wb = w_out[NSA_Q_W:].astype(BF16)
    return even_out(x2, o_a.reshape(b * t, NSA_Q_W), o_b.reshape(b * t, FOX_W), wa, wb)


def _odd_layer(x2, b, t, norm_gain, w_in, conv_w, a_log, dt_bias, gdn_gain, w_out):
    proj = norm_matmul(x2, norm_gain, _pad_cols(w_in, ODD_W).astype(BF16)).reshape(b, t, ODD_W)
    pad8 = lambda v: jnp.pad(v, (0, LANES - GDN_HEADS)).reshape(1, LANES)
    gb = gdn_gates(proj, pad8(a_log), pad8(dt_bias))
    o = gdn_core(proj, conv_w, gb)
    return odd_out(x2, o.reshape(b * t, GDN_WIDTH), proj.reshape(b * t, ODD_W), gdn_gain.reshape(1, GDN_HEAD_DIM),
                   w_out.astype(BF16))


def _moe_layer(x2, gain, w_rg, b_rg, w_re, b_re, w_ein_bf, w_eout_bf, layer):
    d = x2.shape[1]
    wr = _pad_cols(jnp.concatenate([w_rg, w_re], axis=1), LANES)
    wr_hi = wr.astype(BF16)
    wr_lo = (wr - wr_hi.astype(F32)).astype(BF16)
    br = jnp.pad(jnp.concatenate([b_rg, b_re]), (0, LANES - N_GROUPS - N_EXPERTS)).reshape(1, LANES)
    return moe(x2, gain, wr_hi, wr_lo, br, w_ein_bf, w_eout_bf, layer)


def kernel(x, norm_mix, norm_ffn, w_in_even, b_nsa_gate, b_forget, cmp_pe, cmp_w1, cmp_w2, nsa_qk_gain, fox_qk_gain,
           w_out_even, w_in_odd, conv_w, a_log, dt_bias, gdn_norm_gain, w_out_odd, w_router_group, b_router_group,
           w_router_expert, b_router_expert, w_expert_in, w_expert_out):
    b, t, d = x.shape
    x2 = x.reshape(b * t, d)
    w_ein_bf = w_expert_in.astype(BF16)
    w_eout_bf = w_expert_out.astype(BF16)
    for layer in range(norm_mix.shape[0]):
        i = layer // 2
        if layer % 2 == 0:
            x2 = _even_layer(x2, b, t, norm_mix[layer], w_in_even[i], b_nsa_gate[i], b_forget[i], cmp_pe[i], cmp_w1[i],
                             cmp_w2[i], nsa_qk_gain[i], fox_qk_gain[i], w_out_even[i])
        else:
            x2 = _odd_layer(x2, b, t, norm_mix[layer], w_in_odd[i], conv_w[i], a_log[i], dt_bias[i], gdn_norm_gain[i],
                            w_out_odd[i])
        x2 = _moe_layer(x2, norm_ffn[layer], w_router_group[layer], b_router_group[layer], w_router_expert[layer],
                        b_router_expert[layer], w_ein_bf, w_eout_bf, layer)
    return x2.reshape(b, t, d)
```

```python
import functools

import numpy as np
import jax
import jax.numpy as jnp
from jax import lax
from jax.experimental import pallas as pl
from jax.experimental.pallas import tpu as pltpu

F32 = jnp.float32
BF16 = jnp.bfloat16

HEAD_DIM = 64
ROPE_THETA = 10000.0
NSA_HEADS = 8
NSA_KV_HEADS = 2
NSA_GROUP = NSA_HEADS // NSA_KV_HEADS
CMP_BLOCK = 32
CMP_STRIDE = 16
SEL_BLOCK = 64
SEL_TOPK = 8
WINDOW = 256
FOX_HEADS = 8
GDN_HEADS = 8
GDN_HEAD_DIM = 128
GDN_WIDTH = GDN_HEADS * GDN_HEAD_DIM
CONV_WIDTH = 4
GDN_CHUNK = 64
N_GROUPS = 4
EXPERTS_PER_GROUP = 4
N_EXPERTS = N_GROUPS * EXPERTS_PER_GROUP
EXPERT_FF = 256
NORM_EPS = 1e-6
NEG_INF = -1e30
FORCE_SCORE = 1e9

LANES = 128
LOG2E = 1.4426950408889634
SUM_ROWS = 16
NSA_Q_W = NSA_HEADS * HEAD_DIM
NSA_KV_W = NSA_KV_HEADS * HEAD_DIM
NSA_GATE_W = 3 * NSA_HEADS
FOX_W = FOX_HEADS * HEAD_DIM
C_QN = 0
C_KC, C_VC, C_KS, C_VS, C_KW, C_VW = (NSA_Q_W + i * NSA_KV_W for i in range(6))
C_QF = NSA_Q_W + 6 * NSA_KV_W
C_KF = C_QF + FOX_W
C_VF = C_KF + FOX_W
C_MISC = C_VF + FOX_W
EVEN_W = C_MISC + LANES
MISC_F = NSA_GATE_W
C_AB = 4 * GDN_WIDTH
ODD_W = C_AB + LANES

VMEM_LIMIT = 56 * 1024 * 1024


def _params(*sem):
    return pltpu.CompilerParams(dimension_semantics=sem, vmem_limit_bytes=VMEM_LIMIT)


def _dot(a, b):
    return jnp.dot(a, b, preferred_element_type=F32)


def _dot_nt(a, b):
    return lax.dot_general(a, b, (((1,), (1,)), ((), ())), preferred_element_type=F32)


def _dot_tn(a, b):
    return lax.dot_general(a, b, (((0,), (0,)), ((), ())), preferred_element_type=F32)


def _split2(x):
    hi = x.astype(BF16)
    return hi, (x - hi.astype(F32)).astype(BF16)


def _split3(x):
    hi = x.astype(BF16)
    r = x - hi.astype(F32)
    mid = r.astype(BF16)
    return hi, mid, (r - mid.astype(F32)).astype(BF16)


def _sigmoid(z):
    return 1.0 / (1.0 + jnp.exp(-z))


def _silu(z):
    return z * _sigmoid(z)


def _full(shape):
    nd = len(shape)
    return pl.BlockSpec(shape, lambda *_: (0,) * nd)


def _norm_matmul_body(x_ref, g_ref, w_ref, o_ref):
    x = x_ref[...]
    ms = jnp.mean(x * x, axis=-1, keepdims=True)
    h = (x * lax.rsqrt(ms + NORM_EPS) * g_ref[...]).astype(BF16)
    o_ref[...] = _dot(h, w_ref[...])


def norm_matmul(x2, gain, w_bf, tm=512):
    n, d = x2.shape
    wp = w_bf.shape[1]
    return pl.pallas_call(
        _norm_matmul_body,
        grid=(n // tm,),
        in_specs=[pl.BlockSpec((tm, d), lambda i: (i, 0)), _full((1, d)), _full((d, wp))],
        out_specs=pl.BlockSpec((tm, wp), lambda i: (i, 0)),
        out_shape=jax.ShapeDtypeStruct((n, wp), F32),
        compiler_params=_params("parallel"),
    )(x2, gain.reshape(1, d), w_bf)


def _head_rms(x, bd, gain):
    hi, lo = _split2(x * x)
    w = x.shape[1]
    ssum = _dot(hi, bd[:w, :w]) + _dot(lo, bd[:w, :w])
    return x * lax.rsqrt(ssum * (1.0 / HEAD_DIM) + NORM_EPS) * gain


def _rope(x, cos, sin_signed, first_half):
    fwd = pltpu.roll(x, LANES - HEAD_DIM // 2, 1)
    bwd = pltpu.roll(x, HEAD_DIM // 2, 1)
    return x * cos + jnp.where(first_half, fwd, bwd) * sin_signed


def _even_prep_body(p_ref, cos_ref, sin_ref, gq_ref, gks_ref, gkw_ref, gfq_ref, gfk_ref, bias_ref, bd_ref,
                    qa_ref, ks_ref, kw_ref, vs_ref, vw_ref, kc_ref, vc_ref, qb_ref, kb_ref, vf_ref,
                    gate_ref, cum_ref, carry_ref, stage_ref):
    tr = p_ref.shape[1]
    bd = bd_ref[...]
    cos = cos_ref[...]
    sin = sin_ref[...]
    lane = lax.broadcasted_iota(jnp.int32, (1, LANES), 1)
    first_half = (lane % HEAD_DIM) < (HEAD_DIM // 2)
    scale = HEAD_DIM ** -0.5 * LOG2E

    qn = _head_rms(p_ref[0, :, C_QN:C_QN + NSA_Q_W], bd, gq_ref[...])
    for c in range(NSA_Q_W // LANES):
        sl = slice(c * LANES, (c + 1) * LANES)
        qa_ref[0, sl, :] = (_rope(qn[:, sl], cos, sin, first_half) * scale).T.astype(BF16)
    ks = _head_rms(p_ref[0, :, C_KS:C_KS + NSA_KV_W], bd, gks_ref[...])
    ks_ref[0] = _rope(ks, cos, sin, first_half).astype(BF16)
    kw = _head_rms(p_ref[0, :, C_KW:C_KW + NSA_KV_W], bd, gkw_ref[...])
    kw_ref[0] = _rope(kw, cos, sin, first_half).astype(BF16)
    vs_ref[0] = p_ref[0, :, C_VS:C_VS + NSA_KV_W].T.astype(BF16)
    vw_ref[0] = p_ref[0, :, C_VW:C_VW + NSA_KV_W].T.astype(BF16)
    stage_ref[0] = p_ref[0, :, C_KC:C_KC + NSA_KV_W]
    stage_ref[1] = p_ref[0, :, C_VC:C_VC + NSA_KV_W]
    for l in range(CMP_STRIDE):
        rows = pl.ds(l, tr // CMP_STRIDE, stride=CMP_STRIDE)
        kc_ref[0, :, l * NSA_KV_W:(l + 1) * NSA_KV_W] = stage_ref[0, rows, :]
        vc_ref[0, :, l * NSA_KV_W:(l + 1) * NSA_KV_W] = stage_ref[1, rows, :]

    qb = _head_rms(p_ref[0, :, C_QF:C_QF + FOX_W], bd, gfq_ref[...]) * scale
    kb_ref[0] = _head_rms(p_ref[0, :, C_KF:C_KF + FOX_W], bd, gfk_ref[...]).astype(BF16)
    for c in range(FOX_W // LANES):
        sl = slice(c * LANES, (c + 1) * LANES)
        qb_ref[0, sl, :] = qb[:, sl].T.astype(BF16)
        vf_ref[0, sl, :] = p_ref[0, :, C_VF + c * LANES:C_VF + (c + 1) * LANES].T.astype(BF16)

    z = p_ref[0, :, C_MISC:C_MISC + LANES] + bias_ref[...]
    gate_ref[0] = _sigmoid(z).T
    logf = jnp.minimum(z, 0.0) - jnp.log1p(jnp.exp(-jnp.abs(z)))

    @pl.when(pl.program_id(1) == 0)
    def _():
        carry_ref[...] = jnp.zeros_like(carry_ref)

    row = lax.broadcasted_iota(jnp.int32, (tr, tr), 0)
    col = lax.broadcasted_iota(jnp.int32, (tr, tr), 1)
    tril = jnp.where(row >= col, 1.0, 0.0).astype(BF16)
    hi, mid, lo = _split3(logf)
    cum = _dot(tril, hi) + _dot(tril, mid) + _dot(tril, lo) + carry_ref[...]
    cum_ref[0] = cum
    carry_ref[...] = cum[tr - 1:tr, :]


def even_prep(proj, cos, sin, gq, gks, gkw, gfq, gfk, bias, bd, tr=256):
    b, t, _ = proj.shape
    row = lambda w: pl.BlockSpec((1, tr, w), lambda i, j: (i, j, 0))
    tab = pl.BlockSpec((tr, LANES), lambda i, j: (j, 0))
    shp = lambda w, dt: jax.ShapeDtypeStruct((b, t, w), dt)
    col = lambda w: pl.BlockSpec((1, w, tr), lambda i, j: (i, 0, j))
    shp_t = lambda w, dt: jax.ShapeDtypeStruct((b, w, t), dt)
    strd = pl.BlockSpec((1, tr // CMP_STRIDE, CMP_STRIDE * NSA_KV_W), lambda i, j: (i, j, 0))
    strd_shape = jax.ShapeDtypeStruct((b, t // CMP_STRIDE, CMP_STRIDE * NSA_KV_W), F32)
    return pl.pallas_call(
        _even_prep_body,
        grid=(b, t // tr),
        in_specs=[row(EVEN_W), tab, tab, _full((1, NSA_Q_W)), _full((1, LANES)), _full((1, LANES)),
                  _full((1, FOX_W)), _full((1, FOX_W)), _full((1, LANES)), _full((FOX_W, FOX_W))],
        out_specs=[col(NSA_Q_W), row(LANES), row(LANES), col(LANES), col(LANES), strd, strd,
                   col(FOX_W), row(FOX_W), col(FOX_W), col(LANES), row(LANES)],
        out_shape=[shp_t(NSA_Q_W, BF16), shp(LANES, BF16), shp(LANES, BF16), shp_t(LANES, BF16), shp_t(LANES, BF16),
                   strd_shape, strd_shape, shp_t(FOX_W, BF16), shp(FOX_W, BF16), shp_t(FOX_W, BF16),
                   shp_t(LANES, F32), shp(LANES, F32)],
        scratch_shapes=[pltpu.VMEM((1, LANES), F32), pltpu.VMEM((2, tr, LANES), F32)],
        compiler_params=_params("parallel", "arbitrary"),
    )(proj, cos, sin, gq, gks, gkw, gfq, gfk, bias, bd)


def _gelu_tanh(x):
    return 0.5 * x * (1.0 + jnp.tanh(np.sqrt(2.0 / np.pi).astype(np.float32) * (x + 0.044715 * (x * x * x))))


def _compress_body(xk_ref, xv_ref, pe_ref, w1_ref, w2_ref, gk_ref, cos_ref, sin_ref, bd_ref, kc_ref, vc_ref):
    n = xk_ref.shape[1]
    lane = lax.broadcasted_iota(jnp.int32, (1, LANES), 1)
    first_half = (lane % HEAD_DIM) < (HEAD_DIM // 2)

    def mlp(x_ref, i):
        x = x_ref[0]
        nxt = pltpu.roll(x, n - 1, 0)
        xa = (x + pe_ref[i, 0]).astype(BF16)
        xb = (nxt + pe_ref[i, 1]).astype(BF16)
        h = _dot(xa, w1_ref[i, 0]) + _dot(xb, w1_ref[i, 1])
        return _dot(_gelu_tanh(h).astype(BF16), w2_ref[i])

    kc = _head_rms(mlp(xk_ref, 0), bd_ref[...], gk_ref[...])
    kc_ref[0] = _rope(kc, cos_ref[...], sin_ref[...], first_half).astype(BF16)
    vc_ref[0] = mlp(xv_ref, 1).T.astype(BF16)


def compress(xk, xv, pe, w1, w2, gk, cos_c, sin_c, bd):
    b, n, w = xk.shape
    blk = pl.BlockSpec((1, n, w), lambda i: (i, 0, 0))
    out = pl.BlockSpec((1, n, LANES), lambda i: (i, 0, 0))
    return pl.pallas_call(
        _compress_body,
        grid=(b,),
        in_specs=[blk, blk, _full(pe.shape), _full(w1.shape), _full(w2.shape), _full((1, LANES)),
                  _full((n, LANES)), _full((n, LANES)), _full((LANES, LANES))],
        out_specs=[out, pl.BlockSpec((1, LANES, n), lambda i: (i, 0, 0))],
        out_shape=[jax.ShapeDtypeStruct((b, n, LANES), BF16), jax.ShapeDtypeStruct((b, LANES, n), BF16)],
        compiler_params=_params("parallel"),
    )(xk, xv, pe, w1, w2, gk, cos_c, sin_c, bd)


def _flash_step(s_ref, p_ref, acc_ref, v_blk, m_i, l_i, adjust, first=False):
    n_ch = acc_ref.shape[0]
    al, ms = [], []
    for cg in range(s_ref.shape[1] // LANES):
        sl = slice(cg * LANES, (cg + 1) * LANES)
        s = adjust(s_ref[:, sl], cg)
        m_new = jnp.maximum(m_i[:, sl], jnp.max(s, axis=0, keepdims=True))
        p_ref[:, sl] = jnp.exp2(s - m_new).astype(BF16)
        al.append(jnp.exp2(m_i[:, sl] - m_new))
        ms.append(m_new)
    cat = lambda xs: jnp.concatenate(xs, axis=1)
    alpha = cat(al)
    pv = _dot(v_blk, p_ref[...])
    acc_ref[...] = pv[:n_ch] if first else alpha * acc_ref[...] + pv[:n_ch]
    return cat(ms), alpha * l_i + pv[n_ch:n_ch + 1]


NSA_KEYS_PER_QUERY_BLOCK = 2


def _nsa_body(q_ref, kc_ref, vc_ref, ks_ref, vs_ref, kw_ref, vw_ref, gate_ref, ovt_ref, o_ref,
              sel_ref, s0_ref, s1_ref, p_ref, acc_ref, *, k_top):
    tq = q_ref.shape[2]
    t_all = ks_ref.shape[1]
    n_cmp = kc_ref.shape[1]
    n_sel = ovt_ref.shape[0]
    g_n = NSA_GROUP
    c = pl.program_id(1)
    t0 = c * tq
    chan = lax.broadcasted_iota(jnp.int32, (LANES, 1), 0)
    tlane = t0 + lax.broadcasted_iota(jnp.int32, (1, tq), 1)
    gates = gate_ref[0]

    nrow = lax.broadcasted_iota(jnp.int32, (n_cmp, 1), 0)
    valid_c = (nrow * CMP_STRIDE + (CMP_BLOCK - 1)) <= tlane
    jrow = lax.broadcasted_iota(jnp.int32, (n_sel, tq), 0)
    jrow_f = jrow.astype(F32)
    cur = tlane // SEL_BLOCK
    forced = (jrow == 0) | (jrow == cur) | (jrow == cur - 1)
    future = jrow * SEL_BLOCK > tlane
    tk = NSA_KEYS_PER_QUERY_BLOCK * tq
    krow = lax.broadcasted_iota(jnp.int32, (tk, 1), 0)
    per_blk = tk // SEL_BLOCK
    w_len = tq + WINDOW
    w_start = pl.multiple_of(jnp.clip(t0 - WINDOW, 0, t_all - w_len), LANES)
    wrow = w_start + lax.broadcasted_iota(jnp.int32, (w_len, 1), 0)
    valid_w = (wrow <= tlane) & (wrow > tlane - WINDOW)

    heads = [(kvh, g) for kvh in range(NSA_KV_HEADS) for g in range(g_n)]
    zero_half = jnp.zeros((HEAD_DIM, tq), BF16)

    def on_kv_rows(h, kvh):
        blk = q_ref[0, h * HEAD_DIM:(h + 1) * HEAD_DIM, :]
        return jnp.concatenate([blk, zero_half] if kvh == 0 else [zero_half, blk], axis=0)

    qst = jnp.concatenate([on_kv_rows(h, kvh) for h, (kvh, _) in enumerate(heads)], axis=1)
    n_col = len(heads) * tq

    def softmax_cols(s, ok, guard):
        outs = []
        for cg in range(len(heads)):
            sc = jnp.where(ok, s[:, cg * tq:(cg + 1) * tq], NEG_INF)
            e = jnp.exp2(sc - jnp.max(sc, axis=0, keepdims=True))
            if guard:
                e = jnp.where(ok, e, 0.0)
            den = jnp.sum(e, axis=0, keepdims=True)
            outs.append(e * (1.0 / (jnp.where(den > 0.0, den, 1.0) if guard else den)))
        return outs

    p_c = softmax_cols(_dot(kc_ref[0], qst), valid_c, guard=True)
    o_cmp = _dot(vc_ref[0], jnp.concatenate(p_c, axis=1).astype(BF16))

    for kvh in range(NSA_KV_HEADS):
        p_sum = p_c[kvh * g_n]
        for g in range(1, g_n):
            p_sum = p_sum + p_c[kvh * g_n + g]
        p_hi, p_lo = _split2(p_sum)
        imp_t = _dot(ovt_ref[...], p_hi) + _dot(ovt_ref[...], p_lo)
        val = jnp.where(forced, FORCE_SCORE, jnp.where(future, NEG_INF, imp_t))
        sel_t = jnp.zeros((n_sel, tq), F32)
        for _ in range(k_top):
            m = jnp.max(val, axis=0, keepdims=True)
            first = jnp.min(jnp.where(val == m, jrow_f, float(n_sel)), axis=0, keepdims=True)
            pick = jrow_f == first
            sel_t = jnp.where(pick, 1.0, sel_t)
            val = jnp.where(pick, -jnp.inf, val)
        sel_ref[kvh] = sel_t

    p_w = softmax_cols(_dot(kw_ref[0, pl.ds(w_start, w_len), :], qst), valid_w, guard=False)
    o_win = _dot(vw_ref[0, :, pl.ds(w_start, w_len)], jnp.concatenate(p_w, axis=1).astype(BF16))

    def put_scores(buf, kb):
        k0 = pl.multiple_of(jnp.minimum(kb * tk, t_all - tk), tk)
        buf[...] = _dot(ks_ref[0, pl.ds(k0, tk), :], qst)

    def half_step(buf, kb, m_i, l_i):
        k0 = pl.multiple_of(jnp.minimum(kb * tk, t_all - tk), tk)
        causal = (kb * tk + krow) <= tlane
        ok = [causal & (jnp.concatenate([jnp.broadcast_to(sel_ref[kvh, pl.ds(k0 // SEL_BLOCK + r, 1), :],
                                                          (SEL_BLOCK, tq)) for r in range(per_blk)], axis=0) > 0.5)
              for kvh in range(NSA_KV_HEADS)]
        adjust = lambda s_cols, cg: jnp.where(ok[cg // g_n], s_cols, NEG_INF)
        v_blk = jnp.concatenate([vs_ref[0, :, pl.ds(k0, tk)], jnp.ones((SUM_ROWS, tk), BF16)], axis=0)
        return _flash_step(buf, p_ref, acc_ref, v_blk, m_i, l_i, adjust)

    def sel_trip(j, carry):
        put_scores(s1_ref, 2 * j + 1)
        carry = half_step(s0_ref, 2 * j, *carry)
        put_scores(s0_ref, 2 * j + 2)
        return half_step(s1_ref, 2 * j + 1, *carry)

    put_scores(s0_ref, 0)
    acc_ref[...] = jnp.zeros_like(acc_ref)
    init = (jnp.full((1, n_col), NEG_INF, F32), jnp.zeros((1, n_col), F32))
    n_blocks = (t0 + tq + tk - 1) // tk
    _, l_s = lax.fori_loop(0, (n_blocks + 1) // 2, sel_trip, init)
    o_slc = acc_ref[...] * (1.0 / l_s)

    gated = []
    for h, (kvh, _) in enumerate(heads):
        cols = slice(h * tq, (h + 1) * tq)
        rows = slice(kvh * HEAD_DIM, (kvh + 1) * HEAD_DIM)
        gated.append(gates[3 * h:3 * h + 1] * o_cmp[rows, cols] + gates[3 * h + 1:3 * h + 2] * o_slc[rows, cols]
                     + gates[3 * h + 2:3 * h + 3] * o_win[rows, cols])
    for j in range(NSA_HEADS * HEAD_DIM // LANES):
        pair = jnp.concatenate(gated[2 * j:2 * j + 2], axis=0)
        o_ref[0, :, j * LANES:(j + 1) * LANES] = pair.T.astype(BF16)


def nsa_attention(qa_t, kc, vc_t, ks, vs_t, kw, vw_t, gates_t, ovt, tq=LANES):
    b, _, t = qa_t.shape
    n_cmp = kc.shape[1]
    n_sel = ovt.shape[0]
    k_top = min(SEL_TOPK, n_sel)
    tk = NSA_KEYS_PER_QUERY_BLOCK * tq
    tok = lambda n: pl.BlockSpec((1, n, LANES), lambda i, j: (i, 0, 0))
    chn = lambda n: pl.BlockSpec((1, LANES, n), lambda i, j: (i, 0, 0))
    return pl.pallas_call(
        functools.partial(_nsa_body, k_top=k_top),
        grid=(b, t // tq),
        in_specs=[pl.BlockSpec((1, NSA_Q_W, tq), lambda i, j: (i, 0, j)), tok(n_cmp), chn(n_cmp), tok(t), chn(t),
                  tok(t), chn(t), pl.BlockSpec((1, LANES, tq), lambda i, j: (i, 0, j)), _full(ovt.shape)],
        out_specs=pl.BlockSpec((1, tq, NSA_Q_W), lambda i, j: (i, j, 0)),
        out_shape=jax.ShapeDtypeStruct((b, t, NSA_Q_W), BF16),
        scratch_shapes=[pltpu.VMEM((NSA_KV_HEADS, n_sel, tq), F32), pltpu.VMEM((tk, NSA_HEADS * tq), F32),
                        pltpu.VMEM((tk, NSA_HEADS * tq), F32), pltpu.VMEM((tk, NSA_HEADS * tq), BF16),
                        pltpu.VMEM((LANES, NSA_HEADS * tq), F32)],
        compiler_params=_params("parallel", "arbitrary"),
    )(qa_t, kc, vc_t, ks, vs_t, kw, vw_t, gates_t, ovt)


FOX_KEYS_PER_QUERY_BLOCK = 2


def _fox_body(q_ref, k_ref, v_ref, cum_ref, o_ref, ck_ref, s0_ref, s1_ref, p_ref, acc_ref, *, tq):
    t = k_ref.shape[1]
    tk = FOX_KEYS_PER_QUERY_BLOCK * tq
    pair = pl.program_id(1)

    hi, mid, lo = _split3(cum_ref[0])
    pick_row = lax.broadcasted_iota(jnp.int32, (LANES, LANES), 0)
    for h in range(2):
        sel = jnp.where(pick_row == MISC_F + 2 * pair + h, 1.0, 0.0).astype(BF16)
        ck_ref[h] = (_dot(hi, sel) + _dot(mid, sel) + _dot(lo, sel)) * LOG2E

    chan = lax.broadcasted_iota(jnp.int32, (LANES, 1), 0)
    first_head = chan < HEAD_DIM
    krow = lax.broadcasted_iota(jnp.int32, (tk, 1), 0)
    qlane = lax.broadcasted_iota(jnp.int32, (1, tq), 1)
    reps = tq // LANES
    bufs = (s0_ref, s1_ref)
    blocks = [(i, kb) for i in range(t // tq) for kb in range((i * tq) // tk + 1)]
    q_cache = {}

    def q_pair(i):
        if i not in q_cache:
            q = q_ref[0, :, i * tq:(i + 1) * tq]
            q_cache[i] = jnp.concatenate([jnp.where(first_head, q, 0), jnp.where(first_head, 0, q)], axis=1)
        return q_cache[i]

    def put_scores(n):
        i, kb = blocks[n]
        bufs[n % 2][...] = _dot(k_ref[0, kb * tk:(kb + 1) * tk, :], q_pair(i))

    put_scores(0)
    m_i = l_i = None
    for n, (i, kb) in enumerate(blocks):
        if n + 1 < len(blocks):
            put_scores(n + 1)
        last = kb == (i * tq) // tk
        ok = ((kb * tk + krow) <= (i * tq + qlane)) if last else None

        def adjust(s_cols, cg, kb=kb, last=last, ok=ok):
            s_cols = s_cols - ck_ref[cg // reps, kb * tk:(kb + 1) * tk, :]
            return jnp.where(ok[:, (cg % reps) * LANES:(cg % reps + 1) * LANES], s_cols, NEG_INF) if last else s_cols

        if kb == 0:
            m_i = jnp.full((1, 2 * tq), NEG_INF, F32)
            l_i = jnp.zeros((1, 2 * tq), F32)
        v_blk = jnp.concatenate([v_ref[0, :, kb * tk:(kb + 1) * tk], jnp.ones((SUM_ROWS, tk), BF16)], axis=0)
        m_i, l_i = _flash_step(bufs[n % 2], p_ref, acc_ref, v_blk, m_i, l_i, adjust, first=kb == 0)
        if last:
            o = acc_ref[...] * (1.0 / l_i)
            o_ref[0, i * tq:(i + 1) * tq, :] = jnp.where(first_head, o[:, :tq], o[:, tq:]).T.astype(BF16)


def fox_attention(qb_t, kb, vf_t, cum, tq=256):
    b, w, t = qb_t.shape
    pairs = w // LANES
    tk = FOX_KEYS_PER_QUERY_BLOCK * tq
    return pl.pallas_call(
        functools.partial(_fox_body, tq=tq),
        grid=(b, pairs),
        in_specs=[pl.BlockSpec((1, LANES, t), lambda i, p: (i, p, 0)),
                  pl.BlockSpec((1, t, LANES), lambda i, p: (i, 0, p)),
                  pl.BlockSpec((1, LANES, t), lambda i, p: (i, p, 0)),
                  pl.BlockSpec((1, t, LANES), lambda i, p: (i, 0, 0))],
        out_specs=pl.BlockSpec((1, t, LANES), lambda i, p: (i, 0, p)),
        out_shape=jax.ShapeDtypeStruct((b, t, w), BF16),
        scratch_shapes=[pltpu.VMEM((2, t, LANES), F32), pltpu.VMEM((tk, 2 * tq), F32), pltpu.VMEM((tk, 2 * tq), F32),
                        pltpu.VMEM((tk, 2 * tq), BF16), pltpu.VMEM((LANES, 2 * tq), F32)],
        compiler_params=_params("parallel", "arbitrary"),
    )(qb_t, kb, vf_t, cum)


def _even_out_body(x_ref, oa_ref, ob_ref, w_ref, o_ref):
    wa = oa_ref.shape[1]
    o_ref[...] = x_ref[...] + _dot(oa_ref[...], w_ref[:wa, :]) + _dot(ob_ref[...], w_ref[wa:, :])


def even_out(x2, oa, ob, w_bf, tm=512):
    n, d = x2.shape
    row = lambda w: pl.BlockSpec((tm, w), lambda i: (i, 0))
    return pl.pallas_call(
        _even_out_body,
        grid=(n // tm,),
        in_specs=[row(d), row(oa.shape[1]), row(ob.shape[1]), _full(w_bf.shape)],
        out_specs=row(d),
        out_shape=jax.ShapeDtypeStruct((n, d), F32),
        compiler_params=_params("parallel"),
    )(x2, oa, ob, w_bf)


def _odd_out_body(x_ref, o_ref_in, z_ref, g_ref, w_ref, out_ref):
    gain = g_ref[...]
    parts = []
    for h in range(GDN_HEADS):
        sl = slice(h * GDN_HEAD_DIM, (h + 1) * GDN_HEAD_DIM)
        o = o_ref_in[:, sl]
        y = o * lax.rsqrt(jnp.mean(o * o, axis=-1, keepdims=True) + NORM_EPS) * gain
        parts.append((y * _silu(z_ref[:, sl])).astype(BF16))
    out_ref[...] = x_ref[...] + _dot(jnp.concatenate(parts, axis=1), w_ref[...])


def odd_out(x2, o2, proj, gain, w_bf, tm=512):
    n, d = x2.shape
    row = lambda w: pl.BlockSpec((tm, w), lambda i: (i, 0))
    z_col = 3 * GDN_WIDTH // GDN_WIDTH
    return pl.pallas_call(
        _odd_out_body,
        grid=(n // tm,),
        in_specs=[row(d), row(GDN_WIDTH), pl.BlockSpec((tm, GDN_WIDTH), lambda i: (i, z_col)),
                  _full((1, GDN_HEAD_DIM)), _full(w_bf.shape)],
        out_specs=row(d),
        out_shape=jax.ShapeDtypeStruct((n, d), F32),
        compiler_params=_params("parallel"),
    )(x2, o2, proj, gain, w_bf)


R_GROUP = 0
R_EXPERT = N_GROUPS


def _moe_body(x_ref, g_ref, wr_hi_ref, wr_lo_ref, br_ref, win_ref, wout_ref, o_ref, h_ref, gate_ref):
    e = pl.program_id(1)

    @pl.when(e == 0)
    def _():
        x = x_ref[...]
        h = x * lax.rsqrt(jnp.mean(x * x, axis=-1, keepdims=True) + NORM_EPS) * g_ref[...]
        h_ref[...] = h.astype(BF16)
        h_hi, h_lo = _split2(h)
        logit = _dot(h_hi, wr_hi_ref[...]) + _dot(h_lo, wr_hi_ref[...]) + _dot(h_hi, wr_lo_ref[...]) + br_ref[...]
        lane_i = lax.broadcasted_iota(jnp.int32, logit.shape, 1)
        lane = lane_i.astype(F32)
        is_g = lane_i < N_GROUPS
        g_max = jnp.max(jnp.where(is_g, logit, -jnp.inf), axis=-1, keepdims=True)
        g_sel = jnp.min(jnp.where(is_g & (logit == g_max), lane, float(LANES)), axis=-1, keepdims=True)
        p_group = 1.0 / jnp.sum(jnp.where(is_g, jnp.exp(logit - g_max), 0.0), axis=-1, keepdims=True)
        group_of = ((lane_i - R_EXPERT) // EXPERTS_PER_GROUP).astype(F32)
        mine = (lane_i >= R_EXPERT) & (lane_i < R_EXPERT + N_EXPERTS) & (group_of == g_sel)
        v1 = jnp.max(jnp.where(mine, logit, -jnp.inf), axis=-1, keepdims=True)
        i1 = jnp.min(jnp.where(mine & (logit == v1), lane, float(LANES)), axis=-1, keepdims=True)
        rest = mine & (lane != i1)
        v2 = jnp.max(jnp.where(rest, logit, -jnp.inf), axis=-1, keepdims=True)
        i2 = jnp.min(jnp.where(rest & (logit == v2), lane, float(LANES)), axis=-1, keepdims=True)
        e2 = jnp.exp(v2 - v1)
        w1 = p_group / (1.0 + e2)
        w2 = p_group * e2 / (1.0 + e2)
        gate_ref[...] = jnp.where(lane == i1, w1, 0.0) + jnp.where(lane == i2, w2, 0.0)
        o_ref[...] = x

    gates = gate_ref[...]
    lane = lax.broadcasted_iota(jnp.int32, gates.shape, 1)
    acts = []
    for j in range(EXPERTS_PER_GROUP):
        gate_e = jnp.sum(jnp.where(lane == R_EXPERT + e * EXPERTS_PER_GROUP + j, gates, 0.0), axis=-1, keepdims=True)
        gu = _dot(h_ref[...], win_ref[j])
        acts.append((_silu(gu[:, :EXPERT_FF]) * gu[:, EXPERT_FF:] * gate_e).astype(BF16))
    o_ref[...] += _dot(jnp.concatenate(acts, axis=1), wout_ref[0])


def moe(x2, gain, wr_hi, wr_lo, br, win_bf, wout_bf, layer, tm=1024):
    n, d = x2.shape
    row = pl.BlockSpec((tm, d), lambda i, e: (i, 0))
    win_g = win_bf.reshape(-1, d, 2 * EXPERT_FF)
    wout_g = wout_bf.reshape(-1, EXPERTS_PER_GROUP * EXPERT_FF, d)
    return pl.pallas_call(
        _moe_body,
        grid=(n // tm, N_GROUPS),
        in_specs=[row, _full((1, d)), _full((d, LANES)), _full((d, LANES)), _full((1, LANES)),
                  pl.BlockSpec((EXPERTS_PER_GROUP, d, 2 * EXPERT_FF), lambda i, e: (layer * N_GROUPS + e, 0, 0)),
                  pl.BlockSpec((1, EXPERTS_PER_GROUP * EXPERT_FF, d), lambda i, e: (layer * N_GROUPS + e, 0, 0))],
        out_specs=row,
        out_shape=jax.ShapeDtypeStruct((n, d), F32),
        scratch_shapes=[pltpu.VMEM((tm, d), BF16), pltpu.VMEM((tm, LANES), F32)],
        compiler_params=_params("parallel", "arbitrary"),
    )(x2, gain.reshape(1, d), wr_hi, wr_lo, br, win_g, wout_g)


G_CUM, G_BETA, G_LAST = 0, GDN_HEADS, 2 * GDN_HEADS


def _gdn_gates_body(ab_ref, alog_ref, dtb_ref, gb_ref):
    t = ab_ref.shape[1]
    ab = ab_ref[0]
    sp_in = ab + dtb_ref[...]
    softplus = jnp.maximum(sp_in, 0.0) + jnp.log1p(jnp.exp(-jnp.abs(sp_in)))
    lane_row = lax.broadcasted_iota(jnp.int32, (1, LANES), 1)
    g = jnp.where(lane_row < GDN_HEADS, -jnp.exp(alog_ref[...]) * softplus, 0.0)
    blk = 4 * GDN_CHUNK
    r = lax.broadcasted_iota(jnp.int32, (blk, blk), 0)
    c = lax.broadcasted_iota(jnp.int32, (blk, blk), 1)
    same = r // GDN_CHUNK == c // GDN_CHUNK
    tri = jnp.where(same & (r >= c), 1.0, 0.0).astype(BF16)
    tot = jnp.where(same, 1.0, 0.0).astype(BF16)
    lane = lax.broadcasted_iota(jnp.int32, (blk, LANES), 1)
    for s in range(t // blk):
        rs = slice(s * blk, (s + 1) * blk)
        hi, mid, lo = _split3(g[rs])
        gc = _dot(tri, hi) + _dot(tri, mid) + _dot(tri, lo)
        gl = _dot(tot, hi) + _dot(tot, mid) + _dot(tot, lo)
        gl = pltpu.roll(gl, G_LAST, 1)
        gb_ref[0, rs, :] = jnp.where(lane < G_BETA, gc, jnp.where(lane < G_LAST, _sigmoid(ab[rs]), gl))


def gdn_gates(proj, alog_row, dtb_row):
    b, t, _ = proj.shape
    return pl.pallas_call(
        _gdn_gates_body,
        grid=(b,),
        in_specs=[pl.BlockSpec((1, t, LANES), lambda i: (i, 0, C_AB // LANES)), _full((1, LANES)), _full((1, LANES))],
        out_specs=pl.BlockSpec((1, t, LANES), lambda i: (i, 0, 0)),
        out_shape=jax.ShapeDtypeStruct((b, t, LANES), F32),
        compiler_params=_params("parallel"),
    )(proj, alog_row, dtb_row)


GDN_HEADS_PER_STEP = 8
GDN_CHUNKS_PER_TRIP = 2
GDN_SEGMENTS = 4


def _dot3(a, b):
    a_hi, a_lo = _split2(a)
    b_hi, b_lo = _split2(b)
    return _dot(a_hi, b_hi) + _dot(a_hi, b_lo) + _dot(a_lo, b_hi)


def _dot1(a, b):
    return _dot(a.astype(BF16), b.astype(BF16))


CONV_HALO = 8


def _conv_silu(ext_ref, cw_ref, lanes):
    t = ext_ref.shape[0] - CONV_HALO
    y = ext_ref[CONV_HALO:, lanes] * cw_ref[CONV_WIDTH - 1:CONV_WIDTH, lanes]
    for d in range(1, CONV_WIDTH):
        y = y + ext_ref[CONV_HALO - d:CONV_HALO - d + t, lanes] * cw_ref[CONV_WIDTH - 1 - d:CONV_WIDTH - d, lanes]
    return _silu(y)


def _l2norm(y):
    return y * lax.rsqrt(jnp.sum(y * y, axis=-1, keepdims=True) + NORM_EPS)


def _gdn_body(q_ref, k_ref, v_ref, cq_ref, ck_ref, cv_ref, gb_ref, grow_ref, o_ref,
              ext_ref, state_ref, gl_ref, gc_ref, kb_ref, k_ref_s, kbg_ref, vb_ref, qs_ref, qg_ref, kd_ref,
              u_ref, w_ref, a_ref):
    t = q_ref.shape[1]
    cs = GDN_CHUNK
    dk = GDN_HEAD_DIM
    nh = GDN_HEADS_PER_STEP
    hgrp = pl.program_id(1)
    seg = pl.program_id(2)
    gb_hi, gb_mid, gb_lo = _split3(gb_ref[0])
    pick_row = lax.broadcasted_iota(jnp.int32, (LANES, LANES), 0)

    @pl.when(seg == 0)
    def _():
        ext_ref[:, 0:CONV_HALO, :] = jnp.zeros((3, CONV_HALO, ext_ref.shape[2]), F32)
        state_ref[...] = jnp.zeros_like(state_ref)

    @pl.when(seg > 0)
    def _():
        ext_ref[:, 0:CONV_HALO, :] = ext_ref[:, t:t + CONV_HALO, :]

    for i, ref in enumerate((q_ref, k_ref, v_ref)):
        ext_ref[i, CONV_HALO:, :] = ref[0]

    def column(idx):
        sel = jnp.where(pick_row == idx, 1.0, 0.0).astype(BF16)
        return _dot(gb_hi, sel) + _dot(gb_mid, sel) + _dot(gb_lo, sel)

    for s in range(nh):
        lanes = slice(s * dk, (s + 1) * dk)
        head = nh * hgrp + s
        gcol = column(G_CUM + head)
        bcol = column(G_BETA + head)
        glast = column(G_LAST + head)
        eg = jnp.exp(gcol)
        k = _l2norm(_conv_silu(ext_ref.at[1], ck_ref, lanes))
        kb = k * bcol
        k_ref_s[s] = k.astype(BF16)
        kb_ref[s] = kb.astype(BF16)
        kbg_ref[s] = (kb * eg).astype(BF16)
        kd_ref[s] = (k * jnp.exp(glast - gcol)).astype(BF16)
        q = _l2norm(_conv_silu(ext_ref.at[0], cq_ref, lanes)) * (dk ** -0.5)
        qs_ref[s] = q.astype(BF16)
        qg_ref[s] = (q * eg).astype(BF16)
        vb_ref[s] = (_conv_silu(ext_ref.at[2], cv_ref, lanes) * bcol).astype(BF16)
        gl_ref[s] = glast
        gc_ref[s] = gcol

    r = lax.broadcasted_iota(jnp.int32, (cs, cs), 0)
    c = lax.broadcasted_iota(jnp.int32, (cs, cs), 1)
    tril = r >= c
    strict = r > c
    eye = jnp.where(r == c, 1.0, 0.0)

    def prep(trip, _):
        probs = [(s, trip * GDN_CHUNKS_PER_TRIP + j) for j in range(GDN_CHUNKS_PER_TRIP) for s in range(nh)]
        rows = [pl.ds(pl.multiple_of(n * cs, cs), cs) for _, n in probs]
        decay, lmat = [], []
        for (s, n), rw in zip(probs, rows):
            gr = grow_ref[0, s, pl.ds(n, 1), :]
            gc = gc_ref[s, rw, :cs]
            decay.append(jnp.where(tril, jnp.exp(jnp.where(tril, gc - gr, 0.0)), 0.0))
        for i, ((s, _), rw) in enumerate(zip(probs, rows)):
            lmat.append(jnp.where(strict, _dot_nt(kb_ref[s, rw, :], k_ref_s[s, rw, :]) * decay[i], 0.0))
        inv = [eye - m for m in lmat]
        pw = [_dot3(m, m) for m in lmat]
        span = 2
        while span < cs:
            mm = _dot3 if span == 2 else _dot1
            inv = [x + mm(x, p) for x, p in zip(inv, pw)]
            span *= 2
            if span < cs:
                pw = [_dot1(p, p) for p in pw]
        inv_bf = [x.astype(BF16) for x in inv]
        for i, ((s, _), rw) in enumerate(zip(probs, rows)):
            u_ref[s, rw, :] = _dot(inv_bf[i], vb_ref[s, rw, :])
            w_ref[s, rw, :] = _dot(inv_bf[i], kbg_ref[s, rw, :]).astype(BF16)
            a_ref[s, rw, :] = jnp.where(tril, _dot_nt(qs_ref[s, rw, :], k_ref_s[s, rw, :]) * decay[i], 0.0).astype(BF16)
        return 0

    lax.fori_loop(0, t // (cs * GDN_CHUNKS_PER_TRIP), prep, 0)

    def scan(n, states):
        r0 = pl.multiple_of(n * cs, cs)
        rows = pl.ds(r0, cs)
        s_bf = [st.astype(BF16) for st in states]
        v_bf = [(u_ref[s, rows, :] - _dot(w_ref[s, rows, :], s_bf[s])).astype(BF16) for s in range(nh)]
        new = [states[s] * jnp.exp(gl_ref[s, pl.ds(r0, 1), :]) + _dot_tn(kd_ref[s, rows, :], v_bf[s])
               for s in range(nh)]
        for s in range(nh):
            o_ref[0, rows, s * dk:(s + 1) * dk] = _dot(qg_ref[s, rows, :], s_bf[s]) + _dot(a_ref[s, rows, :], v_bf[s])
        return tuple(new)

    final = lax.fori_loop(0, t // cs, scan, tuple(state_ref[s] for s in range(nh)))
    for s in range(nh):
        state_ref[s] = final[s]


def gdn_core(proj, conv_w, gb):
    b, t, _ = proj.shape
    ts = t // GDN_SEGMENTS if t % (GDN_SEGMENTS * GDN_CHUNK * GDN_CHUNKS_PER_TRIP) == 0 else t
    seg_chunks = ts // GDN_CHUNK
    nh = GDN_HEADS_PER_STEP
    wide = nh * GDN_HEAD_DIM
    per = GDN_WIDTH // wide
    g_rows = jnp.swapaxes(gb[:, :, G_CUM:G_CUM + GDN_HEADS], 1, 2).reshape(b, GDN_HEADS, t // GDN_CHUNK, GDN_CHUNK)
    sect = lambda k: pl.BlockSpec((1, ts, wide), lambda i, h, s: (i, s, k * per + h))
    taps = lambda k: pl.BlockSpec((CONV_WIDTH, wide), lambda i, h, s: (0, k * per + h))
    bf = lambda w: pltpu.VMEM((nh, ts, w), BF16)
    return pl.pallas_call(
        _gdn_body,
        grid=(b, per, t // ts),
        in_specs=[sect(0), sect(1), sect(2), taps(0), taps(1), taps(2),
                  pl.BlockSpec((1, ts, LANES), lambda i, h, s: (i, s, 0)),
                  pl.BlockSpec((1, nh, seg_chunks, GDN_CHUNK), lambda i, h, s: (i, h, s, 0))],
        out_specs=pl.BlockSpec((1, ts, wide), lambda i, h, s: (i, s, h)),
        out_shape=jax.ShapeDtypeStruct((b, t, GDN_WIDTH), F32),
        scratch_shapes=[pltpu.VMEM((3, CONV_HALO + ts, wide), F32), pltpu.VMEM((nh, GDN_HEAD_DIM, GDN_HEAD_DIM), F32)]
                       + [pltpu.VMEM((nh, ts, LANES), F32)] * 2 + [bf(GDN_HEAD_DIM)] * 7
                       + [pltpu.VMEM((nh, ts, GDN_HEAD_DIM), F32), bf(GDN_HEAD_DIM), bf(GDN_CHUNK)],
        compiler_params=_params("parallel", "parallel", "arbitrary"),
    )(proj, proj, proj, conv_w, conv_w, conv_w, gb, g_rows)


def _rope_tables(pos):
    half = HEAD_DIM // 2
    inv_freq = ROPE_THETA ** (-jnp.arange(half, dtype=F32) / half)
    ang = pos.astype(F32)[:, None] * inv_freq
    cos = jnp.cos(ang)
    sin = jnp.sin(ang)
    cos_t = jnp.tile(jnp.concatenate([cos, cos], axis=-1), (1, LANES // HEAD_DIM))
    sin_t = jnp.tile(jnp.concatenate([-sin, sin], axis=-1), (1, LANES // HEAD_DIM))
    return cos_t, sin_t


def _block_diag_ones(width, seg):
    idx = np.arange(width) // seg
    return jnp.asarray((idx[:, None] == idx[None, :]).astype(np.float32), dtype=BF16)


def _pad_cols(w, width):
    return jnp.pad(w, ((0, 0), (0, width - w.shape[1])))


def _even_layer(x2, b, t, norm_gain, w_in, b_gate, b_forget, cmp_pe, cmp_w1, cmp_w2, nsa_gain, fox_gain, w_out):
    d = x2.shape[1]
    o_gate = NSA_Q_W + 6 * NSA_KV_W
    o_fox = o_gate + NSA_GATE_W
    w_bf = w_in.astype(BF16)
    w_re = jnp.concatenate([w_bf[:, :o_gate], w_bf[:, o_fox:o_fox + 3 * FOX_W], w_bf[:, o_gate:o_fox],
                            w_bf[:, o_fox + 3 * FOX_W:], jnp.zeros((d, EVEN_W - w_in.shape[1]), BF16)], axis=1)
    proj = norm_matmul(x2, norm_gain, w_re).reshape(b, t, EVEN_W)

    cos, sin = _rope_tables(jnp.arange(t))
    tile = lambda g, n: jnp.tile(g, n).reshape(1, -1)
    bias = jnp.pad(jnp.concatenate([b_gate, b_forget]), (0, LANES - NSA_GATE_W - FOX_HEADS)).reshape(1, LANES)
    bd = _block_diag_ones(FOX_W, HEAD_DIM)
    (qa, ks, kw, vs, vw, kc_raw, vc_raw, qb, kb, vf, gates, cum) = even_prep(
        proj, cos, sin, tile(nsa_gain[0], NSA_HEADS), tile(nsa_gain[2], NSA_KV_HEADS), tile(nsa_gain[3], NSA_KV_HEADS),
        tile(fox_gain[0], FOX_HEADS), tile(fox_gain[1], FOX_HEADS), bias, bd)

    n_str = t // CMP_STRIDE
    half = CMP_BLOCK // 2
    eye2 = jnp.eye(NSA_KV_HEADS, dtype=F32)
    pe = jnp.tile(cmp_pe[:, :, None, :], (1, 1, NSA_KV_HEADS, 1)).reshape(2, 2, 1, half * NSA_KV_W)
    w1 = jnp.einsum('ilde,hg->ilhdge', cmp_w1, eye2).reshape(2, 2, half * NSA_KV_W, NSA_KV_W).astype(BF16)
    w2 = jnp.einsum('ide,hg->ihdge', cmp_w2, eye2).reshape(2, NSA_KV_W, NSA_KV_W).astype(BF16)
    cos_c, sin_c = _rope_tables(jnp.arange(n_str) * CMP_STRIDE + (CMP_BLOCK - 1))
    kc, vc = compress(kc_raw, vc_raw, pe, w1, w2, tile(nsa_gain[1], NSA_KV_HEADS), cos_c, sin_c, _block_diag_ones(LANES, HEAD_DIM))

    n_sel = t // SEL_BLOCK
    cs = np.arange(n_str)[:, None] * CMP_STRIDE
    ss = np.arange(n_sel)[None, :] * SEL_BLOCK
    overlap = np.clip(np.minimum(cs + CMP_BLOCK, ss + SEL_BLOCK) - np.maximum(cs, ss), 0, None) / CMP_BLOCK
    overlap[(t - CMP_BLOCK) // CMP_STRIDE + 1:] = 0.0
    ovt = jnp.asarray(overlap.T.astype(np.float32), dtype=BF16)
    o_a = nsa_attention(qa, kc, vc, ks, vs, kw, vw, gates, ovt)
    o_b = fox_attention(qb, kb, vf, cum)

    return even_out(x2, o_a.reshape(b * t, NSA_Q_W), o_b.reshape(b * t, FOX_W), w_out.astype(BF16))


def _odd_layer(x2, b, t, norm_gain, w_in, conv_w, a_log, dt_bias, gdn_gain, w_out):
    w_bf = w_in.astype(BF16)
    w_pad = jnp.concatenate([w_bf, jnp.zeros((w_in.shape[0], ODD_W - w_in.shape[1]), BF16)], axis=1)
    proj = norm_matmul(x2, norm_gain, w_pad).reshape(b, t, ODD_W)
    pad8 = lambda v: jnp.pad(v, (0, LANES - GDN_HEADS)).reshape(1, LANES)
    gb = gdn_gates(proj, pad8(a_log), pad8(dt_bias))
    o = gdn_core(proj, conv_w, gb)
    return odd_out(x2, o.reshape(b * t, GDN_WIDTH), proj.reshape(b * t, ODD_W), gdn_gain.reshape(1, GDN_HEAD_DIM),
                   w_out.astype(BF16))


def _moe_layer(x2, gain, w_rg, b_rg, w_re, b_re, w_ein_bf, w_eout_bf, layer):
    d = x2.shape[1]
    wr = _pad_cols(jnp.concatenate([w_rg, w_re], axis=1), LANES)
    wr_hi = wr.astype(BF16)
    wr_lo = (wr - wr_hi.astype(F32)).astype(BF16)
    br = jnp.pad(jnp.concatenate([b_rg, b_re]), (0, LANES - N_GROUPS - N_EXPERTS)).reshape(1, LANES)
    return moe(x2, gain, wr_hi, wr_lo, br, w_ein_bf, w_eout_bf, layer)


def kernel(x, norm_mix, norm_ffn, w_in_even, b_nsa_gate, b_forget, cmp_pe, cmp_w1, cmp_w2, nsa_qk_gain, fox_qk_gain,
           w_out_even, w_in_odd, conv_w, a_log, dt_bias, gdn_norm_gain, w_out_odd, w_router_group, b_router_group,
           w_router_expert, b_router_expert, w_expert_in, w_expert_out):
    b, t, d = x.shape
    x2 = x.reshape(b * t, d)
    w_ein_bf = w_expert_in.astype(BF16)
    w_eout_bf = w_expert_out.astype(BF16)
    for layer in range(norm_mix.shape[0]):
        i = layer // 2
        if layer % 2 == 0:
            x2 = _even_layer(x2, b, t, norm_mix[layer], w_in_even[i], b_nsa_gate[i], b_forget[i], cmp_pe[i], cmp_w1[i],
                             cmp_w2[i], nsa_qk_gain[i], fox_qk_gain[i], w_out_even[i])
        else:
            x2 = _odd_layer(x2, b, t, norm_mix[layer], w_in_odd[i], conv_w[i], a_log[i], dt_bias[i], gdn_norm_gain[i],
                            w_out_odd[i])
        x2 = _moe_layer(x2, norm_ffn[layer], w_router_group[layer], b_router_group[layer], w_router_expert[layer],
                        b_router_expert[layer], w_ein_bf, w_eout_bf, layer)
    return x2.reshape(b, t, d)
```

```python
import functools

import numpy as np
import jax
import jax.numpy as jnp
from jax import lax
from jax.experimental import pallas as pl
from jax.experimental.pallas import tpu as pltpu

F32 = jnp.float32
BF16 = jnp.bfloat16

HEAD_DIM = 64
ROPE_THETA = 10000.0
NSA_HEADS = 8
NSA_KV_HEADS = 2
NSA_GROUP = NSA_HEADS // NSA_KV_HEADS
CMP_BLOCK = 32
CMP_STRIDE = 16
SEL_BLOCK = 64
SEL_TOPK = 8
WINDOW = 256
FOX_HEADS = 8
GDN_HEADS = 8
GDN_HEAD_DIM = 128
GDN_WIDTH = GDN_HEADS * GDN_HEAD_DIM
CONV_WIDTH = 4
GDN_CHUNK = 64
N_GROUPS = 4
EXPERTS_PER_GROUP = 4
N_EXPERTS = N_GROUPS * EXPERTS_PER_GROUP
EXPERT_FF = 256
NORM_EPS = 1e-6
NEG_INF = -1e30
FORCE_SCORE = 1e9

LANES = 128
LOG2E = 1.4426950408889634
SUM_ROWS = 16
NSA_Q_W = NSA_HEADS * HEAD_DIM
NSA_KV_W = NSA_KV_HEADS * HEAD_DIM
NSA_GATE_W = 3 * NSA_HEADS
FOX_W = FOX_HEADS * HEAD_DIM
C_QN = 0
C_KC, C_VC, C_KS, C_VS, C_KW, C_VW = (NSA_Q_W + i * NSA_KV_W for i in range(6))
C_QF = NSA_Q_W + 6 * NSA_KV_W
C_KF = C_QF + FOX_W
C_VF = C_KF + FOX_W
C_MISC = C_VF + FOX_W
EVEN_W = C_MISC + LANES
MISC_F = NSA_GATE_W
C_AB = 4 * GDN_WIDTH
ODD_W = C_AB + LANES

VMEM_LIMIT = 56 * 1024 * 1024


def _params(*sem):
    return pltpu.CompilerParams(dimension_semantics=sem, vmem_limit_bytes=VMEM_LIMIT)


def _dot(a, b):
    return jnp.dot(a, b, preferred_element_type=F32)


def _dot_nt(a, b):
    return lax.dot_general(a, b, (((1,), (1,)), ((), ())), preferred_element_type=F32)


def _dot_tn(a, b):
    return lax.dot_general(a, b, (((0,), (0,)), ((), ())), preferred_element_type=F32)


def _split2(x):
    hi = x.astype(BF16)
    return hi, (x - hi.astype(F32)).astype(BF16)


def _split3(x):
    hi = x.astype(BF16)
    r = x - hi.astype(F32)
    mid = r.astype(BF16)
    return hi, mid, (r - mid.astype(F32)).astype(BF16)


def _sigmoid(z):
    return 1.0 / (1.0 + jnp.exp(-z))


def _silu(z):
    return z * _sigmoid(z)


def _full(shape):
    nd = len(shape)
    return pl.BlockSpec(shape, lambda *_: (0,) * nd)


def _norm_matmul_body(x_ref, g_ref, w_ref, o_ref):
    x = x_ref[...]
    ms = jnp.mean(x * x, axis=-1, keepdims=True)
    h = (x * lax.rsqrt(ms + NORM_EPS) * g_ref[...]).astype(BF16)
    o_ref[...] = _dot(h, w_ref[...])


def norm_matmul(x2, gain, w_bf, tm=512):
    n, d = x2.shape
    wp = w_bf.shape[1]
    return pl.pallas_call(
        _norm_matmul_body,
        grid=(n // tm,),
        in_specs=[pl.BlockSpec((tm, d), lambda i: (i, 0)), _full((1, d)), _full((d, wp))],
        out_specs=pl.BlockSpec((tm, wp), lambda i: (i, 0)),
        out_shape=jax.ShapeDtypeStruct((n, wp), F32),
        compiler_params=_params("parallel"),
    )(x2, gain.reshape(1, d), w_bf)


CONV_HALO = 8
CONV_SECTION = 512
CONV_ROWS = 64


def _l2norm(y):
    return y * lax.rsqrt(jnp.sum(y * y, axis=-1, keepdims=True) + NORM_EPS)


def _odd_in_body(x_ref, g_ref, w_ref, cw_ref, o_ref, pre_ref):
    tm = x_ref.shape[1]

    @pl.when(pl.program_id(1) == 0)
    def _():
        pre_ref[0:CONV_HALO, :] = jnp.zeros((CONV_HALO, pre_ref.shape[1]), F32)

    x = x_ref[0]
    h = (x * lax.rsqrt(jnp.mean(x * x, axis=-1, keepdims=True) + NORM_EPS) * g_ref[...]).astype(BF16)
    for c0 in range(0, 3 * GDN_WIDTH, CONV_SECTION):
        pre_ref[CONV_HALO:, c0:c0 + CONV_SECTION] = _dot(h, w_ref[:, c0:c0 + CONV_SECTION])
    for c0 in range(0, 3 * GDN_WIDTH, GDN_HEAD_DIM):
        cols = slice(c0, c0 + GDN_HEAD_DIM)
        for r0 in range(0, tm, CONV_ROWS):
            ext = pre_ref[r0:r0 + CONV_HALO + CONV_ROWS, cols]
            acc = ext[CONV_HALO:] * cw_ref[CONV_WIDTH - 1:CONV_WIDTH, cols]
            for d in range(1, CONV_WIDTH):
                acc = acc + pltpu.roll(ext, d, 0)[CONV_HALO:] * cw_ref[CONV_WIDTH - 1 - d:CONV_WIDTH - d, cols]
            z = _silu(acc)
            if c0 < 2 * GDN_WIDTH:
                z = _l2norm(z)
                if c0 < GDN_WIDTH:
                    z = z * GDN_HEAD_DIM ** -0.5
            o_ref[0, r0:r0 + CONV_ROWS, cols] = z
    pre_ref[0:CONV_HALO, :] = pre_ref[tm:tm + CONV_HALO, :]
    o_ref[0, :, 3 * GDN_WIDTH:] = _dot(h, w_ref[:, 3 * GDN_WIDTH:])


def odd_in_proj(x3, gain, w_bf, conv_w, tm=512):
    b, t, d = x3.shape
    wp = w_bf.shape[1]
    return pl.pallas_call(
        _odd_in_body,
        grid=(b, t // tm),
        in_specs=[pl.BlockSpec((1, tm, d), lambda i, j: (i, j, 0)), _full((1, d)), _full((d, wp)),
                  _full(conv_w.shape)],
        out_specs=pl.BlockSpec((1, tm, wp), lambda i, j: (i, j, 0)),
        out_shape=jax.ShapeDtypeStruct((b, t, wp), F32),
        scratch_shapes=[pltpu.VMEM((CONV_HALO + tm, 3 * GDN_WIDTH), F32)],
        compiler_params=_params("parallel", "arbitrary"),
    )(x3, gain.reshape(1, d), w_bf, conv_w)


def _head_rms(x, bd, gain):
    hi, lo = _split2(x * x)
    ones2 = bd[:LANES, :LANES]
    ssum = jnp.concatenate([_dot(hi[:, c:c + LANES], ones2) + _dot(lo[:, c:c + LANES], ones2)
                            for c in range(0, x.shape[1], LANES)], axis=1)
    return x * lax.rsqrt(ssum * (1.0 / HEAD_DIM) + NORM_EPS) * gain


def _rope(x, cos, sin_signed, first_half):
    fwd = pltpu.roll(x, LANES - HEAD_DIM // 2, 1)
    bwd = pltpu.roll(x, HEAD_DIM // 2, 1)
    return x * cos + jnp.where(first_half, fwd, bwd) * sin_signed


def _even_prep_body(p_ref, cos_ref, sin_ref, gq_ref, gks_ref, gkw_ref, gfq_ref, gfk_ref, bias_ref, bd_ref,
                    qa_ref, ks_ref, kw_ref, vs_ref, vw_ref, kc_ref, vc_ref, qb_ref, kb_ref, vf_ref,
                    gate_ref, cum_ref, carry_ref, stage_ref):
    tr = p_ref.shape[1]
    bd = bd_ref[...]
    cos = cos_ref[...]
    sin = sin_ref[...]
    lane = lax.broadcasted_iota(jnp.int32, (1, LANES), 1)
    first_half = (lane % HEAD_DIM) < (HEAD_DIM // 2)
    scale = HEAD_DIM ** -0.5 * LOG2E

    qn = _head_rms(p_ref[0, :, C_QN:C_QN + NSA_Q_W], bd, gq_ref[...])
    for c in range(NSA_Q_W // LANES):
        sl = slice(c * LANES, (c + 1) * LANES)
        qa_ref[0, sl, :] = (_rope(qn[:, sl], cos, sin, first_half) * scale).T.astype(BF16)
    ks = _head_rms(p_ref[0, :, C_KS:C_KS + NSA_KV_W], bd, gks_ref[...])
    ks_ref[0] = _rope(ks, cos, sin, first_half).astype(BF16)
    kw = _head_rms(p_ref[0, :, C_KW:C_KW + NSA_KV_W], bd, gkw_ref[...])
    kw_ref[0] = _rope(kw, cos, sin, first_half).astype(BF16)
    vs_ref[0] = p_ref[0, :, C_VS:C_VS + NSA_KV_W].T.astype(BF16)
    vw_ref[0] = p_ref[0, :, C_VW:C_VW + NSA_KV_W].T.astype(BF16)
    stage_ref[0] = p_ref[0, :, C_KC:C_KC + NSA_KV_W]
    stage_ref[1] = p_ref[0, :, C_VC:C_VC + NSA_KV_W]
    for l in range(CMP_STRIDE):
        rows = pl.ds(l, tr // CMP_STRIDE, stride=CMP_STRIDE)
        kc_ref[0, :, l * NSA_KV_W:(l + 1) * NSA_KV_W] = stage_ref[0, rows, :]
        vc_ref[0, :, l * NSA_KV_W:(l + 1) * NSA_KV_W] = stage_ref[1, rows, :]

    qb = _head_rms(p_ref[0, :, C_QF:C_QF + FOX_W], bd, gfq_ref[...]) * scale
    kb_ref[0] = _head_rms(p_ref[0, :, C_KF:C_KF + FOX_W], bd, gfk_ref[...]).astype(BF16)
    for c in range(FOX_W // LANES):
        sl = slice(c * LANES, (c + 1) * LANES)
        qb_ref[0, sl, :] = qb[:, sl].T.astype(BF16)
        vf_ref[0, sl, :] = p_ref[0, :, C_VF + c * LANES:C_VF + (c + 1) * LANES].T.astype(BF16)

    z = p_ref[0, :, C_MISC:C_MISC + LANES] + bias_ref[...]
    gate_ref[0] = _sigmoid(z).T
    logf = jnp.minimum(z, 0.0) - jnp.log1p(jnp.exp(-jnp.abs(z)))

    @pl.when(pl.program_id(1) == 0)
    def _():
        carry_ref[...] = jnp.zeros_like(carry_ref)

    row = lax.broadcasted_iota(jnp.int32, (tr, tr), 0)
    col = lax.broadcasted_iota(jnp.int32, (tr, tr), 1)
    tril = jnp.where(row >= col, 1.0, 0.0).astype(BF16)
    hi, mid, lo = _split3(logf)
    cum = _dot(tril, hi) + _dot(tril, mid) + _dot(tril, lo) + carry_ref[...]
    cum_ref[0] = cum
    carry_ref[...] = cum[tr - 1:tr, :]


def even_prep(proj, cos, sin, gq, gks, gkw, gfq, gfk, bias, bd, tr=256):
    b, t, _ = proj.shape
    row = lambda w: pl.BlockSpec((1, tr, w), lambda i, j: (i, j, 0))
    tab = pl.BlockSpec((tr, LANES), lambda i, j: (j, 0))
    shp = lambda w, dt: jax.ShapeDtypeStruct((b, t, w), dt)
    col = lambda w: pl.BlockSpec((1, w, tr), lambda i, j: (i, 0, j))
    shp_t = lambda w, dt: jax.ShapeDtypeStruct((b, w, t), dt)
    strd = pl.BlockSpec((1, tr // CMP_STRIDE, CMP_STRIDE * NSA_KV_W), lambda i, j: (i, j, 0))
    strd_shape = jax.ShapeDtypeStruct((b, t // CMP_STRIDE, CMP_STRIDE * NSA_KV_W), F32)
    return pl.pallas_call(
        _even_prep_body,
        grid=(b, t // tr),
        in_specs=[row(EVEN_W), tab, tab, _full((1, NSA_Q_W)), _full((1, LANES)), _full((1, LANES)),
                  _full((1, FOX_W)), _full((1, FOX_W)), _full((1, LANES)), _full((FOX_W, FOX_W))],
        out_specs=[col(NSA_Q_W), row(LANES), row(LANES), col(LANES), col(LANES), strd, strd,
                   col(FOX_W), row(FOX_W), col(FOX_W), col(LANES), row(LANES)],
        out_shape=[shp_t(NSA_Q_W, BF16), shp(LANES, BF16), shp(LANES, BF16), shp_t(LANES, BF16), shp_t(LANES, BF16),
                   strd_shape, strd_shape, shp_t(FOX_W, BF16), shp(FOX_W, BF16), shp_t(FOX_W, BF16),
                   shp_t(LANES, F32), shp(LANES, F32)],
        scratch_shapes=[pltpu.VMEM((1, LANES), F32), pltpu.VMEM((2, tr, LANES), F32)],
        compiler_params=_params("parallel", "arbitrary"),
    )(proj, cos, sin, gq, gks, gkw, gfq, gfk, bias, bd)


def _gelu_tanh(x):
    return 0.5 * x * (1.0 + jnp.tanh(np.sqrt(2.0 / np.pi).astype(np.float32) * (x + 0.044715 * (x * x * x))))


def _compress_body(xk_ref, xv_ref, pe_ref, w1_ref, w2_ref, gk_ref, cos_ref, sin_ref, bd_ref, kc_ref, vc_ref):
    n = xk_ref.shape[1]
    lane = lax.broadcasted_iota(jnp.int32, (1, LANES), 1)
    first_half = (lane % HEAD_DIM) < (HEAD_DIM // 2)

    def mlp(x_ref, i):
        x = x_ref[0]
        nxt = pltpu.roll(x, n - 1, 0)
        xa = (x + pe_ref[i, 0]).astype(BF16)
        xb = (nxt + pe_ref[i, 1]).astype(BF16)
        h = _dot(xa, w1_ref[i, 0]) + _dot(xb, w1_ref[i, 1])
        return _dot(_gelu_tanh(h).astype(BF16), w2_ref[i])

    kc = _head_rms(mlp(xk_ref, 0), bd_ref[...], gk_ref[...])
    kc_ref[0] = _rope(kc, cos_ref[...], sin_ref[...], first_half).astype(BF16)
    vc_ref[0] = mlp(xv_ref, 1).T.astype(BF16)


def compress(xk, xv, pe, w1, w2, gk, cos_c, sin_c, bd):
    b, n, w = xk.shape
    blk = pl.BlockSpec((1, n, w), lambda i: (i, 0, 0))
    out = pl.BlockSpec((1, n, LANES), lambda i: (i, 0, 0))
    return pl.pallas_call(
        _compress_body,
        grid=(b,),
        in_specs=[blk, blk, _full(pe.shape), _full(w1.shape), _full(w2.shape), _full((1, LANES)),
                  _full((n, LANES)), _full((n, LANES)), _full((LANES, LANES))],
        out_specs=[out, pl.BlockSpec((1, LANES, n), lambda i: (i, 0, 0))],
        out_shape=[jax.ShapeDtypeStruct((b, n, LANES), BF16), jax.ShapeDtypeStruct((b, LANES, n), BF16)],
        compiler_params=_params("parallel"),
    )(xk, xv, pe, w1, w2, gk, cos_c, sin_c, bd)


def _flash_step(s_ref, p_ref, acc_ref, v_blk, m_i, l_i, adjust, first=False):
    n_ch = acc_ref.shape[0]
    whole = slice(0, s_ref.shape[0])
    al, ms = [], []
    for cg in range(s_ref.shape[1] // LANES):
        sl = slice(cg * LANES, (cg + 1) * LANES)
        s = adjust(s_ref[:, sl], cg, whole)
        m_new = jnp.maximum(m_i[:, sl], jnp.max(s, axis=0, keepdims=True))
        p_ref[:, sl] = jnp.exp2(s - m_new).astype(BF16)
        al.append(jnp.exp2(m_i[:, sl] - m_new))
        ms.append(m_new)
    cat = lambda xs: jnp.concatenate(xs, axis=1)
    alpha = cat(al)
    pv = _dot(v_blk, p_ref[...])
    acc_ref[...] = pv[:n_ch] if first else alpha * acc_ref[...] + pv[:n_ch]
    return cat(ms), alpha * l_i + pv[n_ch:n_ch + 1]


NSA_KEYS_PER_QUERY_BLOCK = 2


def _nsa_body(q_ref, kc_ref, vc_ref, ks_ref, vs_ref, kw_ref, vw_ref, gate_ref, ovt_ref, o_ref,
              sel_ref, s0_ref, s1_ref, p_ref, acc_ref, *, k_top):
    tq = q_ref.shape[2]
    t_all = ks_ref.shape[1]
    n_cmp = kc_ref.shape[1]
    n_sel = ovt_ref.shape[0]
    g_n = NSA_GROUP
    c = pl.program_id(1)
    t0 = c * tq
    chan = lax.broadcasted_iota(jnp.int32, (LANES, 1), 0)
    tlane = t0 + lax.broadcasted_iota(jnp.int32, (1, tq), 1)
    gates = gate_ref[0]

    nrow = lax.broadcasted_iota(jnp.int32, (n_cmp, 1), 0)
    valid_c = (nrow * CMP_STRIDE + (CMP_BLOCK - 1)) <= tlane
    jrow = lax.broadcasted_iota(jnp.int32, (n_sel, tq), 0)
    jrow_f = jrow.astype(F32)
    cur = tlane // SEL_BLOCK
    forced = (jrow == 0) | (jrow == cur) | (jrow == cur - 1)
    future = jrow * SEL_BLOCK > tlane
    tk = NSA_KEYS_PER_QUERY_BLOCK * tq
    krow = lax.broadcasted_iota(jnp.int32, (tk, 1), 0)
    per_blk = tk // SEL_BLOCK
    w_len = tq + WINDOW
    w_start = pl.multiple_of(jnp.clip(t0 - WINDOW, 0, t_all - w_len), LANES)
    wrow = w_start + lax.broadcasted_iota(jnp.int32, (w_len, 1), 0)
    valid_w = (wrow <= tlane) & (wrow > tlane - WINDOW)

    heads = [(kvh, g) for kvh in range(NSA_KV_HEADS) for g in range(g_n)]
    zero_half = jnp.zeros((HEAD_DIM, tq), BF16)

    def on_kv_rows(h, kvh):
        blk = q_ref[0, h * HEAD_DIM:(h + 1) * HEAD_DIM, :]
        return jnp.concatenate([blk, zero_half] if kvh == 0 else [zero_half, blk], axis=0)

    qst = jnp.concatenate([on_kv_rows(h, kvh) for h, (kvh, _) in enumerate(heads)], axis=1)
    n_col = len(heads) * tq

    def softmax_cols(s, ok, guard):
        outs = []
        for cg in range(len(heads)):
            sc = jnp.where(ok, s[:, cg * tq:(cg + 1) * tq], NEG_INF)
            e = jnp.exp2(sc - jnp.max(sc, axis=0, keepdims=True))
            if guard:
                e = jnp.where(ok, e, 0.0)
            den = jnp.sum(e, axis=0, keepdims=True)
            outs.append(e * (1.0 / (jnp.where(den > 0.0, den, 1.0) if guard else den)))
        return outs

    p_c = softmax_cols(_dot(kc_ref[0], qst), valid_c, guard=True)
    o_cmp = _dot(vc_ref[0], jnp.concatenate(p_c, axis=1).astype(BF16))

    for kvh in range(NSA_KV_HEADS):
        p_sum = p_c[kvh * g_n]
        for g in range(1, g_n):
            p_sum = p_sum + p_c[kvh * g_n + g]
        p_hi, p_lo = _split2(p_sum)
        imp_t = _dot(ovt_ref[...], p_hi) + _dot(ovt_ref[...], p_lo)
        val = jnp.where(forced, FORCE_SCORE, jnp.where(future, NEG_INF, imp_t))
        sel_t = jnp.zeros((n_sel, tq), F32)
        for _ in range(k_top):
            m = jnp.max(val, axis=0, keepdims=True)
            first = jnp.min(jnp.where(val == m, jrow_f, float(n_sel)), axis=0, keepdims=True)
            pick = jrow_f == first
            sel_t = jnp.where(pick, 1.0, sel_t)
            val = jnp.where(pick, -jnp.inf, val)
        sel_ref[kvh] = sel_t

    p_w = softmax_cols(_dot(kw_ref[0, pl.ds(w_start, w_len), :], qst), valid_w, guard=False)
    o_win = _dot(vw_ref[0, :, pl.ds(w_start, w_len)], jnp.concatenate(p_w, axis=1).astype(BF16))

    def put_scores(buf, kb):
        k0 = pl.multiple_of(jnp.minimum(kb * tk, t_all - tk), tk)
        buf[...] = _dot(ks_ref[0, pl.ds(k0, tk), :], qst)

    def half_step(buf, kb, m_i, l_i):
        k0 = pl.multiple_of(jnp.minimum(kb * tk, t_all - tk), tk)
        causal = (kb * tk + krow) <= tlane
        ok = [causal & (jnp.concatenate([jnp.broadcast_to(sel_ref[kvh, pl.ds(k0 // SEL_BLOCK + r, 1), :],
                                                          (SEL_BLOCK, tq)) for r in range(per_blk)], axis=0) > 0.5)
              for kvh in range(NSA_KV_HEADS)]
        adjust = lambda s_part, cg, rows: jnp.where(ok[cg // g_n][rows], s_part, NEG_INF)
        v_blk = jnp.concatenate([vs_ref[0, :, pl.ds(k0, tk)], jnp.ones((SUM_ROWS, tk), BF16)], axis=0)
        return _flash_step(buf, p_ref, acc_ref, v_blk, m_i, l_i, adjust)

    def sel_trip(j, carry):
        put_scores(s1_ref, 2 * j + 1)
        carry = half_step(s0_ref, 2 * j, *carry)
        put_scores(s0_ref, 2 * j + 2)
        return half_step(s1_ref, 2 * j + 1, *carry)

    put_scores(s0_ref, 0)
    acc_ref[...] = jnp.zeros_like(acc_ref)
    init = (jnp.full((1, n_col), NEG_INF, F32), jnp.zeros((1, n_col), F32))
    n_blocks = (t0 + tq + tk - 1) // tk
    _, l_s = lax.fori_loop(0, (n_blocks + 1) // 2, sel_trip, init)
    o_slc = acc_ref[...] * (1.0 / l_s)

    gated = []
    for h, (kvh, _) in enumerate(heads):
        cols = slice(h * tq, (h + 1) * tq)
        rows = slice(kvh * HEAD_DIM, (kvh + 1) * HEAD_DIM)
        gated.append(gates[3 * h:3 * h + 1] * o_cmp[rows, cols] + gates[3 * h + 1:3 * h + 2] * o_slc[rows, cols]
                     + gates[3 * h + 2:3 * h + 3] * o_win[rows, cols])
    for j in range(NSA_HEADS * HEAD_DIM // LANES):
        pair = jnp.concatenate(gated[2 * j:2 * j + 2], axis=0)
        o_ref[0, :, j * LANES:(j + 1) * LANES] = pair.T.astype(BF16)


def nsa_attention(qa_t, kc, vc_t, ks, vs_t, kw, vw_t, gates_t, ovt, tq=LANES):
    b, _, t = qa_t.shape
    n_cmp = kc.shape[1]
    n_sel = ovt.shape[0]
    k_top = min(SEL_TOPK, n_sel)
    tk = NSA_KEYS_PER_QUERY_BLOCK * tq
    tok = lambda n: pl.BlockSpec((1, n, LANES), lambda i, j: (i, 0, 0))
    chn = lambda n: pl.BlockSpec((1, LANES, n), lambda i, j: (i, 0, 0))
    return pl.pallas_call(
        functools.partial(_nsa_body, k_top=k_top),
        grid=(b, t // tq),
        in_specs=[pl.BlockSpec((1, NSA_Q_W, tq), lambda i, j: (i, 0, j)), tok(n_cmp), chn(n_cmp), tok(t), chn(t),
                  tok(t), chn(t), pl.BlockSpec((1, LANES, tq), lambda i, j: (i, 0, j)), _full(ovt.shape)],
        out_specs=pl.BlockSpec((1, tq, NSA_Q_W), lambda i, j: (i, j, 0)),
        out_shape=jax.ShapeDtypeStruct((b, t, NSA_Q_W), BF16),
        scratch_shapes=[pltpu.VMEM((NSA_KV_HEADS, n_sel, tq), F32), pltpu.VMEM((tk, NSA_HEADS * tq), F32),
                        pltpu.VMEM((tk, NSA_HEADS * tq), F32), pltpu.VMEM((tk, NSA_HEADS * tq), BF16),
                        pltpu.VMEM((LANES, NSA_HEADS * tq), F32)],
        compiler_params=_params("parallel", "arbitrary"),
    )(qa_t, kc, vc_t, ks, vs_t, kw, vw_t, gates_t, ovt)


FOX_KEYS_PER_QUERY_BLOCK = 2


def _fox_body(q_ref, k_ref, v_ref, cum_ref, o_ref, ck_ref, s0_ref, s1_ref, p_ref, acc_ref, *, tq):
    t = k_ref.shape[1]
    tk = FOX_KEYS_PER_QUERY_BLOCK * tq
    pair = pl.program_id(1)

    hi, mid, lo = _split3(cum_ref[0])
    pick_row = lax.broadcasted_iota(jnp.int32, (LANES, LANES), 0)
    for h in range(2):
        sel = jnp.where(pick_row == MISC_F + 2 * pair + h, 1.0, 0.0).astype(BF16)
        ck_ref[h] = (_dot(hi, sel) + _dot(mid, sel) + _dot(lo, sel)) * LOG2E

    chan = lax.broadcasted_iota(jnp.int32, (LANES, 1), 0)
    first_head = chan < HEAD_DIM
    krow = lax.broadcasted_iota(jnp.int32, (tk, 1), 0)
    qlane = lax.broadcasted_iota(jnp.int32, (1, tq), 1)
    reps = tq // LANES
    bufs = (s0_ref, s1_ref)
    blocks = [(i, kb) for i in range(t // tq) for kb in range((i * tq) // tk + 1)]
    q_cache = {}

    def q_pair(i):
        if i not in q_cache:
            q = q_ref[0, :, i * tq:(i + 1) * tq]
            q_cache[i] = jnp.concatenate([jnp.where(first_head, q, 0), jnp.where(first_head, 0, q)], axis=1)
        return q_cache[i]

    def put_scores(n):
        i, kb = blocks[n]
        bufs[n % 2][...] = _dot(k_ref[0, kb * tk:(kb + 1) * tk, :], q_pair(i))

    put_scores(0)
    m_i = l_i = None
    for n, (i, kb) in enumerate(blocks):
        if n + 1 < len(blocks):
            put_scores(n + 1)
        last = kb == (i * tq) // tk
        ok = ((kb * tk + krow) <= (i * tq + qlane)) if last else None

        def adjust(s_part, cg, rows, kb=kb, last=last, ok=ok):
            s_part = s_part - ck_ref[cg // reps, kb * tk + rows.start:kb * tk + rows.stop, :]
            return jnp.where(ok[rows, (cg % reps) * LANES:(cg % reps + 1) * LANES], s_part, NEG_INF) if last else s_part

        if kb == 0:
            m_i = jnp.full((1, 2 * tq), NEG_INF, F32)
            l_i = jnp.zeros((1, 2 * tq), F32)
        v_blk = jnp.concatenate([v_ref[0, :, kb * tk:(kb + 1) * tk], jnp.ones((SUM_ROWS, tk), BF16)], axis=0)
        m_i, l_i = _flash_step(bufs[n % 2], p_ref, acc_ref, v_blk, m_i, l_i, adjust, first=kb == 0)
        if last:
            o = acc_ref[...] * (1.0 / l_i)
            o_ref[0, i * tq:(i + 1) * tq, :] = jnp.where(first_head, o[:, :tq], o[:, tq:]).T.astype(BF16)


def fox_attention(qb_t, kb, vf_t, cum, tq=256):
    b, w, t = qb_t.shape
    pairs = w // LANES
    tk = FOX_KEYS_PER_QUERY_BLOCK * tq
    return pl.pallas_call(
        functools.partial(_fox_body, tq=tq),
        grid=(b, pairs),
        in_specs=[pl.BlockSpec((1, LANES, t), lambda i, p: (i, p, 0)),
                  pl.BlockSpec((1, t, LANES), lambda i, p: (i, 0, p)),
                  pl.BlockSpec((1, LANES, t), lambda i, p: (i, p, 0)),
                  pl.BlockSpec((1, t, LANES), lambda i, p: (i, 0, 0))],
        out_specs=pl.BlockSpec((1, t, LANES), lambda i, p: (i, 0, p)),
        out_shape=jax.ShapeDtypeStruct((b, t, w), BF16),
        scratch_shapes=[pltpu.VMEM((2, t, LANES), F32), pltpu.VMEM((tk, 2 * tq), F32), pltpu.VMEM((tk, 2 * tq), F32),
                        pltpu.VMEM((tk, 2 * tq), BF16), pltpu.VMEM((LANES, 2 * tq), F32)],
        compiler_params=_params("parallel", "arbitrary"),
    )(qb_t, kb, vf_t, cum)


def _even_out_body(x_ref, oa_ref, ob_ref, w_ref, o_ref):
    wa = oa_ref.shape[1]
    o_ref[...] = x_ref[...] + _dot(oa_ref[...], w_ref[:wa, :]) + _dot(ob_ref[...], w_ref[wa:, :])


def even_out(x2, oa, ob, w_bf, tm=512):
    n, d = x2.shape
    row = lambda w: pl.BlockSpec((tm, w), lambda i: (i, 0))
    return pl.pallas_call(
        _even_out_body,
        grid=(n // tm,),
        in_specs=[row(d), row(oa.shape[1]), row(ob.shape[1]), _full(w_bf.shape)],
        out_specs=row(d),
        out_shape=jax.ShapeDtypeStruct((n, d), F32),
        compiler_params=_params("parallel"),
    )(x2, oa, ob, w_bf)


def _odd_out_body(x_ref, o_ref_in, z_ref, g_ref, w_ref, out_ref):
    gain = g_ref[...]
    parts = []
    for h in range(GDN_HEADS):
        sl = slice(h * GDN_HEAD_DIM, (h + 1) * GDN_HEAD_DIM)
        o = o_ref_in[:, sl]
        y = o * lax.rsqrt(jnp.mean(o * o, axis=-1, keepdims=True) + NORM_EPS) * gain
        parts.append((y * _silu(z_ref[:, sl])).astype(BF16))
    out_ref[...] = x_ref[...] + _dot(jnp.concatenate(parts, axis=1), w_ref[...])


def odd_out(x2, o2, proj, gain, w_bf, tm=512):
    n, d = x2.shape
    row = lambda w: pl.BlockSpec((tm, w), lambda i: (i, 0))
    z_col = 3 * GDN_WIDTH // GDN_WIDTH
    return pl.pallas_call(
        _odd_out_body,
        grid=(n // tm,),
        in_specs=[row(d), row(GDN_WIDTH), pl.BlockSpec((tm, GDN_WIDTH), lambda i: (i, z_col)),
                  _full((1, GDN_HEAD_DIM)), _full(w_bf.shape)],
        out_specs=row(d),
        out_shape=jax.ShapeDtypeStruct((n, d), F32),
        compiler_params=_params("parallel"),
    )(x2, o2, proj, gain, w_bf)


R_GROUP = 0
R_EXPERT = N_GROUPS


def _moe_body(x_ref, g_ref, wr_hi_ref, wr_lo_ref, br_ref, win_ref, wout_ref, o_ref, h_ref, gate_ref):
    e = pl.program_id(1)

    @pl.when(e == 0)
    def _():
        x = x_ref[...]
        h = x * lax.rsqrt(jnp.mean(x * x, axis=-1, keepdims=True) + NORM_EPS) * g_ref[...]
        h_ref[...] = h.astype(BF16)
        h_hi, h_lo = _split2(h)
        logit = _dot(h_hi, wr_hi_ref[...]) + _dot(h_lo, wr_hi_ref[...]) + _dot(h_hi, wr_lo_ref[...]) + br_ref[...]
        lane_i = lax.broadcasted_iota(jnp.int32, logit.shape, 1)
        lane = lane_i.astype(F32)
        is_g = lane_i < N_GROUPS
        g_max = jnp.max(jnp.where(is_g, logit, -jnp.inf), axis=-1, keepdims=True)
        g_sel = jnp.min(jnp.where(is_g & (logit == g_max), lane, float(LANES)), axis=-1, keepdims=True)
        p_group = 1.0 / jnp.sum(jnp.where(is_g, jnp.exp(logit - g_max), 0.0), axis=-1, keepdims=True)
        group_of = ((lane_i - R_EXPERT) // EXPERTS_PER_GROUP).astype(F32)
        mine = (lane_i >= R_EXPERT) & (lane_i < R_EXPERT + N_EXPERTS) & (group_of == g_sel)
        v1 = jnp.max(jnp.where(mine, logit, -jnp.inf), axis=-1, keepdims=True)
        i1 = jnp.min(jnp.where(mine & (logit == v1), lane, float(LANES)), axis=-1, keepdims=True)
        rest = mine & (lane != i1)
        v2 = jnp.max(jnp.where(rest, logit, -jnp.inf), axis=-1, keepdims=True)
        i2 = jnp.min(jnp.where(rest & (logit == v2), lane, float(LANES)), axis=-1, keepdims=True)
        e2 = jnp.exp(v2 - v1)
        w1 = p_group / (1.0 + e2)
        w2 = p_group * e2 / (1.0 + e2)
        gate_ref[...] = jnp.where(lane == i1, w1, 0.0) + jnp.where(lane == i2, w2, 0.0)
        o_ref[...] = x

    gates = gate_ref[...]
    lane = lax.broadcasted_iota(jnp.int32, gates.shape, 1)
    acts = []
    for j in range(EXPERTS_PER_GROUP):
        gate_e = jnp.sum(jnp.where(lane == R_EXPERT + e * EXPERTS_PER_GROUP + j, gates, 0.0), axis=-1, keepdims=True)
        gu = _dot(h_ref[...], win_ref[j])
        acts.append((_silu(gu[:, :EXPERT_FF]) * gu[:, EXPERT_FF:] * gate_e).astype(BF16))
    o_ref[...] += _dot(jnp.concatenate(acts, axis=1), wout_ref[0])


def moe(x2, gain, wr_hi, wr_lo, br, win_bf, wout_bf, layer, tm=1024):
    n, d = x2.shape
    row = pl.BlockSpec((tm, d), lambda i, e: (i, 0))
    win_g = win_bf.reshape(-1, d, 2 * EXPERT_FF)
    wout_g = wout_bf.reshape(-1, EXPERTS_PER_GROUP * EXPERT_FF, d)
    return pl.pallas_call(
        _moe_body,
        grid=(n // tm, N_GROUPS),
        in_specs=[row, _full((1, d)), _full((d, LANES)), _full((d, LANES)), _full((1, LANES)),
                  pl.BlockSpec((EXPERTS_PER_GROUP, d, 2 * EXPERT_FF), lambda i, e: (layer * N_GROUPS + e, 0, 0)),
                  pl.BlockSpec((1, EXPERTS_PER_GROUP * EXPERT_FF, d), lambda i, e: (layer * N_GROUPS + e, 0, 0))],
        out_specs=row,
        out_shape=jax.ShapeDtypeStruct((n, d), F32),
        scratch_shapes=[pltpu.VMEM((tm, d), BF16), pltpu.VMEM((tm, LANES), F32)],
        compiler_params=_params("parallel", "arbitrary"),
    )(x2, gain.reshape(1, d), wr_hi, wr_lo, br, win_g, wout_g)


G_CUM, G_BETA, G_LAST = 0, GDN_HEADS, 2 * GDN_HEADS


def _gdn_gates_body(ab_ref, alog_ref, dtb_ref, gb_ref):
    t = ab_ref.shape[1]
    ab = ab_ref[0]
    sp_in = ab + dtb_ref[...]
    softplus = jnp.maximum(sp_in, 0.0) + jnp.log1p(jnp.exp(-jnp.abs(sp_in)))
    lane_row = lax.broadcasted_iota(jnp.int32, (1, LANES), 1)
    g = jnp.where(lane_row < GDN_HEADS, -jnp.exp(alog_ref[...]) * softplus, 0.0)
    blk = 4 * GDN_CHUNK
    r = lax.broadcasted_iota(jnp.int32, (blk, blk), 0)
    c = lax.broadcasted_iota(jnp.int32, (blk, blk), 1)
    same = r // GDN_CHUNK == c // GDN_CHUNK
    tri = jnp.where(same & (r >= c), 1.0, 0.0).astype(BF16)
    tot = jnp.where(same, 1.0, 0.0).astype(BF16)
    lane = lax.broadcasted_iota(jnp.int32, (blk, LANES), 1)
    for s in range(t // blk):
        rs = slice(s * blk, (s + 1) * blk)
        hi, mid, lo = _split3(g[rs])
        gc = _dot(tri, hi) + _dot(tri, mid) + _dot(tri, lo)
        gl = _dot(tot, hi) + _dot(tot, mid) + _dot(tot, lo)
        gl = pltpu.roll(gl, G_LAST, 1)
        gb_ref[0, rs, :] = jnp.where(lane < G_BETA, gc, jnp.where(lane < G_LAST, _sigmoid(ab[rs]), gl))


def gdn_gates(proj, alog_row, dtb_row):
    b, t, _ = proj.shape
    return pl.pallas_call(
        _gdn_gates_body,
        grid=(b,),
        in_specs=[pl.BlockSpec((1, t, LANES), lambda i: (i, 0, C_AB // LANES)), _full((1, LANES)), _full((1, LANES))],
        out_specs=pl.BlockSpec((1, t, LANES), lambda i: (i, 0, 0)),
        out_shape=jax.ShapeDtypeStruct((b, t, LANES), F32),
        compiler_params=_params("parallel"),
    )(proj, alog_row, dtb_row)


GDN_HEADS_PER_STEP = 8
GDN_CHUNKS_PER_TRIP = 2
GDN_SEGMENTS = 4


def _dot3(a, b):
    a_hi, a_lo = _split2(a)
    b_hi, b_lo = _split2(b)
    return _dot(a_hi, b_hi) + _dot(a_hi, b_lo) + _dot(a_lo, b_hi)


def _dot1(a, b):
    return _dot(a.astype(BF16), b.astype(BF16))


def _gdn_body(q_ref, k_ref, v_ref, gb_ref, grow_ref, o_ref,
              state_ref, gl_ref, gc_ref, kb_ref, k_ref_s, kbg_ref, vb_ref, qs_ref, qg_ref, kd_ref,
              u_ref, w_ref, a_ref):
    t = q_ref.shape[1]
    cs = GDN_CHUNK
    dk = GDN_HEAD_DIM
    nh = GDN_HEADS_PER_STEP
    hgrp = pl.program_id(1)
    seg = pl.program_id(2)
    gb_hi, gb_mid, gb_lo = _split3(gb_ref[0])
    pick_row = lax.broadcasted_iota(jnp.int32, (LANES, LANES), 0)

    @pl.when(seg == 0)
    def _():
        state_ref[...] = jnp.zeros_like(state_ref)

    def column(idx):
        sel = jnp.where(pick_row == idx, 1.0, 0.0).astype(BF16)
        return _dot(gb_hi, sel) + _dot(gb_mid, sel) + _dot(gb_lo, sel)

    for s in range(nh):
        lanes = slice(s * dk, (s + 1) * dk)
        head = nh * hgrp + s
        gcol = column(G_CUM + head)
        bcol = column(G_BETA + head)
        glast = column(G_LAST + head)
        eg = jnp.exp(gcol)
        k = k_ref[0, :, lanes]
        kb = k * bcol
        k_ref_s[s] = k.astype(BF16)
        kb_ref[s] = kb.astype(BF16)
        kbg_ref[s] = (kb * eg).astype(BF16)
        kd_ref[s] = (k * jnp.exp(glast - gcol)).astype(BF16)
        q = q_ref[0, :, lanes]
        qs_ref[s] = q.astype(BF16)
        qg_ref[s] = (q * eg).astype(BF16)
        vb_ref[s] = (v_ref[0, :, lanes] * bcol).astype(BF16)
        gl_ref[s] = glast
        gc_ref[s] = gcol

    r = lax.broadcasted_iota(jnp.int32, (cs, cs), 0)
    c = lax.broadcasted_iota(jnp.int32, (cs, cs), 1)
    tril = r >= c
    strict = r > c
    eye = jnp.where(r == c, 1.0, 0.0)

    def prep(trip, _):
        probs = [(s, trip * GDN_CHUNKS_PER_TRIP + j) for j in range(GDN_CHUNKS_PER_TRIP) for s in range(nh)]
        rows = [pl.ds(pl.multiple_of(n * cs, cs), cs) for _, n in probs]
        decay, lmat = [], []
        for (s, n), rw in zip(probs, rows):
            gr = grow_ref[0, s, pl.ds(n, 1), :]
            gc = gc_ref[s, rw, :cs]
            decay.append(jnp.where(tril, jnp.exp(jnp.where(tril, gc - gr, 0.0)), 0.0))
        for i, ((s, _), rw) in enumerate(zip(probs, rows)):
            lmat.append(jnp.where(strict, _dot_nt(kb_ref[s, rw, :], k_ref_s[s, rw, :]) * decay[i], 0.0))
        inv = [eye - m for m in lmat]
        pw = [_dot3(m, m) for m in lmat]
        span = 2
        while span < cs:
            mm = _dot3 if span == 2 else _dot1
            inv = [x + mm(x, p) for x, p in zip(inv, pw)]
            span *= 2
            if span < cs:
                pw = [_dot1(p, p) for p in pw]
        inv_bf = [x.astype(BF16) for x in inv]
        for i, ((s, _), rw) in enumerate(zip(probs, rows)):
            u_ref[s, rw, :] = _dot(inv_bf[i], vb_ref[s, rw, :])
            w_ref[s, rw, :] = _dot(inv_bf[i], kbg_ref[s, rw, :]).astype(BF16)
            a_ref[s, rw, :] = jnp.where(tril, _dot_nt(qs_ref[s, rw, :], k_ref_s[s, rw, :]) * decay[i], 0.0).astype(BF16)
        return 0

    lax.fori_loop(0, t // (cs * GDN_CHUNKS_PER_TRIP), prep, 0)

    def scan(n, states):
        r0 = pl.multiple_of(n * cs, cs)
        rows = pl.ds(r0, cs)
        s_bf = [st.astype(BF16) for st in states]
        v_bf = [(u_ref[s, rows, :] - _dot(w_ref[s, rows, :], s_bf[s])).astype(BF16) for s in range(nh)]
        new = [states[s] * jnp.exp(gl_ref[s, pl.ds(r0, 1), :]) + _dot_tn(kd_ref[s, rows, :], v_bf[s])
               for s in range(nh)]
        for s in range(nh):
            o_ref[0, rows, s * dk:(s + 1) * dk] = _dot(qg_ref[s, rows, :], s_bf[s]) + _dot(a_ref[s, rows, :], v_bf[s])
        return tuple(new)

    final = lax.fori_loop(0, t // cs, scan, tuple(state_ref[s] for s in range(nh)))
    for s in range(nh):
        state_ref[s] = final[s]


def gdn_core(proj, gb):
    b, t, _ = proj.shape
    ts = t // GDN_SEGMENTS if t % (GDN_SEGMENTS * GDN_CHUNK * GDN_CHUNKS_PER_TRIP) == 0 else t
    seg_chunks = ts // GDN_CHUNK
    nh = GDN_HEADS_PER_STEP
    wide = nh * GDN_HEAD_DIM
    per = GDN_WIDTH // wide
    g_rows = jnp.swapaxes(gb[:, :, G_CUM:G_CUM + GDN_HEADS], 1, 2).reshape(b, GDN_HEADS, t // GDN_CHUNK, GDN_CHUNK)
    sect = lambda k: pl.BlockSpec((1, ts, wide), lambda i, h, s: (i, s, k * per + h))
    bf = lambda w: pltpu.VMEM((nh, ts, w), BF16)
    return pl.pallas_call(
        _gdn_body,
        grid=(b, per, t // ts),
        in_specs=[sect(0), sect(1), sect(2),
                  pl.BlockSpec((1, ts, LANES), lambda i, h, s: (i, s, 0)),
                  pl.BlockSpec((1, nh, seg_chunks, GDN_CHUNK), lambda i, h, s: (i, h, s, 0))],
        out_specs=pl.BlockSpec((1, ts, wide), lambda i, h, s: (i, s, h)),
        out_shape=jax.ShapeDtypeStruct((b, t, GDN_WIDTH), F32),
        scratch_shapes=[pltpu.VMEM((nh, GDN_HEAD_DIM, GDN_HEAD_DIM), F32)]
                       + [pltpu.VMEM((nh, ts, LANES), F32)] * 2 + [bf(GDN_HEAD_DIM)] * 7
                       + [pltpu.VMEM((nh, ts, GDN_HEAD_DIM), F32), bf(GDN_HEAD_DIM), bf(GDN_CHUNK)],
        compiler_params=_params("parallel", "parallel", "arbitrary"),
    )(proj, proj, proj, gb, g_rows)


def _rope_tables(pos):
    half = HEAD_DIM // 2
    inv_freq = ROPE_THETA ** (-jnp.arange(half, dtype=F32) / half)
    ang = pos.astype(F32)[:, None] * inv_freq
    cos = jnp.cos(ang)
    sin = jnp.sin(ang)
    cos_t = jnp.tile(jnp.concatenate([cos, cos], axis=-1), (1, LANES // HEAD_DIM))
    sin_t = jnp.tile(jnp.concatenate([-sin, sin], axis=-1), (1, LANES // HEAD_DIM))
    return cos_t, sin_t


def _block_diag_ones(width, seg):
    idx = np.arange(width) // seg
    return jnp.asarray((idx[:, None] == idx[None, :]).astype(np.float32), dtype=BF16)


def _pad_cols(w, width):
    return jnp.pad(w, ((0, 0), (0, width - w.shape[1])))


def _even_layer(x2, b, t, norm_gain, w_in, b_gate, b_forget, cmp_pe, cmp_w1, cmp_w2, nsa_gain, fox_gain, w_out):
    d = x2.shape[1]
    o_gate = NSA_Q_W + 6 * NSA_KV_W
    o_fox = o_gate + NSA_GATE_W
    w_bf = w_in.astype(BF16)
    w_re = jnp.concatenate([w_bf[:, :o_gate], w_bf[:, o_fox:o_fox + 3 * FOX_W], w_bf[:, o_gate:o_fox],
                            w_bf[:, o_fox + 3 * FOX_W:], jnp.zeros((d, EVEN_W - w_in.shape[1]), BF16)], axis=1)
    proj = norm_matmul(x2, norm_gain, w_re).reshape(b, t, EVEN_W)

    cos, sin = _rope_tables(jnp.arange(t))
    tile = lambda g, n: jnp.tile(g, n).reshape(1, -1)
    bias = jnp.pad(jnp.concatenate([b_gate, b_forget]), (0, LANES - NSA_GATE_W - FOX_HEADS)).reshape(1, LANES)
    bd = _block_diag_ones(FOX_W, HEAD_DIM)
    (qa, ks, kw, vs, vw, kc_raw, vc_raw, qb, kb, vf, gates, cum) = even_prep(
        proj, cos, sin, tile(nsa_gain[0], NSA_HEADS), tile(nsa_gain[2], NSA_KV_HEADS), tile(nsa_gain[3], NSA_KV_HEADS),
        tile(fox_gain[0], FOX_HEADS), tile(fox_gain[1], FOX_HEADS), bias, bd)

    n_str = t // CMP_STRIDE
    half = CMP_BLOCK // 2
    eye2 = jnp.eye(NSA_KV_HEADS, dtype=F32)
    pe = jnp.tile(cmp_pe[:, :, None, :], (1, 1, NSA_KV_HEADS, 1)).reshape(2, 2, 1, half * NSA_KV_W)
    w1 = jnp.einsum('ilde,hg->ilhdge', cmp_w1, eye2).reshape(2, 2, half * NSA_KV_W, NSA_KV_W).astype(BF16)
    w2 = jnp.einsum('ide,hg->ihdge', cmp_w2, eye2).reshape(2, NSA_KV_W, NSA_KV_W).astype(BF16)
    cos_c, sin_c = _rope_tables(jnp.arange(n_str) * CMP_STRIDE + (CMP_BLOCK - 1))
    kc, vc = compress(kc_raw, vc_raw, pe, w1, w2, tile(nsa_gain[1], NSA_KV_HEADS), cos_c, sin_c, _block_diag_ones(LANES, HEAD_DIM))

    n_sel = t // SEL_BLOCK
    cs = np.arange(n_str)[:, None] * CMP_STRIDE
    ss = np.arange(n_sel)[None, :] * SEL_BLOCK
    overlap = np.clip(np.minimum(cs + CMP_BLOCK, ss + SEL_BLOCK) - np.maximum(cs, ss), 0, None) / CMP_BLOCK
    overlap[(t - CMP_BLOCK) // CMP_STRIDE + 1:] = 0.0
    ovt = jnp.asarray(overlap.T.astype(np.float32), dtype=BF16)
    o_a = nsa_attention(qa, kc, vc, ks, vs, kw, vw, gates, ovt)
    o_b = fox_attention(qb, kb, vf, cum)

    return even_out(x2, o_a.reshape(b * t, NSA_Q_W), o_b.reshape(b * t, FOX_W), w_out.astype(BF16))


def _odd_layer(x2, b, t, norm_gain, w_in, conv_w, a_log, dt_bias, gdn_gain, w_out):
    w_bf = w_in.astype(BF16)
    w_pad = jnp.concatenate([w_bf, jnp.zeros((w_in.shape[0], ODD_W - w_in.shape[1]), BF16)], axis=1)
    proj = odd_in_proj(x2.reshape(b, t, -1), norm_gain, w_pad, conv_w)
    pad8 = lambda v: jnp.pad(v, (0, LANES - GDN_HEADS)).reshape(1, LANES)
    gb = gdn_gates(proj, pad8(a_log), pad8(dt_bias))
    o = gdn_core(proj, gb)
    return odd_out(x2, o.reshape(b * t, GDN_WIDTH), proj.reshape(b * t, ODD_W), gdn_gain.reshape(1, GDN_HEAD_DIM),
                   w_out.astype(BF16))


def _moe_layer(x2, gain, w_rg, b_rg, w_re, b_re, w_ein_bf, w_eout_bf, layer):
    d = x2.shape[1]
    wr = _pad_cols(jnp.concatenate([w_rg, w_re], axis=1), LANES)
    wr_hi = wr.astype(BF16)
    wr_lo = (wr - wr_hi.astype(F32)).astype(BF16)
    br = jnp.pad(jnp.concatenate([b_rg, b_re]), (0, LANES - N_GROUPS - N_EXPERTS)).reshape(1, LANES)
    return moe(x2, gain, wr_hi, wr_lo, br, w_ein_bf, w_eout_bf, layer)


def kernel(x, norm_mix, norm_ffn, w_in_even, b_nsa_gate, b_forget, cmp_pe, cmp_w1, cmp_w2, nsa_qk_gain, fox_qk_gain,
           w_out_even, w_in_odd, conv_w, a_log, dt_bias, gdn_norm_gain, w_out_odd, w_router_group, b_router_group,
           w_router_expert, b_router_expert, w_expert_in, w_expert_out):
    b, t, d = x.shape
    x2 = x.reshape(b * t, d)
    w_ein_bf = w_expert_in.astype(BF16)
    w_eout_bf = w_expert_out.astype(BF16)
    for layer in range(norm_mix.shape[0]):
        i = layer // 2
        if layer % 2 == 0:
            x2 = _even_layer(x2, b, t, norm_mix[layer], w_in_even[i], b_nsa_gate[i], b_forget[i], cmp_pe[i], cmp_w1[i],
                             cmp_w2[i], nsa_qk_gain[i], fox_qk_gain[i], w_out_even[i])
        else:
            x2 = _odd_layer(x2, b, t, norm_mix[layer], w_in_odd[i], conv_w[i], a_log[i], dt_bias[i], gdn_norm_gain[i],
                            w_out_odd[i])
        x2 = _moe_layer(x2, norm_ffn[layer], w_router_group[layer], b_router_group[layer], w_router_expert[layer],
                        b_router_expert[layer], w_ein_bf, w_eout_bf, layer)
    return x2.reshape(b, t, d)
```

```python
import functools

import numpy as np
import jax
import jax.numpy as jnp
from jax import lax
from jax.experimental import pallas as pl
from jax.experimental.pallas import tpu as pltpu

F32 = jnp.float32
BF16 = jnp.bfloat16

HEAD_DIM = 64
ROPE_THETA = 10000.0
NSA_HEADS = 8
NSA_KV_HEADS = 2
NSA_GROUP = NSA_HEADS // NSA_KV_HEADS
CMP_BLOCK = 32
CMP_STRIDE = 16
SEL_BLOCK = 64
SEL_TOPK = 8
WINDOW = 256
FOX_HEADS = 8
GDN_HEADS = 8
GDN_HEAD_DIM = 128
GDN_WIDTH = GDN_HEADS * GDN_HEAD_DIM
CONV_WIDTH = 4
GDN_CHUNK = 64
N_GROUPS = 4
EXPERTS_PER_GROUP = 4
N_EXPERTS = N_GROUPS * EXPERTS_PER_GROUP
EXPERT_FF = 256
NORM_EPS = 1e-6
NEG_INF = -1e30
FORCE_SCORE = 1e9

LANES = 128
LOG2E = 1.4426950408889634
SUM_ROWS = 16
NSA_Q_W = NSA_HEADS * HEAD_DIM
NSA_KV_W = NSA_KV_HEADS * HEAD_DIM
NSA_GATE_W = 3 * NSA_HEADS
FOX_W = FOX_HEADS * HEAD_DIM
C_QN = 0
C_KC, C_VC, C_KS, C_VS, C_KW, C_VW = (NSA_Q_W + i * NSA_KV_W for i in range(6))
C_QF = NSA_Q_W + 6 * NSA_KV_W
C_KF = C_QF + FOX_W
C_VF = C_KF + FOX_W
C_MISC = C_VF + FOX_W
EVEN_W = C_MISC + LANES
MISC_F = NSA_GATE_W
C_AB = 4 * GDN_WIDTH
ODD_W = C_AB + LANES

VMEM_LIMIT = 56 * 1024 * 1024


def _params(*sem):
    return pltpu.CompilerParams(dimension_semantics=sem, vmem_limit_bytes=VMEM_LIMIT)


def _dot(a, b):
    return jnp.dot(a, b, preferred_element_type=F32)


def _dot_nt(a, b):
    return lax.dot_general(a, b, (((1,), (1,)), ((), ())), preferred_element_type=F32)


def _dot_tn(a, b):
    return lax.dot_general(a, b, (((0,), (0,)), ((), ())), preferred_element_type=F32)


def _split2(x):
    hi = x.astype(BF16)
    return hi, (x - hi.astype(F32)).astype(BF16)


def _split3(x):
    hi = x.astype(BF16)
    r = x - hi.astype(F32)
    mid = r.astype(BF16)
    return hi, mid, (r - mid.astype(F32)).astype(BF16)


def _sigmoid(z):
    return 1.0 / (1.0 + jnp.exp(-z))


def _silu(z):
    return z * _sigmoid(z)


def _full(shape):
    nd = len(shape)
    return pl.BlockSpec(shape, lambda *_: (0,) * nd)


def _norm_matmul_body(x_ref, g_ref, w_ref, o_ref):
    x = x_ref[...]
    ms = jnp.mean(x * x, axis=-1, keepdims=True)
    h = (x * lax.rsqrt(ms + NORM_EPS) * g_ref[...]).astype(BF16)
    o_ref[...] = _dot(h, w_ref[...])


def norm_matmul(x2, gain, w_bf, tm=512):
    n, d = x2.shape
    wp = w_bf.shape[1]
    return pl.pallas_call(
        _norm_matmul_body,
        grid=(n // tm,),
        in_specs=[pl.BlockSpec((tm, d), lambda i: (i, 0)), _full((1, d)), _full((d, wp))],
        out_specs=pl.BlockSpec((tm, wp), lambda i: (i, 0)),
        out_shape=jax.ShapeDtypeStruct((n, wp), F32),
        compiler_params=_params("parallel"),
    )(x2, gain.reshape(1, d), w_bf)


def _head_rms(x, bd, gain):
    hi, lo = _split2(x * x)
    w = x.shape[1]
    ssum = _dot(hi, bd[:w, :w]) + _dot(lo, bd[:w, :w])
    return x * lax.rsqrt(ssum * (1.0 / HEAD_DIM) + NORM_EPS) * gain


def _rope(x, cos, sin_signed, first_half):
    fwd = pltpu.roll(x, LANES - HEAD_DIM // 2, 1)
    bwd = pltpu.roll(x, HEAD_DIM // 2, 1)
    return x * cos + jnp.where(first_half, fwd, bwd) * sin_signed


def _even_prep_body(p_ref, cos_ref, sin_ref, gq_ref, gks_ref, gkw_ref, gfq_ref, gfk_ref, bias_ref, bd_ref,
                    qa_ref, ks_ref, kw_ref, vs_ref, vw_ref, kc_ref, vc_ref, qb_ref, kb_ref, vf_ref,
                    gate_ref, cum_ref, carry_ref, stage_ref):
    tr = p_ref.shape[1]
    bd = bd_ref[...]
    cos = cos_ref[...]
    sin = sin_ref[...]
    lane = lax.broadcasted_iota(jnp.int32, (1, LANES), 1)
    first_half = (lane % HEAD_DIM) < (HEAD_DIM // 2)
    scale = HEAD_DIM ** -0.5 * LOG2E

    qn = _head_rms(p_ref[0, :, C_QN:C_QN + NSA_Q_W], bd, gq_ref[...])
    for c in range(NSA_Q_W // LANES):
        sl = slice(c * LANES, (c + 1) * LANES)
        qa_ref[0, sl, :] = (_rope(qn[:, sl], cos, sin, first_half) * scale).T.astype(BF16)
    ks = _head_rms(p_ref[0, :, C_KS:C_KS + NSA_KV_W], bd, gks_ref[...])
    ks_ref[0] = _rope(ks, cos, sin, first_half).astype(BF16)
    kw = _head_rms(p_ref[0, :, C_KW:C_KW + NSA_KV_W], bd, gkw_ref[...])
    kw_ref[0] = _rope(kw, cos, sin, first_half).astype(BF16)
    vs_ref[0] = p_ref[0, :, C_VS:C_VS + NSA_KV_W].T.astype(BF16)
    vw_ref[0] = p_ref[0, :, C_VW:C_VW + NSA_KV_W].T.astype(BF16)
    stage_ref[0] = p_ref[0, :, C_KC:C_KC + NSA_KV_W]
    stage_ref[1] = p_ref[0, :, C_VC:C_VC + NSA_KV_W]
    for l in range(CMP_STRIDE):
        rows = pl.ds(l, tr // CMP_STRIDE, stride=CMP_STRIDE)
        kc_ref[0, :, l * NSA_KV_W:(l + 1) * NSA_KV_W] = stage_ref[0, rows, :]
        vc_ref[0, :, l * NSA_KV_W:(l + 1) * NSA_KV_W] = stage_ref[1, rows, :]

    qb = _head_rms(p_ref[0, :, C_QF:C_QF + FOX_W], bd, gfq_ref[...]) * scale
    kb_ref[0] = _head_rms(p_ref[0, :, C_KF:C_KF + FOX_W], bd, gfk_ref[...]).astype(BF16)
    for c in range(FOX_W // LANES):
        sl = slice(c * LANES, (c + 1) * LANES)
        qb_ref[0, sl, :] = qb[:, sl].T.astype(BF16)
        vf_ref[0, sl, :] = p_ref[0, :, C_VF + c * LANES:C_VF + (c + 1) * LANES].T.astype(BF16)

    z = p_ref[0, :, C_MISC:C_MISC + LANES] + bias_ref[...]
    gate_ref[0] = _sigmoid(z).T
    logf = jnp.minimum(z, 0.0) - jnp.log1p(jnp.exp(-jnp.abs(z)))

    @pl.when(pl.program_id(1) == 0)
    def _():
        carry_ref[...] = jnp.zeros_like(carry_ref)

    row = lax.broadcasted_iota(jnp.int32, (tr, tr), 0)
    col = lax.broadcasted_iota(jnp.int32, (tr, tr), 1)
    tril = jnp.where(row >= col, 1.0, 0.0).astype(BF16)
    hi, mid, lo = _split3(logf)
    cum = _dot(tril, hi) + _dot(tril, mid) + _dot(tril, lo) + carry_ref[...]
    cum_ref[0] = cum
    carry_ref[...] = cum[tr - 1:tr, :]


def even_prep(proj, cos, sin, gq, gks, gkw, gfq, gfk, bias, bd, tr=256):
    b, t, _ = proj.shape
    row = lambda w: pl.BlockSpec((1, tr, w), lambda i, j: (i, j, 0))
    tab = pl.BlockSpec((tr, LANES), lambda i, j: (j, 0))
    shp = lambda w, dt: jax.ShapeDtypeStruct((b, t, w), dt)
    col = lambda w: pl.BlockSpec((1, w, tr), lambda i, j: (i, 0, j))
    shp_t = lambda w, dt: jax.ShapeDtypeStruct((b, w, t), dt)
    strd = pl.BlockSpec((1, tr // CMP_STRIDE, CMP_STRIDE * NSA_KV_W), lambda i, j: (i, j, 0))
    strd_shape = jax.ShapeDtypeStruct((b, t // CMP_STRIDE, CMP_STRIDE * NSA_KV_W), F32)
    return pl.pallas_call(
        _even_prep_body,
        grid=(b, t // tr),
        in_specs=[row(EVEN_W), tab, tab, _full((1, NSA_Q_W)), _full((1, LANES)), _full((1, LANES)),
                  _full((1, FOX_W)), _full((1, FOX_W)), _full((1, LANES)), _full((FOX_W, FOX_W))],
        out_specs=[col(NSA_Q_W), row(LANES), row(LANES), col(LANES), col(LANES), strd, strd,
                   col(FOX_W), row(FOX_W), col(FOX_W), col(LANES), row(LANES)],
        out_shape=[shp_t(NSA_Q_W, BF16), shp(LANES, BF16), shp(LANES, BF16), shp_t(LANES, BF16), shp_t(LANES, BF16),
                   strd_shape, strd_shape, shp_t(FOX_W, BF16), shp(FOX_W, BF16), shp_t(FOX_W, BF16),
                   shp_t(LANES, F32), shp(LANES, F32)],
        scratch_shapes=[pltpu.VMEM((1, LANES), F32), pltpu.VMEM((2, tr, LANES), F32)],
        compiler_params=_params("parallel", "arbitrary"),
    )(proj, cos, sin, gq, gks, gkw, gfq, gfk, bias, bd)


def _gelu_tanh(x):
    return 0.5 * x * (1.0 + jnp.tanh(np.sqrt(2.0 / np.pi).astype(np.float32) * (x + 0.044715 * (x * x * x))))


def _compress_body(xk_ref, xv_ref, pe_ref, w1_ref, w2_ref, gk_ref, cos_ref, sin_ref, bd_ref, kc_ref, vc_ref):
    n = xk_ref.shape[1]
    lane = lax.broadcasted_iota(jnp.int32, (1, LANES), 1)
    first_half = (lane % HEAD_DIM) < (HEAD_DIM // 2)

    def mlp(x_ref, i):
        x = x_ref[0]
        nxt = pltpu.roll(x, n - 1, 0)
        xa = (x + pe_ref[i, 0]).astype(BF16)
        xb = (nxt + pe_ref[i, 1]).astype(BF16)
        h = _dot(xa, w1_ref[i, 0]) + _dot(xb, w1_ref[i, 1])
        return _dot(_gelu_tanh(h).astype(BF16), w2_ref[i])

    kc = _head_rms(mlp(xk_ref, 0), bd_ref[...], gk_ref[...])
    kc_ref[0] = _rope(kc, cos_ref[...], sin_ref[...], first_half).astype(BF16)
    vc_ref[0] = mlp(xv_ref, 1).T.astype(BF16)


def compress(xk, xv, pe, w1, w2, gk, cos_c, sin_c, bd):
    b, n, w = xk.shape
    blk = pl.BlockSpec((1, n, w), lambda i: (i, 0, 0))
    out = pl.BlockSpec((1, n, LANES), lambda i: (i, 0, 0))
    return pl.pallas_call(
        _compress_body,
        grid=(b,),
        in_specs=[blk, blk, _full(pe.shape), _full(w1.shape), _full(w2.shape), _full((1, LANES)),
                  _full((n, LANES)), _full((n, LANES)), _full((LANES, LANES))],
        out_specs=[out, pl.BlockSpec((1, LANES, n), lambda i: (i, 0, 0))],
        out_shape=[jax.ShapeDtypeStruct((b, n, LANES), BF16), jax.ShapeDtypeStruct((b, LANES, n), BF16)],
        compiler_params=_params("parallel"),
    )(xk, xv, pe, w1, w2, gk, cos_c, sin_c, bd)


def _flash_step(s_ref, p_ref, acc_ref, v_blk, m_i, l_i, adjust, first=False):
    n_ch = acc_ref.shape[0]
    al, ms = [], []
    for cg in range(s_ref.shape[1] // LANES):
        sl = slice(cg * LANES, (cg + 1) * LANES)
        s = adjust(s_ref[:, sl], cg)
        m_new = jnp.maximum(m_i[:, sl], jnp.max(s, axis=0, keepdims=True))
        p_ref[:, sl] = jnp.exp2(s - m_new).astype(BF16)
        al.append(jnp.exp2(m_i[:, sl] - m_new))
        ms.append(m_new)
    cat = lambda xs: jnp.concatenate(xs, axis=1)
    alpha = cat(al)
    pv = _dot(v_blk, p_ref[...])
    acc_ref[...] = pv[:n_ch] if first else alpha * acc_ref[...] + pv[:n_ch]
    return cat(ms), alpha * l_i + pv[n_ch:n_ch + 1]


NSA_KEYS_PER_QUERY_BLOCK = 2


def _nsa_body(q_ref, kc_ref, vc_ref, ks_ref, vs_ref, kw_ref, vw_ref, gate_ref, ovt_ref, o_ref,
              sel_ref, s0_ref, s1_ref, p_ref, acc_ref, *, k_top):
    tq = q_ref.shape[2]
    t_all = ks_ref.shape[1]
    n_cmp = kc_ref.shape[1]
    n_sel = ovt_ref.shape[0]
    g_n = NSA_GROUP
    c = pl.program_id(1)
    t0 = c * tq
    chan = lax.broadcasted_iota(jnp.int32, (LANES, 1), 0)
    tlane = t0 + lax.broadcasted_iota(jnp.int32, (1, tq), 1)
    gates = gate_ref[0]

    nrow = lax.broadcasted_iota(jnp.int32, (n_cmp, 1), 0)
    valid_c = (nrow * CMP_STRIDE + (CMP_BLOCK - 1)) <= tlane
    jrow = lax.broadcasted_iota(jnp.int32, (n_sel, tq), 0)
    jrow_f = jrow.astype(F32)
    cur = tlane // SEL_BLOCK
    forced = (jrow == 0) | (jrow == cur) | (jrow == cur - 1)
    future = jrow * SEL_BLOCK > tlane
    tk = NSA_KEYS_PER_QUERY_BLOCK * tq
    krow = lax.broadcasted_iota(jnp.int32, (tk, 1), 0)
    per_blk = tk // SEL_BLOCK
    w_len = tq + WINDOW
    w_start = pl.multiple_of(jnp.clip(t0 - WINDOW, 0, t_all - w_len), LANES)
    wrow = w_start + lax.broadcasted_iota(jnp.int32, (w_len, 1), 0)
    valid_w = (wrow <= tlane) & (wrow > tlane - WINDOW)

    heads = [(kvh, g) for kvh in range(NSA_KV_HEADS) for g in range(g_n)]
    zero_half = jnp.zeros((HEAD_DIM, tq), BF16)

    def on_kv_rows(h, kvh):
        blk = q_ref[0, h * HEAD_DIM:(h + 1) * HEAD_DIM, :]
        return jnp.concatenate([blk, zero_half] if kvh == 0 else [zero_half, blk], axis=0)

    qst = jnp.concatenate([on_kv_rows(h, kvh) for h, (kvh, _) in enumerate(heads)], axis=1)
    n_col = len(heads) * tq

    def softmax_cols(s, ok, guard):
        outs = []
        for cg in range(len(heads)):
            sc = jnp.where(ok, s[:, cg * tq:(cg + 1) * tq], NEG_INF)
            e = jnp.exp2(sc - jnp.max(sc, axis=0, keepdims=True))
            if guard:
                e = jnp.where(ok, e, 0.0)
            den = jnp.sum(e, axis=0, keepdims=True)
            outs.append(e * (1.0 / (jnp.where(den > 0.0, den, 1.0) if guard else den)))
        return outs

    p_c = softmax_cols(_dot(kc_ref[0], qst), valid_c, guard=True)
    o_cmp = _dot(vc_ref[0], jnp.concatenate(p_c, axis=1).astype(BF16))

    for kvh in range(NSA_KV_HEADS):
        p_sum = p_c[kvh * g_n]
        for g in range(1, g_n):
            p_sum = p_sum + p_c[kvh * g_n + g]
        p_hi, p_lo = _split2(p_sum)
        imp_t = _dot(ovt_ref[...], p_hi) + _dot(ovt_ref[...], p_lo)
        val = jnp.where(forced, FORCE_SCORE, jnp.where(future, NEG_INF, imp_t))
        sel_t = jnp.zeros((n_sel, tq), F32)
        for _ in range(k_top):
            m = jnp.max(val, axis=0, keepdims=True)
            first = jnp.min(jnp.where(val == m, jrow_f, float(n_sel)), axis=0, keepdims=True)
            pick = jrow_f == first
            sel_t = jnp.where(pick, 1.0, sel_t)
            val = jnp.where(pick, -jnp.inf, val)
        sel_ref[kvh] = sel_t

    p_w = softmax_cols(_dot(kw_ref[0, pl.ds(w_start, w_len), :], qst), valid_w, guard=False)
    o_win = _dot(vw_ref[0, :, pl.ds(w_start, w_len)], jnp.concatenate(p_w, axis=1).astype(BF16))

    def put_scores(buf, kb):
        k0 = pl.multiple_of(jnp.minimum(kb * tk, t_all - tk), tk)
        buf[...] = _dot(ks_ref[0, pl.ds(k0, tk), :], qst)

    def half_step(buf, kb, m_i, l_i):
        k0 = pl.multiple_of(jnp.minimum(kb * tk, t_all - tk), tk)
        causal = (kb * tk + krow) <= tlane
        ok = [causal & (jnp.concatenate([jnp.broadcast_to(sel_ref[kvh, pl.ds(k0 // SEL_BLOCK + r, 1), :],
                                                          (SEL_BLOCK, tq)) for r in range(per_blk)], axis=0) > 0.5)
              for kvh in range(NSA_KV_HEADS)]
        adjust = lambda s_cols, cg: jnp.where(ok[cg // g_n], s_cols, NEG_INF)
        v_blk = jnp.concatenate([vs_ref[0, :, pl.ds(k0, tk)], jnp.ones((SUM_ROWS, tk), BF16)], axis=0)
        return _flash_step(buf, p_ref, acc_ref, v_blk, m_i, l_i, adjust)

    def sel_trip(j, carry):
        put_scores(s1_ref, 2 * j + 1)
        carry = half_step(s0_ref, 2 * j, *carry)
        put_scores(s0_ref, 2 * j + 2)
        return half_step(s1_ref, 2 * j + 1, *carry)

    put_scores(s0_ref, 0)
    acc_ref[...] = jnp.zeros_like(acc_ref)
    init = (jnp.full((1, n_col), NEG_INF, F32), jnp.zeros((1, n_col), F32))
    n_blocks = (t0 + tq + tk - 1) // tk
    _, l_s = lax.fori_loop(0, (n_blocks + 1) // 2, sel_trip, init)
    o_slc = acc_ref[...] * (1.0 / l_s)

    gated = []
    for h, (kvh, _) in enumerate(heads):
        cols = slice(h * tq, (h + 1) * tq)
        rows = slice(kvh * HEAD_DIM, (kvh + 1) * HEAD_DIM)
        gated.append(gates[3 * h:3 * h + 1] * o_cmp[rows, cols] + gates[3 * h + 1:3 * h + 2] * o_slc[rows, cols]
                     + gates[3 * h + 2:3 * h + 3] * o_win[rows, cols])
    for j in range(NSA_HEADS * HEAD_DIM // LANES):
        pair = jnp.concatenate(gated[2 * j:2 * j + 2], axis=0)
        o_ref[0, :, j * LANES:(j + 1) * LANES] = pair.T.astype(BF16)


def nsa_attention(qa_t, kc, vc_t, ks, vs_t, kw, vw_t, gates_t, ovt, tq=LANES):
    b, _, t = qa_t.shape
    n_cmp = kc.shape[1]
    n_sel = ovt.shape[0]
    k_top = min(SEL_TOPK, n_sel)
    tk = NSA_KEYS_PER_QUERY_BLOCK * tq
    tok = lambda n: pl.BlockSpec((1, n, LANES), lambda i, j: (i, 0, 0))
    chn = lambda n: pl.BlockSpec((1, LANES, n), lambda i, j: (i, 0, 0))
    return pl.pallas_call(
        functools.partial(_nsa_body, k_top=k_top),
        grid=(b, t // tq),
        in_specs=[pl.BlockSpec((1, NSA_Q_W, tq), lambda i, j: (i, 0, j)), tok(n_cmp), chn(n_cmp), tok(t), chn(t),
                  tok(t), chn(t), pl.BlockSpec((1, LANES, tq), lambda i, j: (i, 0, j)), _full(ovt.shape)],
        out_specs=pl.BlockSpec((1, tq, NSA_Q_W), lambda i, j: (i, j, 0)),
        out_shape=jax.ShapeDtypeStruct((b, t, NSA_Q_W), BF16),
        scratch_shapes=[pltpu.VMEM((NSA_KV_HEADS, n_sel, tq), F32), pltpu.VMEM((tk, NSA_HEADS * tq), F32),
                        pltpu.VMEM((tk, NSA_HEADS * tq), F32), pltpu.VMEM((tk, NSA_HEADS * tq), BF16),
                        pltpu.VMEM((LANES, NSA_HEADS * tq), F32)],
        compiler_params=_params("parallel", "arbitrary"),
    )(qa_t, kc, vc_t, ks, vs_t, kw, vw_t, gates_t, ovt)


FOX_KEYS_PER_QUERY_BLOCK = 2


def _fox_body(q_ref, k_ref, v_ref, cum_ref, o_ref, ck_ref, s0_ref, s1_ref, p_ref, acc_ref, *, tq):
    t = k_ref.shape[1]
    tk = FOX_KEYS_PER_QUERY_BLOCK * tq
    pair = pl.program_id(1)

    hi, mid, lo = _split3(cum_ref[0])
    pick_row = lax.broadcasted_iota(jnp.int32, (LANES, LANES), 0)
    for h in range(2):
        sel = jnp.where(pick_row == MISC_F + 2 * pair + h, 1.0, 0.0).astype(BF16)
        ck_ref[h] = (_dot(hi, sel) + _dot(mid, sel) + _dot(lo, sel)) * LOG2E

    chan = lax.broadcasted_iota(jnp.int32, (LANES, 1), 0)
    first_head = chan < HEAD_DIM
    krow = lax.broadcasted_iota(jnp.int32, (tk, 1), 0)
    qlane = lax.broadcasted_iota(jnp.int32, (1, tq), 1)
    reps = tq // LANES
    bufs = (s0_ref, s1_ref)
    blocks = [(i, kb) for i in range(t // tq) for kb in range((i * tq) // tk + 1)]
    q_cache = {}

    def q_pair(i):
        if i not in q_cache:
            q = q_ref[0, :, i * tq:(i + 1) * tq]
            q_cache[i] = jnp.concatenate([jnp.where(first_head, q, 0), jnp.where(first_head, 0, q)], axis=1)
        return q_cache[i]

    def put_scores(n):
        i, kb = blocks[n]
        bufs[n % 2][...] = _dot(k_ref[0, kb * tk:(kb + 1) * tk, :], q_pair(i))

    put_scores(0)
    m_i = l_i = None
    for n, (i, kb) in enumerate(blocks):
        if n + 1 < len(blocks):
            put_scores(n + 1)
        last = kb == (i * tq) // tk
        ok = ((kb * tk + krow) <= (i * tq + qlane)) if last else None

        def adjust(s_cols, cg, kb=kb, last=last, ok=ok):
            s_cols = s_cols - ck_ref[cg // reps, kb * tk:(kb + 1) * tk, :]
            return jnp.where(ok[:, (cg % reps) * LANES:(cg % reps + 1) * LANES], s_cols, NEG_INF) if last else s_cols

        if kb == 0:
            m_i = jnp.full((1, 2 * tq), NEG_INF, F32)
            l_i = jnp.zeros((1, 2 * tq), F32)
        v_blk = jnp.concatenate([v_ref[0, :, kb * tk:(kb + 1) * tk], jnp.ones((SUM_ROWS, tk), BF16)], axis=0)
        m_i, l_i = _flash_step(bufs[n % 2], p_ref, acc_ref, v_blk, m_i, l_i, adjust, first=kb == 0)
        if last:
            o = acc_ref[...] * (1.0 / l_i)
            o_ref[0, i * tq:(i + 1) * tq, :] = jnp.where(first_head, o[:, :tq], o[:, tq:]).T.astype(BF16)


def fox_attention(qb_t, kb, vf_t, cum, tq=256):
    b, w, t = qb_t.shape
    pairs = w // LANES
    tk = FOX_KEYS_PER_QUERY_BLOCK * tq
    return pl.pallas_call(
        functools.partial(_fox_body, tq=tq),
        grid=(b, pairs),
        in_specs=[pl.BlockSpec((1, LANES, t), lambda i, p: (i, p, 0)),
                  pl.BlockSpec((1, t, LANES), lambda i, p: (i, 0, p)),
                  pl.BlockSpec((1, LANES, t), lambda i, p: (i, p, 0)),
                  pl.BlockSpec((1, t, LANES), lambda i, p: (i, 0, 0))],
        out_specs=pl.BlockSpec((1, t, LANES), lambda i, p: (i, 0, p)),
        out_shape=jax.ShapeDtypeStruct((b, t, w), BF16),
        scratch_shapes=[pltpu.VMEM((2, t, LANES), F32), pltpu.VMEM((tk, 2 * tq), F32), pltpu.VMEM((tk, 2 * tq), F32),
                        pltpu.VMEM((tk, 2 * tq), BF16), pltpu.VMEM((LANES, 2 * tq), F32)],
        compiler_params=_params("parallel", "arbitrary"),
    )(qb_t, kb, vf_t, cum)


def _even_out_body(x_ref, oa_ref, ob_ref, w_ref, o_ref):
    wa = oa_ref.shape[1]
    o_ref[...] = x_ref[...] + _dot(oa_ref[...], w_ref[:wa, :]) + _dot(ob_ref[...], w_ref[wa:, :])


def even_out(x2, oa, ob, w_bf, tm=512):
    n, d = x2.shape
    row = lambda w: pl.BlockSpec((tm, w), lambda i: (i, 0))
    return pl.pallas_call(
        _even_out_body,
        grid=(n // tm,),
        in_specs=[row(d), row(oa.shape[1]), row(ob.shape[1]), _full(w_bf.shape)],
        out_specs=row(d),
        out_shape=jax.ShapeDtypeStruct((n, d), F32),
        compiler_params=_params("parallel"),
    )(x2, oa, ob, w_bf)


R_GROUP = 0
R_EXPERT = N_GROUPS


def _moe_body(x_ref, g_ref, wr_hi_ref, wr_lo_ref, br_ref, win_ref, wout_ref, o_ref, h_ref, gate_ref):
    e = pl.program_id(1)

    @pl.when(e == 0)
    def _():
        x = x_ref[...]
        h = x * lax.rsqrt(jnp.mean(x * x, axis=-1, keepdims=True) + NORM_EPS) * g_ref[...]
        h_ref[...] = h.astype(BF16)
        h_hi, h_lo = _split2(h)
        logit = _dot(h_hi, wr_hi_ref[...]) + _dot(h_lo, wr_hi_ref[...]) + _dot(h_hi, wr_lo_ref[...]) + br_ref[...]
        lane_i = lax.broadcasted_iota(jnp.int32, logit.shape, 1)
        lane = lane_i.astype(F32)
        is_g = lane_i < N_GROUPS
        g_max = jnp.max(jnp.where(is_g, logit, -jnp.inf), axis=-1, keepdims=True)
        g_sel = jnp.min(jnp.where(is_g & (logit == g_max), lane, float(LANES)), axis=-1, keepdims=True)
        p_group = 1.0 / jnp.sum(jnp.where(is_g, jnp.exp(logit - g_max), 0.0), axis=-1, keepdims=True)
        group_of = ((lane_i - R_EXPERT) // EXPERTS_PER_GROUP).astype(F32)
        mine = (lane_i >= R_EXPERT) & (lane_i < R_EXPERT + N_EXPERTS) & (group_of == g_sel)
        v1 = jnp.max(jnp.where(mine, logit, -jnp.inf), axis=-1, keepdims=True)
        i1 = jnp.min(jnp.where(mine & (logit == v1), lane, float(LANES)), axis=-1, keepdims=True)
        rest = mine & (lane != i1)
        v2 = jnp.max(jnp.where(rest, logit, -jnp.inf), axis=-1, keepdims=True)
        i2 = jnp.min(jnp.where(rest & (logit == v2), lane, float(LANES)), axis=-1, keepdims=True)
        e2 = jnp.exp(v2 - v1)
        w1 = p_group / (1.0 + e2)
        w2 = p_group * e2 / (1.0 + e2)
        gate_ref[...] = jnp.where(lane == i1, w1, 0.0) + jnp.where(lane == i2, w2, 0.0)
        o_ref[...] = x

    gates = gate_ref[...]
    lane = lax.broadcasted_iota(jnp.int32, gates.shape, 1)
    acts = []
    for j in range(EXPERTS_PER_GROUP):
        gate_e = jnp.sum(jnp.where(lane == R_EXPERT + e * EXPERTS_PER_GROUP + j, gates, 0.0), axis=-1, keepdims=True)
        gu = _dot(h_ref[...], win_ref[j])
        acts.append((_silu(gu[:, :EXPERT_FF]) * gu[:, EXPERT_FF:] * gate_e).astype(BF16))
    o_ref[...] += _dot(jnp.concatenate(acts, axis=1), wout_ref[0])


def moe(x2, gain, wr_hi, wr_lo, br, win_bf, wout_bf, layer, tm=1024):
    n, d = x2.shape
    row = pl.BlockSpec((tm, d), lambda i, e: (i, 0))
    win_g = win_bf.reshape(-1, d, 2 * EXPERT_FF)
    wout_g = wout_bf.reshape(-1, EXPERTS_PER_GROUP * EXPERT_FF, d)
    return pl.pallas_call(
        _moe_body,
        grid=(n // tm, N_GROUPS),
        in_specs=[row, _full((1, d)), _full((d, LANES)), _full((d, LANES)), _full((1, LANES)),
                  pl.BlockSpec((EXPERTS_PER_GROUP, d, 2 * EXPERT_FF), lambda i, e: (layer * N_GROUPS + e, 0, 0)),
                  pl.BlockSpec((1, EXPERTS_PER_GROUP * EXPERT_FF, d), lambda i, e: (layer * N_GROUPS + e, 0, 0))],
        out_specs=row,
        out_shape=jax.ShapeDtypeStruct((n, d), F32),
        scratch_shapes=[pltpu.VMEM((tm, d), BF16), pltpu.VMEM((tm, LANES), F32)],
        compiler_params=_params("parallel", "arbitrary"),
    )(x2, gain.reshape(1, d), wr_hi, wr_lo, br, win_g, wout_g)


G_CUM, G_BETA, G_LAST = 0, GDN_HEADS, 2 * GDN_HEADS


def _gdn_gates_body(ab_ref, alog_ref, dtb_ref, gb_ref):
    t = ab_ref.shape[1]
    ab = ab_ref[0]
    sp_in = ab + dtb_ref[...]
    softplus = jnp.maximum(sp_in, 0.0) + jnp.log1p(jnp.exp(-jnp.abs(sp_in)))
    lane_row = lax.broadcasted_iota(jnp.int32, (1, LANES), 1)
    g = jnp.where(lane_row < GDN_HEADS, -jnp.exp(alog_ref[...]) * softplus, 0.0)
    blk = 4 * GDN_CHUNK
    r = lax.broadcasted_iota(jnp.int32, (blk, blk), 0)
    c = lax.broadcasted_iota(jnp.int32, (blk, blk), 1)
    same = r // GDN_CHUNK == c // GDN_CHUNK
    tri = jnp.where(same & (r >= c), 1.0, 0.0).astype(BF16)
    tot = jnp.where(same, 1.0, 0.0).astype(BF16)
    lane = lax.broadcasted_iota(jnp.int32, (blk, LANES), 1)
    for s in range(t // blk):
        rs = slice(s * blk, (s + 1) * blk)
        hi, mid, lo = _split3(g[rs])
        gc = _dot(tri, hi) + _dot(tri, mid) + _dot(tri, lo)
        gl = _dot(tot, hi) + _dot(tot, mid) + _dot(tot, lo)
        gl = pltpu.roll(gl, G_LAST, 1)
        gb_ref[0, rs, :] = jnp.where(lane < G_BETA, gc, jnp.where(lane < G_LAST, _sigmoid(ab[rs]), gl))


def gdn_gates(proj, alog_row, dtb_row):
    b, t, _ = proj.shape
    return pl.pallas_call(
        _gdn_gates_body,
        grid=(b,),
        in_specs=[pl.BlockSpec((1, t, LANES), lambda i: (i, 0, C_AB // LANES)), _full((1, LANES)), _full((1, LANES))],
        out_specs=pl.BlockSpec((1, t, LANES), lambda i: (i, 0, 0)),
        out_shape=jax.ShapeDtypeStruct((b, t, LANES), F32),
        compiler_params=_params("parallel"),
    )(proj, alog_row, dtb_row)


GDN_HEADS_PER_STEP = 8
GDN_CHUNKS_PER_TRIP = 2
GDN_SEGMENTS = 4


def _dot3(a, b):
    a_hi, a_lo = _split2(a)
    b_hi, b_lo = _split2(b)
    return _dot(a_hi, b_hi) + _dot(a_hi, b_lo) + _dot(a_lo, b_hi)


def _dot1(a, b):
    return _dot(a.astype(BF16), b.astype(BF16))


CONV_HALO = 8


def _conv_silu(ext_ref, cw_ref, lanes):
    t = ext_ref.shape[0] - CONV_HALO
    y = ext_ref[CONV_HALO:, lanes] * cw_ref[CONV_WIDTH - 1:CONV_WIDTH, lanes]
    for d in range(1, CONV_WIDTH):
        y = y + ext_ref[CONV_HALO - d:CONV_HALO - d + t, lanes] * cw_ref[CONV_WIDTH - 1 - d:CONV_WIDTH - d, lanes]
    return _silu(y)


def _l2norm(y):
    return y * lax.rsqrt(jnp.sum(y * y, axis=-1, keepdims=True) + NORM_EPS)


def _gdn_body(q_ref, k_ref, v_ref, cq_ref, ck_ref, cv_ref, gb_ref, grow_ref, z_ref, x_ref, gain_ref, wout_ref, o_ref,
              ext_ref, state_ref, gl_ref, gc_ref, kb_ref, k_ref_s, kbg_ref, vb_ref, qs_ref, qg_ref, kd_ref,
              u_ref, w_ref, a_ref, mix_ref):
    t = q_ref.shape[1]
    cs = GDN_CHUNK
    dk = GDN_HEAD_DIM
    nh = GDN_HEADS_PER_STEP
    seg = pl.program_id(1)
    gb_hi, gb_mid, gb_lo = _split3(gb_ref[0])
    pick_row = lax.broadcasted_iota(jnp.int32, (LANES, LANES), 0)

    @pl.when(seg == 0)
    def _():
        ext_ref[:, 0:CONV_HALO, :] = jnp.zeros((3, CONV_HALO, ext_ref.shape[2]), F32)
        state_ref[...] = jnp.zeros_like(state_ref)

    @pl.when(seg > 0)
    def _():
        ext_ref[:, 0:CONV_HALO, :] = ext_ref[:, t:t + CONV_HALO, :]

    for i, ref in enumerate((q_ref, k_ref, v_ref)):
        ext_ref[i, CONV_HALO:, :] = ref[0]

    def column(idx):
        sel = jnp.where(pick_row == idx, 1.0, 0.0).astype(BF16)
        return _dot(gb_hi, sel) + _dot(gb_mid, sel) + _dot(gb_lo, sel)

    for s in range(nh):
        lanes = slice(s * dk, (s + 1) * dk)
        gcol = column(G_CUM + s)
        bcol = column(G_BETA + s)
        glast = column(G_LAST + s)
        eg = jnp.exp(gcol)
        k = _l2norm(_conv_silu(ext_ref.at[1], ck_ref, lanes))
        kb = k * bcol
        k_ref_s[s] = k.astype(BF16)
        kb_ref[s] = kb.astype(BF16)
        kbg_ref[s] = (kb * eg).astype(BF16)
        kd_ref[s] = (k * jnp.exp(glast - gcol)).astype(BF16)
        q = _l2norm(_conv_silu(ext_ref.at[0], cq_ref, lanes)) * (dk ** -0.5)
        qs_ref[s] = q.astype(BF16)
        qg_ref[s] = (q * eg).astype(BF16)
        vb_ref[s] = (_conv_silu(ext_ref.at[2], cv_ref, lanes) * bcol).astype(BF16)
        gl_ref[s] = glast
        gc_ref[s] = gcol

    r = lax.broadcasted_iota(jnp.int32, (cs, cs), 0)
    c = lax.broadcasted_iota(jnp.int32, (cs, cs), 1)
    tril = r >= c
    strict = r > c
    eye = jnp.where(r == c, 1.0, 0.0)

    def prep(trip, _):
        probs = [(s, trip * GDN_CHUNKS_PER_TRIP + j) for j in range(GDN_CHUNKS_PER_TRIP) for s in range(nh)]
        rows = [pl.ds(pl.multiple_of(n * cs, cs), cs) for _, n in probs]
        decay, lmat = [], []
        for (s, n), rw in zip(probs, rows):
            gr = grow_ref[0, s, pl.ds(n, 1), :]
            gc = gc_ref[s, rw, :cs]
            decay.append(jnp.where(tril, jnp.exp(jnp.where(tril, gc - gr, 0.0)), 0.0))
        for i, ((s, _), rw) in enumerate(zip(probs, rows)):
            lmat.append(jnp.where(strict, _dot_nt(kb_ref[s, rw, :], k_ref_s[s, rw, :]) * decay[i], 0.0))
        inv = [eye - m for m in lmat]
        pw = [_dot3(m, m) for m in lmat]
        span = 2
        while span < cs:
            mm = _dot3 if span == 2 else _dot1
            inv = [x + mm(x, p) for x, p in zip(inv, pw)]
            span *= 2
            if span < cs:
                pw = [_dot1(p, p) for p in pw]
        inv_bf = [x.astype(BF16) for x in inv]
        for i, ((s, _), rw) in enumerate(zip(probs, rows)):
            u_ref[s, rw, :] = _dot(inv_bf[i], vb_ref[s, rw, :])
            w_ref[s, rw, :] = _dot(inv_bf[i], kbg_ref[s, rw, :]).astype(BF16)
            a_ref[s, rw, :] = jnp.where(tril, _dot_nt(qs_ref[s, rw, :], k_ref_s[s, rw, :]) * decay[i], 0.0).astype(BF16)
        return 0

    lax.fori_loop(0, t // (cs * GDN_CHUNKS_PER_TRIP), prep, 0)

    def scan(n, states):
        r0 = pl.multiple_of(n * cs, cs)
        rows = pl.ds(r0, cs)
        s_bf = [st.astype(BF16) for st in states]
        v_bf = [(u_ref[s, rows, :] - _dot(w_ref[s, rows, :], s_bf[s])).astype(BF16) for s in range(nh)]
        new = [states[s] * jnp.exp(gl_ref[s, pl.ds(r0, 1), :]) + _dot_tn(kd_ref[s, rows, :], v_bf[s])
               for s in range(nh)]
        for s in range(nh):
            mix_ref[rows, s * dk:(s + 1) * dk] = _dot(qg_ref[s, rows, :], s_bf[s]) + _dot(a_ref[s, rows, :], v_bf[s])
        return tuple(new)

    final = lax.fori_loop(0, t // cs, scan, tuple(state_ref[s] for s in range(nh)))
    for s in range(nh):
        state_ref[s] = final[s]

    gain = gain_ref[...]
    parts = []
    for s in range(nh):
        lanes = slice(s * dk, (s + 1) * dk)
        o = mix_ref[:, lanes]
        y = o * lax.rsqrt(jnp.mean(o * o, axis=-1, keepdims=True) + NORM_EPS) * gain
        parts.append((y * _silu(z_ref[0, :, lanes])).astype(BF16))
    o_ref[0] = x_ref[0] + _dot(jnp.concatenate(parts, axis=1), wout_ref[...])


def gdn_mixer(x3, proj, conv_w, gb, gain, wout_bf):
    b, t, d = x3.shape
    ts = t // GDN_SEGMENTS if t % (GDN_SEGMENTS * GDN_CHUNK * GDN_CHUNKS_PER_TRIP) == 0 else t
    seg_chunks = ts // GDN_CHUNK
    nh = GDN_HEADS_PER_STEP
    assert nh == GDN_HEADS, "the fused output projection needs every head of a row in one grid step"
    g_rows = jnp.swapaxes(gb[:, :, G_CUM:G_CUM + GDN_HEADS], 1, 2).reshape(b, GDN_HEADS, t // GDN_CHUNK, GDN_CHUNK)
    sect = lambda k: pl.BlockSpec((1, ts, GDN_WIDTH), lambda i, s: (i, s, k))
    taps = lambda k: pl.BlockSpec((CONV_WIDTH, GDN_WIDTH), lambda i, s: (0, k))
    rows = lambda w: pl.BlockSpec((1, ts, w), lambda i, s: (i, s, 0))
    bf = lambda w: pltpu.VMEM((nh, ts, w), BF16)
    return pl.pallas_call(
        _gdn_body,
        grid=(b, t // ts),
        in_specs=[sect(0), sect(1), sect(2), taps(0), taps(1), taps(2), rows(LANES),
                  pl.BlockSpec((1, nh, seg_chunks, GDN_CHUNK), lambda i, s: (i, 0, s, 0)),
                  sect(3), rows(d), _full((1, GDN_HEAD_DIM)), _full(wout_bf.shape)],
        out_specs=rows(d),
        out_shape=jax.ShapeDtypeStruct((b, t, d), F32),
        scratch_shapes=[pltpu.VMEM((3, CONV_HALO + ts, GDN_WIDTH), F32),
                        pltpu.VMEM((nh, GDN_HEAD_DIM, GDN_HEAD_DIM), F32)]
                       + [pltpu.VMEM((nh, ts, LANES), F32)] * 2 + [bf(GDN_HEAD_DIM)] * 7
                       + [pltpu.VMEM((nh, ts, GDN_HEAD_DIM), F32), bf(GDN_HEAD_DIM), bf(GDN_CHUNK),
                          pltpu.VMEM((ts, GDN_WIDTH), F32)],
        compiler_params=_params("parallel", "arbitrary"),
    )(proj, proj, proj, conv_w, conv_w, conv_w, gb, g_rows, proj, x3, gain, wout_bf)


def _rope_tables(pos):
    half = HEAD_DIM // 2
    inv_freq = ROPE_THETA ** (-jnp.arange(half, dtype=F32) / half)
    ang = pos.astype(F32)[:, None] * inv_freq
    cos = jnp.cos(ang)
    sin = jnp.sin(ang)
    cos_t = jnp.tile(jnp.concatenate([cos, cos], axis=-1), (1, LANES // HEAD_DIM))
    sin_t = jnp.tile(jnp.concatenate([-sin, sin], axis=-1), (1, LANES // HEAD_DIM))
    return cos_t, sin_t


def _block_diag_ones(width, seg):
    idx = np.arange(width) // seg
    return jnp.asarray((idx[:, None] == idx[None, :]).astype(np.float32), dtype=BF16)


def _pad_cols(w, width):
    return jnp.pad(w, ((0, 0), (0, width - w.shape[1])))


def _even_layer(x2, b, t, norm_gain, w_in, b_gate, b_forget, cmp_pe, cmp_w1, cmp_w2, nsa_gain, fox_gain, w_out):
    d = x2.shape[1]
    o_gate = NSA_Q_W + 6 * NSA_KV_W
    o_fox = o_gate + NSA_GATE_W
    w_bf = w_in.astype(BF16)
    w_re = jnp.concatenate([w_bf[:, :o_gate], w_bf[:, o_fox:o_fox + 3 * FOX_W], w_bf[:, o_gate:o_fox],
                            w_bf[:, o_fox + 3 * FOX_W:], jnp.zeros((d, EVEN_W - w_in.shape[1]), BF16)], axis=1)
    proj = norm_matmul(x2, norm_gain, w_re).reshape(b, t, EVEN_W)

    cos, sin = _rope_tables(jnp.arange(t))
    tile = lambda g, n: jnp.tile(g, n).reshape(1, -1)
    bias = jnp.pad(jnp.concatenate([b_gate, b_forget]), (0, LANES - NSA_GATE_W - FOX_HEADS)).reshape(1, LANES)
    bd = _block_diag_ones(FOX_W, HEAD_DIM)
    (qa, ks, kw, vs, vw, kc_raw, vc_raw, qb, kb, vf, gates, cum) = even_prep(
        proj, cos, sin, tile(nsa_gain[0], NSA_HEADS), tile(nsa_gain[2], NSA_KV_HEADS), tile(nsa_gain[3], NSA_KV_HEADS),
        tile(fox_gain[0], FOX_HEADS), tile(fox_gain[1], FOX_HEADS), bias, bd)

    n_str = t // CMP_STRIDE
    half = CMP_BLOCK // 2
    eye2 = jnp.eye(NSA_KV_HEADS, dtype=F32)
    pe = jnp.tile(cmp_pe[:, :, None, :], (1, 1, NSA_KV_HEADS, 1)).reshape(2, 2, 1, half * NSA_KV_W)
    w1 = jnp.einsum('ilde,hg->ilhdge', cmp_w1, eye2).reshape(2, 2, half * NSA_KV_W, NSA_KV_W).astype(BF16)
    w2 = jnp.einsum('ide,hg->ihdge', cmp_w2, eye2).reshape(2, NSA_KV_W, NSA_KV_W).astype(BF16)
    cos_c, sin_c = _rope_tables(jnp.arange(n_str) * CMP_STRIDE + (CMP_BLOCK - 1))
    kc, vc = compress(kc_raw, vc_raw, pe, w1, w2, tile(nsa_gain[1], NSA_KV_HEADS), cos_c, sin_c, _block_diag_ones(LANES, HEAD_DIM))

    n_sel = t // SEL_BLOCK
    cs = np.arange(n_str)[:, None] * CMP_STRIDE
    ss = np.arange(n_sel)[None, :] * SEL_BLOCK
    overlap = np.clip(np.minimum(cs + CMP_BLOCK, ss + SEL_BLOCK) - np.maximum(cs, ss), 0, None) / CMP_BLOCK
    overlap[(t - CMP_BLOCK) // CMP_STRIDE + 1:] = 0.0
    ovt = jnp.asarray(overlap.T.astype(np.float32), dtype=BF16)
    o_a = nsa_attention(qa, kc, vc, ks, vs, kw, vw, gates, ovt)
    o_b = fox_attention(qb, kb, vf, cum)

    return even_out(x2, o_a.reshape(b * t, NSA_Q_W), o_b.reshape(b * t, FOX_W), w_out.astype(BF16))


def _odd_layer(x2, b, t, norm_gain, w_in, conv_w, a_log, dt_bias, gdn_gain, w_out):
    w_bf = w_in.astype(BF16)
    w_pad = jnp.concatenate([w_bf, jnp.zeros((w_in.shape[0], ODD_W - w_in.shape[1]), BF16)], axis=1)
    proj = norm_matmul(x2, norm_gain, w_pad).reshape(b, t, ODD_W)
    pad8 = lambda v: jnp.pad(v, (0, LANES - GDN_HEADS)).reshape(1, LANES)
    gb = gdn_gates(proj, pad8(a_log), pad8(dt_bias))
    out = gdn_mixer(x2.reshape(b, t, -1), proj, conv_w, gb, gdn_gain.reshape(1, GDN_HEAD_DIM), w_out.astype(BF16))
    return out.reshape(b * t, -1)


def _moe_layer(x2, gain, w_rg, b_rg, w_re, b_re, w_ein_bf, w_eout_bf, layer):
    d = x2.shape[1]
    wr = _pad_cols(jnp.concatenate([w_rg, w_re], axis=1), LANES)
    wr_hi = wr.astype(BF16)
    wr_lo = (wr - wr_hi.astype(F32)).astype(BF16)
    br = jnp.pad(jnp.concatenate([b_rg, b_re]), (0, LANES - N_GROUPS - N_EXPERTS)).reshape(1, LANES)
    return moe(x2, gain, wr_hi, wr_lo, br, w_ein_bf, w_eout_bf, layer)


def kernel(x, norm_mix, norm_ffn, w_in_even, b_nsa_gate, b_forget, cmp_pe, cmp_w1, cmp_w2, nsa_qk_gain, fox_qk_gain,
           w_out_even, w_in_odd, conv_w, a_log, dt_bias, gdn_norm_gain, w_out_odd, w_router_group, b_router_group,
           w_router_expert, b_router_expert, w_expert_in, w_expert_out):
    b, t, d = x.shape
    x2 = x.reshape(b * t, d)
    w_ein_bf = w_expert_in.astype(BF16)
    w_eout_bf = w_expert_out.astype(BF16)
    for layer in range(norm_mix.shape[0]):
        i = layer // 2
        if layer % 2 == 0:
            x2 = _even_layer(x2, b, t, norm_mix[layer], w_in_even[i], b_nsa_gate[i], b_forget[i], cmp_pe[i], cmp_w1[i],
                             cmp_w2[i], nsa_qk_gain[i], fox_qk_gain[i], w_out_even[i])
        else:
            x2 = _odd_layer(x2, b, t, norm_mix[layer], w_in_odd[i], conv_w[i], a_log[i], dt_bias[i], gdn_norm_gain[i],
                            w_out_odd[i])
        x2 = _moe_layer(x2, norm_ffn[layer], w_router_group[layer], b_router_group[layer], w_router_expert[layer],
                        b_router_expert[layer], w_ein_bf, w_eout_bf, layer)
    return x2.reshape(b, t, d)
```

```python
import functools

import numpy as np
import jax
import jax.numpy as jnp
from jax import lax
from jax.experimental import pallas as pl
from jax.experimental.pallas import tpu as pltpu

F32 = jnp.float32
BF16 = jnp.bfloat16

HEAD_DIM = 64
ROPE_THETA = 10000.0
NSA_HEADS = 8
NSA_KV_HEADS = 2
NSA_GROUP = NSA_HEADS // NSA_KV_HEADS
CMP_BLOCK = 32
CMP_STRIDE = 16
SEL_BLOCK = 64
SEL_TOPK = 8
WINDOW = 256
FOX_HEADS = 8
GDN_HEADS = 8
GDN_HEAD_DIM = 128
GDN_WIDTH = GDN_HEADS * GDN_HEAD_DIM
CONV_WIDTH = 4
GDN_CHUNK = 64
N_GROUPS = 4
EXPERTS_PER_GROUP = 4
N_EXPERTS = N_GROUPS * EXPERTS_PER_GROUP
EXPERT_FF = 256
NORM_EPS = 1e-6
NEG_INF = -1e30
FORCE_SCORE = 1e9

LANES = 128
LOG2E = 1.4426950408889634
SUM_ROWS = 16
NSA_Q_W = NSA_HEADS * HEAD_DIM
NSA_KV_W = NSA_KV_HEADS * HEAD_DIM
NSA_GATE_W = 3 * NSA_HEADS
FOX_W = FOX_HEADS * HEAD_DIM
C_QN = 0
C_KC, C_VC, C_KS, C_VS, C_KW, C_VW = (NSA_Q_W + i * NSA_KV_W for i in range(6))
C_QF = NSA_Q_W + 6 * NSA_KV_W
C_KF = C_QF + FOX_W
C_VF = C_KF + FOX_W
C_MISC = C_VF + FOX_W
EVEN_W = C_MISC + LANES
MISC_F = NSA_GATE_W
C_AB = 4 * GDN_WIDTH
ODD_W = C_AB + LANES

VMEM_LIMIT = 56 * 1024 * 1024


def _params(*sem):
    return pltpu.CompilerParams(dimension_semantics=sem, vmem_limit_bytes=VMEM_LIMIT)


def _dot(a, b):
    return jnp.dot(a, b, preferred_element_type=F32)


def _dot_nt(a, b):
    return lax.dot_general(a, b, (((1,), (1,)), ((), ())), preferred_element_type=F32)


def _dot_tn(a, b):
    return lax.dot_general(a, b, (((0,), (0,)), ((), ())), preferred_element_type=F32)


def _split2(x):
    hi = x.astype(BF16)
    return hi, (x - hi.astype(F32)).astype(BF16)


def _split3(x):
    hi = x.astype(BF16)
    r = x - hi.astype(F32)
    mid = r.astype(BF16)
    return hi, mid, (r - mid.astype(F32)).astype(BF16)


def _sigmoid(z):
    return 1.0 / (1.0 + jnp.exp(-z))


def _silu(z):
    return z * _sigmoid(z)


def _full(shape):
    nd = len(shape)
    return pl.BlockSpec(shape, lambda *_: (0,) * nd)


def _norm_matmul_body(x_ref, g_ref, w_ref, o_ref):
    x = x_ref[...]
    ms = jnp.mean(x * x, axis=-1, keepdims=True)
    h = (x * lax.rsqrt(ms + NORM_EPS) * g_ref[...]).astype(BF16)
    o_ref[...] = _dot(h, w_ref[...])


def norm_matmul(x2, gain, w_bf, tm=512):
    n, d = x2.shape
    wp = w_bf.shape[1]
    return pl.pallas_call(
        _norm_matmul_body,
        grid=(n // tm,),
        in_specs=[pl.BlockSpec((tm, d), lambda i: (i, 0)), _full((1, d)), _full((d, wp))],
        out_specs=pl.BlockSpec((tm, wp), lambda i: (i, 0)),
        out_shape=jax.ShapeDtypeStruct((n, wp), F32),
        compiler_params=_params("parallel"),
    )(x2, gain.reshape(1, d), w_bf)


def _head_rms(x, bd, gain):
    hi, lo = _split2(x * x)
    w = x.shape[1]
    ssum = _dot(hi, bd[:w, :w]) + _dot(lo, bd[:w, :w])
    return x * lax.rsqrt(ssum * (1.0 / HEAD_DIM) + NORM_EPS) * gain


def _rope(x, cos, sin_signed, first_half):
    fwd = pltpu.roll(x, LANES - HEAD_DIM // 2, 1)
    bwd = pltpu.roll(x, HEAD_DIM // 2, 1)
    return x * cos + jnp.where(first_half, fwd, bwd) * sin_signed


def _even_prep_body(p_ref, cos_ref, sin_ref, gq_ref, gks_ref, gkw_ref, gfq_ref, gfk_ref, bias_ref, bd_ref,
                    qa_ref, ks_ref, kw_ref, vs_ref, vw_ref, kc_ref, vc_ref, qb_ref, kb_ref, vf_ref,
                    gate_ref, cum_ref, carry_ref, stage_ref):
    tr = p_ref.shape[1]
    bd = bd_ref[...]
    cos = cos_ref[...]
    sin = sin_ref[...]
    lane = lax.broadcasted_iota(jnp.int32, (1, LANES), 1)
    first_half = (lane % HEAD_DIM) < (HEAD_DIM // 2)
    scale = HEAD_DIM ** -0.5 * LOG2E

    qn = _head_rms(p_ref[0, :, C_QN:C_QN + NSA_Q_W], bd, gq_ref[...])
    for c in range(NSA_Q_W // LANES):
        sl = slice(c * LANES, (c + 1) * LANES)
        qa_ref[0, sl, :] = (_rope(qn[:, sl], cos, sin, first_half) * scale).T.astype(BF16)
    ks = _head_rms(p_ref[0, :, C_KS:C_KS + NSA_KV_W], bd, gks_ref[...])
    ks_ref[0] = _rope(ks, cos, sin, first_half).astype(BF16)
    kw = _head_rms(p_ref[0, :, C_KW:C_KW + NSA_KV_W], bd, gkw_ref[...])
    kw_ref[0] = _rope(kw, cos, sin, first_half).astype(BF16)
    vs_ref[0] = p_ref[0, :, C_VS:C_VS + NSA_KV_W].T.astype(BF16)
    vw_ref[0] = p_ref[0, :, C_VW:C_VW + NSA_KV_W].T.astype(BF16)
    stage_ref[0] = p_ref[0, :, C_KC:C_KC + NSA_KV_W]
    stage_ref[1] = p_ref[0, :, C_VC:C_VC + NSA_KV_W]
    for l in range(CMP_STRIDE):
        rows = pl.ds(l, tr // CMP_STRIDE, stride=CMP_STRIDE)
        kc_ref[0, :, l * NSA_KV_W:(l + 1) * NSA_KV_W] = stage_ref[0, rows, :]
        vc_ref[0, :, l * NSA_KV_W:(l + 1) * NSA_KV_W] = stage_ref[1, rows, :]

    qb = _head_rms(p_ref[0, :, C_QF:C_QF + FOX_W], bd, gfq_ref[...]) * scale
    kb_ref[0] = _head_rms(p_ref[0, :, C_KF:C_KF + FOX_W], bd, gfk_ref[...]).astype(BF16)
    for c in range(FOX_W // LANES):
        sl = slice(c * LANES, (c + 1) * LANES)
        qb_ref[0, sl, :] = qb[:, sl].T.astype(BF16)
        vf_ref[0, sl, :] = p_ref[0, :, C_VF + c * LANES:C_VF + (c + 1) * LANES].T.astype(BF16)

    z = p_ref[0, :, C_MISC:C_MISC + LANES] + bias_ref[...]
    gate_ref[0] = _sigmoid(z).T
    logf = jnp.minimum(z, 0.0) - jnp.log1p(jnp.exp(-jnp.abs(z)))

    @pl.when(pl.program_id(1) == 0)
    def _():
        carry_ref[...] = jnp.zeros_like(carry_ref)

    row = lax.broadcasted_iota(jnp.int32, (tr, tr), 0)
    col = lax.broadcasted_iota(jnp.int32, (tr, tr), 1)
    tril = jnp.where(row >= col, 1.0, 0.0).astype(BF16)
    hi, mid, lo = _split3(logf)
    cum = _dot(tril, hi) + _dot(tril, mid) + _dot(tril, lo) + carry_ref[...]
    cum_ref[0] = cum
    carry_ref[...] = cum[tr - 1:tr, :]


def even_prep(proj, cos, sin, gq, gks, gkw, gfq, gfk, bias, bd, tr=256):
    b, t, _ = proj.shape
    row = lambda w: pl.BlockSpec((1, tr, w), lambda i, j: (i, j, 0))
    tab = pl.BlockSpec((tr, LANES), lambda i, j: (j, 0))
    shp = lambda w, dt: jax.ShapeDtypeStruct((b, t, w), dt)
    col = lambda w: pl.BlockSpec((1, w, tr), lambda i, j: (i, 0, j))
    shp_t = lambda w, dt: jax.ShapeDtypeStruct((b, w, t), dt)
    strd = pl.BlockSpec((1, tr // CMP_STRIDE, CMP_STRIDE * NSA_KV_W), lambda i, j: (i, j, 0))
    strd_shape = jax.ShapeDtypeStruct((b, t // CMP_STRIDE, CMP_STRIDE * NSA_KV_W), F32)
    return pl.pallas_call(
        _even_prep_body,
        grid=(b, t // tr),
        in_specs=[row(EVEN_W), tab, tab, _full((1, NSA_Q_W)), _full((1, LANES)), _full((1, LANES)),
                  _full((1, FOX_W)), _full((1, FOX_W)), _full((1, LANES)), _full((FOX_W, FOX_W))],
        out_specs=[col(NSA_Q_W), row(LANES), row(LANES), col(LANES), col(LANES), strd, strd,
                   col(FOX_W), row(FOX_W), col(FOX_W), col(LANES), row(LANES)],
        out_shape=[shp_t(NSA_Q_W, BF16), shp(LANES, BF16), shp(LANES, BF16), shp_t(LANES, BF16), shp_t(LANES, BF16),
                   strd_shape, strd_shape, shp_t(FOX_W, BF16), shp(FOX_W, BF16), shp_t(FOX_W, BF16),
                   shp_t(LANES, F32), shp(LANES, F32)],
        scratch_shapes=[pltpu.VMEM((1, LANES), F32), pltpu.VMEM((2, tr, LANES), F32)],
        compiler_params=_params("parallel", "arbitrary"),
    )(proj, cos, sin, gq, gks, gkw, gfq, gfk, bias, bd)


def _gelu_tanh(x):
    return 0.5 * x * (1.0 + jnp.tanh(np.sqrt(2.0 / np.pi).astype(np.float32) * (x + 0.044715 * (x * x * x))))


def _compress_body(xk_ref, xv_ref, pe_ref, w1_ref, w2_ref, gk_ref, cos_ref, sin_ref, bd_ref, kc_ref, vc_ref):
    n = xk_ref.shape[1]
    lane = lax.broadcasted_iota(jnp.int32, (1, LANES), 1)
    first_half = (lane % HEAD_DIM) < (HEAD_DIM // 2)

    def mlp(x_ref, i):
        x = x_ref[0]
        nxt = pltpu.roll(x, n - 1, 0)
        xa = (x + pe_ref[i, 0]).astype(BF16)
        xb = (nxt + pe_ref[i, 1]).astype(BF16)
        h = _dot(xa, w1_ref[i, 0]) + _dot(xb, w1_ref[i, 1])
        return _dot(_gelu_tanh(h).astype(BF16), w2_ref[i])

    kc = _head_rms(mlp(xk_ref, 0), bd_ref[...], gk_ref[...])
    kc_ref[0] = _rope(kc, cos_ref[...], sin_ref[...], first_half).astype(BF16)
    vc_ref[0] = mlp(xv_ref, 1).T.astype(BF16)


def compress(xk, xv, pe, w1, w2, gk, cos_c, sin_c, bd):
    b, n, w = xk.shape
    blk = pl.BlockSpec((1, n, w), lambda i: (i, 0, 0))
    out = pl.BlockSpec((1, n, LANES), lambda i: (i, 0, 0))
    return pl.pallas_call(
        _compress_body,
        grid=(b,),
        in_specs=[blk, blk, _full(pe.shape), _full(w1.shape), _full(w2.shape), _full((1, LANES)),
                  _full((n, LANES)), _full((n, LANES)), _full((LANES, LANES))],
        out_specs=[out, pl.BlockSpec((1, LANES, n), lambda i: (i, 0, 0))],
        out_shape=[jax.ShapeDtypeStruct((b, n, LANES), BF16), jax.ShapeDtypeStruct((b, LANES, n), BF16)],
        compiler_params=_params("parallel"),
    )(xk, xv, pe, w1, w2, gk, cos_c, sin_c, bd)


def _flash_step(s_ref, p_ref, acc_ref, v_blk, m_i, l_i, adjust, first=False):
    n_ch = acc_ref.shape[0]
    al, ms = [], []
    for cg in range(s_ref.shape[1] // LANES):
        sl = slice(cg * LANES, (cg + 1) * LANES)
        s = adjust(s_ref[:, sl], cg)
        m_new = jnp.maximum(m_i[:, sl], jnp.max(s, axis=0, keepdims=True))
        p_ref[:, sl] = jnp.exp2(s - m_new).astype(BF16)
        al.append(jnp.exp2(m_i[:, sl] - m_new))
        ms.append(m_new)
    cat = lambda xs: jnp.concatenate(xs, axis=1)
    alpha = cat(al)
    pv = _dot(v_blk, p_ref[...])
    acc_ref[...] = pv[:n_ch] if first else alpha * acc_ref[...] + pv[:n_ch]
    return cat(ms), alpha * l_i + pv[n_ch:n_ch + 1]


NSA_KEYS_PER_QUERY_BLOCK = 2


def _nsa_body(q_ref, kc_ref, vc_ref, ks_ref, vs_ref, kw_ref, vw_ref, gate_ref, ovt_ref, o_ref,
              sel_ref, s0_ref, s1_ref, p_ref, acc_ref, *, k_top):
    tq = q_ref.shape[2]
    t_all = ks_ref.shape[1]
    n_cmp = kc_ref.shape[1]
    n_sel = ovt_ref.shape[0]
    g_n = NSA_GROUP
    c = pl.program_id(1)
    t0 = c * tq
    chan = lax.broadcasted_iota(jnp.int32, (LANES, 1), 0)
    tlane = t0 + lax.broadcasted_iota(jnp.int32, (1, tq), 1)
    gates = gate_ref[0]

    nrow = lax.broadcasted_iota(jnp.int32, (n_cmp, 1), 0)
    valid_c = (nrow * CMP_STRIDE + (CMP_BLOCK - 1)) <= tlane
    jrow = lax.broadcasted_iota(jnp.int32, (n_sel, tq), 0)
    jrow_f = jrow.astype(F32)
    cur = tlane // SEL_BLOCK
    forced = (jrow == 0) | (jrow == cur) | (jrow == cur - 1)
    future = jrow * SEL_BLOCK > tlane
    tk = NSA_KEYS_PER_QUERY_BLOCK * tq
    krow = lax.broadcasted_iota(jnp.int32, (tk, 1), 0)
    per_blk = tk // SEL_BLOCK
    w_len = tq + WINDOW
    w_start = pl.multiple_of(jnp.clip(t0 - WINDOW, 0, t_all - w_len), LANES)
    wrow = w_start + lax.broadcasted_iota(jnp.int32, (w_len, 1), 0)
    valid_w = (wrow <= tlane) & (wrow > tlane - WINDOW)

    heads = [(kvh, g) for kvh in range(NSA_KV_HEADS) for g in range(g_n)]
    zero_half = jnp.zeros((HEAD_DIM, tq), BF16)

    def on_kv_rows(h, kvh):
        blk = q_ref[0, h * HEAD_DIM:(h + 1) * HEAD_DIM, :]
        return jnp.concatenate([blk, zero_half] if kvh == 0 else [zero_half, blk], axis=0)

    qst = jnp.concatenate([on_kv_rows(h, kvh) for h, (kvh, _) in enumerate(heads)], axis=1)
    n_col = len(heads) * tq

    def softmax_cols(s, ok, guard):
        outs = []
        for cg in range(len(heads)):
            sc = jnp.where(ok, s[:, cg * tq:(cg + 1) * tq], NEG_INF)
            e = jnp.exp2(sc - jnp.max(sc, axis=0, keepdims=True))
            if guard:
                e = jnp.where(ok, e, 0.0)
            den = jnp.sum(e, axis=0, keepdims=True)
            outs.append(e * (1.0 / (jnp.where(den > 0.0, den, 1.0) if guard else den)))
        return outs

    p_c = softmax_cols(_dot(kc_ref[0], qst), valid_c, guard=True)
    o_cmp = _dot(vc_ref[0], jnp.concatenate(p_c, axis=1).astype(BF16))

    for kvh in range(NSA_KV_HEADS):
        p_sum = p_c[kvh * g_n]
        for g in range(1, g_n):
            p_sum = p_sum + p_c[kvh * g_n + g]
        p_hi, p_lo = _split2(p_sum)
        imp_t = _dot(ovt_ref[...], p_hi) + _dot(ovt_ref[...], p_lo)
        val = jnp.where(forced, FORCE_SCORE, jnp.where(future, NEG_INF, imp_t))
        sel_t = jnp.zeros((n_sel, tq), F32)
        for _ in range(k_top):
            m = jnp.max(val, axis=0, keepdims=True)
            first = jnp.min(jnp.where(val == m, jrow_f, float(n_sel)), axis=0, keepdims=True)
            pick = jrow_f == first
            sel_t = jnp.where(pick, 1.0, sel_t)
            val = jnp.where(pick, -jnp.inf, val)
        sel_ref[kvh] = sel_t

    p_w = softmax_cols(_dot(kw_ref[0, pl.ds(w_start, w_len), :], qst), valid_w, guard=False)
    o_win = _dot(vw_ref[0, :, pl.ds(w_start, w_len)], jnp.concatenate(p_w, axis=1).astype(BF16))

    def put_scores(buf, kb):
        k0 = pl.multiple_of(jnp.minimum(kb * tk, t_all - tk), tk)
        buf[...] = _dot(ks_ref[0, pl.ds(k0, tk), :], qst)

    def half_step(buf, kb, m_i, l_i):
        k0 = pl.multiple_of(jnp.minimum(kb * tk, t_all - tk), tk)
        causal = (kb * tk + krow) <= tlane
        ok = [causal & (jnp.concatenate([jnp.broadcast_to(sel_ref[kvh, pl.ds(k0 // SEL_BLOCK + r, 1), :],
                                                          (SEL_BLOCK, tq)) for r in range(per_blk)], axis=0) > 0.5)
              for kvh in range(NSA_KV_HEADS)]
        adjust = lambda s_cols, cg: jnp.where(ok[cg // g_n], s_cols, NEG_INF)
        v_blk = jnp.concatenate([vs_ref[0, :, pl.ds(k0, tk)], jnp.ones((SUM_ROWS, tk), BF16)], axis=0)
        return _flash_step(buf, p_ref, acc_ref, v_blk, m_i, l_i, adjust)

    def sel_trip(j, carry):
        put_scores(s1_ref, 2 * j + 1)
        carry = half_step(s0_ref, 2 * j, *carry)
        put_scores(s0_ref, 2 * j + 2)
        return half_step(s1_ref, 2 * j + 1, *carry)

    put_scores(s0_ref, 0)
    acc_ref[...] = jnp.zeros_like(acc_ref)
    init = (jnp.full((1, n_col), NEG_INF, F32), jnp.zeros((1, n_col), F32))
    n_blocks = (t0 + tq + tk - 1) // tk
    _, l_s = lax.fori_loop(0, (n_blocks + 1) // 2, sel_trip, init)
    o_slc = acc_ref[...] * (1.0 / l_s)

    gated = []
    for h, (kvh, _) in enumerate(heads):
        cols = slice(h * tq, (h + 1) * tq)
        rows = slice(kvh * HEAD_DIM, (kvh + 1) * HEAD_DIM)
        gated.append(gates[3 * h:3 * h + 1] * o_cmp[rows, cols] + gates[3 * h + 1:3 * h + 2] * o_slc[rows, cols]
                     + gates[3 * h + 2:3 * h + 3] * o_win[rows, cols])
    for j in range(NSA_HEADS * HEAD_DIM // LANES):
        pair = jnp.concatenate(gated[2 * j:2 * j + 2], axis=0)
        o_ref[0, :, j * LANES:(j + 1) * LANES] = pair.T.astype(BF16)


def nsa_attention(qa_t, kc, vc_t, ks, vs_t, kw, vw_t, gates_t, ovt, tq=LANES):
    b, _, t = qa_t.shape
    n_cmp = kc.shape[1]
    n_sel = ovt.shape[0]
    k_top = min(SEL_TOPK, n_sel)
    tk = NSA_KEYS_PER_QUERY_BLOCK * tq
    tok = lambda n: pl.BlockSpec((1, n, LANES), lambda i, j: (i, 0, 0))
    chn = lambda n: pl.BlockSpec((1, LANES, n), lambda i, j: (i, 0, 0))
    return pl.pallas_call(
        functools.partial(_nsa_body, k_top=k_top),
        grid=(b, t // tq),
        in_specs=[pl.BlockSpec((1, NSA_Q_W, tq), lambda i, j: (i, 0, j)), tok(n_cmp), chn(n_cmp), tok(t), chn(t),
                  tok(t), chn(t), pl.BlockSpec((1, LANES, tq), lambda i, j: (i, 0, j)), _full(ovt.shape)],
        out_specs=pl.BlockSpec((1, tq, NSA_Q_W), lambda i, j: (i, j, 0)),
        out_shape=jax.ShapeDtypeStruct((b, t, NSA_Q_W), BF16),
        scratch_shapes=[pltpu.VMEM((NSA_KV_HEADS, n_sel, tq), F32), pltpu.VMEM((tk, NSA_HEADS * tq), F32),
                        pltpu.VMEM((tk, NSA_HEADS * tq), F32), pltpu.VMEM((tk, NSA_HEADS * tq), BF16),
                        pltpu.VMEM((LANES, NSA_HEADS * tq), F32)],
        compiler_params=_params("parallel", "arbitrary"),
    )(qa_t, kc, vc_t, ks, vs_t, kw, vw_t, gates_t, ovt)


FOX_KEYS_PER_QUERY_BLOCK = 1


def _fox_body(q_ref, k_ref, v_ref, cum_ref, o_ref, ck_ref, s0_ref, s1_ref, p_ref, acc_ref, *, tq):
    t = k_ref.shape[1]
    tk = FOX_KEYS_PER_QUERY_BLOCK * tq
    pair = pl.program_id(1)

    hi, mid, lo = _split3(cum_ref[0])
    pick_row = lax.broadcasted_iota(jnp.int32, (LANES, LANES), 0)
    for h in range(2):
        sel = jnp.where(pick_row == MISC_F + 2 * pair + h, 1.0, 0.0).astype(BF16)
        ck_ref[h] = (_dot(hi, sel) + _dot(mid, sel) + _dot(lo, sel)) * LOG2E

    chan = lax.broadcasted_iota(jnp.int32, (LANES, 1), 0)
    first_head = chan < HEAD_DIM
    krow = lax.broadcasted_iota(jnp.int32, (tk, 1), 0)
    qlane = lax.broadcasted_iota(jnp.int32, (1, tq), 1)
    reps = tq // LANES
    bufs = (s0_ref, s1_ref)
    blocks = [(i, kb) for i in range(t // tq) for kb in range((i * tq) // tk + 1)]
    q_cache = {}

    def q_pair(i):
        if i not in q_cache:
            q = q_ref[0, :, i * tq:(i + 1) * tq]
            q_cache[i] = jnp.concatenate([jnp.where(first_head, q, 0), jnp.where(first_head, 0, q)], axis=1)
        return q_cache[i]

    def put_scores(n):
        i, kb = blocks[n]
        bufs[n % 2][...] = _dot(k_ref[0, kb * tk:(kb + 1) * tk, :], q_pair(i))

    put_scores(0)
    m_i = l_i = None
    for n, (i, kb) in enumerate(blocks):
        if n + 1 < len(blocks):
            put_scores(n + 1)
        last = kb == (i * tq) // tk
        ok = ((kb * tk + krow) <= (i * tq + qlane)) if last else None

        def adjust(s_cols, cg, kb=kb, last=last, ok=ok):
            s_cols = s_cols - ck_ref[cg // reps, kb * tk:(kb + 1) * tk, :]
            return jnp.where(ok[:, (cg % reps) * LANES:(cg % reps + 1) * LANES], s_cols, NEG_INF) if last else s_cols

        if kb == 0:
            m_i = jnp.full((1, 2 * tq), NEG_INF, F32)
            l_i = jnp.zeros((1, 2 * tq), F32)
        v_blk = jnp.concatenate([v_ref[0, :, kb * tk:(kb + 1) * tk], jnp.ones((SUM_ROWS, tk), BF16)], axis=0)
        m_i, l_i = _flash_step(bufs[n % 2], p_ref, acc_ref, v_blk, m_i, l_i, adjust, first=kb == 0)
        if last:
            o = acc_ref[...] * (1.0 / l_i)
            o_ref[0, i * tq:(i + 1) * tq, :] = jnp.where(first_head, o[:, :tq], o[:, tq:]).T.astype(BF16)


def fox_attention(qb_t, kb, vf_t, cum, tq=512):
    b, w, t = qb_t.shape
    pairs = w // LANES
    tk = FOX_KEYS_PER_QUERY_BLOCK * tq
    return pl.pallas_call(
        functools.partial(_fox_body, tq=tq),
        grid=(b, pairs),
        in_specs=[pl.BlockSpec((1, LANES, t), lambda i, p: (i, p, 0)),
                  pl.BlockSpec((1, t, LANES), lambda i, p: (i, 0, p)),
                  pl.BlockSpec((1, LANES, t), lambda i, p: (i, p, 0)),
                  pl.BlockSpec((1, t, LANES), lambda i, p: (i, 0, 0))],
        out_specs=pl.BlockSpec((1, t, LANES), lambda i, p: (i, 0, p)),
        out_shape=jax.ShapeDtypeStruct((b, t, w), BF16),
        scratch_shapes=[pltpu.VMEM((2, t, LANES), F32), pltpu.VMEM((tk, 2 * tq), F32), pltpu.VMEM((tk, 2 * tq), F32),
                        pltpu.VMEM((tk, 2 * tq), BF16), pltpu.VMEM((LANES, 2 * tq), F32)],
        compiler_params=_params("parallel", "arbitrary"),
    )(qb_t, kb, vf_t, cum)


def _even_out_body(x_ref, oa_ref, ob_ref, w_ref, o_ref):
    wa = oa_ref.shape[1]
    o_ref[...] = x_ref[...] + _dot(oa_ref[...], w_ref[:wa, :]) + _dot(ob_ref[...], w_ref[wa:, :])


def even_out(x2, oa, ob, w_bf, tm=512):
    n, d = x2.shape
    row = lambda w: pl.BlockSpec((tm, w), lambda i: (i, 0))
    return pl.pallas_call(
        _even_out_body,
        grid=(n // tm,),
        in_specs=[row(d), row(oa.shape[1]), row(ob.shape[1]), _full(w_bf.shape)],
        out_specs=row(d),
        out_shape=jax.ShapeDtypeStruct((n, d), F32),
        compiler_params=_params("parallel"),
    )(x2, oa, ob, w_bf)


R_GROUP = 0
R_EXPERT = N_GROUPS


def _moe_body(x_ref, g_ref, wr_hi_ref, wr_lo_ref, br_ref, win_ref, wout_ref, o_ref, h_ref, gate_ref):
    e = pl.program_id(1)

    @pl.when(e == 0)
    def _():
        x = x_ref[...]
        h = x * lax.rsqrt(jnp.mean(x * x, axis=-1, keepdims=True) + NORM_EPS) * g_ref[...]
        h_ref[...] = h.astype(BF16)
        h_hi, h_lo = _split2(h)
        logit = _dot(h_hi, wr_hi_ref[...]) + _dot(h_lo, wr_hi_ref[...]) + _dot(h_hi, wr_lo_ref[...]) + br_ref[...]
        lane_i = lax.broadcasted_iota(jnp.int32, logit.shape, 1)
        lane = lane_i.astype(F32)
        is_g = lane_i < N_GROUPS
        g_max = jnp.max(jnp.where(is_g, logit, -jnp.inf), axis=-1, keepdims=True)
        g_sel = jnp.min(jnp.where(is_g & (logit == g_max), lane, float(LANES)), axis=-1, keepdims=True)
        p_group = 1.0 / jnp.sum(jnp.where(is_g, jnp.exp(logit - g_max), 0.0), axis=-1, keepdims=True)
        group_of = ((lane_i - R_EXPERT) // EXPERTS_PER_GROUP).astype(F32)
        mine = (lane_i >= R_EXPERT) & (lane_i < R_EXPERT + N_EXPERTS) & (group_of == g_sel)
        v1 = jnp.max(jnp.where(mine, logit, -jnp.inf), axis=-1, keepdims=True)
        i1 = jnp.min(jnp.where(mine & (logit == v1), lane, float(LANES)), axis=-1, keepdims=True)
        rest = mine & (lane != i1)
        v2 = jnp.max(jnp.where(rest, logit, -jnp.inf), axis=-1, keepdims=True)
        i2 = jnp.min(jnp.where(rest & (logit == v2), lane, float(LANES)), axis=-1, keepdims=True)
        e2 = jnp.exp(v2 - v1)
        w1 = p_group / (1.0 + e2)
        w2 = p_group * e2 / (1.0 + e2)
        gate_ref[...] = jnp.where(lane == i1, w1, 0.0) + jnp.where(lane == i2, w2, 0.0)
        o_ref[...] = x

    gates = gate_ref[...]
    lane = lax.broadcasted_iota(jnp.int32, gates.shape, 1)
    acts = []
    for j in range(EXPERTS_PER_GROUP):
        gate_e = jnp.sum(jnp.where(lane == R_EXPERT + e * EXPERTS_PER_GROUP + j, gates, 0.0), axis=-1, keepdims=True)
        gu = _dot(h_ref[...], win_ref[j])
        acts.append((_silu(gu[:, :EXPERT_FF]) * gu[:, EXPERT_FF:] * gate_e).astype(BF16))
    o_ref[...] += _dot(jnp.concatenate(acts, axis=1), wout_ref[0])


def moe(x2, gain, wr_hi, wr_lo, br, win_bf, wout_bf, layer, tm=1024):
    n, d = x2.shape
    row = pl.BlockSpec((tm, d), lambda i, e: (i, 0))
    win_g = win_bf.reshape(-1, d, 2 * EXPERT_FF)
    wout_g = wout_bf.reshape(-1, EXPERTS_PER_GROUP * EXPERT_FF, d)
    return pl.pallas_call(
        _moe_body,
        grid=(n // tm, N_GROUPS),
        in_specs=[row, _full((1, d)), _full((d, LANES)), _full((d, LANES)), _full((1, LANES)),
                  pl.BlockSpec((EXPERTS_PER_GROUP, d, 2 * EXPERT_FF), lambda i, e: (layer * N_GROUPS + e, 0, 0)),
                  pl.BlockSpec((1, EXPERTS_PER_GROUP * EXPERT_FF, d), lambda i, e: (layer * N_GROUPS + e, 0, 0))],
        out_specs=row,
        out_shape=jax.ShapeDtypeStruct((n, d), F32),
        scratch_shapes=[pltpu.VMEM((tm, d), BF16), pltpu.VMEM((tm, LANES), F32)],
        compiler_params=_params("parallel", "arbitrary"),
    )(x2, gain.reshape(1, d), wr_hi, wr_lo, br, win_g, wout_g)


G_CUM, G_BETA, G_LAST = 0, GDN_HEADS, 2 * GDN_HEADS


def _gdn_gates_body(ab_ref, alog_ref, dtb_ref, gb_ref):
    t = ab_ref.shape[1]
    ab = ab_ref[0]
    sp_in = ab + dtb_ref[...]
    softplus = jnp.maximum(sp_in, 0.0) + jnp.log1p(jnp.exp(-jnp.abs(sp_in)))
    lane_row = lax.broadcasted_iota(jnp.int32, (1, LANES), 1)
    g = jnp.where(lane_row < GDN_HEADS, -jnp.exp(alog_ref[...]) * softplus, 0.0)
    blk = 4 * GDN_CHUNK
    r = lax.broadcasted_iota(jnp.int32, (blk, blk), 0)
    c = lax.broadcasted_iota(jnp.int32, (blk, blk), 1)
    same = r // GDN_CHUNK == c // GDN_CHUNK
    tri = jnp.where(same & (r >= c), 1.0, 0.0).astype(BF16)
    tot = jnp.where(same, 1.0, 0.0).astype(BF16)
    lane = lax.broadcasted_iota(jnp.int32, (blk, LANES), 1)
    for s in range(t // blk):
        rs = slice(s * blk, (s + 1) * blk)
        hi, mid, lo = _split3(g[rs])
        gc = _dot(tri, hi) + _dot(tri, mid) + _dot(tri, lo)
        gl = _dot(tot, hi) + _dot(tot, mid) + _dot(tot, lo)
        gl = pltpu.roll(gl, G_LAST, 1)
        gb_ref[0, rs, :] = jnp.where(lane < G_BETA, gc, jnp.where(lane < G_LAST, _sigmoid(ab[rs]), gl))


def gdn_gates(proj, alog_row, dtb_row):
    b, t, _ = proj.shape
    return pl.pallas_call(
        _gdn_gates_body,
        grid=(b,),
        in_specs=[pl.BlockSpec((1, t, LANES), lambda i: (i, 0, C_AB // LANES)), _full((1, LANES)), _full((1, LANES))],
        out_specs=pl.BlockSpec((1, t, LANES), lambda i: (i, 0, 0)),
        out_shape=jax.ShapeDtypeStruct((b, t, LANES), F32),
        compiler_params=_params("parallel"),
    )(proj, alog_row, dtb_row)


GDN_HEADS_PER_STEP = 8
GDN_CHUNKS_PER_TRIP = 4
GDN_SEGMENTS = 4


def _dot3(a, b):
    a_hi, a_lo = _split2(a)
    b_hi, b_lo = _split2(b)
    return _dot(a_hi, b_hi) + _dot(a_hi, b_lo) + _dot(a_lo, b_hi)


def _dot1(a, b):
    return _dot(a.astype(BF16), b.astype(BF16))


CONV_HALO = 8


def _conv_silu(ext_ref, cw_ref, lanes):
    t = ext_ref.shape[0] - CONV_HALO
    y = ext_ref[CONV_HALO:, lanes] * cw_ref[CONV_WIDTH - 1:CONV_WIDTH, lanes]
    for d in range(1, CONV_WIDTH):
        y = y + ext_ref[CONV_HALO - d:CONV_HALO - d + t, lanes] * cw_ref[CONV_WIDTH - 1 - d:CONV_WIDTH - d, lanes]
    return _silu(y)


def _l2norm(y):
    return y * lax.rsqrt(jnp.sum(y * y, axis=-1, keepdims=True) + NORM_EPS)


def _gdn_body(q_ref, k_ref, v_ref, cq_ref, ck_ref, cv_ref, gb_ref, grow_ref, z_ref, x_ref, gain_ref, wout_ref, o_ref,
              ext_ref, state_ref, gl_ref, gc_ref, kb_ref, k_ref_s, kbg_ref, vb_ref, qs_ref, qg_ref, kd_ref,
              u_ref, w_ref, a_ref, mix_ref):
    t = q_ref.shape[1]
    cs = GDN_CHUNK
    dk = GDN_HEAD_DIM
    nh = GDN_HEADS_PER_STEP
    seg = pl.program_id(1)
    gb_hi, gb_mid, gb_lo = _split3(gb_ref[0])
    pick_row = lax.broadcasted_iota(jnp.int32, (LANES, LANES), 0)

    @pl.when(seg == 0)
    def _():
        ext_ref[:, 0:CONV_HALO, :] = jnp.zeros((3, CONV_HALO, ext_ref.shape[2]), F32)
        state_ref[...] = jnp.zeros_like(state_ref)

    @pl.when(seg > 0)
    def _():
        ext_ref[:, 0:CONV_HALO, :] = ext_ref[:, t:t + CONV_HALO, :]

    for i, ref in enumerate((q_ref, k_ref, v_ref)):
        ext_ref[i, CONV_HALO:, :] = ref[0]

    def column(idx):
        sel = jnp.where(pick_row == idx, 1.0, 0.0).astype(BF16)
        return _dot(gb_hi, sel) + _dot(gb_mid, sel) + _dot(gb_lo, sel)

    for s in range(nh):
        lanes = slice(s * dk, (s + 1) * dk)
        gcol = column(G_CUM + s)
        bcol = column(G_BETA + s)
        glast = column(G_LAST + s)
        eg = jnp.exp(gcol)
        k = _l2norm(_conv_silu(ext_ref.at[1], ck_ref, lanes))
        kb = k * bcol
        k_ref_s[s] = k.astype(BF16)
        kb_ref[s] = kb.astype(BF16)
        kbg_ref[s] = (kb * eg).astype(BF16)
        kd_ref[s] = (k * jnp.exp(glast - gcol)).astype(BF16)
        q = _l2norm(_conv_silu(ext_ref.at[0], cq_ref, lanes)) * (dk ** -0.5)
        qs_ref[s] = q.astype(BF16)
        qg_ref[s] = (q * eg).astype(BF16)
        vb_ref[s] = (_conv_silu(ext_ref.at[2], cv_ref, lanes) * bcol).astype(BF16)
        gl_ref[s] = glast
        gc_ref[s] = gcol

    r = lax.broadcasted_iota(jnp.int32, (cs, cs), 0)
    c = lax.broadcasted_iota(jnp.int32, (cs, cs), 1)
    tril = r >= c
    strict = r > c
    eye = jnp.where(r == c, 1.0, 0.0)

    def prep(trip, _):
        probs = [(s, trip * GDN_CHUNKS_PER_TRIP + j) for j in range(GDN_CHUNKS_PER_TRIP) for s in range(nh)]
        rows = [pl.ds(pl.multiple_of(n * cs, cs), cs) for _, n in probs]
        decay, lmat = [], []
        for (s, n), rw in zip(probs, rows):
            gr = grow_ref[0, s, pl.ds(n, 1), :]
            gc = gc_ref[s, rw, :cs]
            decay.append(jnp.where(tril, jnp.exp(jnp.where(tril, gc - gr, 0.0)), 0.0))
        for i, ((s, _), rw) in enumerate(zip(probs, rows)):
            lmat.append(jnp.where(strict, _dot_nt(kb_ref[s, rw, :], k_ref_s[s, rw, :]) * decay[i], 0.0))
        inv = [eye - m for m in lmat]
        pw = [_dot3(m, m) for m in lmat]
        span = 2
        while span < cs:
            mm = _dot3 if span == 2 else _dot1
            inv = [x + mm(x, p) for x, p in zip(inv, pw)]
            span *= 2
            if span < cs:
                pw = [_dot1(p, p) for p in pw]
        inv_bf = [x.astype(BF16) for x in inv]
        for i, ((s, _), rw) in enumerate(zip(probs, rows)):
            u_ref[s, rw, :] = _dot(inv_bf[i], vb_ref[s, rw, :])
            w_ref[s, rw, :] = _dot(inv_bf[i], kbg_ref[s, rw, :]).astype(BF16)
            a_ref[s, rw, :] = jnp.where(tril, _dot_nt(qs_ref[s, rw, :], k_ref_s[s, rw, :]) * decay[i], 0.0).astype(BF16)
        return 0

    lax.fori_loop(0, t // (cs * GDN_CHUNKS_PER_TRIP), prep, 0)

    def scan(n, states):
        r0 = pl.multiple_of(n * cs, cs)
        rows = pl.ds(r0, cs)
        s_bf = [st.astype(BF16) for st in states]
        v_bf = [(u_ref[s, rows, :] - _dot(w_ref[s, rows, :], s_bf[s])).astype(BF16) for s in range(nh)]
        new = [states[s] * jnp.exp(gl_ref[s, pl.ds(r0, 1), :]) + _dot_tn(kd_ref[s, rows, :], v_bf[s])
               for s in range(nh)]
        for s in range(nh):
            mix_ref[rows, s * dk:(s + 1) * dk] = _dot(qg_ref[s, rows, :], s_bf[s]) + _dot(a_ref[s, rows, :], v_bf[s])
        return tuple(new)

    final = lax.fori_loop(0, t // cs, scan, tuple(state_ref[s] for s in range(nh)))
    for s in range(nh):
        state_ref[s] = final[s]

    gain = gain_ref[...]
    parts = []
    for s in range(nh):
        lanes = slice(s * dk, (s + 1) * dk)
        o = mix_ref[:, lanes]
        y = o * lax.rsqrt(jnp.mean(o * o, axis=-1, keepdims=True) + NORM_EPS) * gain
        parts.append((y * _silu(z_ref[0, :, lanes])).astype(BF16))
    o_ref[0] = x_ref[0] + _dot(jnp.concatenate(parts, axis=1), wout_ref[...])


def gdn_mixer(x3, proj, conv_w, gb, gain, wout_bf):
    b, t, d = x3.shape
    ts = t // GDN_SEGMENTS if t % (GDN_SEGMENTS * GDN_CHUNK * GDN_CHUNKS_PER_TRIP) == 0 else t
    seg_chunks = ts // GDN_CHUNK
    nh = GDN_HEADS_PER_STEP
    assert nh == GDN_HEADS, "the fused output projection needs every head of a row in one grid step"
    g_rows = jnp.swapaxes(gb[:, :, G_CUM:G_CUM + GDN_HEADS], 1, 2).reshape(b, GDN_HEADS, t // GDN_CHUNK, GDN_CHUNK)
    sect = lambda k: pl.BlockSpec((1, ts, GDN_WIDTH), lambda i, s: (i, s, k))
    taps = lambda k: pl.BlockSpec((CONV_WIDTH, GDN_WIDTH), lambda i, s: (0, k))
    rows = lambda w: pl.BlockSpec((1, ts, w), lambda i, s: (i, s, 0))
    bf = lambda w: pltpu.VMEM((nh, ts, w), BF16)
    return pl.pallas_call(
        _gdn_body,
        grid=(b, t // ts),
        in_specs=[sect(0), sect(1), sect(2), taps(0), taps(1), taps(2), rows(LANES),
                  pl.BlockSpec((1, nh, seg_chunks, GDN_CHUNK), lambda i, s: (i, 0, s, 0)),
                  sect(3), rows(d), _full((1, GDN_HEAD_DIM)), _full(wout_bf.shape)],
        out_specs=rows(d),
        out_shape=jax.ShapeDtypeStruct((b, t, d), F32),
        scratch_shapes=[pltpu.VMEM((3, CONV_HALO + ts, GDN_WIDTH), F32),
                        pltpu.VMEM((nh, GDN_HEAD_DIM, GDN_HEAD_DIM), F32)]
                       + [pltpu.VMEM((nh, ts, LANES), F32)] * 2 + [bf(GDN_HEAD_DIM)] * 7
                       + [pltpu.VMEM((nh, ts, GDN_HEAD_DIM), F32), bf(GDN_HEAD_DIM), bf(GDN_CHUNK),
                          pltpu.VMEM((ts, GDN_WIDTH), F32)],
        compiler_params=_params("parallel", "arbitrary"),
    )(proj, proj, proj, conv_w, conv_w, conv_w, gb, g_rows, proj, x3, gain, wout_bf)


def _rope_tables(pos):
    half = HEAD_DIM // 2
    inv_freq = ROPE_THETA ** (-jnp.arange(half, dtype=F32) / half)
    ang = pos.astype(F32)[:, None] * inv_freq
    cos = jnp.cos(ang)
    sin = jnp.sin(ang)
    cos_t = jnp.tile(jnp.concatenate([cos, cos], axis=-1), (1, LANES // HEAD_DIM))
    sin_t = jnp.tile(jnp.concatenate([-sin, sin], axis=-1), (1, LANES // HEAD_DIM))
    return cos_t, sin_t


def _block_diag_ones(width, seg):
    idx = np.arange(width) // seg
    return jnp.asarray((idx[:, None] == idx[None, :]).astype(np.float32), dtype=BF16)


def _pad_cols(w, width):
    return jnp.pad(w, ((0, 0), (0, width - w.shape[1])))


def _even_layer(x2, b, t, norm_gain, w_in, b_gate, b_forget, cmp_pe, cmp_w1, cmp_w2, nsa_gain, fox_gain, w_out):
    d = x2.shape[1]
    o_gate = NSA_Q_W + 6 * NSA_KV_W
    o_fox = o_gate + NSA_GATE_W
    w_bf = w_in.astype(BF16)
    w_re = jnp.concatenate([w_bf[:, :o_gate], w_bf[:, o_fox:o_fox + 3 * FOX_W], w_bf[:, o_gate:o_fox],
                            w_bf[:, o_fox + 3 * FOX_W:], jnp.zeros((d, EVEN_W - w_in.shape[1]), BF16)], axis=1)
    proj = norm_matmul(x2, norm_gain, w_re).reshape(b, t, EVEN_W)

    cos, sin = _rope_tables(jnp.arange(t))
    tile = lambda g, n: jnp.tile(g, n).reshape(1, -1)
    bias = jnp.pad(jnp.concatenate([b_gate, b_forget]), (0, LANES - NSA_GATE_W - FOX_HEADS)).reshape(1, LANES)
    bd = _block_diag_ones(FOX_W, HEAD_DIM)
    (qa, ks, kw, vs, vw, kc_raw, vc_raw, qb, kb, vf, gates, cum) = even_prep(
        proj, cos, sin, tile(nsa_gain[0], NSA_HEADS), tile(nsa_gain[2], NSA_KV_HEADS), tile(nsa_gain[3], NSA_KV_HEADS),
        tile(fox_gain[0], FOX_HEADS), tile(fox_gain[1], FOX_HEADS), bias, bd)

    n_str = t // CMP_STRIDE
    half = CMP_BLOCK // 2
    eye2 = jnp.eye(NSA_KV_HEADS, dtype=F32)
    pe = jnp.tile(cmp_pe[:, :, None, :], (1, 1, NSA_KV_HEADS, 1)).reshape(2, 2, 1, half * NSA_KV_W)
    w1 = jnp.einsum('ilde,hg->ilhdge', cmp_w1, eye2).reshape(2, 2, half * NSA_KV_W, NSA_KV_W).astype(BF16)
    w2 = jnp.einsum('ide,hg->ihdge', cmp_w2, eye2).reshape(2, NSA_KV_W, NSA_KV_W).astype(BF16)
    cos_c, sin_c = _rope_tables(jnp.arange(n_str) * CMP_STRIDE + (CMP_BLOCK - 1))
    kc, vc = compress(kc_raw, vc_raw, pe, w1, w2, tile(nsa_gain[1], NSA_KV_HEADS), cos_c, sin_c, _block_diag_ones(LANES, HEAD_DIM))

    n_sel = t // SEL_BLOCK
    cs = np.arange(n_str)[:, None] * CMP_STRIDE
    ss = np.arange(n_sel)[None, :] * SEL_BLOCK
    overlap = np.clip(np.minimum(cs + CMP_BLOCK, ss + SEL_BLOCK) - np.maximum(cs, ss), 0, None) / CMP_BLOCK
    overlap[(t - CMP_BLOCK) // CMP_STRIDE + 1:] = 0.0
    ovt = jnp.asarray(overlap.T.astype(np.float32), dtype=BF16)
    o_a = nsa_attention(qa, kc, vc, ks, vs, kw, vw, gates, ovt)
    o_b = fox_attention(qb, kb, vf, cum)

    return even_out(x2, o_a.reshape(b * t, NSA_Q_W), o_b.reshape(b * t, FOX_W), w_out.astype(BF16))


def _odd_layer(x2, b, t, norm_gain, w_in, conv_w, a_log, dt_bias, gdn_gain, w_out):
    w_bf = w_in.astype(BF16)
    w_pad = jnp.concatenate([w_bf, jnp.zeros((w_in.shape[0], ODD_W - w_in.shape[1]), BF16)], axis=1)
    proj = norm_matmul(x2, norm_gain, w_pad).reshape(b, t, ODD_W)
    pad8 = lambda v: jnp.pad(v, (0, LANES - GDN_HEADS)).reshape(1, LANES)
    gb = gdn_gates(proj, pad8(a_log), pad8(dt_bias))
    out = gdn_mixer(x2.reshape(b, t, -1), proj, conv_w, gb, gdn_gain.reshape(1, GDN_HEAD_DIM), w_out.astype(BF16))
    return out.reshape(b * t, -1)


def _moe_layer(x2, gain, w_rg, b_rg, w_re, b_re, w_ein_bf, w_eout_bf, layer):
    d = x2.shape[1]
    wr = _pad_cols(jnp.concatenate([w_rg, w_re], axis=1), LANES)
    wr_hi = wr.astype(BF16)
    wr_lo = (wr - wr_hi.astype(F32)).astype(BF16)
    br = jnp.pad(jnp.concatenate([b_rg, b_re]), (0, LANES - N_GROUPS - N_EXPERTS)).reshape(1, LANES)
    return moe(x2, gain, wr_hi, wr_lo, br, w_ein_bf, w_eout_bf, layer)


def kernel(x, norm_mix, norm_ffn, w_in_even, b_nsa_gate, b_forget, cmp_pe, cmp_w1, cmp_w2, nsa_qk_gain, fox_qk_gain,
           w_out_even, w_in_odd, conv_w, a_log, dt_bias, gdn_norm_gain, w_out_odd, w_router_group, b_router_group,
           w_router_expert, b_router_expert, w_expert_in, w_expert_out):
    b, t, d = x.shape
    x2 = x.reshape(b * t, d)
    w_ein_bf = w_expert_in.astype(BF16)
    w_eout_bf = w_expert_out.astype(BF16)
    for layer in range(norm_mix.shape[0]):
        i = layer // 2
        if layer % 2 == 0:
            x2 = _even_layer(x2, b, t, norm_mix[layer], w_in_even[i], b_nsa_gate[i], b_forget[i], cmp_pe[i], cmp_w1[i],
                             cmp_w2[i], nsa_qk_gain[i], fox_qk_gain[i], w_out_even[i])
        else:
            x2 = _odd_layer(x2, b, t, norm_mix[layer], w_in_odd[i], conv_w[i], a_log[i], dt_bias[i], gdn_norm_gain[i],
                            w_out_odd[i])
        x2 = _moe_layer(x2, norm_ffn[layer], w_router_group[layer], b_router_group[layer], w_router_expert[layer],
                        b_router_expert[layer], w_ein_bf, w_eout_bf, layer)
    return x2.reshape(b, t, d)
```

```python
import functools

import numpy as np
import jax
import jax.numpy as jnp
from jax import lax
from jax.experimental import pallas as pl
from jax.experimental.pallas import tpu as pltpu

F32 = jnp.float32
BF16 = jnp.bfloat16

HEAD_DIM = 64
ROPE_THETA = 10000.0
NSA_HEADS = 8
NSA_KV_HEADS = 2
NSA_GROUP = NSA_HEADS // NSA_KV_HEADS
CMP_BLOCK = 32
CMP_STRIDE = 16
SEL_BLOCK = 64
SEL_TOPK = 8
WINDOW = 256
FOX_HEADS = 8
GDN_HEADS = 8
GDN_HEAD_DIM = 128
GDN_WIDTH = GDN_HEADS * GDN_HEAD_DIM
CONV_WIDTH = 4
GDN_CHUNK = 64
N_GROUPS = 4
EXPERTS_PER_GROUP = 4
N_EXPERTS = N_GROUPS * EXPERTS_PER_GROUP
EXPERT_FF = 256
NORM_EPS = 1e-6
NEG_INF = -1e30
FORCE_SCORE = 1e9

LANES = 128
LOG2E = 1.4426950408889634
SUM_ROWS = 16
NSA_Q_W = NSA_HEADS * HEAD_DIM
NSA_KV_W = NSA_KV_HEADS * HEAD_DIM
NSA_GATE_W = 3 * NSA_HEADS
FOX_W = FOX_HEADS * HEAD_DIM
C_QN = 0
C_KC, C_VC, C_KS, C_VS, C_KW, C_VW = (NSA_Q_W + i * NSA_KV_W for i in range(6))
C_QF = NSA_Q_W + 6 * NSA_KV_W
C_KF = C_QF + FOX_W
C_VF = C_KF + FOX_W
C_MISC = C_VF + FOX_W
EVEN_W = C_MISC + LANES
MISC_F = NSA_GATE_W
C_AB = 4 * GDN_WIDTH
ODD_W = C_AB + LANES

VMEM_LIMIT = 56 * 1024 * 1024


def _params(*sem):
    return pltpu.CompilerParams(dimension_semantics=sem, vmem_limit_bytes=VMEM_LIMIT)


def _dot(a, b):
    return jnp.dot(a, b, preferred_element_type=F32)


def _dot_nt(a, b):
    return lax.dot_general(a, b, (((1,), (1,)), ((), ())), preferred_element_type=F32)


def _dot_tn(a, b):
    return lax.dot_general(a, b, (((0,), (0,)), ((), ())), preferred_element_type=F32)


def _split2(x):
    hi = x.astype(BF16)
    return hi, (x - hi.astype(F32)).astype(BF16)


def _split3(x):
    hi = x.astype(BF16)
    r = x - hi.astype(F32)
    mid = r.astype(BF16)
    return hi, mid, (r - mid.astype(F32)).astype(BF16)


def _sigmoid(z):
    return 1.0 / (1.0 + jnp.exp(-z))


def _silu(z):
    return z * _sigmoid(z)


def _full(shape):
    nd = len(shape)
    return pl.BlockSpec(shape, lambda *_: (0,) * nd)


def _norm_matmul_body(x_ref, g_ref, w_ref, o_ref):
    x = x_ref[...]
    ms = jnp.mean(x * x, axis=-1, keepdims=True)
    h = (x * lax.rsqrt(ms + NORM_EPS) * g_ref[...]).astype(BF16)
    o_ref[...] = _dot(h, w_ref[...])


def norm_matmul(x2, gain, w_bf, tm=512):
    n, d = x2.shape
    wp = w_bf.shape[1]
    return pl.pallas_call(
        _norm_matmul_body,
        grid=(n // tm,),
        in_specs=[pl.BlockSpec((tm, d), lambda i: (i, 0)), _full((1, d)), _full((d, wp))],
        out_specs=pl.BlockSpec((tm, wp), lambda i: (i, 0)),
        out_shape=jax.ShapeDtypeStruct((n, wp), F32),
        compiler_params=_params("parallel"),
    )(x2, gain.reshape(1, d), w_bf)


def _head_rms(x, bd, gain):
    hi, lo = _split2(x * x)
    ones2 = bd[:LANES, :LANES]
    ssum = jnp.concatenate([_dot(hi[:, c:c + LANES], ones2) + _dot(lo[:, c:c + LANES], ones2)
                            for c in range(0, x.shape[1], LANES)], axis=1)
    return x * lax.rsqrt(ssum * (1.0 / HEAD_DIM) + NORM_EPS) * gain


def _rope(x, cos, sin_signed, first_half):
    fwd = pltpu.roll(x, LANES - HEAD_DIM // 2, 1)
    bwd = pltpu.roll(x, HEAD_DIM // 2, 1)
    return x * cos + jnp.where(first_half, fwd, bwd) * sin_signed


def _even_prep_body(p_ref, cos_ref, sin_ref, gq_ref, gks_ref, gkw_ref, gfq_ref, gfk_ref, bias_ref, bd_ref,
                    qa_ref, ks_ref, kw_ref, vs_ref, vw_ref, kc_ref, vc_ref, qb_ref, kb_ref, vf_ref,
                    gate_ref, cum_ref, carry_ref, stage_ref):
    tr = p_ref.shape[1]
    bd = bd_ref[...]
    cos = cos_ref[...]
    sin = sin_ref[...]
    lane = lax.broadcasted_iota(jnp.int32, (1, LANES), 1)
    first_half = (lane % HEAD_DIM) < (HEAD_DIM // 2)
    scale = HEAD_DIM ** -0.5 * LOG2E

    qn = _head_rms(p_ref[0, :, C_QN:C_QN + NSA_Q_W], bd, gq_ref[...])
    for c in range(NSA_Q_W // LANES):
        sl = slice(c * LANES, (c + 1) * LANES)
        qa_ref[0, sl, :] = (_rope(qn[:, sl], cos, sin, first_half) * scale).T.astype(BF16)
    ks = _head_rms(p_ref[0, :, C_KS:C_KS + NSA_KV_W], bd, gks_ref[...])
    ks_ref[0] = _rope(ks, cos, sin, first_half).astype(BF16)
    kw = _head_rms(p_ref[0, :, C_KW:C_KW + NSA_KV_W], bd, gkw_ref[...])
    kw_ref[0] = _rope(kw, cos, sin, first_half).astype(BF16)
    vs_ref[0] = p_ref[0, :, C_VS:C_VS + NSA_KV_W].T.astype(BF16)
    vw_ref[0] = p_ref[0, :, C_VW:C_VW + NSA_KV_W].T.astype(BF16)
    stage_ref[0] = p_ref[0, :, C_KC:C_KC + NSA_KV_W]
    stage_ref[1] = p_ref[0, :, C_VC:C_VC + NSA_KV_W]
    for l in range(CMP_STRIDE):
        rows = pl.ds(l, tr // CMP_STRIDE, stride=CMP_STRIDE)
        kc_ref[0, :, l * NSA_KV_W:(l + 1) * NSA_KV_W] = stage_ref[0, rows, :]
        vc_ref[0, :, l * NSA_KV_W:(l + 1) * NSA_KV_W] = stage_ref[1, rows, :]

    qb = _head_rms(p_ref[0, :, C_QF:C_QF + FOX_W], bd, gfq_ref[...]) * scale
    kb_ref[0] = _head_rms(p_ref[0, :, C_KF:C_KF + FOX_W], bd, gfk_ref[...]).astype(BF16)
    for c in range(FOX_W // LANES):
        sl = slice(c * LANES, (c + 1) * LANES)
        qb_ref[0, sl, :] = qb[:, sl].T.astype(BF16)
        vf_ref[0, sl, :] = p_ref[0, :, C_VF + c * LANES:C_VF + (c + 1) * LANES].T.astype(BF16)

    z = p_ref[0, :, C_MISC:C_MISC + LANES] + bias_ref[...]
    gate_ref[0] = _sigmoid(z).T
    logf = jnp.minimum(z, 0.0) - jnp.log1p(jnp.exp(-jnp.abs(z)))

    @pl.when(pl.program_id(1) == 0)
    def _():
        carry_ref[...] = jnp.zeros_like(carry_ref)

    row = lax.broadcasted_iota(jnp.int32, (tr, tr), 0)
    col = lax.broadcasted_iota(jnp.int32, (tr, tr), 1)
    tril = jnp.where(row >= col, 1.0, 0.0).astype(BF16)
    hi, mid, lo = _split3(logf)
    cum = _dot(tril, hi) + _dot(tril, mid) + _dot(tril, lo) + carry_ref[...]
    cum_ref[0] = cum
    carry_ref[...] = cum[tr - 1:tr, :]


def _even_in_body(x_ref, g_ref, w_ref, *rest):
    proj_ref = rest[-1]
    x = x_ref[0]
    h = (x * lax.rsqrt(jnp.mean(x * x, axis=-1, keepdims=True) + NORM_EPS) * g_ref[...]).astype(BF16)
    proj_ref[0] = _dot(h, w_ref[...])
    _even_prep_body(proj_ref, *rest[:-1])


def even_in_prep(x3, gain, w_bf, cos, sin, gq, gks, gkw, gfq, gfk, bias, bd, tr=512):
    b, t, d = x3.shape
    row = lambda w: pl.BlockSpec((1, tr, w), lambda i, j: (i, j, 0))
    tab = pl.BlockSpec((tr, LANES), lambda i, j: (j, 0))
    shp = lambda w, dt: jax.ShapeDtypeStruct((b, t, w), dt)
    col = lambda w: pl.BlockSpec((1, w, tr), lambda i, j: (i, 0, j))
    shp_t = lambda w, dt: jax.ShapeDtypeStruct((b, w, t), dt)
    strd = pl.BlockSpec((1, tr // CMP_STRIDE, CMP_STRIDE * NSA_KV_W), lambda i, j: (i, j, 0))
    strd_shape = jax.ShapeDtypeStruct((b, t // CMP_STRIDE, CMP_STRIDE * NSA_KV_W), F32)
    return pl.pallas_call(
        _even_in_body,
        grid=(b, t // tr),
        in_specs=[row(d), _full((1, d)), _full(w_bf.shape),
                  tab, tab, _full((1, NSA_Q_W)), _full((1, LANES)), _full((1, LANES)),
                  _full((1, FOX_W)), _full((1, FOX_W)), _full((1, LANES)), _full((FOX_W, FOX_W))],
        out_specs=[col(NSA_Q_W), row(LANES), row(LANES), col(LANES), col(LANES), strd, strd,
                   col(FOX_W), row(FOX_W), col(FOX_W), col(LANES), row(LANES)],
        out_shape=[shp_t(NSA_Q_W, BF16), shp(LANES, BF16), shp(LANES, BF16), shp_t(LANES, BF16), shp_t(LANES, BF16),
                   strd_shape, strd_shape, shp_t(FOX_W, BF16), shp(FOX_W, BF16), shp_t(FOX_W, BF16),
                   shp_t(LANES, F32), shp(LANES, F32)],
        scratch_shapes=[pltpu.VMEM((1, LANES), F32), pltpu.VMEM((2, tr, LANES), F32), pltpu.VMEM((1, tr, EVEN_W), F32)],
        compiler_params=_params("parallel", "arbitrary"),
    )(x3, gain.reshape(1, d), w_bf, cos, sin, gq, gks, gkw, gfq, gfk, bias, bd)


def _gelu_tanh(x):
    return 0.5 * x * (1.0 + jnp.tanh(np.sqrt(2.0 / np.pi).astype(np.float32) * (x + 0.044715 * (x * x * x))))


def _compress_body(xk_ref, xv_ref, pe_ref, w1_ref, w2_ref, gk_ref, cos_ref, sin_ref, bd_ref, kc_ref, vc_ref):
    n = xk_ref.shape[1]
    lane = lax.broadcasted_iota(jnp.int32, (1, LANES), 1)
    first_half = (lane % HEAD_DIM) < (HEAD_DIM // 2)

    def mlp(x_ref, i):
        x = x_ref[0]
        nxt = pltpu.roll(x, n - 1, 0)
        xa = (x + pe_ref[i, 0]).astype(BF16)
        xb = (nxt + pe_ref[i, 1]).astype(BF16)
        h = _dot(xa, w1_ref[i, 0]) + _dot(xb, w1_ref[i, 1])
        return _dot(_gelu_tanh(h).astype(BF16), w2_ref[i])

    kc = _head_rms(mlp(xk_ref, 0), bd_ref[...], gk_ref[...])
    kc_ref[0] = _rope(kc, cos_ref[...], sin_ref[...], first_half).astype(BF16)
    vc_ref[0] = mlp(xv_ref, 1).T.astype(BF16)


def compress(xk, xv, pe, w1, w2, gk, cos_c, sin_c, bd):
    b, n, w = xk.shape
    blk = pl.BlockSpec((1, n, w), lambda i: (i, 0, 0))
    out = pl.BlockSpec((1, n, LANES), lambda i: (i, 0, 0))
    return pl.pallas_call(
        _compress_body,
        grid=(b,),
        in_specs=[blk, blk, _full(pe.shape), _full(w1.shape), _full(w2.shape), _full((1, LANES)),
                  _full((n, LANES)), _full((n, LANES)), _full((LANES, LANES))],
        out_specs=[out, pl.BlockSpec((1, LANES, n), lambda i: (i, 0, 0))],
        out_shape=[jax.ShapeDtypeStruct((b, n, LANES), BF16), jax.ShapeDtypeStruct((b, LANES, n), BF16)],
        compiler_params=_params("parallel"),
    )(xk, xv, pe, w1, w2, gk, cos_c, sin_c, bd)


def _flash_step(s_ref, p_ref, acc_ref, v_blk, m_i, l_i, adjust, first=False):
    n_ch = acc_ref.shape[0]
    al, ms = [], []
    for cg in range(s_ref.shape[1] // LANES):
        sl = slice(cg * LANES, (cg + 1) * LANES)
        s = adjust(s_ref[:, sl], cg)
        m_new = jnp.maximum(m_i[:, sl], jnp.max(s, axis=0, keepdims=True))
        p_ref[:, sl] = jnp.exp2(s - m_new).astype(BF16)
        al.append(jnp.exp2(m_i[:, sl] - m_new))
        ms.append(m_new)
    cat = lambda xs: jnp.concatenate(xs, axis=1)
    alpha = cat(al)
    pv = _dot(v_blk, p_ref[...])
    acc_ref[...] = pv[:n_ch] if first else alpha * acc_ref[...] + pv[:n_ch]
    return cat(ms), alpha * l_i + pv[n_ch:n_ch + 1]


NSA_KEYS_PER_QUERY_BLOCK = 2


def _nsa_body(q_ref, kc_ref, vc_ref, ks_ref, vs_ref, kw_ref, vw_ref, gate_ref, ovt_ref, o_ref,
              sel_ref, s0_ref, s1_ref, p_ref, acc_ref, *, k_top):
    tq = q_ref.shape[2]
    t_all = ks_ref.shape[1]
    n_cmp = kc_ref.shape[1]
    n_sel = ovt_ref.shape[0]
    g_n = NSA_GROUP
    c = pl.program_id(1)
    t0 = c * tq
    chan = lax.broadcasted_iota(jnp.int32, (LANES, 1), 0)
    tlane = t0 + lax.broadcasted_iota(jnp.int32, (1, tq), 1)
    gates = gate_ref[0]

    nrow = lax.broadcasted_iota(jnp.int32, (n_cmp, 1), 0)
    valid_c = (nrow * CMP_STRIDE + (CMP_BLOCK - 1)) <= tlane
    jrow = lax.broadcasted_iota(jnp.int32, (n_sel, tq), 0)
    jrow_f = jrow.astype(F32)
    cur = tlane // SEL_BLOCK
    forced = (jrow == 0) | (jrow == cur) | (jrow == cur - 1)
    future = jrow * SEL_BLOCK > tlane
    tk = NSA_KEYS_PER_QUERY_BLOCK * tq
    krow = lax.broadcasted_iota(jnp.int32, (tk, 1), 0)
    per_blk = tk // SEL_BLOCK
    w_len = tq + WINDOW
    w_start = pl.multiple_of(jnp.clip(t0 - WINDOW, 0, t_all - w_len), LANES)
    wrow = w_start + lax.broadcasted_iota(jnp.int32, (w_len, 1), 0)
    valid_w = (wrow <= tlane) & (wrow > tlane - WINDOW)

    heads = [(kvh, g) for kvh in range(NSA_KV_HEADS) for g in range(g_n)]
    zero_half = jnp.zeros((HEAD_DIM, tq), BF16)

    def on_kv_rows(h, kvh):
        blk = q_ref[0, h * HEAD_DIM:(h + 1) * HEAD_DIM, :]
        return jnp.concatenate([blk, zero_half] if kvh == 0 else [zero_half, blk], axis=0)

    qst = jnp.concatenate([on_kv_rows(h, kvh) for h, (kvh, _) in enumerate(heads)], axis=1)
    n_col = len(heads) * tq

    def softmax_cols(s, ok, guard):
        outs = []
        for cg in range(len(heads)):
            sc = jnp.where(ok, s[:, cg * tq:(cg + 1) * tq], NEG_INF)
            e = jnp.exp2(sc - jnp.max(sc, axis=0, keepdims=True))
            if guard:
                e = jnp.where(ok, e, 0.0)
            den = jnp.sum(e, axis=0, keepdims=True)
            outs.append(e * (1.0 / (jnp.where(den > 0.0, den, 1.0) if guard else den)))
        return outs

    p_c = softmax_cols(_dot(kc_ref[0], qst), valid_c, guard=True)
    o_cmp = _dot(vc_ref[0], jnp.concatenate(p_c, axis=1).astype(BF16))

    for kvh in range(NSA_KV_HEADS):
        p_sum = p_c[kvh * g_n]
        for g in range(1, g_n):
            p_sum = p_sum + p_c[kvh * g_n + g]
        p_hi, p_lo = _split2(p_sum)
        imp_t = _dot(ovt_ref[...], p_hi) + _dot(ovt_ref[...], p_lo)
        val = jnp.where(forced, FORCE_SCORE, jnp.where(future, NEG_INF, imp_t))
        sel_t = jnp.zeros((n_sel, tq), F32)
        for _ in range(k_top):
            m = jnp.max(val, axis=0, keepdims=True)
            first = jnp.min(jnp.where(val == m, jrow_f, float(n_sel)), axis=0, keepdims=True)
            pick = jrow_f == first
            sel_t = jnp.where(pick, 1.0, sel_t)
            val = jnp.where(pick, -jnp.inf, val)
        sel_ref[kvh] = sel_t

    p_w = softmax_cols(_dot(kw_ref[0, pl.ds(w_start, w_len), :], qst), valid_w, guard=False)
    o_win = _dot(vw_ref[0, :, pl.ds(w_start, w_len)], jnp.concatenate(p_w, axis=1).astype(BF16))

    def put_scores(buf, kb):
        k0 = pl.multiple_of(jnp.minimum(kb * tk, t_all - tk), tk)
        buf[...] = _dot(ks_ref[0, pl.ds(k0, tk), :], qst)

    def half_step(buf, kb, m_i, l_i):
        k0 = pl.multiple_of(jnp.minimum(kb * tk, t_all - tk), tk)
        causal = (kb * tk + krow) <= tlane
        ok = [causal & (jnp.concatenate([jnp.broadcast_to(sel_ref[kvh, pl.ds(k0 // SEL_BLOCK + r, 1), :],
                                                          (SEL_BLOCK, tq)) for r in range(per_blk)], axis=0) > 0.5)
              for kvh in range(NSA_KV_HEADS)]
        adjust = lambda s_cols, cg: jnp.where(ok[cg // g_n], s_cols, NEG_INF)
        v_blk = jnp.concatenate([vs_ref[0, :, pl.ds(k0, tk)], jnp.ones((SUM_ROWS, tk), BF16)], axis=0)
        return _flash_step(buf, p_ref, acc_ref, v_blk, m_i, l_i, adjust)

    def sel_trip(j, carry):
        put_scores(s1_ref, 2 * j + 1)
        carry = half_step(s0_ref, 2 * j, *carry)
        put_scores(s0_ref, 2 * j + 2)
        return half_step(s1_ref, 2 * j + 1, *carry)

    put_scores(s0_ref, 0)
    acc_ref[...] = jnp.zeros_like(acc_ref)
    init = (jnp.full((1, n_col), NEG_INF, F32), jnp.zeros((1, n_col), F32))
    n_blocks = (t0 + tq + tk - 1) // tk
    _, l_s = lax.fori_loop(0, (n_blocks + 1) // 2, sel_trip, init)
    o_slc = acc_ref[...] * (1.0 / l_s)

    gated = []
    for h, (kvh, _) in enumerate(heads):
        cols = slice(h * tq, (h + 1) * tq)
        rows = slice(kvh * HEAD_DIM, (kvh + 1) * HEAD_DIM)
        gated.append(gates[3 * h:3 * h + 1] * o_cmp[rows, cols] + gates[3 * h + 1:3 * h + 2] * o_slc[rows, cols]
                     + gates[3 * h + 2:3 * h + 3] * o_win[rows, cols])
    for j in range(NSA_HEADS * HEAD_DIM // LANES):
        pair = jnp.concatenate(gated[2 * j:2 * j + 2], axis=0)
        o_ref[0, :, j * LANES:(j + 1) * LANES] = pair.T.astype(BF16)


def nsa_attention(qa_t, kc, vc_t, ks, vs_t, kw, vw_t, gates_t, ovt, tq=LANES):
    b, _, t = qa_t.shape
    n_cmp = kc.shape[1]
    n_sel = ovt.shape[0]
    k_top = min(SEL_TOPK, n_sel)
    tk = NSA_KEYS_PER_QUERY_BLOCK * tq
    tok = lambda n: pl.BlockSpec((1, n, LANES), lambda i, j: (i, 0, 0))
    chn = lambda n: pl.BlockSpec((1, LANES, n), lambda i, j: (i, 0, 0))
    return pl.pallas_call(
        functools.partial(_nsa_body, k_top=k_top),
        grid=(b, t // tq),
        in_specs=[pl.BlockSpec((1, NSA_Q_W, tq), lambda i, j: (i, 0, j)), tok(n_cmp), chn(n_cmp), tok(t), chn(t),
                  tok(t), chn(t), pl.BlockSpec((1, LANES, tq), lambda i, j: (i, 0, j)), _full(ovt.shape)],
        out_specs=pl.BlockSpec((1, tq, NSA_Q_W), lambda i, j: (i, j, 0)),
        out_shape=jax.ShapeDtypeStruct((b, t, NSA_Q_W), BF16),
        scratch_shapes=[pltpu.VMEM((NSA_KV_HEADS, n_sel, tq), F32), pltpu.VMEM((tk, NSA_HEADS * tq), F32),
                        pltpu.VMEM((tk, NSA_HEADS * tq), F32), pltpu.VMEM((tk, NSA_HEADS * tq), BF16),
                        pltpu.VMEM((LANES, NSA_HEADS * tq), F32)],
        compiler_params=_params("parallel", "arbitrary"),
    )(qa_t, kc, vc_t, ks, vs_t, kw, vw_t, gates_t, ovt)


FOX_KEYS_PER_QUERY_BLOCK = 2


def _fox_body(q_ref, k_ref, v_ref, cum_ref, o_ref, ck_ref, s0_ref, s1_ref, p_ref, acc_ref, *, tq):
    t = k_ref.shape[1]
    tk = FOX_KEYS_PER_QUERY_BLOCK * tq
    pair = pl.program_id(1)

    hi, mid, lo = _split3(cum_ref[0])
    pick_row = lax.broadcasted_iota(jnp.int32, (LANES, LANES), 0)
    for h in range(2):
        sel = jnp.where(pick_row == MISC_F + 2 * pair + h, 1.0, 0.0).astype(BF16)
        ck_ref[h] = (_dot(hi, sel) + _dot(mid, sel) + _dot(lo, sel)) * LOG2E

    chan = lax.broadcasted_iota(jnp.int32, (LANES, 1), 0)
    first_head = chan < HEAD_DIM
    krow = lax.broadcasted_iota(jnp.int32, (tk, 1), 0)
    qlane = lax.broadcasted_iota(jnp.int32, (1, tq), 1)
    reps = tq // LANES
    bufs = (s0_ref, s1_ref)
    blocks = [(i, kb) for i in range(t // tq) for kb in range((i * tq) // tk + 1)]
    q_cache = {}

    def q_pair(i):
        if i not in q_cache:
            q = q_ref[0, :, i * tq:(i + 1) * tq]
            q_cache[i] = jnp.concatenate([jnp.where(first_head, q, 0), jnp.where(first_head, 0, q)], axis=1)
        return q_cache[i]

    def put_scores(n):
        i, kb = blocks[n]
        bufs[n % 2][...] = _dot(k_ref[0, kb * tk:(kb + 1) * tk, :], q_pair(i))

    put_scores(0)
    m_i = l_i = None
    for n, (i, kb) in enumerate(blocks):
        if n + 1 < len(blocks):
            put_scores(n + 1)
        last = kb == (i * tq) // tk
        ok = ((kb * tk + krow) <= (i * tq + qlane)) if last else None

        def adjust(s_cols, cg, kb=kb, last=last, ok=ok):
            s_cols = s_cols - ck_ref[cg // reps, kb * tk:(kb + 1) * tk, :]
            return jnp.where(ok[:, (cg % reps) * LANES:(cg % reps + 1) * LANES], s_cols, NEG_INF) if last else s_cols

        if kb == 0:
            m_i = jnp.full((1, 2 * tq), NEG_INF, F32)
            l_i = jnp.zeros((1, 2 * tq), F32)
        v_blk = jnp.concatenate([v_ref[0, :, kb * tk:(kb + 1) * tk], jnp.ones((SUM_ROWS, tk), BF16)], axis=0)
        m_i, l_i = _flash_step(bufs[n % 2], p_ref, acc_ref, v_blk, m_i, l_i, adjust, first=kb == 0)
        if last:
            o = acc_ref[...] * (1.0 / l_i)
            o_ref[0, i * tq:(i + 1) * tq, :] = jnp.where(first_head, o[:, :tq], o[:, tq:]).T.astype(BF16)


def fox_attention(qb_t, kb, vf_t, cum, tq=256):
    b, w, t = qb_t.shape
    pairs = w // LANES
    tk = FOX_KEYS_PER_QUERY_BLOCK * tq
    return pl.pallas_call(
        functools.partial(_fox_body, tq=tq),
        grid=(b, pairs),
        in_specs=[pl.BlockSpec((1, LANES, t), lambda i, p: (i, p, 0)),
                  pl.BlockSpec((1, t, LANES), lambda i, p: (i, 0, p)),
                  pl.BlockSpec((1, LANES, t), lambda i, p: (i, p, 0)),
                  pl.BlockSpec((1, t, LANES), lambda i, p: (i, 0, 0))],
        out_specs=pl.BlockSpec((1, t, LANES), lambda i, p: (i, 0, p)),
        out_shape=jax.ShapeDtypeStruct((b, t, w), BF16),
        scratch_shapes=[pltpu.VMEM((2, t, LANES), F32), pltpu.VMEM((tk, 2 * tq), F32), pltpu.VMEM((tk, 2 * tq), F32),
                        pltpu.VMEM((tk, 2 * tq), BF16), pltpu.VMEM((LANES, 2 * tq), F32)],
        compiler_params=_params("parallel", "arbitrary"),
    )(qb_t, kb, vf_t, cum)


R_GROUP = 0
R_EXPERT = N_GROUPS


def _moe_body(x_ref, *refs, mixer_pending):
    if mixer_pending:
        oa_ref, ob_ref, wo_ref, *refs = refs
    g_ref, wr_hi_ref, wr_lo_ref, br_ref, win_ref, wout_ref, o_ref, h_ref, gate_ref = refs
    e = pl.program_id(1)

    @pl.when(e == 0)
    def _():
        x = x_ref[...]
        if mixer_pending:
            wa = oa_ref.shape[1]
            x = x + _dot(oa_ref[...], wo_ref[:wa, :]) + _dot(ob_ref[...], wo_ref[wa:, :])
        h = x * lax.rsqrt(jnp.mean(x * x, axis=-1, keepdims=True) + NORM_EPS) * g_ref[...]
        h_ref[...] = h.astype(BF16)
        h_hi, h_lo = _split2(h)
        logit = _dot(h_hi, wr_hi_ref[...]) + _dot(h_lo, wr_hi_ref[...]) + _dot(h_hi, wr_lo_ref[...]) + br_ref[...]
        lane_i = lax.broadcasted_iota(jnp.int32, logit.shape, 1)
        lane = lane_i.astype(F32)
        is_g = lane_i < N_GROUPS
        g_max = jnp.max(jnp.where(is_g, logit, -jnp.inf), axis=-1, keepdims=True)
        g_sel = jnp.min(jnp.where(is_g & (logit == g_max), lane, float(LANES)), axis=-1, keepdims=True)
        p_group = 1.0 / jnp.sum(jnp.where(is_g, jnp.exp(logit - g_max), 0.0), axis=-1, keepdims=True)
        group_of = ((lane_i - R_EXPERT) // EXPERTS_PER_GROUP).astype(F32)
        mine = (lane_i >= R_EXPERT) & (lane_i < R_EXPERT + N_EXPERTS) & (group_of == g_sel)
        v1 = jnp.max(jnp.where(mine, logit, -jnp.inf), axis=-1, keepdims=True)
        i1 = jnp.min(jnp.where(mine & (logit == v1), lane, float(LANES)), axis=-1, keepdims=True)
        rest = mine & (lane != i1)
        v2 = jnp.max(jnp.where(rest, logit, -jnp.inf), axis=-1, keepdims=True)
        i2 = jnp.min(jnp.where(rest & (logit == v2), lane, float(LANES)), axis=-1, keepdims=True)
        e2 = jnp.exp(v2 - v1)
        w1 = p_group / (1.0 + e2)
        w2 = p_group * e2 / (1.0 + e2)
        gate_ref[...] = jnp.where(lane == i1, w1, 0.0) + jnp.where(lane == i2, w2, 0.0)
        o_ref[...] = x

    gates = gate_ref[...]
    lane = lax.broadcasted_iota(jnp.int32, gates.shape, 1)
    acts = []
    for j in range(EXPERTS_PER_GROUP):
        gate_e = jnp.sum(jnp.where(lane == R_EXPERT + e * EXPERTS_PER_GROUP + j, gates, 0.0), axis=-1, keepdims=True)
        gu = _dot(h_ref[...], win_ref[j])
        acts.append((_silu(gu[:, :EXPERT_FF]) * gu[:, EXPERT_FF:] * gate_e).astype(BF16))
    o_ref[...] += _dot(jnp.concatenate(acts, axis=1), wout_ref[0])


def moe(x2, gain, wr_hi, wr_lo, br, win_bf, wout_bf, layer, mixer_pending=(), tm=1024):
    n, d = x2.shape
    row = lambda w: pl.BlockSpec((tm, w), lambda i, e: (i, 0))
    win_g = win_bf.reshape(-1, d, 2 * EXPERT_FF)
    wout_g = wout_bf.reshape(-1, EXPERTS_PER_GROUP * EXPERT_FF, d)
    pending_specs = [row(mixer_pending[0].shape[1]), row(mixer_pending[1].shape[1]),
                     _full(mixer_pending[2].shape)] if mixer_pending else []
    return pl.pallas_call(
        functools.partial(_moe_body, mixer_pending=bool(mixer_pending)),
        grid=(n // tm, N_GROUPS),
        in_specs=[row(d)] + pending_specs
                 + [_full((1, d)), _full((d, LANES)), _full((d, LANES)), _full((1, LANES)),
                    pl.BlockSpec((EXPERTS_PER_GROUP, d, 2 * EXPERT_FF), lambda i, e: (layer * N_GROUPS + e, 0, 0)),
                    pl.BlockSpec((1, EXPERTS_PER_GROUP * EXPERT_FF, d), lambda i, e: (layer * N_GROUPS + e, 0, 0))],
        out_specs=row(d),
        out_shape=jax.ShapeDtypeStruct((n, d), F32),
        scratch_shapes=[pltpu.VMEM((tm, d), BF16), pltpu.VMEM((tm, LANES), F32)],
        compiler_params=_params("parallel", "arbitrary"),
    )(x2, *mixer_pending, gain.reshape(1, d), wr_hi, wr_lo, br, win_g, wout_g)


G_CUM, G_BETA, G_LAST = 0, GDN_HEADS, 2 * GDN_HEADS


def _gdn_gates_body(ab_ref, alog_ref, dtb_ref, gb_ref):
    t = ab_ref.shape[1]
    ab = ab_ref[0]
    sp_in = ab + dtb_ref[...]
    softplus = jnp.maximum(sp_in, 0.0) + jnp.log1p(jnp.exp(-jnp.abs(sp_in)))
    lane_row = lax.broadcasted_iota(jnp.int32, (1, LANES), 1)
    g = jnp.where(lane_row < GDN_HEADS, -jnp.exp(alog_ref[...]) * softplus, 0.0)
    blk = 4 * GDN_CHUNK
    r = lax.broadcasted_iota(jnp.int32, (blk, blk), 0)
    c = lax.broadcasted_iota(jnp.int32, (blk, blk), 1)
    same = r // GDN_CHUNK == c // GDN_CHUNK
    tri = jnp.where(same & (r >= c), 1.0, 0.0).astype(BF16)
    tot = jnp.where(same, 1.0, 0.0).astype(BF16)
    lane = lax.broadcasted_iota(jnp.int32, (blk, LANES), 1)
    for s in range(t // blk):
        rs = slice(s * blk, (s + 1) * blk)
        hi, mid, lo = _split3(g[rs])
        gc = _dot(tri, hi) + _dot(tri, mid) + _dot(tri, lo)
        gl = _dot(tot, hi) + _dot(tot, mid) + _dot(tot, lo)
        gl = pltpu.roll(gl, G_LAST, 1)
        gb_ref[0, rs, :] = jnp.where(lane < G_BETA, gc, jnp.where(lane < G_LAST, _sigmoid(ab[rs]), gl))


def gdn_gates(proj, alog_row, dtb_row):
    b, t, _ = proj.shape
    return pl.pallas_call(
        _gdn_gates_body,
        grid=(b,),
        in_specs=[pl.BlockSpec((1, t, LANES), lambda i: (i, 0, C_AB // LANES)), _full((1, LANES)), _full((1, LANES))],
        out_specs=pl.BlockSpec((1, t, LANES), lambda i: (i, 0, 0)),
        out_shape=jax.ShapeDtypeStruct((b, t, LANES), F32),
        compiler_params=_params("parallel"),
    )(proj, alog_row, dtb_row)


GDN_HEADS_PER_STEP = 8
GDN_CHUNKS_PER_TRIP = 4
GDN_SEGMENTS = 4


def _dot3(a, b):
    a_hi, a_lo = _split2(a)
    b_hi, b_lo = _split2(b)
    return _dot(a_hi, b_hi) + _dot(a_hi, b_lo) + _dot(a_lo, b_hi)


def _dot1(a, b):
    return _dot(a.astype(BF16), b.astype(BF16))


CONV_HALO = 8


def _conv_silu(ext_ref, cw_ref, lanes):
    t = ext_ref.shape[0] - CONV_HALO
    y = ext_ref[CONV_HALO:, lanes] * cw_ref[CONV_WIDTH - 1:CONV_WIDTH, lanes]
    for d in range(1, CONV_WIDTH):
        y = y + ext_ref[CONV_HALO - d:CONV_HALO - d + t, lanes] * cw_ref[CONV_WIDTH - 1 - d:CONV_WIDTH - d, lanes]
    return _silu(y)


def _l2norm(y):
    return y * lax.rsqrt(jnp.sum(y * y, axis=-1, keepdims=True) + NORM_EPS)


def _gdn_body(q_ref, k_ref, v_ref, cq_ref, ck_ref, cv_ref, gb_ref, grow_ref, z_ref, x_ref, gain_ref, wout_ref, o_ref,
              ext_ref, state_ref, gl_ref, gc_ref, kb_ref, k_ref_s, kbg_ref, vb_ref, qs_ref, qg_ref, kd_ref,
              u_ref, w_ref, a_ref, mix_ref):
    t = q_ref.shape[1]
    cs = GDN_CHUNK
    dk = GDN_HEAD_DIM
    nh = GDN_HEADS_PER_STEP
    seg = pl.program_id(1)
    gb_hi, gb_mid, gb_lo = _split3(gb_ref[0])
    pick_row = lax.broadcasted_iota(jnp.int32, (LANES, LANES), 0)

    @pl.when(seg == 0)
    def _():
        ext_ref[:, 0:CONV_HALO, :] = jnp.zeros((3, CONV_HALO, ext_ref.shape[2]), F32)
        state_ref[...] = jnp.zeros_like(state_ref)

    @pl.when(seg > 0)
    def _():
        ext_ref[:, 0:CONV_HALO, :] = ext_ref[:, t:t + CONV_HALO, :]

    for i, ref in enumerate((q_ref, k_ref, v_ref)):
        ext_ref[i, CONV_HALO:, :] = ref[0]

    def column(idx):
        sel = jnp.where(pick_row == idx, 1.0, 0.0).astype(BF16)
        return _dot(gb_hi, sel) + _dot(gb_mid, sel) + _dot(gb_lo, sel)

    for s in range(nh):
        lanes = slice(s * dk, (s + 1) * dk)
        gcol = column(G_CUM + s)
        bcol = column(G_BETA + s)
        glast = column(G_LAST + s)
        eg = jnp.exp(gcol)
        k = _l2norm(_conv_silu(ext_ref.at[1], ck_ref, lanes))
        kb = k * bcol
        k_ref_s[s] = k.astype(BF16)
        kb_ref[s] = kb.astype(BF16)
        kbg_ref[s] = (kb * eg).astype(BF16)
        kd_ref[s] = (k * jnp.exp(glast - gcol)).astype(BF16)
        q = _l2norm(_conv_silu(ext_ref.at[0], cq_ref, lanes)) * (dk ** -0.5)
        qs_ref[s] = q.astype(BF16)
        qg_ref[s] = (q * eg).astype(BF16)
        vb_ref[s] = (_conv_silu(ext_ref.at[2], cv_ref, lanes) * bcol).astype(BF16)
        gl_ref[s] = glast
        gc_ref[s] = gcol

    r = lax.broadcasted_iota(jnp.int32, (cs, cs), 0)
    c = lax.broadcasted_iota(jnp.int32, (cs, cs), 1)
    tril = r >= c
    strict = r > c
    eye = jnp.where(r == c, 1.0, 0.0)

    def prep(trip, _):
        probs = [(s, trip * GDN_CHUNKS_PER_TRIP + j) for j in range(GDN_CHUNKS_PER_TRIP) for s in range(nh)]
        rows = [pl.ds(pl.multiple_of(n * cs, cs), cs) for _, n in probs]
        decay, lmat = [], []
        for (s, n), rw in zip(probs, rows):
            gr = grow_ref[0, s, pl.ds(n, 1), :]
            gc = gc_ref[s, rw, :cs]
            decay.append(jnp.where(tril, jnp.exp(jnp.where(tril, gc - gr, 0.0)), 0.0))
        for i, ((s, _), rw) in enumerate(zip(probs, rows)):
            lmat.append(jnp.where(strict, _dot_nt(kb_ref[s, rw, :], k_ref_s[s, rw, :]) * decay[i], 0.0))
        inv = [eye - m for m in lmat]
        pw = [_dot3(m, m) for m in lmat]
        span = 2
        while span < cs:
            mm = _dot3 if span == 2 else _dot1
            inv = [x + mm(x, p) for x, p in zip(inv, pw)]
            span *= 2
            if span < cs:
                pw = [_dot1(p, p) for p in pw]
        inv_bf = [x.astype(BF16) for x in inv]
        for i, ((s, _), rw) in enumerate(zip(probs, rows)):
            u_ref[s, rw, :] = _dot(inv_bf[i], vb_ref[s, rw, :])
            w_ref[s, rw, :] = _dot(inv_bf[i], kbg_ref[s, rw, :]).astype(BF16)
            a_ref[s, rw, :] = jnp.where(tril, _dot_nt(qs_ref[s, rw, :], k_ref_s[s, rw, :]) * decay[i], 0.0).astype(BF16)
        return 0

    lax.fori_loop(0, t // (cs * GDN_CHUNKS_PER_TRIP), prep, 0)

    def scan(n, states):
        r0 = pl.multiple_of(n * cs, cs)
        rows = pl.ds(r0, cs)
        s_bf = [st.astype(BF16) for st in states]
        v_bf = [(u_ref[s, rows, :] - _dot(w_ref[s, rows, :], s_bf[s])).astype(BF16) for s in range(nh)]
        new = [states[s] * jnp.exp(gl_ref[s, pl.ds(r0, 1), :]) + _dot_tn(kd_ref[s, rows, :], v_bf[s])
               for s in range(nh)]
        for s in range(nh):
            mix_ref[rows, s * dk:(s + 1) * dk] = _dot(qg_ref[s, rows, :], s_bf[s]) + _dot(a_ref[s, rows, :], v_bf[s])
        return tuple(new)

    final = lax.fori_loop(0, t // cs, scan, tuple(state_ref[s] for s in range(nh)))
    for s in range(nh):
        state_ref[s] = final[s]

    gain = gain_ref[...]
    parts = []
    for s in range(nh):
        lanes = slice(s * dk, (s + 1) * dk)
        o = mix_ref[:, lanes]
        y = o * lax.rsqrt(jnp.mean(o * o, axis=-1, keepdims=True) + NORM_EPS) * gain
        parts.append((y * _silu(z_ref[0, :, lanes])).astype(BF16))
    o_ref[0] = x_ref[0] + _dot(jnp.concatenate(parts, axis=1), wout_ref[...])


def gdn_mixer(x3, proj, conv_w, gb, gain, wout_bf):
    b, t, d = x3.shape
    ts = t // GDN_SEGMENTS if t % (GDN_SEGMENTS * GDN_CHUNK * GDN_CHUNKS_PER_TRIP) == 0 else t
    seg_chunks = ts // GDN_CHUNK
    nh = GDN_HEADS_PER_STEP
    assert nh == GDN_HEADS, "the fused output projection needs every head of a row in one grid step"
    g_rows = jnp.swapaxes(gb[:, :, G_CUM:G_CUM + GDN_HEADS], 1, 2).reshape(b, GDN_HEADS, t // GDN_CHUNK, GDN_CHUNK)
    sect = lambda k: pl.BlockSpec((1, ts, GDN_WIDTH), lambda i, s: (i, s, k))
    taps = lambda k: pl.BlockSpec((CONV_WIDTH, GDN_WIDTH), lambda i, s: (0, k))
    rows = lambda w: pl.BlockSpec((1, ts, w), lambda i, s: (i, s, 0))
    bf = lambda w: pltpu.VMEM((nh, ts, w), BF16)
    return pl.pallas_call(
        _gdn_body,
        grid=(b, t // ts),
        in_specs=[sect(0), sect(1), sect(2), taps(0), taps(1), taps(2), rows(LANES),
                  pl.BlockSpec((1, nh, seg_chunks, GDN_CHUNK), lambda i, s: (i, 0, s, 0)),
                  sect(3), rows(d), _full((1, GDN_HEAD_DIM)), _full(wout_bf.shape)],
        out_specs=rows(d),
        out_shape=jax.ShapeDtypeStruct((b, t, d), F32),
        scratch_shapes=[pltpu.VMEM((3, CONV_HALO + ts, GDN_WIDTH), F32),
                        pltpu.VMEM((nh, GDN_HEAD_DIM, GDN_HEAD_DIM), F32)]
                       + [pltpu.VMEM((nh, ts, LANES), F32)] * 2 + [bf(GDN_HEAD_DIM)] * 7
                       + [pltpu.VMEM((nh, ts, GDN_HEAD_DIM), F32), bf(GDN_HEAD_DIM), bf(GDN_CHUNK),
                          pltpu.VMEM((ts, GDN_WIDTH), F32)],
        compiler_params=_params("parallel", "arbitrary"),
    )(proj, proj, proj, conv_w, conv_w, conv_w, gb, g_rows, proj, x3, gain, wout_bf)


def _rope_tables(pos):
    half = HEAD_DIM // 2
    inv_freq = ROPE_THETA ** (-jnp.arange(half, dtype=F32) / half)
    ang = pos.astype(F32)[:, None] * inv_freq
    cos = jnp.cos(ang)
    sin = jnp.sin(ang)
    cos_t = jnp.tile(jnp.concatenate([cos, cos], axis=-1), (1, LANES // HEAD_DIM))
    sin_t = jnp.tile(jnp.concatenate([-sin, sin], axis=-1), (1, LANES // HEAD_DIM))
    return cos_t, sin_t


def _block_diag_ones(width, seg):
    idx = np.arange(width) // seg
    return jnp.asarray((idx[:, None] == idx[None, :]).astype(np.float32), dtype=BF16)


def _pad_cols(w, width):
    return jnp.pad(w, ((0, 0), (0, width - w.shape[1])))


def _even_layer(x2, b, t, norm_gain, w_in, b_gate, b_forget, cmp_pe, cmp_w1, cmp_w2, nsa_gain, fox_gain, w_out):
    d = x2.shape[1]
    o_gate = NSA_Q_W + 6 * NSA_KV_W
    o_fox = o_gate + NSA_GATE_W
    w_bf = w_in.astype(BF16)
    w_re = jnp.concatenate([w_bf[:, :o_gate], w_bf[:, o_fox:o_fox + 3 * FOX_W], w_bf[:, o_gate:o_fox],
                            w_bf[:, o_fox + 3 * FOX_W:], jnp.zeros((d, EVEN_W - w_in.shape[1]), BF16)], axis=1)
    cos, sin = _rope_tables(jnp.arange(t))
    tile = lambda g, n: jnp.tile(g, n).reshape(1, -1)
    bias = jnp.pad(jnp.concatenate([b_gate, b_forget]), (0, LANES - NSA_GATE_W - FOX_HEADS)).reshape(1, LANES)
    bd = _block_diag_ones(FOX_W, HEAD_DIM)
    (qa, ks, kw, vs, vw, kc_raw, vc_raw, qb, kb, vf, gates, cum) = even_in_prep(
        x2.reshape(b, t, d), norm_gain, w_re, cos, sin, tile(nsa_gain[0], NSA_HEADS), tile(nsa_gain[2], NSA_KV_HEADS), tile(nsa_gain[3], NSA_KV_HEADS),
        tile(fox_gain[0], FOX_HEADS), tile(fox_gain[1], FOX_HEADS), bias, bd)

    n_str = t // CMP_STRIDE
    half = CMP_BLOCK // 2
    eye2 = jnp.eye(NSA_KV_HEADS, dtype=F32)
    pe = jnp.tile(cmp_pe[:, :, None, :], (1, 1, NSA_KV_HEADS, 1)).reshape(2, 2, 1, half * NSA_KV_W)
    w1 = jnp.einsum('ilde,hg->ilhdge', cmp_w1, eye2).reshape(2, 2, half * NSA_KV_W, NSA_KV_W).astype(BF16)
    w2 = jnp.einsum('ide,hg->ihdge', cmp_w2, eye2).reshape(2, NSA_KV_W, NSA_KV_W).astype(BF16)
    cos_c, sin_c = _rope_tables(jnp.arange(n_str) * CMP_STRIDE + (CMP_BLOCK - 1))
    kc, vc = compress(kc_raw, vc_raw, pe, w1, w2, tile(nsa_gain[1], NSA_KV_HEADS), cos_c, sin_c, _block_diag_ones(LANES, HEAD_DIM))

    n_sel = t // SEL_BLOCK
    cs = np.arange(n_str)[:, None] * CMP_STRIDE
    ss = np.arange(n_sel)[None, :] * SEL_BLOCK
    overlap = np.clip(np.minimum(cs + CMP_BLOCK, ss + SEL_BLOCK) - np.maximum(cs, ss), 0, None) / CMP_BLOCK
    overlap[(t - CMP_BLOCK) // CMP_STRIDE + 1:] = 0.0
    ovt = jnp.asarray(overlap.T.astype(np.float32), dtype=BF16)
    o_a = nsa_attention(qa, kc, vc, ks, vs, kw, vw, gates, ovt)
    o_b = fox_attention(qb, kb, vf, cum)

    return o_a.reshape(b * t, NSA_Q_W), o_b.reshape(b * t, FOX_W), w_out.astype(BF16)


def _odd_layer(x2, b, t, norm_gain, w_in, conv_w, a_log, dt_bias, gdn_gain, w_out):
    w_bf = w_in.astype(BF16)
    w_pad = jnp.concatenate([w_bf, jnp.zeros((w_in.shape[0], ODD_W - w_in.shape[1]), BF16)], axis=1)
    proj = norm_matmul(x2, norm_gain, w_pad).reshape(b, t, ODD_W)
    pad8 = lambda v: jnp.pad(v, (0, LANES - GDN_HEADS)).reshape(1, LANES)
    gb = gdn_gates(proj, pad8(a_log), pad8(dt_bias))
    out = gdn_mixer(x2.reshape(b, t, -1), proj, conv_w, gb, gdn_gain.reshape(1, GDN_HEAD_DIM), w_out.astype(BF16))
    return out.reshape(b * t, -1)


def _moe_layer(x2, gain, w_rg, b_rg, w_re, b_re, w_ein_bf, w_eout_bf, layer, mixer_pending=()):
    d = x2.shape[1]
    wr = _pad_cols(jnp.concatenate([w_rg, w_re], axis=1), LANES)
    wr_hi = wr.astype(BF16)
    wr_lo = (wr - wr_hi.astype(F32)).astype(BF16)
    br = jnp.pad(jnp.concatenate([b_rg, b_re]), (0, LANES - N_GROUPS - N_EXPERTS)).reshape(1, LANES)
    return moe(x2, gain, wr_hi, wr_lo, br, w_ein_bf, w_eout_bf, layer, mixer_pending)


def kernel(x, norm_mix, norm_ffn, w_in_even, b_nsa_gate, b_forget, cmp_pe, cmp_w1, cmp_w2, nsa_qk_gain, fox_qk_gain,
           w_out_even, w_in_odd, conv_w, a_log, dt_bias, gdn_norm_gain, w_out_odd, w_router_group, b_router_group,
           w_router_expert, b_router_expert, w_expert_in, w_expert_out):
    b, t, d = x.shape
    x2 = x.reshape(b * t, d)
    w_ein_bf = w_expert_in.astype(BF16)
    w_eout_bf = w_expert_out.astype(BF16)
    for layer in range(norm_mix.shape[0]):
        i = layer // 2
        mixer_pending = ()
        if layer % 2 == 0:
            mixer_pending = _even_layer(x2, b, t, norm_mix[layer], w_in_even[i], b_nsa_gate[i], b_forget[i], cmp_pe[i],
                                        cmp_w1[i], cmp_w2[i], nsa_qk_gain[i], fox_qk_gain[i], w_out_even[i])
        else:
            x2 = _odd_layer(x2, b, t, norm_mix[layer], w_in_odd[i], conv_w[i], a_log[i], dt_bias[i], gdn_norm_gain[i],
                            w_out_odd[i])
        x2 = _moe_layer(x2, norm_ffn[layer], w_router_group[layer], b_router_group[layer], w_router_expert[layer],
                        b_router_expert[layer], w_ein_bf, w_eout_bf, layer, mixer_pending)
    return x2.reshape(b, t, d)
```

```python
import functools

import numpy as np
import jax
import jax.numpy as jnp
from jax import lax
from jax.experimental import pallas as pl
from jax.experimental.pallas import tpu as pltpu

F32 = jnp.float32
BF16 = jnp.bfloat16

HEAD_DIM = 64
ROPE_THETA = 10000.0
NSA_HEADS = 8
NSA_KV_HEADS = 2
NSA_GROUP = NSA_HEADS // NSA_KV_HEADS
CMP_BLOCK = 32
CMP_STRIDE = 16
SEL_BLOCK = 64
SEL_TOPK = 8
WINDOW = 256
FOX_HEADS = 8
GDN_HEADS = 8
GDN_HEAD_DIM = 128
GDN_WIDTH = GDN_HEADS * GDN_HEAD_DIM
CONV_WIDTH = 4
GDN_CHUNK = 64
N_GROUPS = 4
EXPERTS_PER_GROUP = 4
N_EXPERTS = N_GROUPS * EXPERTS_PER_GROUP
EXPERT_FF = 256
NORM_EPS = 1e-6
NEG_INF = -1e30
FORCE_SCORE = 1e9

LANES = 128
LOG2E = 1.4426950408889634
SUM_ROWS = 16
NSA_Q_W = NSA_HEADS * HEAD_DIM
NSA_KV_W = NSA_KV_HEADS * HEAD_DIM
NSA_GATE_W = 3 * NSA_HEADS
FOX_W = FOX_HEADS * HEAD_DIM
C_QN = 0
C_KC, C_VC, C_KS, C_VS, C_KW, C_VW = (NSA_Q_W + i * NSA_KV_W for i in range(6))
C_QF = NSA_Q_W + 6 * NSA_KV_W
C_KF = C_QF + FOX_W
C_VF = C_KF + FOX_W
C_MISC = C_VF + FOX_W
EVEN_W = C_MISC + LANES
MISC_F = NSA_GATE_W
C_AB = 4 * GDN_WIDTH
ODD_W = C_AB + LANES

VMEM_LIMIT = 56 * 1024 * 1024


def _params(*sem):
    return pltpu.CompilerParams(dimension_semantics=sem, vmem_limit_bytes=VMEM_LIMIT)


def _dot(a, b):
    return jnp.dot(a, b, preferred_element_type=F32)


def _dot_nt(a, b):
    return lax.dot_general(a, b, (((1,), (1,)), ((), ())), preferred_element_type=F32)


def _dot_tn(a, b):
    return lax.dot_general(a, b, (((0,), (0,)), ((), ())), preferred_element_type=F32)


def _split2(x):
    hi = x.astype(BF16)
    return hi, (x - hi.astype(F32)).astype(BF16)


def _split3(x):
    hi = x.astype(BF16)
    r = x - hi.astype(F32)
    mid = r.astype(BF16)
    return hi, mid, (r - mid.astype(F32)).astype(BF16)


def _sigmoid(z):
    return 1.0 / (1.0 + jnp.exp(-z))


def _silu(z):
    return z * _sigmoid(z)


def _full(shape):
    nd = len(shape)
    return pl.BlockSpec(shape, lambda *_: (0,) * nd)


def _norm_matmul_body(x_ref, g_ref, w_ref, o_ref, wbf_ref):
    @pl.when(pl.program_id(0) == 0)
    def _():
        n_in = w_ref.shape[1]
        main = n_in // LANES * LANES
        wbf_ref[:, :main] = w_ref[:, :main].astype(BF16)
        if main < wbf_ref.shape[1]:
            wbf_ref[:, main:] = jnp.zeros((w_ref.shape[0], wbf_ref.shape[1] - main), BF16)
            wbf_ref[:, main:n_in] = w_ref[:, main:n_in].astype(BF16)

    x = x_ref[...]
    ms = jnp.mean(x * x, axis=-1, keepdims=True)
    h = (x * lax.rsqrt(ms + NORM_EPS) * g_ref[...]).astype(BF16)
    o_ref[...] = _dot(h, wbf_ref[...])


def norm_matmul(x2, gain, w, tm=512):
    n, d = x2.shape
    wp = -(-w.shape[1] // LANES) * LANES
    return pl.pallas_call(
        _norm_matmul_body,
        grid=(n // tm,),
        in_specs=[pl.BlockSpec((tm, d), lambda i: (i, 0)), _full((1, d)),
                  pl.BlockSpec(w.shape, lambda i: (0, 0), pipeline_mode=pl.Buffered(1))],
        out_specs=pl.BlockSpec((tm, wp), lambda i: (i, 0)),
        out_shape=jax.ShapeDtypeStruct((n, wp), F32),
        scratch_shapes=[pltpu.VMEM((d, wp), BF16)],
        compiler_params=_params("arbitrary"),
    )(x2, gain.reshape(1, d), w)


def _head_rms(x, bd, gain):
    hi, lo = _split2(x * x)
    ones2 = bd[:LANES, :LANES]
    ssum = jnp.concatenate([_dot(hi[:, c:c + LANES], ones2) + _dot(lo[:, c:c + LANES], ones2)
                            for c in range(0, x.shape[1], LANES)], axis=1)
    return x * lax.rsqrt(ssum * (1.0 / HEAD_DIM) + NORM_EPS) * gain


def _rope(x, cos, sin_signed, first_half):
    fwd = pltpu.roll(x, LANES - HEAD_DIM // 2, 1)
    bwd = pltpu.roll(x, HEAD_DIM // 2, 1)
    return x * cos + jnp.where(first_half, fwd, bwd) * sin_signed


def _even_prep_body(p_ref, cos_ref, sin_ref, gq_ref, gks_ref, gkw_ref, gfq_ref, gfk_ref, bias_ref, bd_ref,
                    qa_ref, ks_ref, kw_ref, vs_ref, vw_ref, kc_ref, vc_ref, qb_ref, kb_ref, vf_ref,
                    gate_ref, cum_ref, carry_ref, stage_ref):
    tr = p_ref.shape[1]
    bd = bd_ref[...]
    cos = cos_ref[...]
    sin = sin_ref[...]
    lane = lax.broadcasted_iota(jnp.int32, (1, LANES), 1)
    first_half = (lane % HEAD_DIM) < (HEAD_DIM // 2)
    scale = HEAD_DIM ** -0.5 * LOG2E

    qn = _head_rms(p_ref[0, :, C_QN:C_QN + NSA_Q_W], bd, gq_ref[...])
    for c in range(NSA_Q_W // LANES):
        sl = slice(c * LANES, (c + 1) * LANES)
        qa_ref[0, sl, :] = (_rope(qn[:, sl], cos, sin, first_half) * scale).T.astype(BF16)
    ks = _head_rms(p_ref[0, :, C_KS:C_KS + NSA_KV_W], bd, gks_ref[...])
    ks_ref[0] = _rope(ks, cos, sin, first_half).astype(BF16)
    kw = _head_rms(p_ref[0, :, C_KW:C_KW + NSA_KV_W], bd, gkw_ref[...])
    kw_ref[0] = _rope(kw, cos, sin, first_half).astype(BF16)
    vs_ref[0] = p_ref[0, :, C_VS:C_VS + NSA_KV_W].T.astype(BF16)
    vw_ref[0] = p_ref[0, :, C_VW:C_VW + NSA_KV_W].T.astype(BF16)
    stage_ref[0] = p_ref[0, :, C_KC:C_KC + NSA_KV_W]
    stage_ref[1] = p_ref[0, :, C_VC:C_VC + NSA_KV_W]
    for l in range(CMP_STRIDE):
        rows = pl.ds(l, tr // CMP_STRIDE, stride=CMP_STRIDE)
        kc_ref[0, :, l * NSA_KV_W:(l + 1) * NSA_KV_W] = stage_ref[0, rows, :]
        vc_ref[0, :, l * NSA_KV_W:(l + 1) * NSA_KV_W] = stage_ref[1, rows, :]

    qb = _head_rms(p_ref[0, :, C_QF:C_QF + FOX_W], bd, gfq_ref[...]) * scale
    kb_ref[0] = _head_rms(p_ref[0, :, C_KF:C_KF + FOX_W], bd, gfk_ref[...]).astype(BF16)
    for c in range(FOX_W // LANES):
        sl = slice(c * LANES, (c + 1) * LANES)
        qb_ref[0, sl, :] = qb[:, sl].T.astype(BF16)
        vf_ref[0, sl, :] = p_ref[0, :, C_VF + c * LANES:C_VF + (c + 1) * LANES].T.astype(BF16)

    z = p_ref[0, :, C_MISC:C_MISC + LANES] + bias_ref[...]
    gate_ref[0] = _sigmoid(z).T
    logf = jnp.minimum(z, 0.0) - jnp.log1p(jnp.exp(-jnp.abs(z)))

    @pl.when(pl.program_id(1) == 0)
    def _():
        carry_ref[...] = jnp.zeros_like(carry_ref)

    row = lax.broadcasted_iota(jnp.int32, (tr, tr), 0)
    col = lax.broadcasted_iota(jnp.int32, (tr, tr), 1)
    tril = jnp.where(row >= col, 1.0, 0.0).astype(BF16)
    hi, mid, lo = _split3(logf)
    cum = _dot(tril, hi) + _dot(tril, mid) + _dot(tril, lo) + carry_ref[...]
    cum_ref[0] = cum
    carry_ref[...] = cum[tr - 1:tr, :]


def _even_in_body(x_ref, g_ref, w_ref, *rest):
    proj_ref, wbf_ref = rest[-2:]

    @pl.when((pl.program_id(0) == 0) & (pl.program_id(1) == 0))
    def _():
        o_fox = C_QF + NSA_GATE_W
        n_fox = 3 * FOX_W
        wbf_ref[:, :C_QF] = w_ref[:, :C_QF].astype(BF16)
        wbf_ref[:, C_QF:C_MISC] = w_ref[:, o_fox:o_fox + n_fox].astype(BF16)
        wbf_ref[:, C_MISC:] = jnp.zeros((w_ref.shape[0], LANES), BF16)
        wbf_ref[:, C_MISC:C_MISC + NSA_GATE_W] = w_ref[:, C_QF:o_fox].astype(BF16)
        wbf_ref[:, C_MISC + NSA_GATE_W:C_MISC + NSA_GATE_W + FOX_HEADS] = w_ref[:, o_fox + n_fox:].astype(BF16)

    x = x_ref[0]
    h = (x * lax.rsqrt(jnp.mean(x * x, axis=-1, keepdims=True) + NORM_EPS) * g_ref[...]).astype(BF16)
    proj_ref[0] = _dot(h, wbf_ref[...])
    _even_prep_body(proj_ref, *rest[:-2])


def even_in_prep(x3, gain, w, cos, sin, gq, gks, gkw, gfq, gfk, bias, bd, tr=512):
    b, t, d = x3.shape
    row = lambda w: pl.BlockSpec((1, tr, w), lambda i, j: (i, j, 0))
    tab = pl.BlockSpec((tr, LANES), lambda i, j: (j, 0))
    shp = lambda w, dt: jax.ShapeDtypeStruct((b, t, w), dt)
    col = lambda w: pl.BlockSpec((1, w, tr), lambda i, j: (i, 0, j))
    shp_t = lambda w, dt: jax.ShapeDtypeStruct((b, w, t), dt)
    strd = pl.BlockSpec((1, tr // CMP_STRIDE, CMP_STRIDE * NSA_KV_W), lambda i, j: (i, j, 0))
    strd_shape = jax.ShapeDtypeStruct((b, t // CMP_STRIDE, CMP_STRIDE * NSA_KV_W), F32)
    return pl.pallas_call(
        _even_in_body,
        grid=(b, t // tr),
        in_specs=[row(d), _full((1, d)), pl.BlockSpec(w.shape, lambda i, j: (0, 0), pipeline_mode=pl.Buffered(1)),
                  tab, tab, _full((1, NSA_Q_W)), _full((1, LANES)), _full((1, LANES)),
                  _full((1, FOX_W)), _full((1, FOX_W)), _full((1, LANES)), _full((FOX_W, FOX_W))],
        out_specs=[col(NSA_Q_W), row(LANES), row(LANES), col(LANES), col(LANES), strd, strd,
                   col(FOX_W), row(FOX_W), col(FOX_W), col(LANES), row(LANES)],
        out_shape=[shp_t(NSA_Q_W, BF16), shp(LANES, BF16), shp(LANES, BF16), shp_t(LANES, BF16), shp_t(LANES, BF16),
                   strd_shape, strd_shape, shp_t(FOX_W, BF16), shp(FOX_W, BF16), shp_t(FOX_W, BF16),
                   shp_t(LANES, F32), shp(LANES, F32)],
        scratch_shapes=[pltpu.VMEM((1, LANES), F32), pltpu.VMEM((2, tr, LANES), F32), pltpu.VMEM((1, tr, EVEN_W), F32),
                        pltpu.VMEM((d, EVEN_W), BF16)],
        compiler_params=_params("arbitrary", "arbitrary"),
    )(x3, gain.reshape(1, d), w, cos, sin, gq, gks, gkw, gfq, gfk, bias, bd)


def _gelu_tanh(x):
    return 0.5 * x * (1.0 + jnp.tanh(np.sqrt(2.0 / np.pi).astype(np.float32) * (x + 0.044715 * (x * x * x))))


def _compress_body(xk_ref, xv_ref, pe_ref, w1_ref, w2_ref, gk_ref, cos_ref, sin_ref, bd_ref, kc_ref, vc_ref):
    n = xk_ref.shape[1]
    lane = lax.broadcasted_iota(jnp.int32, (1, LANES), 1)
    first_half = (lane % HEAD_DIM) < (HEAD_DIM // 2)

    def mlp(x_ref, i):
        x = x_ref[0]
        nxt = pltpu.roll(x, n - 1, 0)
        xa = (x + pe_ref[i, 0]).astype(BF16)
        xb = (nxt + pe_ref[i, 1]).astype(BF16)
        h = _dot(xa, w1_ref[i, 0]) + _dot(xb, w1_ref[i, 1])
        return _dot(_gelu_tanh(h).astype(BF16), w2_ref[i])

    kc = _head_rms(mlp(xk_ref, 0), bd_ref[...], gk_ref[...])
    kc_ref[0] = _rope(kc, cos_ref[...], sin_ref[...], first_half).astype(BF16)
    vc_ref[0] = mlp(xv_ref, 1).T.astype(BF16)


def compress(xk, xv, pe, w1, w2, gk, cos_c, sin_c, bd):
    b, n, w = xk.shape
    blk = pl.BlockSpec((1, n, w), lambda i: (i, 0, 0))
    out = pl.BlockSpec((1, n, LANES), lambda i: (i, 0, 0))
    return pl.pallas_call(
        _compress_body,
        grid=(b,),
        in_specs=[blk, blk, _full(pe.shape), _full(w1.shape), _full(w2.shape), _full((1, LANES)),
                  _full((n, LANES)), _full((n, LANES)), _full((LANES, LANES))],
        out_specs=[out, pl.BlockSpec((1, LANES, n), lambda i: (i, 0, 0))],
        out_shape=[jax.ShapeDtypeStruct((b, n, LANES), BF16), jax.ShapeDtypeStruct((b, LANES, n), BF16)],
        compiler_params=_params("parallel"),
    )(xk, xv, pe, w1, w2, gk, cos_c, sin_c, bd)


def _flash_step(s_ref, p_ref, acc_ref, v_blk, m_i, l_i, adjust, first=False):
    n_ch = acc_ref.shape[0]
    al, ms = [], []
    for cg in range(s_ref.shape[1] // LANES):
        sl = slice(cg * LANES, (cg + 1) * LANES)
        s = adjust(s_ref[:, sl], cg)
        m_new = jnp.maximum(m_i[:, sl], jnp.max(s, axis=0, keepdims=True))
        p_ref[:, sl] = jnp.exp2(s - m_new).astype(BF16)
        al.append(jnp.exp2(m_i[:, sl] - m_new))
        ms.append(m_new)
    cat = lambda xs: jnp.concatenate(xs, axis=1)
    alpha = cat(al)
    pv = _dot(v_blk, p_ref[...])
    acc_ref[...] = pv[:n_ch] if first else alpha * acc_ref[...] + pv[:n_ch]
    return cat(ms), alpha * l_i + pv[n_ch:n_ch + 1]


NSA_KEYS_PER_QUERY_BLOCK = 2


def _nsa_body(q_ref, kc_ref, vc_ref, ks_ref, vs_ref, kw_ref, vw_ref, gate_ref, ovt_ref, o_ref,
              sel_ref, s0_ref, s1_ref, p_ref, acc_ref, *, k_top):
    tq = q_ref.shape[2]
    t_all = ks_ref.shape[1]
    n_cmp = kc_ref.shape[1]
    n_sel = ovt_ref.shape[0]
    g_n = NSA_GROUP
    c = pl.program_id(1)
    t0 = c * tq
    chan = lax.broadcasted_iota(jnp.int32, (LANES, 1), 0)
    tlane = t0 + lax.broadcasted_iota(jnp.int32, (1, tq), 1)
    gates = gate_ref[0]

    nrow = lax.broadcasted_iota(jnp.int32, (n_cmp, 1), 0)
    valid_c = (nrow * CMP_STRIDE + (CMP_BLOCK - 1)) <= tlane
    jrow = lax.broadcasted_iota(jnp.int32, (n_sel, tq), 0)
    jrow_f = jrow.astype(F32)
    cur = tlane // SEL_BLOCK
    forced = (jrow == 0) | (jrow == cur) | (jrow == cur - 1)
    future = jrow * SEL_BLOCK > tlane
    tk = NSA_KEYS_PER_QUERY_BLOCK * tq
    krow = lax.broadcasted_iota(jnp.int32, (tk, 1), 0)
    per_blk = tk // SEL_BLOCK
    w_len = tq + WINDOW
    w_start = pl.multiple_of(jnp.clip(t0 - WINDOW, 0, t_all - w_len), LANES)
    wrow = w_start + lax.broadcasted_iota(jnp.int32, (w_len, 1), 0)
    valid_w = (wrow <= tlane) & (wrow > tlane - WINDOW)

    heads = [(kvh, g) for kvh in range(NSA_KV_HEADS) for g in range(g_n)]
    zero_half = jnp.zeros((HEAD_DIM, tq), BF16)

    def on_kv_rows(h, kvh):
        blk = q_ref[0, h * HEAD_DIM:(h + 1) * HEAD_DIM, :]
        return jnp.concatenate([blk, zero_half] if kvh == 0 else [zero_half, blk], axis=0)

    qst = jnp.concatenate([on_kv_rows(h, kvh) for h, (kvh, _) in enumerate(heads)], axis=1)
    n_col = len(heads) * tq

    def softmax_cols(s, ok, guard):
        outs = []
        for cg in range(len(heads)):
            sc = jnp.where(ok, s[:, cg * tq:(cg + 1) * tq], NEG_INF)
            e = jnp.exp2(sc - jnp.max(sc, axis=0, keepdims=True))
            if guard:
                e = jnp.where(ok, e, 0.0)
            den = jnp.sum(e, axis=0, keepdims=True)
            outs.append(e * (1.0 / (jnp.where(den > 0.0, den, 1.0) if guard else den)))
        return outs

    p_c = softmax_cols(_dot(kc_ref[0], qst), valid_c, guard=True)
    o_cmp = _dot(vc_ref[0], jnp.concatenate(p_c, axis=1).astype(BF16))

    for kvh in range(NSA_KV_HEADS):
        p_sum = p_c[kvh * g_n]
        for g in range(1, g_n):
            p_sum = p_sum + p_c[kvh * g_n + g]
        p_hi, p_lo = _split2(p_sum)
        imp_t = _dot(ovt_ref[...], p_hi) + _dot(ovt_ref[...], p_lo)
        val = jnp.where(forced, FORCE_SCORE, jnp.where(future, NEG_INF, imp_t))
        sel_t = jnp.zeros((n_sel, tq), F32)
        for _ in range(k_top):
            m = jnp.max(val, axis=0, keepdims=True)
            first = jnp.min(jnp.where(val == m, jrow_f, float(n_sel)), axis=0, keepdims=True)
            pick = jrow_f == first
            sel_t = jnp.where(pick, 1.0, sel_t)
            val = jnp.where(pick, -jnp.inf, val)
        sel_ref[kvh] = sel_t

    p_w = softmax_cols(_dot(kw_ref[0, pl.ds(w_start, w_len), :], qst), valid_w, guard=False)
    o_win = _dot(vw_ref[0, :, pl.ds(w_start, w_len)], jnp.concatenate(p_w, axis=1).astype(BF16))

    def put_scores(buf, kb):
        k0 = pl.multiple_of(jnp.minimum(kb * tk, t_all - tk), tk)
        buf[...] = _dot(ks_ref[0, pl.ds(k0, tk), :], qst)

    def half_step(buf, kb, m_i, l_i):
        k0 = pl.multiple_of(jnp.minimum(kb * tk, t_all - tk), tk)
        causal = (kb * tk + krow) <= tlane
        ok = [causal & (jnp.concatenate([jnp.broadcast_to(sel_ref[kvh, pl.ds(k0 // SEL_BLOCK + r, 1), :],
                                                          (SEL_BLOCK, tq)) for r in range(per_blk)], axis=0) > 0.5)
              for kvh in range(NSA_KV_HEADS)]
        adjust = lambda s_cols, cg: jnp.where(ok[cg // g_n], s_cols, NEG_INF)
        v_blk = jnp.concatenate([vs_ref[0, :, pl.ds(k0, tk)], jnp.ones((SUM_ROWS, tk), BF16)], axis=0)
        return _flash_step(buf, p_ref, acc_ref, v_blk, m_i, l_i, adjust)

    def sel_trip(j, carry):
        put_scores(s1_ref, 2 * j + 1)
        carry = half_step(s0_ref, 2 * j, *carry)
        put_scores(s0_ref, 2 * j + 2)
        return half_step(s1_ref, 2 * j + 1, *carry)

    put_scores(s0_ref, 0)
    acc_ref[...] = jnp.zeros_like(acc_ref)
    init = (jnp.full((1, n_col), NEG_INF, F32), jnp.zeros((1, n_col), F32))
    n_blocks = (t0 + tq + tk - 1) // tk
    _, l_s = lax.fori_loop(0, (n_blocks + 1) // 2, sel_trip, init)
    o_slc = acc_ref[...] * (1.0 / l_s)

    gated = []
    for h, (kvh, _) in enumerate(heads):
        cols = slice(h * tq, (h + 1) * tq)
        rows = slice(kvh * HEAD_DIM, (kvh + 1) * HEAD_DIM)
        gated.append(gates[3 * h:3 * h + 1] * o_cmp[rows, cols] + gates[3 * h + 1:3 * h + 2] * o_slc[rows, cols]
                     + gates[3 * h + 2:3 * h + 3] * o_win[rows, cols])
    for j in range(NSA_HEADS * HEAD_DIM // LANES):
        pair = jnp.concatenate(gated[2 * j:2 * j + 2], axis=0)
        o_ref[0, :, j * LANES:(j + 1) * LANES] = pair.T.astype(BF16)


def nsa_attention(qa_t, kc, vc_t, ks, vs_t, kw, vw_t, gates_t, ovt, tq=LANES):
    b, _, t = qa_t.shape
    n_cmp = kc.shape[1]
    n_sel = ovt.shape[0]
    k_top = min(SEL_TOPK, n_sel)
    tk = NSA_KEYS_PER_QUERY_BLOCK * tq
    tok = lambda n: pl.BlockSpec((1, n, LANES), lambda i, j: (i, 0, 0))
    chn = lambda n: pl.BlockSpec((1, LANES, n), lambda i, j: (i, 0, 0))
    return pl.pallas_call(
        functools.partial(_nsa_body, k_top=k_top),
        grid=(b, t // tq),
        in_specs=[pl.BlockSpec((1, NSA_Q_W, tq), lambda i, j: (i, 0, j)), tok(n_cmp), chn(n_cmp), tok(t), chn(t),
                  tok(t), chn(t), pl.BlockSpec((1, LANES, tq), lambda i, j: (i, 0, j)), _full(ovt.shape)],
        out_specs=pl.BlockSpec((1, tq, NSA_Q_W), lambda i, j: (i, j, 0)),
        out_shape=jax.ShapeDtypeStruct((b, t, NSA_Q_W), BF16),
        scratch_shapes=[pltpu.VMEM((NSA_KV_HEADS, n_sel, tq), F32), pltpu.VMEM((tk, NSA_HEADS * tq), F32),
                        pltpu.VMEM((tk, NSA_HEADS * tq), F32), pltpu.VMEM((tk, NSA_HEADS * tq), BF16),
                        pltpu.VMEM((LANES, NSA_HEADS * tq), F32)],
        compiler_params=_params("parallel", "arbitrary"),
    )(qa_t, kc, vc_t, ks, vs_t, kw, vw_t, gates_t, ovt)


FOX_KEYS_PER_QUERY_BLOCK = 2


def _fox_body(q_ref, k_ref, v_ref, cum_ref, o_ref, ck_ref, s0_ref, s1_ref, p_ref, acc_ref, *, tq):
    t = k_ref.shape[1]
    tk = FOX_KEYS_PER_QUERY_BLOCK * tq
    pair = pl.program_id(1)

    hi, mid, lo = _split3(cum_ref[0])
    pick_row = lax.broadcasted_iota(jnp.int32, (LANES, LANES), 0)
    for h in range(2):
        sel = jnp.where(pick_row == MISC_F + 2 * pair + h, 1.0, 0.0).astype(BF16)
        ck_ref[h] = (_dot(hi, sel) + _dot(mid, sel) + _dot(lo, sel)) * LOG2E

    chan = lax.broadcasted_iota(jnp.int32, (LANES, 1), 0)
    first_head = chan < HEAD_DIM
    krow = lax.broadcasted_iota(jnp.int32, (tk, 1), 0)
    qlane = lax.broadcasted_iota(jnp.int32, (1, tq), 1)
    reps = tq // LANES
    bufs = (s0_ref, s1_ref)
    blocks = [(i, kb) for i in range(t // tq) for kb in range((i * tq) // tk + 1)]
    q_cache = {}

    def q_pair(i):
        if i not in q_cache:
            q = q_ref[0, :, i * tq:(i + 1) * tq]
            q_cache[i] = jnp.concatenate([jnp.where(first_head, q, 0), jnp.where(first_head, 0, q)], axis=1)
        return q_cache[i]

    def put_scores(n):
        i, kb = blocks[n]
        bufs[n % 2][...] = _dot(k_ref[0, kb * tk:(kb + 1) * tk, :], q_pair(i))

    put_scores(0)
    m_i = l_i = None
    for n, (i, kb) in enumerate(blocks):
        if n + 1 < len(blocks):
            put_scores(n + 1)
        last = kb == (i * tq) // tk
        ok = ((kb * tk + krow) <= (i * tq + qlane)) if last else None

        def adjust(s_cols, cg, kb=kb, last=last, ok=ok):
            s_cols = s_cols - ck_ref[cg // reps, kb * tk:(kb + 1) * tk, :]
            return jnp.where(ok[:, (cg % reps) * LANES:(cg % reps + 1) * LANES], s_cols, NEG_INF) if last else s_cols

        if kb == 0:
            m_i = jnp.full((1, 2 * tq), NEG_INF, F32)
            l_i = jnp.zeros((1, 2 * tq), F32)
        v_blk = jnp.concatenate([v_ref[0, :, kb * tk:(kb + 1) * tk], jnp.ones((SUM_ROWS, tk), BF16)], axis=0)
        m_i, l_i = _flash_step(bufs[n % 2], p_ref, acc_ref, v_blk, m_i, l_i, adjust, first=kb == 0)
        if last:
            o = acc_ref[...] * (1.0 / l_i)
            o_ref[0, i * tq:(i + 1) * tq, :] = jnp.where(first_head, o[:, :tq], o[:, tq:]).T.astype(BF16)


def fox_attention(qb_t, kb, vf_t, cum, tq=256):
    b, w, t = qb_t.shape
    pairs = w // LANES
    tk = FOX_KEYS_PER_QUERY_BLOCK * tq
    return pl.pallas_call(
        functools.partial(_fox_body, tq=tq),
        grid=(b, pairs),
        in_specs=[pl.BlockSpec((1, LANES, t), lambda i, p: (i, p, 0)),
                  pl.BlockSpec((1, t, LANES), lambda i, p: (i, 0, p)),
                  pl.BlockSpec((1, LANES, t), lambda i, p: (i, p, 0)),
                  pl.BlockSpec((1, t, LANES), lambda i, p: (i, 0, 0))],
        out_specs=pl.BlockSpec((1, t, LANES), lambda i, p: (i, 0, p)),
        out_shape=jax.ShapeDtypeStruct((b, t, w), BF16),
        scratch_shapes=[pltpu.VMEM((2, t, LANES), F32), pltpu.VMEM((tk, 2 * tq), F32), pltpu.VMEM((tk, 2 * tq), F32),
                        pltpu.VMEM((tk, 2 * tq), BF16), pltpu.VMEM((LANES, 2 * tq), F32)],
        compiler_params=_params("parallel", "arbitrary"),
    )(qb_t, kb, vf_t, cum)


R_GROUP = 0
R_EXPERT = N_GROUPS


def _moe_body(x_ref, *refs, mixer_pending):
    if mixer_pending:
        oa_ref, ob_ref, wo_ref, *refs = refs
    g_ref, wr_hi_ref, wr_lo_ref, br_ref, win_ref, wout_ref, o_ref, h_ref, gate_ref = refs
    e = pl.program_id(1)

    @pl.when(e == 0)
    def _():
        x = x_ref[...]
        if mixer_pending:
            wa = oa_ref.shape[1]
            x = x + _dot(oa_ref[...], wo_ref[:wa, :]) + _dot(ob_ref[...], wo_ref[wa:, :])
        h = x * lax.rsqrt(jnp.mean(x * x, axis=-1, keepdims=True) + NORM_EPS) * g_ref[...]
        h_ref[...] = h.astype(BF16)
        h_hi, h_lo = _split2(h)
        logit = _dot(h_hi, wr_hi_ref[...]) + _dot(h_lo, wr_hi_ref[...]) + _dot(h_hi, wr_lo_ref[...]) + br_ref[...]
        lane_i = lax.broadcasted_iota(jnp.int32, logit.shape, 1)
        lane = lane_i.astype(F32)
        is_g = lane_i < N_GROUPS
        g_max = jnp.max(jnp.where(is_g, logit, -jnp.inf), axis=-1, keepdims=True)
        g_sel = jnp.min(jnp.where(is_g & (logit == g_max), lane, float(LANES)), axis=-1, keepdims=True)
        p_group = 1.0 / jnp.sum(jnp.where(is_g, jnp.exp(logit - g_max), 0.0), axis=-1, keepdims=True)
        group_of = ((lane_i - R_EXPERT) // EXPERTS_PER_GROUP).astype(F32)
        mine = (lane_i >= R_EXPERT) & (lane_i < R_EXPERT + N_EXPERTS) & (group_of == g_sel)
        v1 = jnp.max(jnp.where(mine, logit, -jnp.inf), axis=-1, keepdims=True)
        i1 = jnp.min(jnp.where(mine & (logit == v1), lane, float(LANES)), axis=-1, keepdims=True)
        rest = mine & (lane != i1)
        v2 = jnp.max(jnp.where(rest, logit, -jnp.inf), axis=-1, keepdims=True)
        i2 = jnp.min(jnp.where(rest & (logit == v2), lane, float(LANES)), axis=-1, keepdims=True)
        e2 = jnp.exp(v2 - v1)
        w1 = p_group / (1.0 + e2)
        w2 = p_group * e2 / (1.0 + e2)
        gate_ref[...] = jnp.where(lane == i1, w1, 0.0) + jnp.where(lane == i2, w2, 0.0)
        o_ref[...] = x

    gates = gate_ref[...]
    lane = lax.broadcasted_iota(jnp.int32, gates.shape, 1)
    acts = []
    for j in range(EXPERTS_PER_GROUP):
        gate_e = jnp.sum(jnp.where(lane == R_EXPERT + e * EXPERTS_PER_GROUP + j, gates, 0.0), axis=-1, keepdims=True)
        gu = _dot(h_ref[...], win_ref[j])
        acts.append((_silu(gu[:, :EXPERT_FF]) * gu[:, EXPERT_FF:] * gate_e).astype(BF16))
    o_ref[...] += _dot(jnp.concatenate(acts, axis=1), wout_ref[0])


def moe(x2, gain, wr_hi, wr_lo, br, win_bf, wout_bf, layer, mixer_pending=(), tm=1024):
    n, d = x2.shape
    row = lambda w: pl.BlockSpec((tm, w), lambda i, e: (i, 0))
    win_g = win_bf.reshape(-1, d, 2 * EXPERT_FF)
    wout_g = wout_bf.reshape(-1, EXPERTS_PER_GROUP * EXPERT_FF, d)
    pending_specs = [row(mixer_pending[0].shape[1]), row(mixer_pending[1].shape[1]),
                     _full(mixer_pending[2].shape)] if mixer_pending else []
    return pl.pallas_call(
        functools.partial(_moe_body, mixer_pending=bool(mixer_pending)),
        grid=(n // tm, N_GROUPS),
        in_specs=[row(d)] + pending_specs
                 + [_full((1, d)), _full((d, LANES)), _full((d, LANES)), _full((1, LANES)),
                    pl.BlockSpec((EXPERTS_PER_GROUP, d, 2 * EXPERT_FF), lambda i, e: (layer * N_GROUPS + e, 0, 0)),
                    pl.BlockSpec((1, EXPERTS_PER_GROUP * EXPERT_FF, d), lambda i, e: (layer * N_GROUPS + e, 0, 0))],
        out_specs=row(d),
        out_shape=jax.ShapeDtypeStruct((n, d), F32),
        scratch_shapes=[pltpu.VMEM((tm, d), BF16), pltpu.VMEM((tm, LANES), F32)],
        compiler_params=_params("parallel", "arbitrary"),
    )(x2, *mixer_pending, gain.reshape(1, d), wr_hi, wr_lo, br, win_g, wout_g)


G_CUM, G_BETA, G_LAST = 0, GDN_HEADS, 2 * GDN_HEADS


def _gdn_gates_body(ab_ref, alog_ref, dtb_ref, gb_ref):
    t = ab_ref.shape[1]
    ab = ab_ref[0]
    sp_in = ab + dtb_ref[...]
    softplus = jnp.maximum(sp_in, 0.0) + jnp.log1p(jnp.exp(-jnp.abs(sp_in)))
    lane_row = lax.broadcasted_iota(jnp.int32, (1, LANES), 1)
    g = jnp.where(lane_row < GDN_HEADS, -jnp.exp(alog_ref[...]) * softplus, 0.0)
    blk = 4 * GDN_CHUNK
    r = lax.broadcasted_iota(jnp.int32, (blk, blk), 0)
    c = lax.broadcasted_iota(jnp.int32, (blk, blk), 1)
    same = r // GDN_CHUNK == c // GDN_CHUNK
    tri = jnp.where(same & (r >= c), 1.0, 0.0).astype(BF16)
    tot = jnp.where(same, 1.0, 0.0).astype(BF16)
    lane = lax.broadcasted_iota(jnp.int32, (blk, LANES), 1)
    for s in range(t // blk):
        rs = slice(s * blk, (s + 1) * blk)
        hi, mid, lo = _split3(g[rs])
        gc = _dot(tri, hi) + _dot(tri, mid) + _dot(tri, lo)
        gl = _dot(tot, hi) + _dot(tot, mid) + _dot(tot, lo)
        gl = pltpu.roll(gl, G_LAST, 1)
        gb_ref[0, rs, :] = jnp.where(lane < G_BETA, gc, jnp.where(lane < G_LAST, _sigmoid(ab[rs]), gl))


def gdn_gates(proj, alog_row, dtb_row):
    b, t, _ = proj.shape
    return pl.pallas_call(
        _gdn_gates_body,
        grid=(b,),
        in_specs=[pl.BlockSpec((1, t, LANES), lambda i: (i, 0, C_AB // LANES)), _full((1, LANES)), _full((1, LANES))],
        out_specs=pl.BlockSpec((1, t, LANES), lambda i: (i, 0, 0)),
        out_shape=jax.ShapeDtypeStruct((b, t, LANES), F32),
        compiler_params=_params("parallel"),
    )(proj, alog_row, dtb_row)


GDN_HEADS_PER_STEP = 8
GDN_CHUNKS_PER_TRIP = 4
GDN_SEGMENTS = 4


def _dot3(a, b):
    a_hi, a_lo = _split2(a)
    b_hi, b_lo = _split2(b)
    return _dot(a_hi, b_hi) + _dot(a_hi, b_lo) + _dot(a_lo, b_hi)


def _dot1(a, b):
    return _dot(a.astype(BF16), b.astype(BF16))


CONV_HALO = 8


def _conv_silu(ext_ref, cw_ref, lanes):
    t = ext_ref.shape[0] - CONV_HALO
    y = ext_ref[CONV_HALO:, lanes] * cw_ref[CONV_WIDTH - 1:CONV_WIDTH, lanes]
    for d in range(1, CONV_WIDTH):
        y = y + ext_ref[CONV_HALO - d:CONV_HALO - d + t, lanes] * cw_ref[CONV_WIDTH - 1 - d:CONV_WIDTH - d, lanes]
    return _silu(y)


def _l2norm(y):
    return y * lax.rsqrt(jnp.sum(y * y, axis=-1, keepdims=True) + NORM_EPS)


def _gdn_body(q_ref, k_ref, v_ref, cq_ref, ck_ref, cv_ref, gb_ref, grow_ref, z_ref, x_ref, gain_ref, wout_ref, o_ref,
              ext_ref, state_ref, gl_ref, gc_ref, kb_ref, k_ref_s, kbg_ref, vb_ref, qs_ref, qg_ref, kd_ref,
              u_ref, w_ref, a_ref, mix_ref):
    t = q_ref.shape[1]
    cs = GDN_CHUNK
    dk = GDN_HEAD_DIM
    nh = GDN_HEADS_PER_STEP
    seg = pl.program_id(1)
    gb_hi, gb_mid, gb_lo = _split3(gb_ref[0])
    pick_row = lax.broadcasted_iota(jnp.int32, (LANES, LANES), 0)

    @pl.when(seg == 0)
    def _():
        ext_ref[:, 0:CONV_HALO, :] = jnp.zeros((3, CONV_HALO, ext_ref.shape[2]), F32)
        state_ref[...] = jnp.zeros_like(state_ref)

    @pl.when(seg > 0)
    def _():
        ext_ref[:, 0:CONV_HALO, :] = ext_ref[:, t:t + CONV_HALO, :]

    for i, ref in enumerate((q_ref, k_ref, v_ref)):
        ext_ref[i, CONV_HALO:, :] = ref[0]

    def column(idx):
        sel = jnp.where(pick_row == idx, 1.0, 0.0).astype(BF16)
        return _dot(gb_hi, sel) + _dot(gb_mid, sel) + _dot(gb_lo, sel)

    for s in range(nh):
        lanes = slice(s * dk, (s + 1) * dk)
        gcol = column(G_CUM + s)
        bcol = column(G_BETA + s)
        glast = column(G_LAST + s)
        eg = jnp.exp(gcol)
        k = _l2norm(_conv_silu(ext_ref.at[1], ck_ref, lanes))
        kb = k * bcol
        k_ref_s[s] = k.astype(BF16)
        kb_ref[s] = kb.astype(BF16)
        kbg_ref[s] = (kb * eg).astype(BF16)
        kd_ref[s] = (k * jnp.exp(glast - gcol)).astype(BF16)
        q = _l2norm(_conv_silu(ext_ref.at[0], cq_ref, lanes)) * (dk ** -0.5)
        qs_ref[s] = q.astype(BF16)
        qg_ref[s] = (q * eg).astype(BF16)
        vb_ref[s] = (_conv_silu(ext_ref.at[2], cv_ref, lanes) * bcol).astype(BF16)
        gl_ref[s] = glast
        gc_ref[s] = gcol

    r = lax.broadcasted_iota(jnp.int32, (cs, cs), 0)
    c = lax.broadcasted_iota(jnp.int32, (cs, cs), 1)
    tril = r >= c
    strict = r > c
    eye = jnp.where(r == c, 1.0, 0.0)

    def prep(trip, _):
        probs = [(s, trip * GDN_CHUNKS_PER_TRIP + j) for j in range(GDN_CHUNKS_PER_TRIP) for s in range(nh)]
        rows = [pl.ds(pl.multiple_of(n * cs, cs), cs) for _, n in probs]
        decay, lmat = [], []
        for (s, n), rw in zip(probs, rows):
            gr = grow_ref[0, s, pl.ds(n, 1), :]
            gc = gc_ref[s, rw, :cs]
            decay.append(jnp.where(tril, jnp.exp(jnp.where(tril, gc - gr, 0.0)), 0.0))
        for i, ((s, _), rw) in enumerate(zip(probs, rows)):
            lmat.append(jnp.where(strict, _dot_nt(kb_ref[s, rw, :], k_ref_s[s, rw, :]) * decay[i], 0.0))
        inv = [eye - m for m in lmat]
        pw = [_dot3(m, m) for m in lmat]
        span = 2
        while span < cs:
            mm = _dot3 if span == 2 else _dot1
            inv = [x + mm(x, p) for x, p in zip(inv, pw)]
            span *= 2
            if span < cs:
                pw = [_dot1(p, p) for p in pw]
        inv_bf = [x.astype(BF16) for x in inv]
        for i, ((s, _), rw) in enumerate(zip(probs, rows)):
            u_ref[s, rw, :] = _dot(inv_bf[i], vb_ref[s, rw, :])
            w_ref[s, rw, :] = _dot(inv_bf[i], kbg_ref[s, rw, :]).astype(BF16)
            a_ref[s, rw, :] = jnp.where(tril, _dot_nt(qs_ref[s, rw, :], k_ref_s[s, rw, :]) * decay[i], 0.0).astype(BF16)
        return 0

    lax.fori_loop(0, t // (cs * GDN_CHUNKS_PER_TRIP), prep, 0)

    def scan(n, states):
        r0 = pl.multiple_of(n * cs, cs)
        rows = pl.ds(r0, cs)
        s_bf = [st.astype(BF16) for st in states]
        v_bf = [(u_ref[s, rows, :] - _dot(w_ref[s, rows, :], s_bf[s])).astype(BF16) for s in range(nh)]
        new = [states[s] * jnp.exp(gl_ref[s, pl.ds(r0, 1), :]) + _dot_tn(kd_ref[s, rows, :], v_bf[s])
               for s in range(nh)]
        for s in range(nh):
            mix_ref[rows, s * dk:(s + 1) * dk] = _dot(qg_ref[s, rows, :], s_bf[s]) + _dot(a_ref[s, rows, :], v_bf[s])
        return tuple(new)

    final = lax.fori_loop(0, t // cs, scan, tuple(state_ref[s] for s in range(nh)))
    for s in range(nh):
        state_ref[s] = final[s]

    gain = gain_ref[...]
    parts = []
    for s in range(nh):
        lanes = slice(s * dk, (s + 1) * dk)
        o = mix_ref[:, lanes]
        y = o * lax.rsqrt(jnp.mean(o * o, axis=-1, keepdims=True) + NORM_EPS) * gain
        parts.append((y * _silu(z_ref[0, :, lanes])).astype(BF16))
    o_ref[0] = x_ref[0] + _dot(jnp.concatenate(parts, axis=1), wout_ref[...])


def gdn_mixer(x3, proj, conv_w, gb, gain, wout_bf):
    b, t, d = x3.shape
    ts = t // GDN_SEGMENTS if t % (GDN_SEGMENTS * GDN_CHUNK * GDN_CHUNKS_PER_TRIP) == 0 else t
    seg_chunks = ts // GDN_CHUNK
    nh = GDN_HEADS_PER_STEP
    assert nh == GDN_HEADS, "the fused output projection needs every head of a row in one grid step"
    g_rows = jnp.swapaxes(gb[:, :, G_CUM:G_CUM + GDN_HEADS], 1, 2).reshape(b, GDN_HEADS, t // GDN_CHUNK, GDN_CHUNK)
    sect = lambda k: pl.BlockSpec((1, ts, GDN_WIDTH), lambda i, s: (i, s, k))
    taps = lambda k: pl.BlockSpec((CONV_WIDTH, GDN_WIDTH), lambda i, s: (0, k))
    rows = lambda w: pl.BlockSpec((1, ts, w), lambda i, s: (i, s, 0))
    bf = lambda w: pltpu.VMEM((nh, ts, w), BF16)
    return pl.pallas_call(
        _gdn_body,
        grid=(b, t // ts),
        in_specs=[sect(0), sect(1), sect(2), taps(0), taps(1), taps(2), rows(LANES),
                  pl.BlockSpec((1, nh, seg_chunks, GDN_CHUNK), lambda i, s: (i, 0, s, 0)),
                  sect(3), rows(d), _full((1, GDN_HEAD_DIM)), _full(wout_bf.shape)],
        out_specs=rows(d),
        out_shape=jax.ShapeDtypeStruct((b, t, d), F32),
        scratch_shapes=[pltpu.VMEM((3, CONV_HALO + ts, GDN_WIDTH), F32),
                        pltpu.VMEM((nh, GDN_HEAD_DIM, GDN_HEAD_DIM), F32)]
                       + [pltpu.VMEM((nh, ts, LANES), F32)] * 2 + [bf(GDN_HEAD_DIM)] * 7
                       + [pltpu.VMEM((nh, ts, GDN_HEAD_DIM), F32), bf(GDN_HEAD_DIM), bf(GDN_CHUNK),
                          pltpu.VMEM((ts, GDN_WIDTH), F32)],
        compiler_params=_params("parallel", "arbitrary"),
    )(proj, proj, proj, conv_w, conv_w, conv_w, gb, g_rows, proj, x3, gain, wout_bf)


def _rope_tables(pos):
    half = HEAD_DIM // 2
    inv_freq = ROPE_THETA ** (-jnp.arange(half, dtype=F32) / half)
    ang = pos.astype(F32)[:, None] * inv_freq
    cos = jnp.cos(ang)
    sin = jnp.sin(ang)
    cos_t = jnp.tile(jnp.concatenate([cos, cos], axis=-1), (1, LANES // HEAD_DIM))
    sin_t = jnp.tile(jnp.concatenate([-sin, sin], axis=-1), (1, LANES // HEAD_DIM))
    return cos_t, sin_t


def _block_diag_ones(width, seg):
    idx = np.arange(width) // seg
    return jnp.asarray((idx[:, None] == idx[None, :]).astype(np.float32), dtype=BF16)


def _pad_cols(w, width):
    return jnp.pad(w, ((0, 0), (0, width - w.shape[1])))


def _even_layer(x2, b, t, norm_gain, w_in, b_gate, b_forget, cmp_pe, cmp_w1, cmp_w2, nsa_gain, fox_gain, w_out):
    d = x2.shape[1]
    cos, sin = _rope_tables(jnp.arange(t))
    tile = lambda g, n: jnp.tile(g, n).reshape(1, -1)
    bias = jnp.pad(jnp.concatenate([b_gate, b_forget]), (0, LANES - NSA_GATE_W - FOX_HEADS)).reshape(1, LANES)
    bd = _block_diag_ones(FOX_W, HEAD_DIM)
    (qa, ks, kw, vs, vw, kc_raw, vc_raw, qb, kb, vf, gates, cum) = even_in_prep(
        x2.reshape(b, t, d), norm_gain, w_in, cos, sin, tile(nsa_gain[0], NSA_HEADS), tile(nsa_gain[2], NSA_KV_HEADS), tile(nsa_gain[3], NSA_KV_HEADS),
        tile(fox_gain[0], FOX_HEADS), tile(fox_gain[1], FOX_HEADS), bias, bd)

    n_str = t // CMP_STRIDE
    half = CMP_BLOCK // 2
    eye2 = jnp.eye(NSA_KV_HEADS, dtype=F32)
    pe = jnp.tile(cmp_pe[:, :, None, :], (1, 1, NSA_KV_HEADS, 1)).reshape(2, 2, 1, half * NSA_KV_W)
    w1 = jnp.einsum('ilde,hg->ilhdge', cmp_w1, eye2).reshape(2, 2, half * NSA_KV_W, NSA_KV_W).astype(BF16)
    w2 = jnp.einsum('ide,hg->ihdge', cmp_w2, eye2).reshape(2, NSA_KV_W, NSA_KV_W).astype(BF16)
    cos_c, sin_c = _rope_tables(jnp.arange(n_str) * CMP_STRIDE + (CMP_BLOCK - 1))
    kc, vc = compress(kc_raw, vc_raw, pe, w1, w2, tile(nsa_gain[1], NSA_KV_HEADS), cos_c, sin_c, _block_diag_ones(LANES, HEAD_DIM))

    n_sel = t // SEL_BLOCK
    cs = np.arange(n_str)[:, None] * CMP_STRIDE
    ss = np.arange(n_sel)[None, :] * SEL_BLOCK
    overlap = np.clip(np.minimum(cs + CMP_BLOCK, ss + SEL_BLOCK) - np.maximum(cs, ss), 0, None) / CMP_BLOCK
    overlap[(t - CMP_BLOCK) // CMP_STRIDE + 1:] = 0.0
    ovt = jnp.asarray(overlap.T.astype(np.float32), dtype=BF16)
    o_a = nsa_attention(qa, kc, vc, ks, vs, kw, vw, gates, ovt)
    o_b = fox_attention(qb, kb, vf, cum)

    return o_a.reshape(b * t, NSA_Q_W), o_b.reshape(b * t, FOX_W), w_out.astype(BF16)


def _odd_layer(x2, b, t, norm_gain, w_in, conv_w, a_log, dt_bias, gdn_gain, w_out):
    proj = norm_matmul(x2, norm_gain, w_in).reshape(b, t, ODD_W)
    pad8 = lambda v: jnp.pad(v, (0, LANES - GDN_HEADS)).reshape(1, LANES)
    gb = gdn_gates(proj, pad8(a_log), pad8(dt_bias))
    out = gdn_mixer(x2.reshape(b, t, -1), proj, conv_w, gb, gdn_gain.reshape(1, GDN_HEAD_DIM), w_out.astype(BF16))
    return out.reshape(b * t, -1)


def _moe_layer(x2, gain, w_rg, b_rg, w_re, b_re, w_ein_bf, w_eout_bf, layer, mixer_pending=()):
    d = x2.shape[1]
    wr = _pad_cols(jnp.concatenate([w_rg, w_re], axis=1), LANES)
    wr_hi = wr.astype(BF16)
    wr_lo = (wr - wr_hi.astype(F32)).astype(BF16)
    br = jnp.pad(jnp.concatenate([b_rg, b_re]), (0, LANES - N_GROUPS - N_EXPERTS)).reshape(1, LANES)
    return moe(x2, gain, wr_hi, wr_lo, br, w_ein_bf, w_eout_bf, layer, mixer_pending)


def kernel(x, norm_mix, norm_ffn, w_in_even, b_nsa_gate, b_forget, cmp_pe, cmp_w1, cmp_w2, nsa_qk_gain, fox_qk_gain,
           w_out_even, w_in_odd, conv_w, a_log, dt_bias, gdn_norm_gain, w_out_odd, w_router_group, b_router_group,
           w_router_expert, b_router_expert, w_expert_in, w_expert_out):
    b, t, d = x.shape
    x2 = x.reshape(b * t, d)
    w_ein_bf = w_expert_in.astype(BF16)
    w_eout_bf = w_expert_out.astype(BF16)
    for layer in range(norm_mix.shape[0]):
        i = layer // 2
        mixer_pending = ()
        if layer % 2 == 0:
            mixer_pending = _even_layer(x2, b, t, norm_mix[layer], w_in_even[i], b_nsa_gate[i], b_forget[i], cmp_pe[i],
                                        cmp_w1[i], cmp_w2[i], nsa_qk_gain[i], fox_qk_gain[i], w_out_even[i])
        else:
            x2 = _odd_layer(x2, b, t, norm_mix[layer], w_in_odd[i], conv_w[i], a_log[i], dt_bias[i], gdn_norm_gain[i],
                            w_out_odd[i])
        x2 = _moe_layer(x2, norm_ffn[layer], w_router_group[layer], b_router_group[layer], w_router_expert[layer],
                        b_router_expert[layer], w_ein_bf, w_eout_bf, layer, mixer_pending)
    return x2.reshape(b, t, d)
```

```python
import functools

import numpy as np
import jax
import jax.numpy as jnp
from jax import lax
from jax.experimental import pallas as pl
from jax.experimental.pallas import tpu as pltpu

F32 = jnp.float32
BF16 = jnp.bfloat16

HEAD_DIM = 64
ROPE_THETA = 10000.0
NSA_HEADS = 8
NSA_KV_HEADS = 2
NSA_GROUP = NSA_HEADS // NSA_KV_HEADS
CMP_BLOCK = 32
CMP_STRIDE = 16
SEL_BLOCK = 64
SEL_TOPK = 8
WINDOW = 256
FOX_HEADS = 8
GDN_HEADS = 8
GDN_HEAD_DIM = 128
GDN_WIDTH = GDN_HEADS * GDN_HEAD_DIM
CONV_WIDTH = 4
GDN_CHUNK = 64
N_GROUPS = 4
EXPERTS_PER_GROUP = 4
N_EXPERTS = N_GROUPS * EXPERTS_PER_GROUP
EXPERT_FF = 256
NORM_EPS = 1e-6
NEG_INF = -1e30
FORCE_SCORE = 1e9

LANES = 128
LOG2E = 1.4426950408889634
SUM_ROWS = 16
NSA_Q_W = NSA_HEADS * HEAD_DIM
NSA_KV_W = NSA_KV_HEADS * HEAD_DIM
NSA_GATE_W = 3 * NSA_HEADS
FOX_W = FOX_HEADS * HEAD_DIM
C_QN = 0
C_KC, C_VC, C_KS, C_VS, C_KW, C_VW = (NSA_Q_W + i * NSA_KV_W for i in range(6))
C_QF = NSA_Q_W + 6 * NSA_KV_W
C_KF = C_QF + FOX_W
C_VF = C_KF + FOX_W
C_MISC = C_VF + FOX_W
EVEN_W = C_MISC + LANES
MISC_F = NSA_GATE_W
C_AB = 4 * GDN_WIDTH
ODD_W = C_AB + LANES

VMEM_LIMIT = 56 * 1024 * 1024


def _params(*sem):
    return pltpu.CompilerParams(dimension_semantics=sem, vmem_limit_bytes=VMEM_LIMIT)


def _dot(a, b):
    return jnp.dot(a, b, preferred_element_type=F32)


def _dot_nt(a, b):
    return lax.dot_general(a, b, (((1,), (1,)), ((), ())), preferred_element_type=F32)


def _dot_tn(a, b):
    return lax.dot_general(a, b, (((0,), (0,)), ((), ())), preferred_element_type=F32)


def _split2(x):
    hi = x.astype(BF16)
    return hi, (x - hi.astype(F32)).astype(BF16)


def _split3(x):
    hi = x.astype(BF16)
    r = x - hi.astype(F32)
    mid = r.astype(BF16)
    return hi, mid, (r - mid.astype(F32)).astype(BF16)


def _sigmoid(z):
    return 1.0 / (1.0 + jnp.exp(-z))


def _silu(z):
    return z * _sigmoid(z)


def _full(shape):
    nd = len(shape)
    return pl.BlockSpec(shape, lambda *_: (0,) * nd)


def _norm_matmul_body(x_ref, g_ref, w_ref, o_ref, wbf_ref):
    @pl.when(pl.program_id(0) == 0)
    def _():
        n_in = w_ref.shape[1]
        main = n_in // LANES * LANES
        wbf_ref[:, :main] = w_ref[:, :main].astype(BF16)
        if main < wbf_ref.shape[1]:
            wbf_ref[:, main:] = jnp.zeros((w_ref.shape[0], wbf_ref.shape[1] - main), BF16)
            wbf_ref[:, main:n_in] = w_ref[:, main:n_in].astype(BF16)

    x = x_ref[...]
    ms = jnp.mean(x * x, axis=-1, keepdims=True)
    h = (x * lax.rsqrt(ms + NORM_EPS) * g_ref[...]).astype(BF16)
    o_ref[...] = _dot(h, wbf_ref[...])


def norm_matmul(x2, gain, w, tm=512):
    n, d = x2.shape
    wp = -(-w.shape[1] // LANES) * LANES
    return pl.pallas_call(
        _norm_matmul_body,
        grid=(n // tm,),
        in_specs=[pl.BlockSpec((tm, d), lambda i: (i, 0)), _full((1, d)),
                  pl.BlockSpec(w.shape, lambda i: (0, 0), pipeline_mode=pl.Buffered(1))],
        out_specs=pl.BlockSpec((tm, wp), lambda i: (i, 0)),
        out_shape=jax.ShapeDtypeStruct((n, wp), F32),
        scratch_shapes=[pltpu.VMEM((d, wp), BF16)],
        compiler_params=_params("arbitrary"),
    )(x2, gain.reshape(1, d), w)


def _head_rms(x, bd, gain):
    hi, lo = _split2(x * x)
    ones2 = bd[:LANES, :LANES]
    ssum = jnp.concatenate([_dot(hi[:, c:c + LANES], ones2) + _dot(lo[:, c:c + LANES], ones2)
                            for c in range(0, x.shape[1], LANES)], axis=1)
    return x * lax.rsqrt(ssum * (1.0 / HEAD_DIM) + NORM_EPS) * gain


def _rope(x, cos, sin_signed, first_half):
    fwd = pltpu.roll(x, LANES - HEAD_DIM // 2, 1)
    bwd = pltpu.roll(x, HEAD_DIM // 2, 1)
    return x * cos + jnp.where(first_half, fwd, bwd) * sin_signed


def _even_prep_body(p_ref, cos_ref, sin_ref, gq_ref, gks_ref, gkw_ref, gfq_ref, gfk_ref, bias_ref, bd_ref,
                    qa_ref, ks_ref, kw_ref, vs_ref, vw_ref, kc_ref, vc_ref, qb_ref, kb_ref, vf_ref,
                    gate_ref, cum_ref, carry_ref, stage_ref):
    tr = p_ref.shape[1]
    bd = bd_ref[...]
    cos = cos_ref[...]
    sin = sin_ref[...]
    lane = lax.broadcasted_iota(jnp.int32, (1, LANES), 1)
    first_half = (lane % HEAD_DIM) < (HEAD_DIM // 2)
    scale = HEAD_DIM ** -0.5 * LOG2E

    qn = _head_rms(p_ref[0, :, C_QN:C_QN + NSA_Q_W], bd, gq_ref[...])
    for c in range(NSA_Q_W // LANES):
        sl = slice(c * LANES, (c + 1) * LANES)
        qa_ref[0, sl, :] = (_rope(qn[:, sl], cos, sin, first_half) * scale).T.astype(BF16)
    ks = _head_rms(p_ref[0, :, C_KS:C_KS + NSA_KV_W], bd, gks_ref[...])
    ks_ref[0] = _rope(ks, cos, sin, first_half).astype(BF16)
    kw = _head_rms(p_ref[0, :, C_KW:C_KW + NSA_KV_W], bd, gkw_ref[...])
    kw_ref[0] = _rope(kw, cos, sin, first_half).astype(BF16)
    vs_ref[0] = p_ref[0, :, C_VS:C_VS + NSA_KV_W].T.astype(BF16)
    vw_ref[0] = p_ref[0, :, C_VW:C_VW + NSA_KV_W].T.astype(BF16)
    stage_ref[0] = p_ref[0, :, C_KC:C_KC + NSA_KV_W]
    stage_ref[1] = p_ref[0, :, C_VC:C_VC + NSA_KV_W]
    for l in range(CMP_STRIDE):
        rows = pl.ds(l, tr // CMP_STRIDE, stride=CMP_STRIDE)
        kc_ref[0, :, l * NSA_KV_W:(l + 1) * NSA_KV_W] = stage_ref[0, rows, :]
        vc_ref[0, :, l * NSA_KV_W:(l + 1) * NSA_KV_W] = stage_ref[1, rows, :]

    qb = _head_rms(p_ref[0, :, C_QF:C_QF + FOX_W], bd, gfq_ref[...]) * scale
    kb_ref[0] = _head_rms(p_ref[0, :, C_KF:C_KF + FOX_W], bd, gfk_ref[...]).astype(BF16)
    for c in range(FOX_W // LANES):
        sl = slice(c * LANES, (c + 1) * LANES)
        qb_ref[0, sl, :] = qb[:, sl].T.astype(BF16)
        vf_ref[0, sl, :] = p_ref[0, :, C_VF + c * LANES:C_VF + (c + 1) * LANES].T.astype(BF16)

    z = p_ref[0, :, C_MISC:C_MISC + LANES] + bias_ref[...]
    gate_ref[0] = _sigmoid(z).T
    logf = jnp.minimum(z, 0.0) - jnp.log1p(jnp.exp(-jnp.abs(z)))

    @pl.when(pl.program_id(1) == 0)
    def _():
        carry_ref[...] = jnp.zeros_like(carry_ref)

    row = lax.broadcasted_iota(jnp.int32, (tr, tr), 0)
    col = lax.broadcasted_iota(jnp.int32, (tr, tr), 1)
    tril = jnp.where(row >= col, 1.0, 0.0).astype(BF16)
    hi, mid, lo = _split3(logf)
    cum = _dot(tril, hi) + _dot(tril, mid) + _dot(tril, lo) + carry_ref[...]
    cum_ref[0] = cum
    carry_ref[...] = cum[tr - 1:tr, :]


def _even_in_body(x_ref, g_ref, w_ref, *rest):
    proj_ref, wbf_ref = rest[-2:]

    @pl.when((pl.program_id(0) == 0) & (pl.program_id(1) == 0))
    def _():
        o_fox = C_QF + NSA_GATE_W
        n_fox = 3 * FOX_W
        wbf_ref[:, :C_QF] = w_ref[:, :C_QF].astype(BF16)
        wbf_ref[:, C_QF:C_MISC] = w_ref[:, o_fox:o_fox + n_fox].astype(BF16)
        wbf_ref[:, C_MISC:] = jnp.zeros((w_ref.shape[0], LANES), BF16)
        wbf_ref[:, C_MISC:C_MISC + NSA_GATE_W] = w_ref[:, C_QF:o_fox].astype(BF16)
        wbf_ref[:, C_MISC + NSA_GATE_W:C_MISC + NSA_GATE_W + FOX_HEADS] = w_ref[:, o_fox + n_fox:].astype(BF16)

    x = x_ref[0]
    h = (x * lax.rsqrt(jnp.mean(x * x, axis=-1, keepdims=True) + NORM_EPS) * g_ref[...]).astype(BF16)
    proj_ref[0] = _dot(h, wbf_ref[...])
    _even_prep_body(proj_ref, *rest[:-2])


def even_in_prep(x3, gain, w, cos, sin, gq, gks, gkw, gfq, gfk, bias, bd, tr=512):
    b, t, d = x3.shape
    row = lambda w: pl.BlockSpec((1, tr, w), lambda i, j: (i, j, 0))
    tab = pl.BlockSpec((tr, LANES), lambda i, j: (j, 0))
    shp = lambda w, dt: jax.ShapeDtypeStruct((b, t, w), dt)
    col = lambda w: pl.BlockSpec((1, w, tr), lambda i, j: (i, 0, j))
    shp_t = lambda w, dt: jax.ShapeDtypeStruct((b, w, t), dt)
    strd = pl.BlockSpec((1, tr // CMP_STRIDE, CMP_STRIDE * NSA_KV_W), lambda i, j: (i, j, 0))
    strd_shape = jax.ShapeDtypeStruct((b, t // CMP_STRIDE, CMP_STRIDE * NSA_KV_W), F32)
    return pl.pallas_call(
        _even_in_body,
        grid=(b, t // tr),
        in_specs=[row(d), _full((1, d)), pl.BlockSpec(w.shape, lambda i, j: (0, 0), pipeline_mode=pl.Buffered(1)),
                  tab, tab, _full((1, NSA_Q_W)), _full((1, LANES)), _full((1, LANES)),
                  _full((1, FOX_W)), _full((1, FOX_W)), _full((1, LANES)), _full((FOX_W, FOX_W))],
        out_specs=[col(NSA_Q_W), row(LANES), row(LANES), col(LANES), col(LANES), strd, strd,
                   col(FOX_W), row(FOX_W), col(FOX_W), col(LANES), row(LANES)],
        out_shape=[shp_t(NSA_Q_W, BF16), shp(LANES, BF16), shp(LANES, BF16), shp_t(LANES, BF16), shp_t(LANES, BF16),
                   strd_shape, strd_shape, shp_t(FOX_W, BF16), shp(FOX_W, BF16), shp_t(FOX_W, BF16),
                   shp_t(LANES, F32), shp(LANES, F32)],
        scratch_shapes=[pltpu.VMEM((1, LANES), F32), pltpu.VMEM((2, tr, LANES), F32), pltpu.VMEM((1, tr, EVEN_W), F32),
                        pltpu.VMEM((d, EVEN_W), BF16)],
        compiler_params=_params("arbitrary", "arbitrary"),
    )(x3, gain.reshape(1, d), w, cos, sin, gq, gks, gkw, gfq, gfk, bias, bd)


def _gelu_tanh(x):
    return 0.5 * x * (1.0 + jnp.tanh(np.sqrt(2.0 / np.pi).astype(np.float32) * (x + 0.044715 * (x * x * x))))


def _compress_body(xk_ref, xv_ref, pe_ref, w1_ref, w2_ref, gk_ref, cos_ref, sin_ref, bd_ref, kc_ref, vc_ref):
    n = xk_ref.shape[1]
    lane = lax.broadcasted_iota(jnp.int32, (1, LANES), 1)
    first_half = (lane % HEAD_DIM) < (HEAD_DIM // 2)

    def mlp(x_ref, i):
        x = x_ref[0]
        nxt = pltpu.roll(x, n - 1, 0)
        xa = (x + pe_ref[i, 0]).astype(BF16)
        xb = (nxt + pe_ref[i, 1]).astype(BF16)
        h = _dot(xa, w1_ref[i, 0]) + _dot(xb, w1_ref[i, 1])
        return _dot(_gelu_tanh(h).astype(BF16), w2_ref[i])

    kc = _head_rms(mlp(xk_ref, 0), bd_ref[...], gk_ref[...])
    kc_ref[0] = _rope(kc, cos_ref[...], sin_ref[...], first_half).astype(BF16)
    vc_ref[0] = mlp(xv_ref, 1).T.astype(BF16)


def compress(xk, xv, pe, w1, w2, gk, cos_c, sin_c, bd):
    b, n, w = xk.shape
    blk = pl.BlockSpec((1, n, w), lambda i: (i, 0, 0))
    out = pl.BlockSpec((1, n, LANES), lambda i: (i, 0, 0))
    return pl.pallas_call(
        _compress_body,
        grid=(b,),
        in_specs=[blk, blk, _full(pe.shape), _full(w1.shape), _full(w2.shape), _full((1, LANES)),
                  _full((n, LANES)), _full((n, LANES)), _full((LANES, LANES))],
        out_specs=[out, pl.BlockSpec((1, LANES, n), lambda i: (i, 0, 0))],
        out_shape=[jax.ShapeDtypeStruct((b, n, LANES), BF16), jax.ShapeDtypeStruct((b, LANES, n), BF16)],
        compiler_params=_params("parallel"),
    )(xk, xv, pe, w1, w2, gk, cos_c, sin_c, bd)


def _flash_step(s_ref, p_ref, acc_ref, v_blk, m_i, l_i, adjust, first=False):
    n_ch = acc_ref.shape[0]
    al, ms = [], []
    for cg in range(s_ref.shape[1] // LANES):
        sl = slice(cg * LANES, (cg + 1) * LANES)
        s = adjust(s_ref[:, sl], cg)
        m_new = jnp.maximum(m_i[:, sl], jnp.max(s, axis=0, keepdims=True))
        p_ref[:, sl] = jnp.exp2(s - m_new).astype(BF16)
        al.append(jnp.exp2(m_i[:, sl] - m_new))
        ms.append(m_new)
    cat = lambda xs: jnp.concatenate(xs, axis=1)
    alpha = cat(al)
    pv = _dot(v_blk, p_ref[...])
    acc_ref[...] = pv[:n_ch] if first else alpha * acc_ref[...] + pv[:n_ch]
    return cat(ms), alpha * l_i + pv[n_ch:n_ch + 1]


NSA_KEYS_PER_QUERY_BLOCK = 2


def _nsa_body(q_ref, kc_ref, vc_ref, ks_ref, vs_ref, kw_ref, vw_ref, gate_ref, ovt_ref, o_ref,
              sel_ref, s0_ref, s1_ref, p_ref, acc_ref, *, k_top):
    tq = q_ref.shape[2]
    t_all = ks_ref.shape[1]
    n_cmp = kc_ref.shape[1]
    n_sel = ovt_ref.shape[0]
    g_n = NSA_GROUP
    c = pl.program_id(1)
    t0 = c * tq
    chan = lax.broadcasted_iota(jnp.int32, (LANES, 1), 0)
    tlane = t0 + lax.broadcasted_iota(jnp.int32, (1, tq), 1)
    gates = gate_ref[0]

    nrow = lax.broadcasted_iota(jnp.int32, (n_cmp, 1), 0)
    valid_c = (nrow * CMP_STRIDE + (CMP_BLOCK - 1)) <= tlane
    jrow = lax.broadcasted_iota(jnp.int32, (n_sel, tq), 0)
    jrow_f = jrow.astype(F32)
    cur = tlane // SEL_BLOCK
    forced = (jrow == 0) | (jrow == cur) | (jrow == cur - 1)
    future = jrow * SEL_BLOCK > tlane
    tk = NSA_KEYS_PER_QUERY_BLOCK * tq
    krow = lax.broadcasted_iota(jnp.int32, (tk, 1), 0)
    per_blk = tk // SEL_BLOCK
    w_len = tq + WINDOW
    w_start = pl.multiple_of(jnp.clip(t0 - WINDOW, 0, t_all - w_len), LANES)
    wrow = w_start + lax.broadcasted_iota(jnp.int32, (w_len, 1), 0)
    valid_w = (wrow <= tlane) & (wrow > tlane - WINDOW)

    heads = [(kvh, g) for kvh in range(NSA_KV_HEADS) for g in range(g_n)]
    zero_half = jnp.zeros((HEAD_DIM, tq), BF16)

    def on_kv_rows(h, kvh):
        blk = q_ref[0, h * HEAD_DIM:(h + 1) * HEAD_DIM, :]
        return jnp.concatenate([blk, zero_half] if kvh == 0 else [zero_half, blk], axis=0)

    qst = jnp.concatenate([on_kv_rows(h, kvh) for h, (kvh, _) in enumerate(heads)], axis=1)
    n_col = len(heads) * tq

    def softmax_cols(s, ok, guard):
        outs = []
        for cg in range(len(heads)):
            sc = jnp.where(ok, s[:, cg * tq:(cg + 1) * tq], NEG_INF)
            e = jnp.exp2(sc - jnp.max(sc, axis=0, keepdims=True))
            if guard:
                e = jnp.where(ok, e, 0.0)
            den = jnp.sum(e, axis=0, keepdims=True)
            outs.append(e * (1.0 / (jnp.where(den > 0.0, den, 1.0) if guard else den)))
        return outs

    p_c = softmax_cols(_dot(kc_ref[0], qst), valid_c, guard=True)
    o_cmp = _dot(vc_ref[0], jnp.concatenate(p_c, axis=1).astype(BF16))

    for kvh in range(NSA_KV_HEADS):
        p_sum = p_c[kvh * g_n]
        for g in range(1, g_n):
            p_sum = p_sum + p_c[kvh * g_n + g]
        p_hi, p_lo = _split2(p_sum)
        imp_t = _dot(ovt_ref[...], p_hi) + _dot(ovt_ref[...], p_lo)
        val = jnp.where(forced, FORCE_SCORE, jnp.where(future, NEG_INF, imp_t))
        sel_t = jnp.zeros((n_sel, tq), F32)
        for _ in range(k_top):
            m = jnp.max(val, axis=0, keepdims=True)
            first = jnp.min(jnp.where(val == m, jrow_f, float(n_sel)), axis=0, keepdims=True)
            pick = jrow_f == first
            sel_t = jnp.where(pick, 1.0, sel_t)
            val = jnp.where(pick, -jnp.inf, val)
        sel_ref[kvh] = sel_t

    p_w = softmax_cols(_dot(kw_ref[0, pl.ds(w_start, w_len), :], qst), valid_w, guard=False)
    o_win = _dot(vw_ref[0, :, pl.ds(w_start, w_len)], jnp.concatenate(p_w, axis=1).astype(BF16))

    def put_scores(buf, kb):
        k0 = pl.multiple_of(jnp.minimum(kb * tk, t_all - tk), tk)
        buf[...] = _dot(ks_ref[0, pl.ds(k0, tk), :], qst)

    def half_step(buf, kb, m_i, l_i):
        k0 = pl.multiple_of(jnp.minimum(kb * tk, t_all - tk), tk)
        causal = (kb * tk + krow) <= tlane
        ok = [causal & (jnp.concatenate([jnp.broadcast_to(sel_ref[kvh, pl.ds(k0 // SEL_BLOCK + r, 1), :],
                                                          (SEL_BLOCK, tq)) for r in range(per_blk)], axis=0) > 0.5)
              for kvh in range(NSA_KV_HEADS)]
        adjust = lambda s_cols, cg: jnp.where(ok[cg // g_n], s_cols, NEG_INF)
        v_blk = jnp.concatenate([vs_ref[0, :, pl.ds(k0, tk)], jnp.ones((SUM_ROWS, tk), BF16)], axis=0)
        return _flash_step(buf, p_ref, acc_ref, v_blk, m_i, l_i, adjust)

    def sel_trip(j, carry):
        put_scores(s1_ref, 2 * j + 1)
        carry = half_step(s0_ref, 2 * j, *carry)
        put_scores(s0_ref, 2 * j + 2)
        return half_step(s1_ref, 2 * j + 1, *carry)

    put_scores(s0_ref, 0)
    acc_ref[...] = jnp.zeros_like(acc_ref)
    init = (jnp.full((1, n_col), NEG_INF, F32), jnp.zeros((1, n_col), F32))
    n_blocks = (t0 + tq + tk - 1) // tk
    pairs = n_blocks // 2
    carry = lax.fori_loop(0, pairs, sel_trip, init)
    _, l_s = lax.cond(n_blocks % 2 == 1, lambda m, l: half_step(s0_ref, 2 * pairs, m, l), lambda m, l: (m, l), *carry)
    o_slc = acc_ref[...] * (1.0 / l_s)

    gated = []
    for h, (kvh, _) in enumerate(heads):
        cols = slice(h * tq, (h + 1) * tq)
        rows = slice(kvh * HEAD_DIM, (kvh + 1) * HEAD_DIM)
        gated.append(gates[3 * h:3 * h + 1] * o_cmp[rows, cols] + gates[3 * h + 1:3 * h + 2] * o_slc[rows, cols]
                     + gates[3 * h + 2:3 * h + 3] * o_win[rows, cols])
    for j in range(NSA_HEADS * HEAD_DIM // LANES):
        pair = jnp.concatenate(gated[2 * j:2 * j + 2], axis=0)
        o_ref[0, :, j * LANES:(j + 1) * LANES] = pair.T.astype(BF16)


def nsa_attention(qa_t, kc, vc_t, ks, vs_t, kw, vw_t, gates_t, ovt, tq=LANES):
    b, _, t = qa_t.shape
    n_cmp = kc.shape[1]
    n_sel = ovt.shape[0]
    k_top = min(SEL_TOPK, n_sel)
    tk = NSA_KEYS_PER_QUERY_BLOCK * tq
    tok = lambda n: pl.BlockSpec((1, n, LANES), lambda i, j: (i, 0, 0))
    chn = lambda n: pl.BlockSpec((1, LANES, n), lambda i, j: (i, 0, 0))
    return pl.pallas_call(
        functools.partial(_nsa_body, k_top=k_top),
        grid=(b, t // tq),
        in_specs=[pl.BlockSpec((1, NSA_Q_W, tq), lambda i, j: (i, 0, j)), tok(n_cmp), chn(n_cmp), tok(t), chn(t),
                  tok(t), chn(t), pl.BlockSpec((1, LANES, tq), lambda i, j: (i, 0, j)), _full(ovt.shape)],
        out_specs=pl.BlockSpec((1, tq, NSA_Q_W), lambda i, j: (i, j, 0)),
        out_shape=jax.ShapeDtypeStruct((b, t, NSA_Q_W), BF16),
        scratch_shapes=[pltpu.VMEM((NSA_KV_HEADS, n_sel, tq), F32), pltpu.VMEM((tk, NSA_HEADS * tq), F32),
                        pltpu.VMEM((tk, NSA_HEADS * tq), F32), pltpu.VMEM((tk, NSA_HEADS * tq), BF16),
                        pltpu.VMEM((LANES, NSA_HEADS * tq), F32)],
        compiler_params=_params("parallel", "arbitrary"),
    )(qa_t, kc, vc_t, ks, vs_t, kw, vw_t, gates_t, ovt)


FOX_KEYS_PER_QUERY_BLOCK = 2


def _fox_body(q_ref, k_ref, v_ref, cum_ref, o_ref, ck_ref, s0_ref, s1_ref, p_ref, acc_ref, *, tq):
    t = k_ref.shape[1]
    tk = FOX_KEYS_PER_QUERY_BLOCK * tq
    pair = pl.program_id(1)

    hi, mid, lo = _split3(cum_ref[0])
    pick_row = lax.broadcasted_iota(jnp.int32, (LANES, LANES), 0)
    for h in range(2):
        sel = jnp.where(pick_row == MISC_F + 2 * pair + h, 1.0, 0.0).astype(BF16)
        ck_ref[h] = (_dot(hi, sel) + _dot(mid, sel) + _dot(lo, sel)) * LOG2E

    chan = lax.broadcasted_iota(jnp.int32, (LANES, 1), 0)
    first_head = chan < HEAD_DIM
    krow = lax.broadcasted_iota(jnp.int32, (tk, 1), 0)
    qlane = lax.broadcasted_iota(jnp.int32, (1, tq), 1)
    reps = tq // LANES
    bufs = (s0_ref, s1_ref)
    blocks = [(i, kb) for i in range(t // tq) for kb in range((i * tq) // tk + 1)]
    q_cache = {}

    def q_pair(i):
        if i not in q_cache:
            q = q_ref[0, :, i * tq:(i + 1) * tq]
            q_cache[i] = jnp.concatenate([jnp.where(first_head, q, 0), jnp.where(first_head, 0, q)], axis=1)
        return q_cache[i]

    def put_scores(n):
        i, kb = blocks[n]
        bufs[n % 2][...] = _dot(k_ref[0, kb * tk:(kb + 1) * tk, :], q_pair(i))

    put_scores(0)
    m_i = l_i = None
    for n, (i, kb) in enumerate(blocks):
        if n + 1 < len(blocks):
            put_scores(n + 1)
        last = kb == (i * tq) // tk
        ok = ((kb * tk + krow) <= (i * tq + qlane)) if last else None

        def adjust(s_cols, cg, kb=kb, last=last, ok=ok):
            s_cols = s_cols - ck_ref[cg // reps, kb * tk:(kb + 1) * tk, :]
            return jnp.where(ok[:, (cg % reps) * LANES:(cg % reps + 1) * LANES], s_cols, NEG_INF) if last else s_cols

        if kb == 0:
            m_i = jnp.full((1, 2 * tq), NEG_INF, F32)
            l_i = jnp.zeros((1, 2 * tq), F32)
        v_blk = jnp.concatenate([v_ref[0, :, kb * tk:(kb + 1) * tk], jnp.ones((SUM_ROWS, tk), BF16)], axis=0)
        m_i, l_i = _flash_step(bufs[n % 2], p_ref, acc_ref, v_blk, m_i, l_i, adjust, first=kb == 0)
        if last:
            o = acc_ref[...] * (1.0 / l_i)
            o_ref[0, i * tq:(i + 1) * tq, :] = jnp.where(first_head, o[:, :tq], o[:, tq:]).T.astype(BF16)


def fox_attention(qb_t, kb, vf_t, cum, tq=256):
    b, w, t = qb_t.shape
    pairs = w // LANES
    tk = FOX_KEYS_PER_QUERY_BLOCK * tq
    return pl.pallas_call(
        functools.partial(_fox_body, tq=tq),
        grid=(b, pairs),
        in_specs=[pl.BlockSpec((1, LANES, t), lambda i, p: (i, p, 0)),
                  pl.BlockSpec((1, t, LANES), lambda i, p: (i, 0, p)),
                  pl.BlockSpec((1, LANES, t), lambda i, p: (i, p, 0)),
                  pl.BlockSpec((1, t, LANES), lambda i, p: (i, 0, 0))],
        out_specs=pl.BlockSpec((1, t, LANES), lambda i, p: (i, 0, p)),
        out_shape=jax.ShapeDtypeStruct((b, t, w), BF16),
        scratch_shapes=[pltpu.VMEM((2, t, LANES), F32), pltpu.VMEM((tk, 2 * tq), F32), pltpu.VMEM((tk, 2 * tq), F32),
                        pltpu.VMEM((tk, 2 * tq), BF16), pltpu.VMEM((LANES, 2 * tq), F32)],
        compiler_params=_params("parallel", "arbitrary"),
    )(qb_t, kb, vf_t, cum)


R_GROUP = 0
R_EXPERT = N_GROUPS


def _moe_body(x_ref, *refs, mixer_pending):
    if mixer_pending:
        oa_ref, ob_ref, wo_ref, *refs = refs
    g_ref, wr_hi_ref, wr_lo_ref, br_ref, win_ref, wout_ref, o_ref, h_ref, gate_ref = refs
    e = pl.program_id(1)

    @pl.when(e == 0)
    def _():
        x = x_ref[...]
        if mixer_pending:
            wa = oa_ref.shape[1]
            x = x + _dot(oa_ref[...], wo_ref[:wa, :]) + _dot(ob_ref[...], wo_ref[wa:, :])
        h = x * lax.rsqrt(jnp.mean(x * x, axis=-1, keepdims=True) + NORM_EPS) * g_ref[...]
        h_ref[...] = h.astype(BF16)
        h_hi, h_lo = _split2(h)
        logit = _dot(h_hi, wr_hi_ref[...]) + _dot(h_lo, wr_hi_ref[...]) + _dot(h_hi, wr_lo_ref[...]) + br_ref[...]
        lane_i = lax.broadcasted_iota(jnp.int32, logit.shape, 1)
        lane = lane_i.astype(F32)
        is_g = lane_i < N_GROUPS
        g_max = jnp.max(jnp.where(is_g, logit, -jnp.inf), axis=-1, keepdims=True)
        g_sel = jnp.min(jnp.where(is_g & (logit == g_max), lane, float(LANES)), axis=-1, keepdims=True)
        p_group = 1.0 / jnp.sum(jnp.where(is_g, jnp.exp(logit - g_max), 0.0), axis=-1, keepdims=True)
        group_of = ((lane_i - R_EXPERT) // EXPERTS_PER_GROUP).astype(F32)
        mine = (lane_i >= R_EXPERT) & (lane_i < R_EXPERT + N_EXPERTS) & (group_of == g_sel)
        v1 = jnp.max(jnp.where(mine, logit, -jnp.inf), axis=-1, keepdims=True)
        i1 = jnp.min(jnp.where(mine & (logit == v1), lane, float(LANES)), axis=-1, keepdims=True)
        rest = mine & (lane != i1)
        v2 = jnp.max(jnp.where(rest, logit, -jnp.inf), axis=-1, keepdims=True)
        i2 = jnp.min(jnp.where(rest & (logit == v2), lane, float(LANES)), axis=-1, keepdims=True)
        e2 = jnp.exp(v2 - v1)
        w1 = p_group / (1.0 + e2)
        w2 = p_group * e2 / (1.0 + e2)
        gate_ref[...] = jnp.where(lane == i1, w1, 0.0) + jnp.where(lane == i2, w2, 0.0)
        o_ref[...] = x

    gates = gate_ref[...]
    lane = lax.broadcasted_iota(jnp.int32, gates.shape, 1)
    acts = []
    for j in range(EXPERTS_PER_GROUP):
        gate_e = jnp.sum(jnp.where(lane == R_EXPERT + e * EXPERTS_PER_GROUP + j, gates, 0.0), axis=-1, keepdims=True)
        gu = _dot(h_ref[...], win_ref[j])
        acts.append((_silu(gu[:, :EXPERT_FF]) * gu[:, EXPERT_FF:] * gate_e).astype(BF16))
    o_ref[...] += _dot(jnp.concatenate(acts, axis=1), wout_ref[0])


def moe(x2, gain, wr_hi, wr_lo, br, win_bf, wout_bf, layer, mixer_pending=(), tm=1024):
    n, d = x2.shape
    row = lambda w: pl.BlockSpec((tm, w), lambda i, e: (i, 0))
    win_g = win_bf.reshape(-1, d, 2 * EXPERT_FF)
    wout_g = wout_bf.reshape(-1, EXPERTS_PER_GROUP * EXPERT_FF, d)
    pending_specs = [row(mixer_pending[0].shape[1]), row(mixer_pending[1].shape[1]),
                     _full(mixer_pending[2].shape)] if mixer_pending else []
    return pl.pallas_call(
        functools.partial(_moe_body, mixer_pending=bool(mixer_pending)),
        grid=(n // tm, N_GROUPS),
        in_specs=[row(d)] + pending_specs
                 + [_full((1, d)), _full((d, LANES)), _full((d, LANES)), _full((1, LANES)),
                    pl.BlockSpec((EXPERTS_PER_GROUP, d, 2 * EXPERT_FF), lambda i, e: (layer * N_GROUPS + e, 0, 0)),
                    pl.BlockSpec((1, EXPERTS_PER_GROUP * EXPERT_FF, d), lambda i, e: (layer * N_GROUPS + e, 0, 0))],
        out_specs=row(d),
        out_shape=jax.ShapeDtypeStruct((n, d), F32),
        scratch_shapes=[pltpu.VMEM((tm, d), BF16), pltpu.VMEM((tm, LANES), F32)],
        compiler_params=_params("parallel", "arbitrary"),
    )(x2, *mixer_pending, gain.reshape(1, d), wr_hi, wr_lo, br, win_g, wout_g)


G_CUM, G_BETA, G_LAST = 0, GDN_HEADS, 2 * GDN_HEADS


def _gdn_gates_body(ab_ref, alog_ref, dtb_ref, gb_ref):
    t = ab_ref.shape[1]
    ab = ab_ref[0]
    sp_in = ab + dtb_ref[...]
    softplus = jnp.maximum(sp_in, 0.0) + jnp.log1p(jnp.exp(-jnp.abs(sp_in)))
    lane_row = lax.broadcasted_iota(jnp.int32, (1, LANES), 1)
    g = jnp.where(lane_row < GDN_HEADS, -jnp.exp(alog_ref[...]) * softplus, 0.0)
    blk = 4 * GDN_CHUNK
    r = lax.broadcasted_iota(jnp.int32, (blk, blk), 0)
    c = lax.broadcasted_iota(jnp.int32, (blk, blk), 1)
    same = r // GDN_CHUNK == c // GDN_CHUNK
    tri = jnp.where(same & (r >= c), 1.0, 0.0).astype(BF16)
    tot = jnp.where(same, 1.0, 0.0).astype(BF16)
    lane = lax.broadcasted_iota(jnp.int32, (blk, LANES), 1)
    for s in range(t // blk):
        rs = slice(s * blk, (s + 1) * blk)
        hi, mid, lo = _split3(g[rs])
        gc = _dot(tri, hi) + _dot(tri, mid) + _dot(tri, lo)
        gl = _dot(tot, hi) + _dot(tot, mid) + _dot(tot, lo)
        gl = pltpu.roll(gl, G_LAST, 1)
        gb_ref[0, rs, :] = jnp.where(lane < G_BETA, gc, jnp.where(lane < G_LAST, _sigmoid(ab[rs]), gl))


def gdn_gates(proj, alog_row, dtb_row):
    b, t, _ = proj.shape
    return pl.pallas_call(
        _gdn_gates_body,
        grid=(b,),
        in_specs=[pl.BlockSpec((1, t, LANES), lambda i: (i, 0, C_AB // LANES)), _full((1, LANES)), _full((1, LANES))],
        out_specs=pl.BlockSpec((1, t, LANES), lambda i: (i, 0, 0)),
        out_shape=jax.ShapeDtypeStruct((b, t, LANES), F32),
        compiler_params=_params("parallel"),
    )(proj, alog_row, dtb_row)


GDN_HEADS_PER_STEP = 8
GDN_CHUNKS_PER_TRIP = 4
GDN_SEGMENTS = 4


def _dot3(a, b):
    a_hi, a_lo = _split2(a)
    b_hi, b_lo = _split2(b)
    return _dot(a_hi, b_hi) + _dot(a_hi, b_lo) + _dot(a_lo, b_hi)


def _dot1(a, b):
    return _dot(a.astype(BF16), b.astype(BF16))


CONV_HALO = 8


def _conv_silu(ext_ref, cw_ref, lanes):
    t = ext_ref.shape[0] - CONV_HALO
    y = ext_ref[CONV_HALO:, lanes] * cw_ref[CONV_WIDTH - 1:CONV_WIDTH, lanes]
    for d in range(1, CONV_WIDTH):
        y = y + ext_ref[CONV_HALO - d:CONV_HALO - d + t, lanes] * cw_ref[CONV_WIDTH - 1 - d:CONV_WIDTH - d, lanes]
    return _silu(y)


def _l2norm(y):
    return y * lax.rsqrt(jnp.sum(y * y, axis=-1, keepdims=True) + NORM_EPS)


def _gdn_body(q_ref, k_ref, v_ref, cq_ref, ck_ref, cv_ref, gb_ref, grow_ref, z_ref, x_ref, gain_ref, wout_ref, o_ref,
              ext_ref, state_ref, gl_ref, gc_ref, kb_ref, k_ref_s, kbg_ref, vb_ref, qs_ref, qg_ref, kd_ref,
              u_ref, w_ref, a_ref, mix_ref):
    t = q_ref.shape[1]
    cs = GDN_CHUNK
    dk = GDN_HEAD_DIM
    nh = GDN_HEADS_PER_STEP
    seg = pl.program_id(1)
    gb_hi, gb_mid, gb_lo = _split3(gb_ref[0])
    pick_row = lax.broadcasted_iota(jnp.int32, (LANES, LANES), 0)

    @pl.when(seg == 0)
    def _():
        ext_ref[:, 0:CONV_HALO, :] = jnp.zeros((3, CONV_HALO, ext_ref.shape[2]), F32)
        state_ref[...] = jnp.zeros_like(state_ref)

    @pl.when(seg > 0)
    def _():
        ext_ref[:, 0:CONV_HALO, :] = ext_ref[:, t:t + CONV_HALO, :]

    for i, ref in enumerate((q_ref, k_ref, v_ref)):
        ext_ref[i, CONV_HALO:, :] = ref[0]

    def column(idx):
        sel = jnp.where(pick_row == idx, 1.0, 0.0).astype(BF16)
        return _dot(gb_hi, sel) + _dot(gb_mid, sel) + _dot(gb_lo, sel)

    for s in range(nh):
        lanes = slice(s * dk, (s + 1) * dk)
        gcol = column(G_CUM + s)
        bcol = column(G_BETA + s)
        glast = column(G_LAST + s)
        eg = jnp.exp(gcol)
        k = _l2norm(_conv_silu(ext_ref.at[1], ck_ref, lanes))
        kb = k * bcol
        k_ref_s[s] = k.astype(BF16)
        kb_ref[s] = kb.astype(BF16)
        kbg_ref[s] = (kb * eg).astype(BF16)
        kd_ref[s] = (k * jnp.exp(glast - gcol)).astype(BF16)
        q = _l2norm(_conv_silu(ext_ref.at[0], cq_ref, lanes)) * (dk ** -0.5)
        qs_ref[s] = q.astype(BF16)
        qg_ref[s] = (q * eg).astype(BF16)
        vb_ref[s] = (_conv_silu(ext_ref.at[2], cv_ref, lanes) * bcol).astype(BF16)
        gl_ref[s] = glast
        gc_ref[s] = gcol

    r = lax.broadcasted_iota(jnp.int32, (cs, cs), 0)
    c = lax.broadcasted_iota(jnp.int32, (cs, cs), 1)
    tril = r >= c
    strict = r > c
    eye = jnp.where(r == c, 1.0, 0.0)

    def prep(trip, _):
        probs = [(s, trip * GDN_CHUNKS_PER_TRIP + j) for j in range(GDN_CHUNKS_PER_TRIP) for s in range(nh)]
        rows = [pl.ds(pl.multiple_of(n * cs, cs), cs) for _, n in probs]
        decay, lmat = [], []
        for (s, n), rw in zip(probs, rows):
            gr = grow_ref[0, s, pl.ds(n, 1), :]
            gc = gc_ref[s, rw, :cs]
            decay.append(jnp.where(tril, jnp.exp(jnp.where(tril, gc - gr, 0.0)), 0.0))
        for i, ((s, _), rw) in enumerate(zip(probs, rows)):
            lmat.append(jnp.where(strict, _dot_nt(kb_ref[s, rw, :], k_ref_s[s, rw, :]) * decay[i], 0.0))
        inv = [eye - m for m in lmat]
        pw = [_dot3(m, m) for m in lmat]
        span = 2
        while span < cs:
            mm = _dot3 if span == 2 else _dot1
            inv = [x + mm(x, p) for x, p in zip(inv, pw)]
            span *= 2
            if span < cs:
                pw = [_dot1(p, p) for p in pw]
        inv_bf = [x.astype(BF16) for x in inv]
        for i, ((s, _), rw) in enumerate(zip(probs, rows)):
            u_ref[s, rw, :] = _dot(inv_bf[i], vb_ref[s, rw, :])
            w_ref[s, rw, :] = _dot(inv_bf[i], kbg_ref[s, rw, :]).astype(BF16)
            a_ref[s, rw, :] = jnp.where(tril, _dot_nt(qs_ref[s, rw, :], k_ref_s[s, rw, :]) * decay[i], 0.0).astype(BF16)
        return 0

    lax.fori_loop(0, t // (cs * GDN_CHUNKS_PER_TRIP), prep, 0)

    def scan(n, states):
        r0 = pl.multiple_of(n * cs, cs)
        rows = pl.ds(r0, cs)
        s_bf = [st.astype(BF16) for st in states]
        v_bf = [(u_ref[s, rows, :] - _dot(w_ref[s, rows, :], s_bf[s])).astype(BF16) for s in range(nh)]
        new = [states[s] * jnp.exp(gl_ref[s, pl.ds(r0, 1), :]) + _dot_tn(kd_ref[s, rows, :], v_bf[s])
               for s in range(nh)]
        for s in range(nh):
            mix_ref[rows, s * dk:(s + 1) * dk] = _dot(qg_ref[s, rows, :], s_bf[s]) + _dot(a_ref[s, rows, :], v_bf[s])
        return tuple(new)

    final = lax.fori_loop(0, t // cs, scan, tuple(state_ref[s] for s in range(nh)))
    for s in range(nh):
        state_ref[s] = final[s]

    gain = gain_ref[...]
    parts = []
    for s in range(nh):
        lanes = slice(s * dk, (s + 1) * dk)
        o = mix_ref[:, lanes]
        y = o * lax.rsqrt(jnp.mean(o * o, axis=-1, keepdims=True) + NORM_EPS) * gain
        parts.append((y * _silu(z_ref[0, :, lanes])).astype(BF16))
    o_ref[0] = x_ref[0] + _dot(jnp.concatenate(parts, axis=1), wout_ref[...])


def gdn_mixer(x3, proj, conv_w, gb, gain, wout_bf):
    b, t, d = x3.shape
    ts = t // GDN_SEGMENTS if t % (GDN_SEGMENTS * GDN_CHUNK * GDN_CHUNKS_PER_TRIP) == 0 else t
    seg_chunks = ts // GDN_CHUNK
    nh = GDN_HEADS_PER_STEP
    assert nh == GDN_HEADS, "the fused output projection needs every head of a row in one grid step"
    g_rows = jnp.swapaxes(gb[:, :, G_CUM:G_CUM + GDN_HEADS], 1, 2).reshape(b, GDN_HEADS, t // GDN_CHUNK, GDN_CHUNK)
    sect = lambda k: pl.BlockSpec((1, ts, GDN_WIDTH), lambda i, s: (i, s, k))
    taps = lambda k: pl.BlockSpec((CONV_WIDTH, GDN_WIDTH), lambda i, s: (0, k))
    rows = lambda w: pl.BlockSpec((1, ts, w), lambda i, s: (i, s, 0))
    bf = lambda w: pltpu.VMEM((nh, ts, w), BF16)
    return pl.pallas_call(
        _gdn_body,
        grid=(b, t // ts),
        in_specs=[sect(0), sect(1), sect(2), taps(0), taps(1), taps(2), rows(LANES),
                  pl.BlockSpec((1, nh, seg_chunks, GDN_CHUNK), lambda i, s: (i, 0, s, 0)),
                  sect(3), rows(d), _full((1, GDN_HEAD_DIM)), _full(wout_bf.shape)],
        out_specs=rows(d),
        out_shape=jax.ShapeDtypeStruct((b, t, d), F32),
        scratch_shapes=[pltpu.VMEM((3, CONV_HALO + ts, GDN_WIDTH), F32),
                        pltpu.VMEM((nh, GDN_HEAD_DIM, GDN_HEAD_DIM), F32)]
                       + [pltpu.VMEM((nh, ts, LANES), F32)] * 2 + [bf(GDN_HEAD_DIM)] * 7
                       + [pltpu.VMEM((nh, ts, GDN_HEAD_DIM), F32), bf(GDN_HEAD_DIM), bf(GDN_CHUNK),
                          pltpu.VMEM((ts, GDN_WIDTH), F32)],
        compiler_params=_params("parallel", "arbitrary"),
    )(proj, proj, proj, conv_w, conv_w, conv_w, gb, g_rows, proj, x3, gain, wout_bf)


def _rope_tables(pos):
    half = HEAD_DIM // 2
    inv_freq = ROPE_THETA ** (-jnp.arange(half, dtype=F32) / half)
    ang = pos.astype(F32)[:, None] * inv_freq
    cos = jnp.cos(ang)
    sin = jnp.sin(ang)
    cos_t = jnp.tile(jnp.concatenate([cos, cos], axis=-1), (1, LANES // HEAD_DIM))
    sin_t = jnp.tile(jnp.concatenate([-sin, sin], axis=-1), (1, LANES // HEAD_DIM))
    return cos_t, sin_t


def _block_diag_ones(width, seg):
    idx = np.arange(width) // seg
    return jnp.asarray((idx[:, None] == idx[None, :]).astype(np.float32), dtype=BF16)


def _pad_cols(w, width):
    return jnp.pad(w, ((0, 0), (0, width - w.shape[1])))


def _even_layer(x2, b, t, norm_gain, w_in, b_gate, b_forget, cmp_pe, cmp_w1, cmp_w2, nsa_gain, fox_gain, w_out):
    d = x2.shape[1]
    cos, sin = _rope_tables(jnp.arange(t))
    tile = lambda g, n: jnp.tile(g, n).reshape(1, -1)
    bias = jnp.pad(jnp.concatenate([b_gate, b_forget]), (0, LANES - NSA_GATE_W - FOX_HEADS)).reshape(1, LANES)
    bd = _block_diag_ones(FOX_W, HEAD_DIM)
    (qa, ks, kw, vs, vw, kc_raw, vc_raw, qb, kb, vf, gates, cum) = even_in_prep(
        x2.reshape(b, t, d), norm_gain, w_in, cos, sin, tile(nsa_gain[0], NSA_HEADS), tile(nsa_gain[2], NSA_KV_HEADS), tile(nsa_gain[3], NSA_KV_HEADS),
        tile(fox_gain[0], FOX_HEADS), tile(fox_gain[1], FOX_HEADS), bias, bd)

    n_str = t // CMP_STRIDE
    half = CMP_BLOCK // 2
    eye2 = jnp.eye(NSA_KV_HEADS, dtype=F32)
    pe = jnp.tile(cmp_pe[:, :, None, :], (1, 1, NSA_KV_HEADS, 1)).reshape(2, 2, 1, half * NSA_KV_W)
    w1 = jnp.einsum('ilde,hg->ilhdge', cmp_w1, eye2).reshape(2, 2, half * NSA_KV_W, NSA_KV_W).astype(BF16)
    w2 = jnp.einsum('ide,hg->ihdge', cmp_w2, eye2).reshape(2, NSA_KV_W, NSA_KV_W).astype(BF16)
    cos_c, sin_c = _rope_tables(jnp.arange(n_str) * CMP_STRIDE + (CMP_BLOCK - 1))
    kc, vc = compress(kc_raw, vc_raw, pe, w1, w2, tile(nsa_gain[1], NSA_KV_HEADS), cos_c, sin_c, _block_diag_ones(LANES, HEAD_DIM))

    n_sel = t // SEL_BLOCK
    cs = np.arange(n_str)[:, None] * CMP_STRIDE
    ss = np.arange(n_sel)[None, :] * SEL_BLOCK
    overlap = np.clip(np.minimum(cs + CMP_BLOCK, ss + SEL_BLOCK) - np.maximum(cs, ss), 0, None) / CMP_BLOCK
    overlap[(t - CMP_BLOCK) // CMP_STRIDE + 1:] = 0.0
    ovt = jnp.asarray(overlap.T.astype(np.float32), dtype=BF16)
    o_a = nsa_attention(qa, kc, vc, ks, vs, kw, vw, gates, ovt)
    o_b = fox_attention(qb, kb, vf, cum)

    return o_a.reshape(b * t, NSA_Q_W), o_b.reshape(b * t, FOX_W), w_out.astype(BF16)


def _odd_layer(x2, b, t, norm_gain, w_in, conv_w, a_log, dt_bias, gdn_gain, w_out):
    proj = norm_matmul(x2, norm_gain, w_in).reshape(b, t, ODD_W)
    pad8 = lambda v: jnp.pad(v, (0, LANES - GDN_HEADS)).reshape(1, LANES)
    gb = gdn_gates(proj, pad8(a_log), pad8(dt_bias))
    out = gdn_mixer(x2.reshape(b, t, -1), proj, conv_w, gb, gdn_gain.reshape(1, GDN_HEAD_DIM), w_out.astype(BF16))
    return out.reshape(b * t, -1)


def _moe_layer(x2, gain, w_rg, b_rg, w_re, b_re, w_ein_bf, w_eout_bf, layer, mixer_pending=()):
    d = x2.shape[1]
    wr = _pad_cols(jnp.concatenate([w_rg, w_re], axis=1), LANES)
    wr_hi = wr.astype(BF16)
    wr_lo = (wr - wr_hi.astype(F32)).astype(BF16)
    br = jnp.pad(jnp.concatenate([b_rg, b_re]), (0, LANES - N_GROUPS - N_EXPERTS)).reshape(1, LANES)
    return moe(x2, gain, wr_hi, wr_lo, br, w_ein_bf, w_eout_bf, layer, mixer_pending)


def kernel(x, norm_mix, norm_ffn, w_in_even, b_nsa_gate, b_forget, cmp_pe, cmp_w1, cmp_w2, nsa_qk_gain, fox_qk_gain,
           w_out_even, w_in_odd, conv_w, a_log, dt_bias, gdn_norm_gain, w_out_odd, w_router_group, b_router_group,
           w_router_expert, b_router_expert, w_expert_in, w_expert_out):
    b, t, d = x.shape
    x2 = x.reshape(b * t, d)
    w_ein_bf = w_expert_in.astype(BF16)
    w_eout_bf = w_expert_out.astype(BF16)
    for layer in range(norm_mix.shape[0]):
        i = layer // 2
        mixer_pending = ()
        if layer % 2 == 0:
            mixer_pending = _even_layer(x2, b, t, norm_mix[layer], w_in_even[i], b_nsa_gate[i], b_forget[i], cmp_pe[i],
                                        cmp_w1[i], cmp_w2[i], nsa_qk_gain[i], fox_qk_gain[i], w_out_even[i])
        else:
            x2 = _odd_layer(x2, b, t, norm_mix[layer], w_in_odd[i], conv_w[i], a_log[i], dt_bias[i], gdn_norm_gain[i],
                            w_out_odd[i])
        x2 = _moe_layer(x2, norm_ffn[layer], w_router_group[layer], b_router_group[layer], w_router_expert[layer],
                        b_router_expert[layer], w_ein_bf, w_eout_bf, layer, mixer_pending)
    return x2.reshape(b, t, d)
```

```python
import functools

import numpy as np
import jax
import jax.numpy as jnp
from jax import lax
from jax.experimental import pallas as pl
from jax.experimental.pallas import tpu as pltpu

F32 = jnp.float32
BF16 = jnp.bfloat16

HEAD_DIM = 64
ROPE_THETA = 10000.0
NSA_HEADS = 8
NSA_KV_HEADS = 2
NSA_GROUP = NSA_HEADS // NSA_KV_HEADS
CMP_BLOCK = 32
CMP_STRIDE = 16
SEL_BLOCK = 64
SEL_TOPK = 8
WINDOW = 256
FOX_HEADS = 8
GDN_HEADS = 8
GDN_HEAD_DIM = 128
GDN_WIDTH = GDN_HEADS * GDN_HEAD_DIM
CONV_WIDTH = 4
GDN_CHUNK = 64
N_GROUPS = 4
EXPERTS_PER_GROUP = 4
N_EXPERTS = N_GROUPS * EXPERTS_PER_GROUP
EXPERT_FF = 256
NORM_EPS = 1e-6
NEG_INF = -1e30
FORCE_SCORE = 1e9

LANES = 128
LOG2E = 1.4426950408889634
SUM_ROWS = 16
NSA_Q_W = NSA_HEADS * HEAD_DIM
NSA_KV_W = NSA_KV_HEADS * HEAD_DIM
NSA_GATE_W = 3 * NSA_HEADS
FOX_W = FOX_HEADS * HEAD_DIM
C_QN = 0
C_KC, C_VC, C_KS, C_VS, C_KW, C_VW = (NSA_Q_W + i * NSA_KV_W for i in range(6))
C_QF = NSA_Q_W + 6 * NSA_KV_W
C_KF = C_QF + FOX_W
C_VF = C_KF + FOX_W
C_MISC = C_VF + FOX_W
EVEN_W = C_MISC + LANES
MISC_F = NSA_GATE_W
C_AB = 4 * GDN_WIDTH
ODD_W = C_AB + LANES

VMEM_LIMIT = 56 * 1024 * 1024


def _params(*sem):
    return pltpu.CompilerParams(dimension_semantics=sem, vmem_limit_bytes=VMEM_LIMIT)


def _dot(a, b):
    return jnp.dot(a, b, preferred_element_type=F32)


def _dot_nt(a, b):
    return lax.dot_general(a, b, (((1,), (1,)), ((), ())), preferred_element_type=F32)


def _dot_tn(a, b):
    return lax.dot_general(a, b, (((0,), (0,)), ((), ())), preferred_element_type=F32)


def _split2(x):
    hi = x.astype(BF16)
    return hi, (x - hi.astype(F32)).astype(BF16)


def _split3(x):
    hi = x.astype(BF16)
    r = x - hi.astype(F32)
    mid = r.astype(BF16)
    return hi, mid, (r - mid.astype(F32)).astype(BF16)


def _sigmoid(z):
    return 1.0 / (1.0 + jnp.exp(-z))


def _silu(z):
    return z * _sigmoid(z)


def _full(shape):
    nd = len(shape)
    return pl.BlockSpec(shape, lambda *_: (0,) * nd)


def _norm_matmul_body(x_ref, g_ref, w_ref, o_ref, wbf_ref):
    @pl.when(pl.program_id(0) == 0)
    def _():
        n_in = w_ref.shape[1]
        main = n_in // LANES * LANES
        wbf_ref[:, :main] = w_ref[:, :main].astype(BF16)
        if main < wbf_ref.shape[1]:
            wbf_ref[:, main:] = jnp.zeros((w_ref.shape[0], wbf_ref.shape[1] - main), BF16)
            wbf_ref[:, main:n_in] = w_ref[:, main:n_in].astype(BF16)

    x = x_ref[...]
    ms = jnp.mean(x * x, axis=-1, keepdims=True)
    h = (x * lax.rsqrt(ms + NORM_EPS) * g_ref[...]).astype(BF16)
    o_ref[...] = _dot(h, wbf_ref[...])


def norm_matmul(x2, gain, w, tm=512):
    n, d = x2.shape
    wp = -(-w.shape[1] // LANES) * LANES
    return pl.pallas_call(
        _norm_matmul_body,
        grid=(n // tm,),
        in_specs=[pl.BlockSpec((tm, d), lambda i: (i, 0)), _full((1, d)),
                  pl.BlockSpec(w.shape, lambda i: (0, 0), pipeline_mode=pl.Buffered(1))],
        out_specs=pl.BlockSpec((tm, wp), lambda i: (i, 0)),
        out_shape=jax.ShapeDtypeStruct((n, wp), F32),
        scratch_shapes=[pltpu.VMEM((d, wp), BF16)],
        compiler_params=_params("arbitrary"),
    )(x2, gain.reshape(1, d), w)


def _head_rms(x, bd, gain):
    hi, lo = _split2(x * x)
    ones2 = bd[:LANES, :LANES]
    ssum = jnp.concatenate([_dot(hi[:, c:c + LANES], ones2) + _dot(lo[:, c:c + LANES], ones2)
                            for c in range(0, x.shape[1], LANES)], axis=1)
    return x * lax.rsqrt(ssum * (1.0 / HEAD_DIM) + NORM_EPS) * gain


def _rope(x, cos, sin_signed, first_half):
    fwd = pltpu.roll(x, LANES - HEAD_DIM // 2, 1)
    bwd = pltpu.roll(x, HEAD_DIM // 2, 1)
    return x * cos + jnp.where(first_half, fwd, bwd) * sin_signed


def _even_prep_body(p_ref, cos_ref, sin_ref, gq_ref, gks_ref, gkw_ref, gfq_ref, gfk_ref, bias_ref, bd_ref,
                    qa_ref, ks_ref, kw_ref, vs_ref, vw_ref, kc_ref, vc_ref, qb_ref, kb_ref, vf_ref,
                    gate_ref, cum_ref, carry_ref, stage_ref):
    tr = p_ref.shape[1]
    bd = bd_ref[...]
    cos = cos_ref[...]
    sin = sin_ref[...]
    lane = lax.broadcasted_iota(jnp.int32, (1, LANES), 1)
    first_half = (lane % HEAD_DIM) < (HEAD_DIM // 2)
    scale = HEAD_DIM ** -0.5 * LOG2E

    qn = _head_rms(p_ref[0, :, C_QN:C_QN + NSA_Q_W], bd, gq_ref[...])
    for c in range(NSA_Q_W // LANES):
        sl = slice(c * LANES, (c + 1) * LANES)
        qa_ref[0, sl, :] = (_rope(qn[:, sl], cos, sin, first_half) * scale).T.astype(BF16)
    ks = _head_rms(p_ref[0, :, C_KS:C_KS + NSA_KV_W], bd, gks_ref[...])
    ks_ref[0] = _rope(ks, cos, sin, first_half).astype(BF16)
    kw = _head_rms(p_ref[0, :, C_KW:C_KW + NSA_KV_W], bd, gkw_ref[...])
    kw_ref[0] = _rope(kw, cos, sin, first_half).astype(BF16)
    vs_ref[0] = p_ref[0, :, C_VS:C_VS + NSA_KV_W].T.astype(BF16)
    vw_ref[0] = p_ref[0, :, C_VW:C_VW + NSA_KV_W].T.astype(BF16)
    stage_ref[0] = p_ref[0, :, C_KC:C_KC + NSA_KV_W]
    stage_ref[1] = p_ref[0, :, C_VC:C_VC + NSA_KV_W]
    for l in range(CMP_STRIDE):
        rows = pl.ds(l, tr // CMP_STRIDE, stride=CMP_STRIDE)
        kc_ref[0, :, l * NSA_KV_W:(l + 1) * NSA_KV_W] = stage_ref[0, rows, :]
        vc_ref[0, :, l * NSA_KV_W:(l + 1) * NSA_KV_W] = stage_ref[1, rows, :]

    qb = _head_rms(p_ref[0, :, C_QF:C_QF + FOX_W], bd, gfq_ref[...]) * scale
    kb_ref[0] = _head_rms(p_ref[0, :, C_KF:C_KF + FOX_W], bd, gfk_ref[...]).astype(BF16)
    for c in range(FOX_W // LANES):
        sl = slice(c * LANES, (c + 1) * LANES)
        qb_ref[0, sl, :] = qb[:, sl].T.astype(BF16)
        vf_ref[0, sl, :] = p_ref[0, :, C_VF + c * LANES:C_VF + (c + 1) * LANES].T.astype(BF16)

    z = p_ref[0, :, C_MISC:C_MISC + LANES] + bias_ref[...]
    gate_ref[0] = _sigmoid(z).T
    logf = jnp.minimum(z, 0.0) - jnp.log1p(jnp.exp(-jnp.abs(z)))

    @pl.when(pl.program_id(1) == 0)
    def _():
        carry_ref[...] = jnp.zeros_like(carry_ref)

    row = lax.broadcasted_iota(jnp.int32, (tr, tr), 0)
    col = lax.broadcasted_iota(jnp.int32, (tr, tr), 1)
    tril = jnp.where(row >= col, 1.0, 0.0).astype(BF16)
    hi, mid, lo = _split3(logf)
    cum = _dot(tril, hi) + _dot(tril, mid) + _dot(tril, lo) + carry_ref[...]
    cum_ref[0] = cum
    carry_ref[...] = cum[tr - 1:tr, :]


def _even_in_body(x_ref, g_ref, w_ref, *rest):
    proj_ref, wbf_ref = rest[-2:]

    @pl.when((pl.program_id(0) == 0) & (pl.program_id(1) == 0))
    def _():
        o_fox = C_QF + NSA_GATE_W
        n_fox = 3 * FOX_W
        wbf_ref[:, :C_QF] = w_ref[:, :C_QF].astype(BF16)
        wbf_ref[:, C_QF:C_MISC] = w_ref[:, o_fox:o_fox + n_fox].astype(BF16)
        wbf_ref[:, C_MISC:] = jnp.zeros((w_ref.shape[0], LANES), BF16)
        wbf_ref[:, C_MISC:C_MISC + NSA_GATE_W] = w_ref[:, C_QF:o_fox].astype(BF16)
        wbf_ref[:, C_MISC + NSA_GATE_W:C_MISC + NSA_GATE_W + FOX_HEADS] = w_ref[:, o_fox + n_fox:].astype(BF16)

    x = x_ref[0]
    h = (x * lax.rsqrt(jnp.mean(x * x, axis=-1, keepdims=True) + NORM_EPS) * g_ref[...]).astype(BF16)
    proj_ref[0] = _dot(h, wbf_ref[...])
    _even_prep_body(proj_ref, *rest[:-2])


def even_in_prep(x3, gain, w, cos, sin, gq, gks, gkw, gfq, gfk, bias, bd, tr=512):
    b, t, d = x3.shape
    row = lambda w: pl.BlockSpec((1, tr, w), lambda i, j: (i, j, 0))
    tab = pl.BlockSpec((tr, LANES), lambda i, j: (j, 0))
    shp = lambda w, dt: jax.ShapeDtypeStruct((b, t, w), dt)
    col = lambda w: pl.BlockSpec((1, w, tr), lambda i, j: (i, 0, j))
    shp_t = lambda w, dt: jax.ShapeDtypeStruct((b, w, t), dt)
    strd = pl.BlockSpec((1, tr // CMP_STRIDE, CMP_STRIDE * NSA_KV_W), lambda i, j: (i, j, 0))
    strd_shape = jax.ShapeDtypeStruct((b, t // CMP_STRIDE, CMP_STRIDE * NSA_KV_W), F32)
    return pl.pallas_call(
        _even_in_body,
        grid=(b, t // tr),
        in_specs=[row(d), _full((1, d)), pl.BlockSpec(w.shape, lambda i, j: (0, 0), pipeline_mode=pl.Buffered(1)),
                  tab, tab, _full((1, NSA_Q_W)), _full((1, LANES)), _full((1, LANES)),
                  _full((1, FOX_W)), _full((1, FOX_W)), _full((1, LANES)), _full((FOX_W, FOX_W))],
        out_specs=[col(NSA_Q_W), row(LANES), row(LANES), col(LANES), col(LANES), strd, strd,
                   col(FOX_W), row(FOX_W), col(FOX_W), col(LANES), row(LANES)],
        out_shape=[shp_t(NSA_Q_W, BF16), shp(LANES, BF16), shp(LANES, BF16), shp_t(LANES, BF16), shp_t(LANES, BF16),
                   strd_shape, strd_shape, shp_t(FOX_W, BF16), shp(FOX_W, BF16), shp_t(FOX_W, BF16),
                   shp_t(LANES, F32), shp(LANES, F32)],
        scratch_shapes=[pltpu.VMEM((1, LANES), F32), pltpu.VMEM((2, tr, LANES), F32), pltpu.VMEM((1, tr, EVEN_W), F32),
                        pltpu.VMEM((d, EVEN_W), BF16)],
        compiler_params=_params("arbitrary", "arbitrary"),
    )(x3, gain.reshape(1, d), w, cos, sin, gq, gks, gkw, gfq, gfk, bias, bd)


def _gelu_tanh(x):
    return 0.5 * x * (1.0 + jnp.tanh(np.sqrt(2.0 / np.pi).astype(np.float32) * (x + 0.044715 * (x * x * x))))


def _compress_body(xk_ref, xv_ref, pe_ref, w1_ref, w2_ref, gk_ref, cos_ref, sin_ref, bd_ref, kc_ref, vc_ref):
    n = xk_ref.shape[1]
    lane = lax.broadcasted_iota(jnp.int32, (1, LANES), 1)
    first_half = (lane % HEAD_DIM) < (HEAD_DIM // 2)

    def mlp(x_ref, i):
        x = x_ref[0]
        nxt = pltpu.roll(x, n - 1, 0)
        xa = (x + pe_ref[i, 0]).astype(BF16)
        xb = (nxt + pe_ref[i, 1]).astype(BF16)
        h = _dot(xa, w1_ref[i, 0]) + _dot(xb, w1_ref[i, 1])
        return _dot(_gelu_tanh(h).astype(BF16), w2_ref[i])

    kc = _head_rms(mlp(xk_ref, 0), bd_ref[...], gk_ref[...])
    kc_ref[0] = _rope(kc, cos_ref[...], sin_ref[...], first_half).astype(BF16)
    vc_ref[0] = mlp(xv_ref, 1).T.astype(BF16)


def compress(xk, xv, pe, w1, w2, gk, cos_c, sin_c, bd):
    b, n, w = xk.shape
    blk = pl.BlockSpec((1, n, w), lambda i: (i, 0, 0))
    out = pl.BlockSpec((1, n, LANES), lambda i: (i, 0, 0))
    return pl.pallas_call(
        _compress_body,
        grid=(b,),
        in_specs=[blk, blk, _full(pe.shape), _full(w1.shape), _full(w2.shape), _full((1, LANES)),
                  _full((n, LANES)), _full((n, LANES)), _full((LANES, LANES))],
        out_specs=[out, pl.BlockSpec((1, LANES, n), lambda i: (i, 0, 0))],
        out_shape=[jax.ShapeDtypeStruct((b, n, LANES), BF16), jax.ShapeDtypeStruct((b, LANES, n), BF16)],
        compiler_params=_params("parallel"),
    )(xk, xv, pe, w1, w2, gk, cos_c, sin_c, bd)


def _flash_step(s_ref, p_ref, acc_ref, v_blk, m_i, l_i, adjust, first=False):
    n_ch = acc_ref.shape[0]
    al, ms = [], []
    for cg in range(s_ref.shape[1] // LANES):
        sl = slice(cg * LANES, (cg + 1) * LANES)
        s = adjust(s_ref[:, sl], cg)
        m_new = jnp.maximum(m_i[:, sl], jnp.max(s, axis=0, keepdims=True))
        p_ref[:, sl] = jnp.exp2(s - m_new).astype(BF16)
        al.append(jnp.exp2(m_i[:, sl] - m_new))
        ms.append(m_new)
    cat = lambda xs: jnp.concatenate(xs, axis=1)
    alpha = cat(al)
    pv = _dot(v_blk, p_ref[...])
    acc_ref[...] = pv[:n_ch] if first else alpha * acc_ref[...] + pv[:n_ch]
    return cat(ms), alpha * l_i + pv[n_ch:n_ch + 1]


NSA_KEYS_PER_QUERY_BLOCK = 2


def _nsa_body(q_ref, kc_ref, vc_ref, ks_ref, vs_ref, kw_ref, vw_ref, gate_ref, ovt_ref, o_ref,
              sel_ref, s0_ref, s1_ref, p_ref, acc_ref, *, k_top):
    tq = q_ref.shape[2]
    t_all = ks_ref.shape[1]
    n_cmp = kc_ref.shape[1]
    n_sel = ovt_ref.shape[0]
    g_n = NSA_GROUP
    c = pl.program_id(1)
    t0 = c * tq
    chan = lax.broadcasted_iota(jnp.int32, (LANES, 1), 0)
    tlane = t0 + lax.broadcasted_iota(jnp.int32, (1, tq), 1)
    gates = gate_ref[0]

    nrow = lax.broadcasted_iota(jnp.int32, (n_cmp, 1), 0)
    valid_c = (nrow * CMP_STRIDE + (CMP_BLOCK - 1)) <= tlane
    jrow = lax.broadcasted_iota(jnp.int32, (n_sel, tq), 0)
    jrow_f = jrow.astype(F32)
    cur = tlane // SEL_BLOCK
    forced = (jrow == 0) | (jrow == cur) | (jrow == cur - 1)
    future = jrow * SEL_BLOCK > tlane
    tk = NSA_KEYS_PER_QUERY_BLOCK * tq
    krow = lax.broadcasted_iota(jnp.int32, (tk, 1), 0)
    per_blk = tk // SEL_BLOCK
    w_len = tq + WINDOW
    w_start = pl.multiple_of(jnp.clip(t0 - WINDOW, 0, t_all - w_len), LANES)
    wrow = w_start + lax.broadcasted_iota(jnp.int32, (w_len, 1), 0)
    valid_w = (wrow <= tlane) & (wrow > tlane - WINDOW)

    heads = [(kvh, g) for kvh in range(NSA_KV_HEADS) for g in range(g_n)]
    zero_half = jnp.zeros((HEAD_DIM, tq), BF16)

    def on_kv_rows(h, kvh):
        blk = q_ref[0, h * HEAD_DIM:(h + 1) * HEAD_DIM, :]
        return jnp.concatenate([blk, zero_half] if kvh == 0 else [zero_half, blk], axis=0)

    qst = jnp.concatenate([on_kv_rows(h, kvh) for h, (kvh, _) in enumerate(heads)], axis=1)
    n_col = len(heads) * tq

    def softmax_cols(s, ok, guard):
        outs = []
        for cg in range(len(heads)):
            sc = jnp.where(ok, s[:, cg * tq:(cg + 1) * tq], NEG_INF)
            e = jnp.exp2(sc - jnp.max(sc, axis=0, keepdims=True))
            if guard:
                e = jnp.where(ok, e, 0.0)
            den = jnp.sum(e, axis=0, keepdims=True)
            outs.append(e * (1.0 / (jnp.where(den > 0.0, den, 1.0) if guard else den)))
        return outs

    p_c = softmax_cols(_dot(kc_ref[0], qst), valid_c, guard=True)
    o_cmp = _dot(vc_ref[0], jnp.concatenate(p_c, axis=1).astype(BF16))

    for kvh in range(NSA_KV_HEADS):
        p_sum = p_c[kvh * g_n]
        for g in range(1, g_n):
            p_sum = p_sum + p_c[kvh * g_n + g]
        p_hi, p_lo = _split2(p_sum)
        imp_t = _dot(ovt_ref[...], p_hi) + _dot(ovt_ref[...], p_lo)
        val = jnp.where(forced, FORCE_SCORE, jnp.where(future, NEG_INF, imp_t))
        sel_t = jnp.zeros((n_sel, tq), F32)
        for _ in range(k_top):
            m = jnp.max(val, axis=0, keepdims=True)
            first = jnp.min(jnp.where(val == m, jrow_f, float(n_sel)), axis=0, keepdims=True)
            pick = jrow_f == first
            sel_t = jnp.where(pick, 1.0, sel_t)
            val = jnp.where(pick, -jnp.inf, val)
        sel_ref[kvh] = sel_t

    p_w = softmax_cols(_dot(kw_ref[0, pl.ds(w_start, w_len), :], qst), valid_w, guard=False)
    o_win = _dot(vw_ref[0, :, pl.ds(w_start, w_len)], jnp.concatenate(p_w, axis=1).astype(BF16))

    def put_scores(buf, kb):
        k0 = pl.multiple_of(jnp.minimum(kb * tk, t_all - tk), tk)
        buf[...] = _dot(ks_ref[0, pl.ds(k0, tk), :], qst)

    def half_step(buf, kb, m_i, l_i):
        k0 = pl.multiple_of(jnp.minimum(kb * tk, t_all - tk), tk)
        causal = (kb * tk + krow) <= tlane
        ok = [causal & (jnp.concatenate([jnp.broadcast_to(sel_ref[kvh, pl.ds(k0 // SEL_BLOCK + r, 1), :],
                                                          (SEL_BLOCK, tq)) for r in range(per_blk)], axis=0) > 0.5)
              for kvh in range(NSA_KV_HEADS)]
        adjust = lambda s_cols, cg: jnp.where(ok[cg // g_n], s_cols, NEG_INF)
        v_blk = jnp.concatenate([vs_ref[0, :, pl.ds(k0, tk)], jnp.ones((SUM_ROWS, tk), BF16)], axis=0)
        return _flash_step(buf, p_ref, acc_ref, v_blk, m_i, l_i, adjust)

    def sel_trip(j, carry):
        put_scores(s1_ref, 2 * j + 1)
        carry = half_step(s0_ref, 2 * j, *carry)
        put_scores(s0_ref, 2 * j + 2)
        return half_step(s1_ref, 2 * j + 1, *carry)

    put_scores(s0_ref, 0)
    acc_ref[...] = jnp.zeros_like(acc_ref)
    init = (jnp.full((1, n_col), NEG_INF, F32), jnp.zeros((1, n_col), F32))
    n_blocks = (t0 + tq + tk - 1) // tk
    pairs = n_blocks // 2
    carry = lax.fori_loop(0, pairs, sel_trip, init)
    _, l_s = lax.cond(n_blocks % 2 == 1, lambda m, l: half_step(s0_ref, 2 * pairs, m, l), lambda m, l: (m, l), *carry)
    o_slc = acc_ref[...] * (1.0 / l_s)

    gated = []
    for h, (kvh, _) in enumerate(heads):
        cols = slice(h * tq, (h + 1) * tq)
        rows = slice(kvh * HEAD_DIM, (kvh + 1) * HEAD_DIM)
        gated.append(gates[3 * h:3 * h + 1] * o_cmp[rows, cols] + gates[3 * h + 1:3 * h + 2] * o_slc[rows, cols]
                     + gates[3 * h + 2:3 * h + 3] * o_win[rows, cols])
    for j in range(NSA_HEADS * HEAD_DIM // LANES):
        pair = jnp.concatenate(gated[2 * j:2 * j + 2], axis=0)
        o_ref[0, :, j * LANES:(j + 1) * LANES] = pair.T.astype(BF16)


def nsa_attention(qa_t, kc, vc_t, ks, vs_t, kw, vw_t, gates_t, ovt, tq=LANES):
    b, _, t = qa_t.shape
    n_cmp = kc.shape[1]
    n_sel = ovt.shape[0]
    k_top = min(SEL_TOPK, n_sel)
    tk = NSA_KEYS_PER_QUERY_BLOCK * tq
    tok = lambda n: pl.BlockSpec((1, n, LANES), lambda i, j: (i, 0, 0))
    chn = lambda n: pl.BlockSpec((1, LANES, n), lambda i, j: (i, 0, 0))
    return pl.pallas_call(
        functools.partial(_nsa_body, k_top=k_top),
        grid=(b, t // tq),
        in_specs=[pl.BlockSpec((1, NSA_Q_W, tq), lambda i, j: (i, 0, j)), tok(n_cmp), chn(n_cmp), tok(t), chn(t),
                  tok(t), chn(t), pl.BlockSpec((1, LANES, tq), lambda i, j: (i, 0, j)), _full(ovt.shape)],
        out_specs=pl.BlockSpec((1, tq, NSA_Q_W), lambda i, j: (i, j, 0)),
        out_shape=jax.ShapeDtypeStruct((b, t, NSA_Q_W), BF16),
        scratch_shapes=[pltpu.VMEM((NSA_KV_HEADS, n_sel, tq), F32), pltpu.VMEM((tk, NSA_HEADS * tq), F32),
                        pltpu.VMEM((tk, NSA_HEADS * tq), F32), pltpu.VMEM((tk, NSA_HEADS * tq), BF16),
                        pltpu.VMEM((LANES, NSA_HEADS * tq), F32)],
        compiler_params=_params("parallel", "arbitrary"),
    )(qa_t, kc, vc_t, ks, vs_t, kw, vw_t, gates_t, ovt)


FOX_KEYS_PER_QUERY_BLOCK = 2


def _fox_body(q_ref, k_ref, v_ref, cum_ref, o_ref, ck_ref, s0_ref, s1_ref, p_ref, acc_ref, *, tq):
    t = k_ref.shape[1]
    tk = FOX_KEYS_PER_QUERY_BLOCK * tq
    pair = pl.program_id(1)

    cum = cum_ref[0]
    lanes = lax.broadcasted_iota(jnp.int32, cum.shape, 1)
    for h in range(2):
        col = jnp.sum(jnp.where(lanes == MISC_F + 2 * pair + h, cum, 0.0), axis=-1, keepdims=True)
        ck_ref[h] = jnp.broadcast_to(col * LOG2E, cum.shape)

    chan = lax.broadcasted_iota(jnp.int32, (LANES, 1), 0)
    first_head = chan < HEAD_DIM
    krow = lax.broadcasted_iota(jnp.int32, (tk, 1), 0)
    qlane = lax.broadcasted_iota(jnp.int32, (1, tq), 1)
    reps = tq // LANES
    bufs = (s0_ref, s1_ref)
    blocks = [(i, kb) for i in range(t // tq) for kb in range((i * tq) // tk + 1)]
    q_cache = {}

    def q_pair(i):
        if i not in q_cache:
            q = q_ref[0, :, i * tq:(i + 1) * tq]
            q_cache[i] = jnp.concatenate([jnp.where(first_head, q, 0), jnp.where(first_head, 0, q)], axis=1)
        return q_cache[i]

    def put_scores(n):
        i, kb = blocks[n]
        bufs[n % 2][...] = _dot(k_ref[0, kb * tk:(kb + 1) * tk, :], q_pair(i))

    put_scores(0)
    m_i = l_i = None
    for n, (i, kb) in enumerate(blocks):
        if n + 1 < len(blocks):
            put_scores(n + 1)
        last = kb == (i * tq) // tk
        ok = ((kb * tk + krow) <= (i * tq + qlane)) if last else None

        def adjust(s_cols, cg, kb=kb, last=last, ok=ok):
            s_cols = s_cols - ck_ref[cg // reps, kb * tk:(kb + 1) * tk, :]
            return jnp.where(ok[:, (cg % reps) * LANES:(cg % reps + 1) * LANES], s_cols, NEG_INF) if last else s_cols

        if kb == 0:
            m_i = jnp.full((1, 2 * tq), NEG_INF, F32)
            l_i = jnp.zeros((1, 2 * tq), F32)
        v_blk = jnp.concatenate([v_ref[0, :, kb * tk:(kb + 1) * tk], jnp.ones((SUM_ROWS, tk), BF16)], axis=0)
        m_i, l_i = _flash_step(bufs[n % 2], p_ref, acc_ref, v_blk, m_i, l_i, adjust, first=kb == 0)
        if last:
            o = acc_ref[...] * (1.0 / l_i)
            o_ref[0, i * tq:(i + 1) * tq, :] = jnp.where(first_head, o[:, :tq], o[:, tq:]).T.astype(BF16)


def fox_attention(qb_t, kb, vf_t, cum, tq=256):
    b, w, t = qb_t.shape
    pairs = w // LANES
    tk = FOX_KEYS_PER_QUERY_BLOCK * tq
    return pl.pallas_call(
        functools.partial(_fox_body, tq=tq),
        grid=(b, pairs),
        in_specs=[pl.BlockSpec((1, LANES, t), lambda i, p: (i, p, 0)),
                  pl.BlockSpec((1, t, LANES), lambda i, p: (i, 0, p)),
                  pl.BlockSpec((1, LANES, t), lambda i, p: (i, p, 0)),
                  pl.BlockSpec((1, t, LANES), lambda i, p: (i, 0, 0))],
        out_specs=pl.BlockSpec((1, t, LANES), lambda i, p: (i, 0, p)),
        out_shape=jax.ShapeDtypeStruct((b, t, w), BF16),
        scratch_shapes=[pltpu.VMEM((2, t, LANES), F32), pltpu.VMEM((tk, 2 * tq), F32), pltpu.VMEM((tk, 2 * tq), F32),
                        pltpu.VMEM((tk, 2 * tq), BF16), pltpu.VMEM((LANES, 2 * tq), F32)],
        compiler_params=_params("parallel", "arbitrary"),
    )(qb_t, kb, vf_t, cum)


R_GROUP = 0
R_EXPERT = N_GROUPS


def _moe_body(x_ref, *refs, mixer_pending):
    if mixer_pending:
        oa_ref, ob_ref, wo_ref, *refs = refs
    g_ref, wr_hi_ref, wr_lo_ref, br_ref, win_ref, wout_ref, o_ref, h_ref, gate_ref = refs
    e = pl.program_id(1)

    @pl.when(e == 0)
    def _():
        x = x_ref[...]
        if mixer_pending:
            wa = oa_ref.shape[1]
            x = x + _dot(oa_ref[...], wo_ref[:wa, :]) + _dot(ob_ref[...], wo_ref[wa:, :])
        h = x * lax.rsqrt(jnp.mean(x * x, axis=-1, keepdims=True) + NORM_EPS) * g_ref[...]
        h_ref[...] = h.astype(BF16)
        h_hi, h_lo = _split2(h)
        logit = _dot(h_hi, wr_hi_ref[...]) + _dot(h_lo, wr_hi_ref[...]) + _dot(h_hi, wr_lo_ref[...]) + br_ref[...]
        lane_i = lax.broadcasted_iota(jnp.int32, logit.shape, 1)
        lane = lane_i.astype(F32)
        is_g = lane_i < N_GROUPS
        g_max = jnp.max(jnp.where(is_g, logit, -jnp.inf), axis=-1, keepdims=True)
        g_sel = jnp.min(jnp.where(is_g & (logit == g_max), lane, float(LANES)), axis=-1, keepdims=True)
        p_group = 1.0 / jnp.sum(jnp.where(is_g, jnp.exp(logit - g_max), 0.0), axis=-1, keepdims=True)
        group_of = ((lane_i - R_EXPERT) // EXPERTS_PER_GROUP).astype(F32)
        mine = (lane_i >= R_EXPERT) & (lane_i < R_EXPERT + N_EXPERTS) & (group_of == g_sel)
        v1 = jnp.max(jnp.where(mine, logit, -jnp.inf), axis=-1, keepdims=True)
        i1 = jnp.min(jnp.where(mine & (logit == v1), lane, float(LANES)), axis=-1, keepdims=True)
        rest = mine & (lane != i1)
        v2 = jnp.max(jnp.where(rest, logit, -jnp.inf), axis=-1, keepdims=True)
        i2 = jnp.min(jnp.where(rest & (logit == v2), lane, float(LANES)), axis=-1, keepdims=True)
        e2 = jnp.exp(v2 - v1)
        w1 = p_group / (1.0 + e2)
        w2 = p_group * e2 / (1.0 + e2)
        gate_ref[...] = jnp.where(lane == i1, w1, 0.0) + jnp.where(lane == i2, w2, 0.0)
        o_ref[...] = x

    gates = gate_ref[...]
    lane = lax.broadcasted_iota(jnp.int32, gates.shape, 1)
    acts = []
    for j in range(EXPERTS_PER_GROUP):
        gate_e = jnp.sum(jnp.where(lane == R_EXPERT + e * EXPERTS_PER_GROUP + j, gates, 0.0), axis=-1, keepdims=True)
        gu = _dot(h_ref[...], win_ref[j])
        acts.append((_silu(gu[:, :EXPERT_FF]) * gu[:, EXPERT_FF:] * gate_e).astype(BF16))
    o_ref[...] += _dot(jnp.concatenate(acts, axis=1), wout_ref[0])


def moe(x2, gain, wr_hi, wr_lo, br, win_bf, wout_bf, layer, mixer_pending=(), tm=1024):
    n, d = x2.shape
    row = lambda w: pl.BlockSpec((tm, w), lambda i, e: (i, 0))
    win_g = win_bf.reshape(-1, d, 2 * EXPERT_FF)
    wout_g = wout_bf.reshape(-1, EXPERTS_PER_GROUP * EXPERT_FF, d)
    pending_specs = [row(mixer_pending[0].shape[1]), row(mixer_pending[1].shape[1]),
                     _full(mixer_pending[2].shape)] if mixer_pending else []
    return pl.pallas_call(
        functools.partial(_moe_body, mixer_pending=bool(mixer_pending)),
        grid=(n // tm, N_GROUPS),
        in_specs=[row(d)] + pending_specs
                 + [_full((1, d)), _full((d, LANES)), _full((d, LANES)), _full((1, LANES)),
                    pl.BlockSpec((EXPERTS_PER_GROUP, d, 2 * EXPERT_FF), lambda i, e: (layer * N_GROUPS + e, 0, 0)),
                    pl.BlockSpec((1, EXPERTS_PER_GROUP * EXPERT_FF, d), lambda i, e: (layer * N_GROUPS + e, 0, 0))],
        out_specs=row(d),
        out_shape=jax.ShapeDtypeStruct((n, d), F32),
        scratch_shapes=[pltpu.VMEM((tm, d), BF16), pltpu.VMEM((tm, LANES), F32)],
        compiler_params=_params("parallel", "arbitrary"),
    )(x2, *mixer_pending, gain.reshape(1, d), wr_hi, wr_lo, br, win_g, wout_g)


G_CUM, G_BETA, G_LAST = 0, GDN_HEADS, 2 * GDN_HEADS


def _gdn_gates_body(ab_ref, alog_ref, dtb_ref, gb_ref):
    t = ab_ref.shape[1]
    ab = ab_ref[0]
    sp_in = ab + dtb_ref[...]
    softplus = jnp.maximum(sp_in, 0.0) + jnp.log1p(jnp.exp(-jnp.abs(sp_in)))
    lane_row = lax.broadcasted_iota(jnp.int32, (1, LANES), 1)
    g = jnp.where(lane_row < GDN_HEADS, -jnp.exp(alog_ref[...]) * softplus, 0.0)
    blk = 4 * GDN_CHUNK
    r = lax.broadcasted_iota(jnp.int32, (blk, blk), 0)
    c = lax.broadcasted_iota(jnp.int32, (blk, blk), 1)
    same = r // GDN_CHUNK == c // GDN_CHUNK
    tri = jnp.where(same & (r >= c), 1.0, 0.0).astype(BF16)
    tot = jnp.where(same, 1.0, 0.0).astype(BF16)
    lane = lax.broadcasted_iota(jnp.int32, (blk, LANES), 1)
    for s in range(t // blk):
        rs = slice(s * blk, (s + 1) * blk)
        hi, mid, lo = _split3(g[rs])
        gc = _dot(tri, hi) + _dot(tri, mid) + _dot(tri, lo)
        gl = _dot(tot, hi) + _dot(tot, mid) + _dot(tot, lo)
        gl = pltpu.roll(gl, G_LAST, 1)
        gb_ref[0, rs, :] = jnp.where(lane < G_BETA, gc, jnp.where(lane < G_LAST, _sigmoid(ab[rs]), gl))


def gdn_gates(proj, alog_row, dtb_row):
    b, t, _ = proj.shape
    return pl.pallas_call(
        _gdn_gates_body,
        grid=(b,),
        in_specs=[pl.BlockSpec((1, t, LANES), lambda i: (i, 0, C_AB // LANES)), _full((1, LANES)), _full((1, LANES))],
        out_specs=pl.BlockSpec((1, t, LANES), lambda i: (i, 0, 0)),
        out_shape=jax.ShapeDtypeStruct((b, t, LANES), F32),
        compiler_params=_params("parallel"),
    )(proj, alog_row, dtb_row)


GDN_HEADS_PER_STEP = 8
GDN_CHUNKS_PER_TRIP = 4
GDN_SEGMENTS = 4


def _dot3(a, b):
    a_hi, a_lo = _split2(a)
    b_hi, b_lo = _split2(b)
    return _dot(a_hi, b_hi) + _dot(a_hi, b_lo) + _dot(a_lo, b_hi)


def _dot1(a, b):
    return _dot(a.astype(BF16), b.astype(BF16))


CONV_HALO = 8


def _conv_silu(ext_ref, cw_ref, lanes):
    t = ext_ref.shape[0] - CONV_HALO
    y = ext_ref[CONV_HALO:, lanes] * cw_ref[CONV_WIDTH - 1:CONV_WIDTH, lanes]
    for d in range(1, CONV_WIDTH):
        y = y + ext_ref[CONV_HALO - d:CONV_HALO - d + t, lanes] * cw_ref[CONV_WIDTH - 1 - d:CONV_WIDTH - d, lanes]
    return _silu(y)


def _l2norm(y):
    return y * lax.rsqrt(jnp.sum(y * y, axis=-1, keepdims=True) + NORM_EPS)


def _gdn_body(q_ref, k_ref, v_ref, cq_ref, ck_ref, cv_ref, gb_ref, grow_ref, z_ref, x_ref, gain_ref, wout_ref, o_ref,
              ext_ref, state_ref, gl_ref, gc_ref, kb_ref, k_ref_s, kbg_ref, vb_ref, qs_ref, qg_ref, kd_ref,
              u_ref, w_ref, a_ref, mix_ref):
    t = q_ref.shape[1]
    cs = GDN_CHUNK
    dk = GDN_HEAD_DIM
    nh = GDN_HEADS_PER_STEP
    seg = pl.program_id(1)
    gb_hi, gb_mid, gb_lo = _split3(gb_ref[0])
    pick_row = lax.broadcasted_iota(jnp.int32, (LANES, LANES), 0)

    @pl.when(seg == 0)
    def _():
        ext_ref[:, 0:CONV_HALO, :] = jnp.zeros((3, CONV_HALO, ext_ref.shape[2]), F32)
        state_ref[...] = jnp.zeros_like(state_ref)

    @pl.when(seg > 0)
    def _():
        ext_ref[:, 0:CONV_HALO, :] = ext_ref[:, t:t + CONV_HALO, :]

    for i, ref in enumerate((q_ref, k_ref, v_ref)):
        ext_ref[i, CONV_HALO:, :] = ref[0]

    def column(idx):
        sel = jnp.where(pick_row == idx, 1.0, 0.0).astype(BF16)
        return _dot(gb_hi, sel) + _dot(gb_mid, sel) + _dot(gb_lo, sel)

    for s in range(nh):
        lanes = slice(s * dk, (s + 1) * dk)
        gcol = column(G_CUM + s)
        bcol = column(G_BETA + s)
        glast = column(G_LAST + s)
        eg = jnp.exp(gcol)
        k = _l2norm(_conv_silu(ext_ref.at[1], ck_ref, lanes))
        kb = k * bcol
        k_ref_s[s] = k.astype(BF16)
        kb_ref[s] = kb.astype(BF16)
        kbg_ref[s] = (kb * eg).astype(BF16)
        kd_ref[s] = (k * jnp.exp(glast - gcol)).astype(BF16)
        q = _l2norm(_conv_silu(ext_ref.at[0], cq_ref, lanes)) * (dk ** -0.5)
        qs_ref[s] = q.astype(BF16)
        qg_ref[s] = (q * eg).astype(BF16)
        vb_ref[s] = (_conv_silu(ext_ref.at[2], cv_ref, lanes) * bcol).astype(BF16)
        gl_ref[s] = glast
        gc_ref[s] = gcol

    r = lax.broadcasted_iota(jnp.int32, (cs, cs), 0)
    c = lax.broadcasted_iota(jnp.int32, (cs, cs), 1)
    tril = r >= c
    strict = r > c
    eye = jnp.where(r == c, 1.0, 0.0)

    def prep(trip, _):
        probs = [(s, trip * GDN_CHUNKS_PER_TRIP + j) for j in range(GDN_CHUNKS_PER_TRIP) for s in range(nh)]
        rows = [pl.ds(pl.multiple_of(n * cs, cs), cs) for _, n in probs]
        decay, lmat = [], []
        for (s, n), rw in zip(probs, rows):
            gr = grow_ref[0, s, pl.ds(n, 1), :]
            gc = gc_ref[s, rw, :cs]
            decay.append(jnp.where(tril, jnp.exp(jnp.where(tril, gc - gr, 0.0)), 0.0))
        for i, ((s, _), rw) in enumerate(zip(probs, rows)):
            lmat.append(jnp.where(strict, _dot_nt(kb_ref[s, rw, :], k_ref_s[s, rw, :]) * decay[i], 0.0))
        inv = [eye - m for m in lmat]
        pw = [_dot3(m, m) for m in lmat]
        span = 2
        while span < cs:
            mm = _dot3 if span == 2 else _dot1
            inv = [x + mm(x, p) for x, p in zip(inv, pw)]
            span *= 2
            if span < cs:
                pw = [_dot1(p, p) for p in pw]
        inv_bf = [x.astype(BF16) for x in inv]
        for i, ((s, _), rw) in enumerate(zip(probs, rows)):
            u_ref[s, rw, :] = _dot(inv_bf[i], vb_ref[s, rw, :])
            w_ref[s, rw, :] = _dot(inv_bf[i], kbg_ref[s, rw, :]).astype(BF16)
            a_ref[s, rw, :] = jnp.where(tril, _dot_nt(qs_ref[s, rw, :], k_ref_s[s, rw, :]) * decay[i], 0.0).astype(BF16)
        return 0

    lax.fori_loop(0, t // (cs * GDN_CHUNKS_PER_TRIP), prep, 0)

    def scan(n, states):
        r0 = pl.multiple_of(n * cs, cs)
        rows = pl.ds(r0, cs)
        s_bf = [st.astype(BF16) for st in states]
        v_bf = [(u_ref[s, rows, :] - _dot(w_ref[s, rows, :], s_bf[s])).astype(BF16) for s in range(nh)]
        new = [states[s] * jnp.exp(gl_ref[s, pl.ds(r0, 1), :]) + _dot_tn(kd_ref[s, rows, :], v_bf[s])
               for s in range(nh)]
        for s in range(nh):
            mix_ref[rows, s * dk:(s + 1) * dk] = _dot(qg_ref[s, rows, :], s_bf[s]) + _dot(a_ref[s, rows, :], v_bf[s])
        return tuple(new)

    final = lax.fori_loop(0, t // cs, scan, tuple(state_ref[s] for s in range(nh)))
    for s in range(nh):
        state_ref[s] = final[s]

    gain = gain_ref[...]
    parts = []
    for s in range(nh):
        lanes = slice(s * dk, (s + 1) * dk)
        o = mix_ref[:, lanes]
        y = o * lax.rsqrt(jnp.mean(o * o, axis=-1, keepdims=True) + NORM_EPS) * gain
        parts.append((y * _silu(z_ref[0, :, lanes])).astype(BF16))
    o_ref[0] = x_ref[0] + _dot(jnp.concatenate(parts, axis=1), wout_ref[...])


def gdn_mixer(x3, proj, conv_w, gb, gain, wout_bf):
    b, t, d = x3.shape
    ts = t // GDN_SEGMENTS if t % (GDN_SEGMENTS * GDN_CHUNK * GDN_CHUNKS_PER_TRIP) == 0 else t
    seg_chunks = ts // GDN_CHUNK
    nh = GDN_HEADS_PER_STEP
    assert nh == GDN_HEADS, "the fused output projection needs every head of a row in one grid step"
    g_rows = jnp.swapaxes(gb[:, :, G_CUM:G_CUM + GDN_HEADS], 1, 2).reshape(b, GDN_HEADS, t // GDN_CHUNK, GDN_CHUNK)
    sect = lambda k: pl.BlockSpec((1, ts, GDN_WIDTH), lambda i, s: (i, s, k))
    taps = lambda k: pl.BlockSpec((CONV_WIDTH, GDN_WIDTH), lambda i, s: (0, k))
    rows = lambda w: pl.BlockSpec((1, ts, w), lambda i, s: (i, s, 0))
    bf = lambda w: pltpu.VMEM((nh, ts, w), BF16)
    return pl.pallas_call(
        _gdn_body,
        grid=(b, t // ts),
        in_specs=[sect(0), sect(1), sect(2), taps(0), taps(1), taps(2), rows(LANES),
                  pl.BlockSpec((1, nh, seg_chunks, GDN_CHUNK), lambda i, s: (i, 0, s, 0)),
                  sect(3), rows(d), _full((1, GDN_HEAD_DIM)), _full(wout_bf.shape)],
        out_specs=rows(d),
        out_shape=jax.ShapeDtypeStruct((b, t, d), F32),
        scratch_shapes=[pltpu.VMEM((3, CONV_HALO + ts, GDN_WIDTH), F32),
                        pltpu.VMEM((nh, GDN_HEAD_DIM, GDN_HEAD_DIM), F32)]
                       + [pltpu.VMEM((nh, ts, LANES), F32)] * 2 + [bf(GDN_HEAD_DIM)] * 7
                       + [pltpu.VMEM((nh, ts, GDN_HEAD_DIM), F32), bf(GDN_HEAD_DIM), bf(GDN_CHUNK),
                          pltpu.VMEM((ts, GDN_WIDTH), F32)],
        compiler_params=_params("parallel", "arbitrary"),
    )(proj, proj, proj, conv_w, conv_w, conv_w, gb, g_rows, proj, x3, gain, wout_bf)


def _rope_tables(pos):
    half = HEAD_DIM // 2
    inv_freq = ROPE_THETA ** (-jnp.arange(half, dtype=F32) / half)
    ang = pos.astype(F32)[:, None] * inv_freq
    cos = jnp.cos(ang)
    sin = jnp.sin(ang)
    cos_t = jnp.tile(jnp.concatenate([cos, cos], axis=-1), (1, LANES // HEAD_DIM))
    sin_t = jnp.tile(jnp.concatenate([-sin, sin], axis=-1), (1, LANES // HEAD_DIM))
    return cos_t, sin_t


def _block_diag_ones(width, seg):
    idx = np.arange(width) // seg
    return jnp.asarray((idx[:, None] == idx[None, :]).astype(np.float32), dtype=BF16)


def _pad_cols(w, width):
    return jnp.pad(w, ((0, 0), (0, width - w.shape[1])))


def _even_layer(x2, b, t, norm_gain, w_in, b_gate, b_forget, cmp_pe, cmp_w1, cmp_w2, nsa_gain, fox_gain, w_out):
    d = x2.shape[1]
    cos, sin = _rope_tables(jnp.arange(t))
    tile = lambda g, n: jnp.tile(g, n).reshape(1, -1)
    bias = jnp.pad(jnp.concatenate([b_gate, b_forget]), (0, LANES - NSA_GATE_W - FOX_HEADS)).reshape(1, LANES)
    bd = _block_diag_ones(FOX_W, HEAD_DIM)
    (qa, ks, kw, vs, vw, kc_raw, vc_raw, qb, kb, vf, gates, cum) = even_in_prep(
        x2.reshape(b, t, d), norm_gain, w_in, cos, sin, tile(nsa_gain[0], NSA_HEADS), tile(nsa_gain[2], NSA_KV_HEADS), tile(nsa_gain[3], NSA_KV_HEADS),
        tile(fox_gain[0], FOX_HEADS), tile(fox_gain[1], FOX_HEADS), bias, bd)

    n_str = t // CMP_STRIDE
    half = CMP_BLOCK // 2
    eye2 = jnp.eye(NSA_KV_HEADS, dtype=F32)
    pe = jnp.tile(cmp_pe[:, :, None, :], (1, 1, NSA_KV_HEADS, 1)).reshape(2, 2, 1, half * NSA_KV_W)
    w1 = jnp.einsum('ilde,hg->ilhdge', cmp_w1, eye2).reshape(2, 2, half * NSA_KV_W, NSA_KV_W).astype(BF16)
    w2 = jnp.einsum('ide,hg->ihdge', cmp_w2, eye2).reshape(2, NSA_KV_W, NSA_KV_W).astype(BF16)
    cos_c, sin_c = _rope_tables(jnp.arange(n_str) * CMP_STRIDE + (CMP_BLOCK - 1))
    kc, vc = compress(kc_raw, vc_raw, pe, w1, w2, tile(nsa_gain[1], NSA_KV_HEADS), cos_c, sin_c, _block_diag_ones(LANES, HEAD_DIM))

    n_sel = t // SEL_BLOCK
    cs = np.arange(n_str)[:, None] * CMP_STRIDE
    ss = np.arange(n_sel)[None, :] * SEL_BLOCK
    overlap = np.clip(np.minimum(cs + CMP_BLOCK, ss + SEL_BLOCK) - np.maximum(cs, ss), 0, None) / CMP_BLOCK
    overlap[(t - CMP_BLOCK) // CMP_STRIDE + 1:] = 0.0
    ovt = jnp.asarray(overlap.T.astype(np.float32), dtype=BF16)
    o_a = nsa_attention(qa, kc, vc, ks, vs, kw, vw, gates, ovt)
    o_b = fox_attention(qb, kb, vf, cum)

    return o_a.reshape(b * t, NSA_Q_W), o_b.reshape(b * t, FOX_W), w_out.astype(BF16)


def _odd_layer(x2, b, t, norm_gain, w_in, conv_w, a_log, dt_bias, gdn_gain, w_out):
    proj = norm_matmul(x2, norm_gain, w_in).reshape(b, t, ODD_W)
    pad8 = lambda v: jnp.pad(v, (0, LANES - GDN_HEADS)).reshape(1, LANES)
    gb = gdn_gates(proj, pad8(a_log), pad8(dt_bias))
    out = gdn_mixer(x2.reshape(b, t, -1), proj, conv_w, gb, gdn_gain.reshape(1, GDN_HEAD_DIM), w_out.astype(BF16))
    return out.reshape(b * t, -1)


def _moe_layer(x2, gain, w_rg, b_rg, w_re, b_re, w_ein_bf, w_eout_bf, layer, mixer_pending=()):
    d = x2.shape[1]
    wr = _pad_cols(jnp.concatenate([w_rg, w_re], axis=1), LANES)
    wr_hi = wr.astype(BF16)
    wr_lo = (wr - wr_hi.astype(F32)).astype(BF16)
    br = jnp.pad(jnp.concatenate([b_rg, b_re]), (0, LANES - N_GROUPS - N_EXPERTS)).reshape(1, LANES)
    return moe(x2, gain, wr_hi, wr_lo, br, w_ein_bf, w_eout_bf, layer, mixer_pending)


def kernel(x, norm_mix, norm_ffn, w_in_even, b_nsa_gate, b_forget, cmp_pe, cmp_w1, cmp_w2, nsa_qk_gain, fox_qk_gain,
           w_out_even, w_in_odd, conv_w, a_log, dt_bias, gdn_norm_gain, w_out_odd, w_router_group, b_router_group,
           w_router_expert, b_router_expert, w_expert_in, w_expert_out):
    b, t, d = x.shape
    x2 = x.reshape(b * t, d)
    w_ein_bf = w_expert_in.astype(BF16)
    w_eout_bf = w_expert_out.astype(BF16)
    for layer in range(norm_mix.shape[0]):
        i = layer // 2
        mixer_pending = ()
        if layer % 2 == 0:
            mixer_pending = _even_layer(x2, b, t, norm_mix[layer], w_in_even[i], b_nsa_gate[i], b_forget[i], cmp_pe[i],
                                        cmp_w1[i], cmp_w2[i], nsa_qk_gain[i], fox_qk_gain[i], w_out_even[i])
        else:
            x2 = _odd_layer(x2, b, t, norm_mix[layer], w_in_odd[i], conv_w[i], a_log[i], dt_bias[i], gdn_norm_gain[i],
                            w_out_odd[i])
        x2 = _moe_layer(x2, norm_ffn[layer], w_router_group[layer], b_router_group[layer], w_router_expert[layer],
                        b_router_expert[layer], w_ein_bf, w_eout_bf, layer, mixer_pending)
    return x2.reshape(b, t, d)
```

```python
import functools

import numpy as np
import jax
import jax.numpy as jnp
from jax import lax
from jax.experimental import pallas as pl
from jax.experimental.pallas import tpu as pltpu

F32 = jnp.float32
BF16 = jnp.bfloat16

HEAD_DIM = 64
ROPE_THETA = 10000.0
NSA_HEADS = 8
NSA_KV_HEADS = 2
NSA_GROUP = NSA_HEADS // NSA_KV_HEADS
CMP_BLOCK = 32
CMP_STRIDE = 16
SEL_BLOCK = 64
SEL_TOPK = 8
WINDOW = 256
FOX_HEADS = 8
GDN_HEADS = 8
GDN_HEAD_DIM = 128
GDN_WIDTH = GDN_HEADS * GDN_HEAD_DIM
CONV_WIDTH = 4
GDN_CHUNK = 64
N_GROUPS = 4
EXPERTS_PER_GROUP = 4
N_EXPERTS = N_GROUPS * EXPERTS_PER_GROUP
EXPERT_FF = 256
NORM_EPS = 1e-6
NEG_INF = -1e30
FORCE_SCORE = 1e9

LANES = 128
LOG2E = 1.4426950408889634
SUM_ROWS = 16
NSA_Q_W = NSA_HEADS * HEAD_DIM
NSA_KV_W = NSA_KV_HEADS * HEAD_DIM
NSA_GATE_W = 3 * NSA_HEADS
FOX_W = FOX_HEADS * HEAD_DIM
C_QN = 0
C_KC, C_VC, C_KS, C_VS, C_KW, C_VW = (NSA_Q_W + i * NSA_KV_W for i in range(6))
C_QF = NSA_Q_W + 6 * NSA_KV_W
C_KF = C_QF + FOX_W
C_VF = C_KF + FOX_W
C_MISC = C_VF + FOX_W
EVEN_W = C_MISC + LANES
MISC_F = NSA_GATE_W
C_AB = 4 * GDN_WIDTH
ODD_W = C_AB + LANES

VMEM_LIMIT = 56 * 1024 * 1024


def _params(*sem):
    return pltpu.CompilerParams(dimension_semantics=sem, vmem_limit_bytes=VMEM_LIMIT)


def _dot(a, b):
    return jnp.dot(a, b, preferred_element_type=F32)


def _dot_nt(a, b):
    return lax.dot_general(a, b, (((1,), (1,)), ((), ())), preferred_element_type=F32)


def _dot_tn(a, b):
    return lax.dot_general(a, b, (((0,), (0,)), ((), ())), preferred_element_type=F32)


def _split2(x):
    hi = x.astype(BF16)
    return hi, (x - hi.astype(F32)).astype(BF16)


def _split3(x):
    hi = x.astype(BF16)
    r = x - hi.astype(F32)
    mid = r.astype(BF16)
    return hi, mid, (r - mid.astype(F32)).astype(BF16)


def _sigmoid(z):
    return 1.0 / (1.0 + jnp.exp(-z))


def _silu(z):
    return z * _sigmoid(z)


def _full(shape):
    nd = len(shape)
    return pl.BlockSpec(shape, lambda *_: (0,) * nd)


def _norm_matmul_body(x_ref, g_ref, w_ref, o_ref, wbf_ref):
    @pl.when(pl.program_id(0) == 0)
    def _():
        n_in = w_ref.shape[1]
        main = n_in // LANES * LANES
        wbf_ref[:, :main] = w_ref[:, :main].astype(BF16)
        if main < wbf_ref.shape[1]:
            wbf_ref[:, main:] = jnp.zeros((w_ref.shape[0], wbf_ref.shape[1] - main), BF16)
            wbf_ref[:, main:n_in] = w_ref[:, main:n_in].astype(BF16)

    x = x_ref[...]
    ms = jnp.mean(x * x, axis=-1, keepdims=True)
    h = (x * lax.rsqrt(ms + NORM_EPS) * g_ref[...]).astype(BF16)
    o_ref[...] = _dot(h, wbf_ref[...])


def norm_matmul(x2, gain, w, tm=512):
    n, d = x2.shape
    wp = -(-w.shape[1] // LANES) * LANES
    return pl.pallas_call(
        _norm_matmul_body,
        grid=(n // tm,),
        in_specs=[pl.BlockSpec((tm, d), lambda i: (i, 0)), _full((1, d)),
                  pl.BlockSpec(w.shape, lambda i: (0, 0), pipeline_mode=pl.Buffered(1))],
        out_specs=pl.BlockSpec((tm, wp), lambda i: (i, 0)),
        out_shape=jax.ShapeDtypeStruct((n, wp), F32),
        scratch_shapes=[pltpu.VMEM((d, wp), BF16)],
        compiler_params=_params("arbitrary"),
    )(x2, gain.reshape(1, d), w)


def _head_rms(x, bd, gain):
    hi, lo = _split2(x * x)
    ones2 = bd[:LANES, :LANES]
    ssum = jnp.concatenate([_dot(hi[:, c:c + LANES], ones2) + _dot(lo[:, c:c + LANES], ones2)
                            for c in range(0, x.shape[1], LANES)], axis=1)
    return x * lax.rsqrt(ssum * (1.0 / HEAD_DIM) + NORM_EPS) * gain


def _rope(x, cos, sin_signed, first_half):
    fwd = pltpu.roll(x, LANES - HEAD_DIM // 2, 1)
    bwd = pltpu.roll(x, HEAD_DIM // 2, 1)
    return x * cos + jnp.where(first_half, fwd, bwd) * sin_signed


def _even_prep_body(p_ref, cos_ref, sin_ref, gq_ref, gks_ref, gkw_ref, gfq_ref, gfk_ref, bias_ref, bd_ref,
                    qa_ref, ks_ref, kw_ref, vs_ref, vw_ref, kc_ref, vc_ref, qb_ref, kb_ref, vf_ref,
                    gate_ref, cum_ref, carry_ref, stage_ref):
    tr = p_ref.shape[1]
    bd = bd_ref[...]
    cos = cos_ref[...]
    sin = sin_ref[...]
    lane = lax.broadcasted_iota(jnp.int32, (1, LANES), 1)
    first_half = (lane % HEAD_DIM) < (HEAD_DIM // 2)
    scale = HEAD_DIM ** -0.5 * LOG2E

    qn = _head_rms(p_ref[0, :, C_QN:C_QN + NSA_Q_W], bd, gq_ref[...])
    for c in range(NSA_Q_W // LANES):
        sl = slice(c * LANES, (c + 1) * LANES)
        qa_ref[0, sl, :] = (_rope(qn[:, sl], cos, sin, first_half) * scale).T.astype(BF16)
    ks = _head_rms(p_ref[0, :, C_KS:C_KS + NSA_KV_W], bd, gks_ref[...])
    ks_ref[0] = _rope(ks, cos, sin, first_half).astype(BF16)
    kw = _head_rms(p_ref[0, :, C_KW:C_KW + NSA_KV_W], bd, gkw_ref[...])
    kw_ref[0] = _rope(kw, cos, sin, first_half).astype(BF16)
    vs_ref[0] = p_ref[0, :, C_VS:C_VS + NSA_KV_W].T.astype(BF16)
    vw_ref[0] = p_ref[0, :, C_VW:C_VW + NSA_KV_W].T.astype(BF16)
    stage_ref[0] = p_ref[0, :, C_KC:C_KC + NSA_KV_W]
    stage_ref[1] = p_ref[0, :, C_VC:C_VC + NSA_KV_W]
    for l in range(CMP_STRIDE):
        rows = pl.ds(l, tr // CMP_STRIDE, stride=CMP_STRIDE)
        kc_ref[0, :, l * NSA_KV_W:(l + 1) * NSA_KV_W] = stage_ref[0, rows, :]
        vc_ref[0, :, l * NSA_KV_W:(l + 1) * NSA_KV_W] = stage_ref[1, rows, :]

    qb = _head_rms(p_ref[0, :, C_QF:C_QF + FOX_W], bd, gfq_ref[...]) * scale
    kb_ref[0] = _head_rms(p_ref[0, :, C_KF:C_KF + FOX_W], bd, gfk_ref[...]).astype(BF16)
    for c in range(FOX_W // LANES):
        sl = slice(c * LANES, (c + 1) * LANES)
        qb_ref[0, sl, :] = qb[:, sl].T.astype(BF16)
        vf_ref[0, sl, :] = p_ref[0, :, C_VF + c * LANES:C_VF + (c + 1) * LANES].T.astype(BF16)

    z = p_ref[0, :, C_MISC:C_MISC + LANES] + bias_ref[...]
    gate_ref[0] = _sigmoid(z).T
    logf = jnp.minimum(z, 0.0) - jnp.log1p(jnp.exp(-jnp.abs(z)))

    @pl.when(pl.program_id(1) == 0)
    def _():
        carry_ref[...] = jnp.zeros_like(carry_ref)

    row = lax.broadcasted_iota(jnp.int32, (tr, tr), 0)
    col = lax.broadcasted_iota(jnp.int32, (tr, tr), 1)
    tril = jnp.where(row >= col, 1.0, 0.0).astype(BF16)
    hi, mid, lo = _split3(logf)
    cum = _dot(tril, hi) + _dot(tril, mid) + _dot(tril, lo) + carry_ref[...]
    cum_ref[0] = cum
    carry_ref[...] = cum[tr - 1:tr, :]


def _even_in_body(x_ref, g_ref, w_ref, *rest):
    proj_ref, wbf_ref = rest[-2:]

    @pl.when((pl.program_id(0) == 0) & (pl.program_id(1) == 0))
    def _():
        o_fox = C_QF + NSA_GATE_W
        n_fox = 3 * FOX_W
        wbf_ref[:, :C_QF] = w_ref[:, :C_QF].astype(BF16)
        wbf_ref[:, C_QF:C_MISC] = w_ref[:, o_fox:o_fox + n_fox].astype(BF16)
        wbf_ref[:, C_MISC:] = jnp.zeros((w_ref.shape[0], LANES), BF16)
        wbf_ref[:, C_MISC:C_MISC + NSA_GATE_W] = w_ref[:, C_QF:o_fox].astype(BF16)
        wbf_ref[:, C_MISC + NSA_GATE_W:C_MISC + NSA_GATE_W + FOX_HEADS] = w_ref[:, o_fox + n_fox:].astype(BF16)

    x = x_ref[0]
    h = (x * lax.rsqrt(jnp.mean(x * x, axis=-1, keepdims=True) + NORM_EPS) * g_ref[...]).astype(BF16)
    proj_ref[0] = _dot(h, wbf_ref[...])
    _even_prep_body(proj_ref, *rest[:-2])


def even_in_prep(x3, gain, w, cos, sin, gq, gks, gkw, gfq, gfk, bias, bd, tr=512):
    b, t, d = x3.shape
    row = lambda w: pl.BlockSpec((1, tr, w), lambda i, j: (i, j, 0))
    tab = pl.BlockSpec((tr, LANES), lambda i, j: (j, 0))
    shp = lambda w, dt: jax.ShapeDtypeStruct((b, t, w), dt)
    col = lambda w: pl.BlockSpec((1, w, tr), lambda i, j: (i, 0, j))
    shp_t = lambda w, dt: jax.ShapeDtypeStruct((b, w, t), dt)
    strd = pl.BlockSpec((1, tr // CMP_STRIDE, CMP_STRIDE * NSA_KV_W), lambda i, j: (i, j, 0))
    strd_shape = jax.ShapeDtypeStruct((b, t // CMP_STRIDE, CMP_STRIDE * NSA_KV_W), F32)
    return pl.pallas_call(
        _even_in_body,
        grid=(b, t // tr),
        in_specs=[row(d), _full((1, d)), pl.BlockSpec(w.shape, lambda i, j: (0, 0), pipeline_mode=pl.Buffered(1)),
                  tab, tab, _full((1, NSA_Q_W)), _full((1, LANES)), _full((1, LANES)),
                  _full((1, FOX_W)), _full((1, FOX_W)), _full((1, LANES)), _full((FOX_W, FOX_W))],
        out_specs=[col(NSA_Q_W), row(LANES), row(LANES), col(LANES), col(LANES), strd, strd,
                   col(FOX_W), row(FOX_W), col(FOX_W), col(LANES), row(LANES)],
        out_shape=[shp_t(NSA_Q_W, BF16), shp(LANES, BF16), shp(LANES, BF16), shp_t(LANES, BF16), shp_t(LANES, BF16),
                   strd_shape, strd_shape, shp_t(FOX_W, BF16), shp(FOX_W, BF16), shp_t(FOX_W, BF16),
                   shp_t(LANES, F32), shp(LANES, F32)],
        scratch_shapes=[pltpu.VMEM((1, LANES), F32), pltpu.VMEM((2, tr, LANES), F32), pltpu.VMEM((1, tr, EVEN_W), F32),
                        pltpu.VMEM((d, EVEN_W), BF16)],
        compiler_params=_params("arbitrary", "arbitrary"),
    )(x3, gain.reshape(1, d), w, cos, sin, gq, gks, gkw, gfq, gfk, bias, bd)


def _gelu_tanh(x):
    return 0.5 * x * (1.0 + jnp.tanh(np.sqrt(2.0 / np.pi).astype(np.float32) * (x + 0.044715 * (x * x * x))))


def _compress_body(xk_ref, xv_ref, pe_ref, w1_ref, w2_ref, gk_ref, cos_ref, sin_ref, bd_ref, kc_ref, vc_ref):
    n = xk_ref.shape[1]
    lane = lax.broadcasted_iota(jnp.int32, (1, LANES), 1)
    first_half = (lane % HEAD_DIM) < (HEAD_DIM // 2)

    def mlp(x_ref, i):
        x = x_ref[0]
        nxt = pltpu.roll(x, n - 1, 0)
        xa = (x + pe_ref[i, 0]).astype(BF16)
        xb = (nxt + pe_ref[i, 1]).astype(BF16)
        h = _dot(xa, w1_ref[i, 0]) + _dot(xb, w1_ref[i, 1])
        return _dot(_gelu_tanh(h).astype(BF16), w2_ref[i])

    kc = _head_rms(mlp(xk_ref, 0), bd_ref[...], gk_ref[...])
    kc_ref[0] = _rope(kc, cos_ref[...], sin_ref[...], first_half).astype(BF16)
    vc_ref[0] = mlp(xv_ref, 1).T.astype(BF16)


def compress(xk, xv, pe, w1, w2, gk, cos_c, sin_c, bd):
    b, n, w = xk.shape
    blk = pl.BlockSpec((1, n, w), lambda i: (i, 0, 0))
    out = pl.BlockSpec((1, n, LANES), lambda i: (i, 0, 0))
    return pl.pallas_call(
        _compress_body,
        grid=(b,),
        in_specs=[blk, blk, _full(pe.shape), _full(w1.shape), _full(w2.shape), _full((1, LANES)),
                  _full((n, LANES)), _full((n, LANES)), _full((LANES, LANES))],
        out_specs=[out, pl.BlockSpec((1, LANES, n), lambda i: (i, 0, 0))],
        out_shape=[jax.ShapeDtypeStruct((b, n, LANES), BF16), jax.ShapeDtypeStruct((b, LANES, n), BF16)],
        compiler_params=_params("parallel"),
    )(xk, xv, pe, w1, w2, gk, cos_c, sin_c, bd)


def _flash_step(s_ref, p_ref, acc_ref, v_blk, m_i, l_i, adjust, first=False):
    n_ch = acc_ref.shape[0]
    al, ms = [], []
    for cg in range(s_ref.shape[1] // LANES):
        sl = slice(cg * LANES, (cg + 1) * LANES)
        s = adjust(s_ref[:, sl], cg)
        m_new = jnp.maximum(m_i[:, sl], jnp.max(s, axis=0, keepdims=True))
        p_ref[:, sl] = jnp.exp2(s - m_new).astype(BF16)
        al.append(jnp.exp2(m_i[:, sl] - m_new))
        ms.append(m_new)
    cat = lambda xs: jnp.concatenate(xs, axis=1)
    alpha = cat(al)
    pv = _dot(v_blk, p_ref[...])
    acc_ref[...] = pv[:n_ch] if first else alpha * acc_ref[...] + pv[:n_ch]
    return cat(ms), alpha * l_i + pv[n_ch:n_ch + 1]


NSA_KEYS_PER_QUERY_BLOCK = 2


def _nsa_body(q_ref, kc_ref, vc_ref, ks_ref, vs_ref, kw_ref, vw_ref, gate_ref, ovt_ref, o_ref,
              sel_ref, s0_ref, s1_ref, p_ref, acc_ref, *, k_top):
    tq = q_ref.shape[2]
    t_all = ks_ref.shape[1]
    n_cmp = kc_ref.shape[1]
    n_sel = ovt_ref.shape[0]
    g_n = NSA_GROUP
    c = pl.program_id(1)
    t0 = c * tq
    chan = lax.broadcasted_iota(jnp.int32, (LANES, 1), 0)
    tlane = t0 + lax.broadcasted_iota(jnp.int32, (1, tq), 1)
    gates = gate_ref[0]

    nrow = lax.broadcasted_iota(jnp.int32, (n_cmp, 1), 0)
    valid_c = (nrow * CMP_STRIDE + (CMP_BLOCK - 1)) <= tlane
    jrow = lax.broadcasted_iota(jnp.int32, (n_sel, tq), 0)
    jrow_f = jrow.astype(F32)
    cur = tlane // SEL_BLOCK
    forced = (jrow == 0) | (jrow == cur) | (jrow == cur - 1)
    future = jrow * SEL_BLOCK > tlane
    tk = NSA_KEYS_PER_QUERY_BLOCK * tq
    krow = lax.broadcasted_iota(jnp.int32, (tk, 1), 0)
    per_blk = tk // SEL_BLOCK
    w_len = tq + WINDOW
    w_start = pl.multiple_of(jnp.clip(t0 - WINDOW, 0, t_all - w_len), LANES)
    wrow = w_start + lax.broadcasted_iota(jnp.int32, (w_len, 1), 0)
    valid_w = (wrow <= tlane) & (wrow > tlane - WINDOW)

    heads = [(kvh, g) for kvh in range(NSA_KV_HEADS) for g in range(g_n)]
    zero_half = jnp.zeros((HEAD_DIM, tq), BF16)

    def on_kv_rows(h, kvh):
        blk = q_ref[0, h * HEAD_DIM:(h + 1) * HEAD_DIM, :]
        return jnp.concatenate([blk, zero_half] if kvh == 0 else [zero_half, blk], axis=0)

    qst = jnp.concatenate([on_kv_rows(h, kvh) for h, (kvh, _) in enumerate(heads)], axis=1)
    n_col = len(heads) * tq

    def softmax_cols(s, ok, guard):
        outs = []
        for cg in range(len(heads)):
            sc = jnp.where(ok, s[:, cg * tq:(cg + 1) * tq], NEG_INF)
            e = jnp.exp2(sc - jnp.max(sc, axis=0, keepdims=True))
            if guard:
                e = jnp.where(ok, e, 0.0)
            den = jnp.sum(e, axis=0, keepdims=True)
            outs.append(e * (1.0 / (jnp.where(den > 0.0, den, 1.0) if guard else den)))
        return outs

    p_c = softmax_cols(_dot(kc_ref[0], qst), valid_c, guard=True)
    o_cmp = _dot(vc_ref[0], jnp.concatenate(p_c, axis=1).astype(BF16))

    for kvh in range(NSA_KV_HEADS):
        p_sum = p_c[kvh * g_n]
        for g in range(1, g_n):
            p_sum = p_sum + p_c[kvh * g_n + g]
        p_hi, p_lo = _split2(p_sum)
        imp_t = _dot(ovt_ref[...], p_hi) + _dot(ovt_ref[...], p_lo)
        val = jnp.where(forced, FORCE_SCORE, jnp.where(future, NEG_INF, imp_t))
        sel_t = jnp.zeros((n_sel, tq), F32)
        for _ in range(k_top):
            m = jnp.max(val, axis=0, keepdims=True)
            first = jnp.min(jnp.where(val == m, jrow_f, float(n_sel)), axis=0, keepdims=True)
            pick = jrow_f == first
            sel_t = jnp.where(pick, 1.0, sel_t)
            val = jnp.where(pick, -jnp.inf, val)
        sel_ref[kvh] = sel_t

    p_w = softmax_cols(_dot(kw_ref[0, pl.ds(w_start, w_len), :], qst), valid_w, guard=False)
    o_win = _dot(vw_ref[0, :, pl.ds(w_start, w_len)], jnp.concatenate(p_w, axis=1).astype(BF16))

    def put_scores(buf, kb):
        k0 = pl.multiple_of(jnp.minimum(kb * tk, t_all - tk), tk)
        buf[...] = _dot(ks_ref[0, pl.ds(k0, tk), :], qst)

    def half_step(buf, kb, m_i, l_i):
        k0 = pl.multiple_of(jnp.minimum(kb * tk, t_all - tk), tk)
        causal = (kb * tk + krow) <= tlane
        ok = [causal & (jnp.concatenate([jnp.broadcast_to(sel_ref[kvh, pl.ds(k0 // SEL_BLOCK + r, 1), :],
                                                          (SEL_BLOCK, tq)) for r in range(per_blk)], axis=0) > 0.5)
              for kvh in range(NSA_KV_HEADS)]
        adjust = lambda s_cols, cg: jnp.where(ok[cg // g_n], s_cols, NEG_INF)
        v_blk = jnp.concatenate([vs_ref[0, :, pl.ds(k0, tk)], jnp.ones((SUM_ROWS, tk), BF16)], axis=0)
        return _flash_step(buf, p_ref, acc_ref, v_blk, m_i, l_i, adjust)

    def sel_trip(j, carry):
        put_scores(s1_ref, 2 * j + 1)
        carry = half_step(s0_ref, 2 * j, *carry)
        put_scores(s0_ref, 2 * j + 2)
        return half_step(s1_ref, 2 * j + 1, *carry)

    put_scores(s0_ref, 0)
    acc_ref[...] = jnp.zeros_like(acc_ref)
    init = (jnp.full((1, n_col), NEG_INF, F32), jnp.zeros((1, n_col), F32))
    n_blocks = (t0 + tq + tk - 1) // tk
    pairs = n_blocks // 2
    carry = lax.fori_loop(0, pairs, sel_trip, init)
    _, l_s = lax.cond(n_blocks % 2 == 1, lambda m, l: half_step(s0_ref, 2 * pairs, m, l), lambda m, l: (m, l), *carry)
    o_slc = acc_ref[...] * (1.0 / l_s)

    gated = []
    for h, (kvh, _) in enumerate(heads):
        cols = slice(h * tq, (h + 1) * tq)
        rows = slice(kvh * HEAD_DIM, (kvh + 1) * HEAD_DIM)
        gated.append(gates[3 * h:3 * h + 1] * o_cmp[rows, cols] + gates[3 * h + 1:3 * h + 2] * o_slc[rows, cols]
                     + gates[3 * h + 2:3 * h + 3] * o_win[rows, cols])
    for j in range(NSA_HEADS * HEAD_DIM // LANES):
        pair = jnp.concatenate(gated[2 * j:2 * j + 2], axis=0)
        o_ref[0, :, j * LANES:(j + 1) * LANES] = pair.T.astype(BF16)


def nsa_attention(qa_t, kc, vc_t, ks, vs_t, kw, vw_t, gates_t, ovt, tq=LANES):
    b, _, t = qa_t.shape
    n_cmp = kc.shape[1]
    n_sel = ovt.shape[0]
    k_top = min(SEL_TOPK, n_sel)
    tk = NSA_KEYS_PER_QUERY_BLOCK * tq
    tok = lambda n: pl.BlockSpec((1, n, LANES), lambda i, j: (i, 0, 0))
    chn = lambda n: pl.BlockSpec((1, LANES, n), lambda i, j: (i, 0, 0))
    return pl.pallas_call(
        functools.partial(_nsa_body, k_top=k_top),
        grid=(b, t // tq),
        in_specs=[pl.BlockSpec((1, NSA_Q_W, tq), lambda i, j: (i, 0, j)), tok(n_cmp), chn(n_cmp), tok(t), chn(t),
                  tok(t), chn(t), pl.BlockSpec((1, LANES, tq), lambda i, j: (i, 0, j)), _full(ovt.shape)],
        out_specs=pl.BlockSpec((1, tq, NSA_Q_W), lambda i, j: (i, j, 0)),
        out_shape=jax.ShapeDtypeStruct((b, t, NSA_Q_W), BF16),
        scratch_shapes=[pltpu.VMEM((NSA_KV_HEADS, n_sel, tq), F32), pltpu.VMEM((tk, NSA_HEADS * tq), F32),
                        pltpu.VMEM((tk, NSA_HEADS * tq), F32), pltpu.VMEM((tk, NSA_HEADS * tq), BF16),
                        pltpu.VMEM((LANES, NSA_HEADS * tq), F32)],
        compiler_params=_params("parallel", "arbitrary"),
    )(qa_t, kc, vc_t, ks, vs_t, kw, vw_t, gates_t, ovt)


FOX_KEYS_PER_QUERY_BLOCK = 2


def _fox_body(q_ref, k_ref, v_ref, cum_ref, o_ref, ck_ref, s0_ref, s1_ref, p_ref, acc_ref, *, tq):
    t = k_ref.shape[1]
    tk = FOX_KEYS_PER_QUERY_BLOCK * tq
    pair = pl.program_id(1)

    cum = cum_ref[0]
    lanes = lax.broadcasted_iota(jnp.int32, cum.shape, 1)
    for h in range(2):
        col = jnp.sum(jnp.where(lanes == MISC_F + 2 * pair + h, cum, 0.0), axis=-1, keepdims=True)
        ck_ref[h] = jnp.broadcast_to(col * LOG2E, cum.shape)

    chan = lax.broadcasted_iota(jnp.int32, (LANES, 1), 0)
    first_head = chan < HEAD_DIM
    krow = lax.broadcasted_iota(jnp.int32, (tk, 1), 0)
    qlane = lax.broadcasted_iota(jnp.int32, (1, tq), 1)
    reps = tq // LANES
    bufs = (s0_ref, s1_ref)
    blocks = [(i, kb) for i in range(t // tq) for kb in range((i * tq) // tk + 1)]
    q_cache = {}

    def q_pair(i):
        if i not in q_cache:
            q = q_ref[0, :, i * tq:(i + 1) * tq]
            q_cache[i] = jnp.concatenate([jnp.where(first_head, q, 0), jnp.where(first_head, 0, q)], axis=1)
        return q_cache[i]

    def put_scores(n):
        i, kb = blocks[n]
        bufs[n % 2][...] = _dot(k_ref[0, kb * tk:(kb + 1) * tk, :], q_pair(i))

    put_scores(0)
    m_i = l_i = None
    for n, (i, kb) in enumerate(blocks):
        if n + 1 < len(blocks):
            put_scores(n + 1)
        last = kb == (i * tq) // tk
        ok = ((kb * tk + krow) <= (i * tq + qlane)) if last else None

        def adjust(s_cols, cg, kb=kb, last=last, ok=ok):
            s_cols = s_cols - ck_ref[cg // reps, kb * tk:(kb + 1) * tk, :]
            return jnp.where(ok[:, (cg % reps) * LANES:(cg % reps + 1) * LANES], s_cols, NEG_INF) if last else s_cols

        if kb == 0:
            m_i = jnp.full((1, 2 * tq), NEG_INF, F32)
            l_i = jnp.zeros((1, 2 * tq), F32)
        v_blk = jnp.concatenate([v_ref[0, :, kb * tk:(kb + 1) * tk], jnp.ones((SUM_ROWS, tk), BF16)], axis=0)
        m_i, l_i = _flash_step(bufs[n % 2], p_ref, acc_ref, v_blk, m_i, l_i, adjust, first=kb == 0)
        if last:
            o = acc_ref[...] * (1.0 / l_i)
            o_ref[0, i * tq:(i + 1) * tq, :] = jnp.where(first_head, o[:, :tq], o[:, tq:]).T.astype(BF16)


def fox_attention(qb_t, kb, vf_t, cum, tq=256):
    b, w, t = qb_t.shape
    pairs = w // LANES
    tk = FOX_KEYS_PER_QUERY_BLOCK * tq
    return pl.pallas_call(
        functools.partial(_fox_body, tq=tq),
        grid=(b, pairs),
        in_specs=[pl.BlockSpec((1, LANES, t), lambda i, p: (i, p, 0)),
                  pl.BlockSpec((1, t, LANES), lambda i, p: (i, 0, p)),
                  pl.BlockSpec((1, LANES, t), lambda i, p: (i, p, 0)),
                  pl.BlockSpec((1, t, LANES), lambda i, p: (i, 0, 0))],
        out_specs=pl.BlockSpec((1, t, LANES), lambda i, p: (i, 0, p)),
        out_shape=jax.ShapeDtypeStruct((b, t, w), BF16),
        scratch_shapes=[pltpu.VMEM((2, t, LANES), F32), pltpu.VMEM((tk, 2 * tq), F32), pltpu.VMEM((tk, 2 * tq), F32),
                        pltpu.VMEM((tk, 2 * tq), BF16), pltpu.VMEM((LANES, 2 * tq), F32)],
        compiler_params=_params("parallel", "arbitrary"),
    )(qb_t, kb, vf_t, cum)


R_GROUP = 0
R_EXPERT = N_GROUPS
MOE_PROLOGUE_ROWS = 256


def _moe_body(x_ref, *refs, mixer_pending):
    if mixer_pending:
        oa_ref, ob_ref, wo_ref, *refs = refs
    g_ref, wr_hi_ref, wr_lo_ref, br_ref, win_ref, wout_ref, o_ref, h_ref, gate_ref = refs
    e = pl.program_id(1)

    @pl.when(e == 0)
    def _():
        for r0 in range(0, x_ref.shape[0], MOE_PROLOGUE_ROWS):
            rows = slice(r0, r0 + MOE_PROLOGUE_ROWS)
            x = x_ref[rows, :]
            if mixer_pending:
                wa = oa_ref.shape[1]
                x = x + _dot(oa_ref[rows, :], wo_ref[:wa, :]) + _dot(ob_ref[rows, :], wo_ref[wa:, :])
            h = x * lax.rsqrt(jnp.mean(x * x, axis=-1, keepdims=True) + NORM_EPS) * g_ref[...]
            h_ref[rows, :] = h.astype(BF16)
            h_hi, h_lo = _split2(h)
            logit = (_dot(h_hi, wr_hi_ref[...]) + _dot(h_lo, wr_hi_ref[...]) + _dot(h_hi, wr_lo_ref[...])
                     + br_ref[...])
            lane_i = lax.broadcasted_iota(jnp.int32, logit.shape, 1)
            lane = lane_i.astype(F32)
            is_g = lane_i < N_GROUPS
            g_max = jnp.max(jnp.where(is_g, logit, -jnp.inf), axis=-1, keepdims=True)
            g_sel = jnp.min(jnp.where(is_g & (logit == g_max), lane, float(LANES)), axis=-1, keepdims=True)
            p_group = 1.0 / jnp.sum(jnp.where(is_g, jnp.exp(logit - g_max), 0.0), axis=-1, keepdims=True)
            group_of = ((lane_i - R_EXPERT) // EXPERTS_PER_GROUP).astype(F32)
            mine = (lane_i >= R_EXPERT) & (lane_i < R_EXPERT + N_EXPERTS) & (group_of == g_sel)
            v1 = jnp.max(jnp.where(mine, logit, -jnp.inf), axis=-1, keepdims=True)
            i1 = jnp.min(jnp.where(mine & (logit == v1), lane, float(LANES)), axis=-1, keepdims=True)
            rest = mine & (lane != i1)
            v2 = jnp.max(jnp.where(rest, logit, -jnp.inf), axis=-1, keepdims=True)
            i2 = jnp.min(jnp.where(rest & (logit == v2), lane, float(LANES)), axis=-1, keepdims=True)
            e2 = jnp.exp(v2 - v1)
            w1 = p_group / (1.0 + e2)
            w2 = p_group * e2 / (1.0 + e2)
            gate_ref[rows, :] = jnp.where(lane == i1, w1, 0.0) + jnp.where(lane == i2, w2, 0.0)
            o_ref[rows, :] = x

    gates = gate_ref[...]
    lane = lax.broadcasted_iota(jnp.int32, gates.shape, 1)
    acts = []
    for j in range(EXPERTS_PER_GROUP):
        gate_e = jnp.sum(jnp.where(lane == R_EXPERT + e * EXPERTS_PER_GROUP + j, gates, 0.0), axis=-1, keepdims=True)
        gu = _dot(h_ref[...], win_ref[j])
        acts.append((_silu(gu[:, :EXPERT_FF]) * gu[:, EXPERT_FF:] * gate_e).astype(BF16))
    o_ref[...] += _dot(jnp.concatenate(acts, axis=1), wout_ref[0])


def moe(x2, gain, wr_hi, wr_lo, br, win_bf, wout_bf, layer, mixer_pending=(), tm=1024):
    n, d = x2.shape
    row = lambda w: pl.BlockSpec((tm, w), lambda i, e: (i, 0))
    win_g = win_bf.reshape(-1, d, 2 * EXPERT_FF)
    wout_g = wout_bf.reshape(-1, EXPERTS_PER_GROUP * EXPERT_FF, d)
    pending_specs = [row(mixer_pending[0].shape[1]), row(mixer_pending[1].shape[1]),
                     _full(mixer_pending[2].shape)] if mixer_pending else []
    return pl.pallas_call(
        functools.partial(_moe_body, mixer_pending=bool(mixer_pending)),
        grid=(n // tm, N_GROUPS),
        in_specs=[row(d)] + pending_specs
                 + [_full((1, d)), _full((d, LANES)), _full((d, LANES)), _full((1, LANES)),
                    pl.BlockSpec((EXPERTS_PER_GROUP, d, 2 * EXPERT_FF), lambda i, e: (layer * N_GROUPS + e, 0, 0)),
                    pl.BlockSpec((1, EXPERTS_PER_GROUP * EXPERT_FF, d), lambda i, e: (layer * N_GROUPS + e, 0, 0))],
        out_specs=row(d),
        out_shape=jax.ShapeDtypeStruct((n, d), F32),
        scratch_shapes=[pltpu.VMEM((tm, d), BF16), pltpu.VMEM((tm, LANES), F32)],
        compiler_params=_params("parallel", "arbitrary"),
    )(x2, *mixer_pending, gain.reshape(1, d), wr_hi, wr_lo, br, win_g, wout_g)


G_CUM, G_BETA, G_LAST = 0, GDN_HEADS, 2 * GDN_HEADS


def _gdn_gates_body(ab_ref, alog_ref, dtb_ref, gb_ref, grow_ref):
    t = ab_ref.shape[1]
    ab = ab_ref[0]
    sp_in = ab + dtb_ref[...]
    softplus = jnp.maximum(sp_in, 0.0) + jnp.log1p(jnp.exp(-jnp.abs(sp_in)))
    lane_row = lax.broadcasted_iota(jnp.int32, (1, LANES), 1)
    g = jnp.where(lane_row < GDN_HEADS, -jnp.exp(alog_ref[...]) * softplus, 0.0)
    blk = 4 * GDN_CHUNK
    r = lax.broadcasted_iota(jnp.int32, (blk, blk), 0)
    c = lax.broadcasted_iota(jnp.int32, (blk, blk), 1)
    same = r // GDN_CHUNK == c // GDN_CHUNK
    tri = jnp.where(same & (r >= c), 1.0, 0.0).astype(BF16)
    tot = jnp.where(same, 1.0, 0.0).astype(BF16)
    lane = lax.broadcasted_iota(jnp.int32, (blk, LANES), 1)
    for s in range(t // blk):
        rs = slice(s * blk, (s + 1) * blk)
        hi, mid, lo = _split3(g[rs])
        gc = _dot(tri, hi) + _dot(tri, mid) + _dot(tri, lo)
        gl = _dot(tot, hi) + _dot(tot, mid) + _dot(tot, lo)
        gl = pltpu.roll(gl, G_LAST, 1)
        gb_ref[0, rs, :] = jnp.where(lane < G_BETA, gc, jnp.where(lane < G_LAST, _sigmoid(ab[rs]), gl))
        gc_t = gc.T
        for n in range(blk // GDN_CHUNK):
            grow_ref[0, :, s * (blk // GDN_CHUNK) + n, :] = gc_t[G_CUM:G_CUM + GDN_HEADS,
                                                                 n * GDN_CHUNK:(n + 1) * GDN_CHUNK]


def gdn_gates(proj, alog_row, dtb_row):
    b, t, _ = proj.shape
    n_chunks = t // GDN_CHUNK
    return pl.pallas_call(
        _gdn_gates_body,
        grid=(b,),
        in_specs=[pl.BlockSpec((1, t, LANES), lambda i: (i, 0, C_AB // LANES)), _full((1, LANES)), _full((1, LANES))],
        out_specs=[pl.BlockSpec((1, t, LANES), lambda i: (i, 0, 0)),
                   pl.BlockSpec((1, GDN_HEADS, n_chunks, GDN_CHUNK), lambda i: (i, 0, 0, 0))],
        out_shape=[jax.ShapeDtypeStruct((b, t, LANES), F32),
                   jax.ShapeDtypeStruct((b, GDN_HEADS, n_chunks, GDN_CHUNK), F32)],
        compiler_params=_params("parallel"),
    )(proj, alog_row, dtb_row)


GDN_HEADS_PER_STEP = 8
GDN_CHUNKS_PER_TRIP = 4
GDN_SEGMENTS = 4


def _dot3(a, b):
    a_hi, a_lo = _split2(a)
    b_hi, b_lo = _split2(b)
    return _dot(a_hi, b_hi) + _dot(a_hi, b_lo) + _dot(a_lo, b_hi)


def _dot1(a, b):
    return _dot(a.astype(BF16), b.astype(BF16))


CONV_HALO = 8


def _conv_silu(ext_ref, cw_ref, lanes):
    t = ext_ref.shape[0] - CONV_HALO
    y = ext_ref[CONV_HALO:, lanes] * cw_ref[CONV_WIDTH - 1:CONV_WIDTH, lanes]
    for d in range(1, CONV_WIDTH):
        y = y + ext_ref[CONV_HALO - d:CONV_HALO - d + t, lanes] * cw_ref[CONV_WIDTH - 1 - d:CONV_WIDTH - d, lanes]
    return _silu(y)


def _l2norm(y):
    return y * lax.rsqrt(jnp.sum(y * y, axis=-1, keepdims=True) + NORM_EPS)


def _gdn_body(q_ref, k_ref, v_ref, cq_ref, ck_ref, cv_ref, gb_ref, grow_ref, z_ref, x_ref, gain_ref, wout_ref, o_ref,
              ext_ref, state_ref, gl_ref, gc_ref, kb_ref, k_ref_s, kbg_ref, vb_ref, qs_ref, qg_ref, kd_ref,
              u_ref, w_ref, a_ref, mix_ref):
    t = q_ref.shape[1]
    cs = GDN_CHUNK
    dk = GDN_HEAD_DIM
    nh = GDN_HEADS_PER_STEP
    seg = pl.program_id(1)
    gb_hi, gb_mid, gb_lo = _split3(gb_ref[0])
    pick_row = lax.broadcasted_iota(jnp.int32, (LANES, LANES), 0)

    @pl.when(seg == 0)
    def _():
        ext_ref[:, 0:CONV_HALO, :] = jnp.zeros((3, CONV_HALO, ext_ref.shape[2]), F32)
        state_ref[...] = jnp.zeros_like(state_ref)

    @pl.when(seg > 0)
    def _():
        ext_ref[:, 0:CONV_HALO, :] = ext_ref[:, t:t + CONV_HALO, :]

    for i, ref in enumerate((q_ref, k_ref, v_ref)):
        ext_ref[i, CONV_HALO:, :] = ref[0]

    def column(idx):
        sel = jnp.where(pick_row == idx, 1.0, 0.0).astype(BF16)
        return _dot(gb_hi, sel) + _dot(gb_mid, sel) + _dot(gb_lo, sel)

    for s in range(nh):
        lanes = slice(s * dk, (s + 1) * dk)
        gcol = column(G_CUM + s)
        bcol = column(G_BETA + s)
        glast = column(G_LAST + s)
        eg = jnp.exp(gcol)
        k = _l2norm(_conv_silu(ext_ref.at[1], ck_ref, lanes))
        kb = k * bcol
        k_ref_s[s] = k.astype(BF16)
        kb_ref[s] = kb.astype(BF16)
        kbg_ref[s] = (kb * eg).astype(BF16)
        kd_ref[s] = (k * jnp.exp(glast - gcol)).astype(BF16)
        q = _l2norm(_conv_silu(ext_ref.at[0], cq_ref, lanes)) * (dk ** -0.5)
        qs_ref[s] = q.astype(BF16)
        qg_ref[s] = (q * eg).astype(BF16)
        vb_ref[s] = (_conv_silu(ext_ref.at[2], cv_ref, lanes) * bcol).astype(BF16)
        gl_ref[s] = glast
        gc_ref[s] = gcol

    r = lax.broadcasted_iota(jnp.int32, (cs, cs), 0)
    c = lax.broadcasted_iota(jnp.int32, (cs, cs), 1)
    tril = r >= c
    strict = r > c
    eye = jnp.where(r == c, 1.0, 0.0)

    def prep(trip, _):
        probs = [(s, trip * GDN_CHUNKS_PER_TRIP + j) for j in range(GDN_CHUNKS_PER_TRIP) for s in range(nh)]
        rows = [pl.ds(pl.multiple_of(n * cs, cs), cs) for _, n in probs]
        decay, lmat = [], []
        for (s, n), rw in zip(probs, rows):
            gr = grow_ref[0, s, pl.ds(n, 1), :]
            gc = gc_ref[s, rw, :cs]
            decay.append(jnp.where(tril, jnp.exp(jnp.where(tril, gc - gr, 0.0)), 0.0))
        for i, ((s, _), rw) in enumerate(zip(probs, rows)):
            lmat.append(jnp.where(strict, _dot_nt(kb_ref[s, rw, :], k_ref_s[s, rw, :]) * decay[i], 0.0))
        inv = [eye - m for m in lmat]
        pw = [_dot3(m, m) for m in lmat]
        span = 2
        while span < cs:
            mm = _dot3 if span == 2 else _dot1
            inv = [x + mm(x, p) for x, p in zip(inv, pw)]
            span *= 2
            if span < cs:
                pw = [_dot1(p, p) for p in pw]
        inv_bf = [x.astype(BF16) for x in inv]
        for i, ((s, _), rw) in enumerate(zip(probs, rows)):
            u_ref[s, rw, :] = _dot(inv_bf[i], vb_ref[s, rw, :])
            w_ref[s, rw, :] = _dot(inv_bf[i], kbg_ref[s, rw, :]).astype(BF16)
            a_ref[s, rw, :] = jnp.where(tril, _dot_nt(qs_ref[s, rw, :], k_ref_s[s, rw, :]) * decay[i], 0.0).astype(BF16)
        return 0

    lax.fori_loop(0, t // (cs * GDN_CHUNKS_PER_TRIP), prep, 0)

    def scan(n, _):
        r0 = pl.multiple_of(n * cs, cs)
        rows = pl.ds(r0, cs)
        s_bf = [state_ref[s].astype(BF16) for s in range(nh)]
        v_bf = [(u_ref[s, rows, :] - _dot(w_ref[s, rows, :], s_bf[s])).astype(BF16) for s in range(nh)]
        for s in range(nh):
            state_ref[s] = state_ref[s] * jnp.exp(gl_ref[s, pl.ds(r0, 1), :]) + _dot_tn(kd_ref[s, rows, :], v_bf[s])
        for s in range(nh):
            mix_ref[rows, s * dk:(s + 1) * dk] = _dot(qg_ref[s, rows, :], s_bf[s]) + _dot(a_ref[s, rows, :], v_bf[s])
        return 0

    lax.fori_loop(0, t // cs, scan, 0)

    gain = gain_ref[...]
    parts = []
    for s in range(nh):
        lanes = slice(s * dk, (s + 1) * dk)
        o = mix_ref[:, lanes]
        y = o * lax.rsqrt(jnp.mean(o * o, axis=-1, keepdims=True) + NORM_EPS) * gain
        parts.append((y * _silu(z_ref[0, :, lanes])).astype(BF16))
    o_ref[0] = x_ref[0] + _dot(jnp.concatenate(parts, axis=1), wout_ref[...])


def gdn_mixer(x3, proj, conv_w, gb, g_rows, gain, wout_bf):
    b, t, d = x3.shape
    ts = t // GDN_SEGMENTS if t % (GDN_SEGMENTS * GDN_CHUNK * GDN_CHUNKS_PER_TRIP) == 0 else t
    seg_chunks = ts // GDN_CHUNK
    nh = GDN_HEADS_PER_STEP
    assert nh == GDN_HEADS, "the fused output projection needs every head of a row in one grid step"
    sect = lambda k: pl.BlockSpec((1, ts, GDN_WIDTH), lambda i, s: (i, s, k))
    taps = lambda k: pl.BlockSpec((CONV_WIDTH, GDN_WIDTH), lambda i, s: (0, k))
    rows = lambda w: pl.BlockSpec((1, ts, w), lambda i, s: (i, s, 0))
    bf = lambda w: pltpu.VMEM((nh, ts, w), BF16)
    return pl.pallas_call(
        _gdn_body,
        grid=(b, t // ts),
        in_specs=[sect(0), sect(1), sect(2), taps(0), taps(1), taps(2), rows(LANES),
                  pl.BlockSpec((1, nh, seg_chunks, GDN_CHUNK), lambda i, s: (i, 0, s, 0)),
                  sect(3), rows(d), _full((1, GDN_HEAD_DIM)), _full(wout_bf.shape)],
        out_specs=rows(d),
        out_shape=jax.ShapeDtypeStruct((b, t, d), F32),
        scratch_shapes=[pltpu.VMEM((3, CONV_HALO + ts, GDN_WIDTH), F32),
                        pltpu.VMEM((nh, GDN_HEAD_DIM, GDN_HEAD_DIM), F32)]
                       + [pltpu.VMEM((nh, ts, LANES), F32)] * 2 + [bf(GDN_HEAD_DIM)] * 7
                       + [pltpu.VMEM((nh, ts, GDN_HEAD_DIM), F32), bf(GDN_HEAD_DIM), bf(GDN_CHUNK),
                          pltpu.VMEM((ts, GDN_WIDTH), F32)],
        compiler_params=_params("parallel", "arbitrary"),
    )(proj, proj, proj, conv_w, conv_w, conv_w, gb, g_rows, proj, x3, gain, wout_bf)


def _rope_tables(pos):
    half = HEAD_DIM // 2
    inv_freq = ROPE_THETA ** (-jnp.arange(half, dtype=F32) / half)
    ang = pos.astype(F32)[:, None] * inv_freq
    cos = jnp.cos(ang)
    sin = jnp.sin(ang)
    cos_t = jnp.tile(jnp.concatenate([cos, cos], axis=-1), (1, LANES // HEAD_DIM))
    sin_t = jnp.tile(jnp.concatenate([-sin, sin], axis=-1), (1, LANES // HEAD_DIM))
    return cos_t, sin_t


def _block_diag_ones(width, seg):
    idx = np.arange(width) // seg
    return jnp.asarray((idx[:, None] == idx[None, :]).astype(np.float32), dtype=BF16)


def _pad_cols(w, width):
    return jnp.pad(w, ((0, 0), (0, width - w.shape[1])))


def _even_layer(x2, b, t, norm_gain, w_in, b_gate, b_forget, cmp_pe, cmp_w1, cmp_w2, nsa_gain, fox_gain, w_out):
    d = x2.shape[1]
    cos, sin = _rope_tables(jnp.arange(t))
    tile = lambda g, n: jnp.tile(g, n).reshape(1, -1)
    bias = jnp.pad(jnp.concatenate([b_gate, b_forget]), (0, LANES - NSA_GATE_W - FOX_HEADS)).reshape(1, LANES)
    bd = _block_diag_ones(FOX_W, HEAD_DIM)
    (qa, ks, kw, vs, vw, kc_raw, vc_raw, qb, kb, vf, gates, cum) = even_in_prep(
        x2.reshape(b, t, d), norm_gain, w_in, cos, sin, tile(nsa_gain[0], NSA_HEADS), tile(nsa_gain[2], NSA_KV_HEADS), tile(nsa_gain[3], NSA_KV_HEADS),
        tile(fox_gain[0], FOX_HEADS), tile(fox_gain[1], FOX_HEADS), bias, bd)

    n_str = t // CMP_STRIDE
    half = CMP_BLOCK // 2
    eye2 = jnp.eye(NSA_KV_HEADS, dtype=F32)
    pe = jnp.tile(cmp_pe[:, :, None, :], (1, 1, NSA_KV_HEADS, 1)).reshape(2, 2, 1, half * NSA_KV_W)
    w1 = jnp.einsum('ilde,hg->ilhdge', cmp_w1, eye2).reshape(2, 2, half * NSA_KV_W, NSA_KV_W).astype(BF16)
    w2 = jnp.einsum('ide,hg->ihdge', cmp_w2, eye2).reshape(2, NSA_KV_W, NSA_KV_W).astype(BF16)
    cos_c, sin_c = _rope_tables(jnp.arange(n_str) * CMP_STRIDE + (CMP_BLOCK - 1))
    kc, vc = compress(kc_raw, vc_raw, pe, w1, w2, tile(nsa_gain[1], NSA_KV_HEADS), cos_c, sin_c, _block_diag_ones(LANES, HEAD_DIM))

    n_sel = t // SEL_BLOCK
    cs = np.arange(n_str)[:, None] * CMP_STRIDE
    ss = np.arange(n_sel)[None, :] * SEL_BLOCK
    overlap = np.clip(np.minimum(cs + CMP_BLOCK, ss + SEL_BLOCK) - np.maximum(cs, ss), 0, None) / CMP_BLOCK
    overlap[(t - CMP_BLOCK) // CMP_STRIDE + 1:] = 0.0
    ovt = jnp.asarray(overlap.T.astype(np.float32), dtype=BF16)
    o_a = nsa_attention(qa, kc, vc, ks, vs, kw, vw, gates, ovt)
    o_b = fox_attention(qb, kb, vf, cum)

    return o_a.reshape(b * t, NSA_Q_W), o_b.reshape(b * t, FOX_W), w_out.astype(BF16)


def _odd_layer(x2, b, t, norm_gain, w_in, conv_w, a_log, dt_bias, gdn_gain, w_out):
    proj = norm_matmul(x2, norm_gain, w_in).reshape(b, t, ODD_W)
    pad8 = lambda v: jnp.pad(v, (0, LANES - GDN_HEADS)).reshape(1, LANES)
    gb, g_rows = gdn_gates(proj, pad8(a_log), pad8(dt_bias))
    out = gdn_mixer(x2.reshape(b, t, -1), proj, conv_w, gb, g_rows, gdn_gain.reshape(1, GDN_HEAD_DIM),
                    w_out.astype(BF16))
    return out.reshape(b * t, -1)


def _moe_layer(x2, gain, w_rg, b_rg, w_re, b_re, w_ein_bf, w_eout_bf, layer, mixer_pending=()):
    d = x2.shape[1]
    wr = _pad_cols(jnp.concatenate([w_rg, w_re], axis=1), LANES)
    wr_hi = wr.astype(BF16)
    wr_lo = (wr - wr_hi.astype(F32)).astype(BF16)
    br = jnp.pad(jnp.concatenate([b_rg, b_re]), (0, LANES - N_GROUPS - N_EXPERTS)).reshape(1, LANES)
    return moe(x2, gain, wr_hi, wr_lo, br, w_ein_bf, w_eout_bf, layer, mixer_pending)


def kernel(x, norm_mix, norm_ffn, w_in_even, b_nsa_gate, b_forget, cmp_pe, cmp_w1, cmp_w2, nsa_qk_gain, fox_qk_gain,
           w_out_even, w_in_odd, conv_w, a_log, dt_bias, gdn_norm_gain, w_out_odd, w_router_group, b_router_group,
           w_router_expert, b_router_expert, w_expert_in, w_expert_out):
    b, t, d = x.shape
    x2 = x.reshape(b * t, d)
    w_ein_bf = w_expert_in.astype(BF16)
    w_eout_bf = w_expert_out.astype(BF16)
    for layer in range(norm_mix.shape[0]):
        i = layer // 2
        mixer_pending = ()
        if layer % 2 == 0:
            mixer_pending = _even_layer(x2, b, t, norm_mix[layer], w_in_even[i], b_nsa_gate[i], b_forget[i], cmp_pe[i],
                                        cmp_w1[i], cmp_w2[i], nsa_qk_gain[i], fox_qk_gain[i], w_out_even[i])
        else:
            x2 = _odd_layer(x2, b, t, norm_mix[layer], w_in_odd[i], conv_w[i], a_log[i], dt_bias[i], gdn_norm_gain[i],
                            w_out_odd[i])
        x2 = _moe_layer(x2, norm_ffn[layer], w_router_group[layer], b_router_group[layer], w_router_expert[layer],
                        b_router_expert[layer], w_ein_bf, w_eout_bf, layer, mixer_pending)
    return x2.reshape(b, t, d)
```

```python
import functools

import numpy as np
import jax
import jax.numpy as jnp
from jax import lax
from jax.experimental import pallas as pl
from jax.experimental.pallas import tpu as pltpu

F32 = jnp.float32
BF16 = jnp.bfloat16

HEAD_DIM = 64
ROPE_THETA = 10000.0
NSA_HEADS = 8
NSA_KV_HEADS = 2
NSA_GROUP = NSA_HEADS // NSA_KV_HEADS
CMP_BLOCK = 32
CMP_STRIDE = 16
SEL_BLOCK = 64
SEL_TOPK = 8
WINDOW = 256
FOX_HEADS = 8
GDN_HEADS = 8
GDN_HEAD_DIM = 128
GDN_WIDTH = GDN_HEADS * GDN_HEAD_DIM
CONV_WIDTH = 4
GDN_CHUNK = 64
N_GROUPS = 4
EXPERTS_PER_GROUP = 4
N_EXPERTS = N_GROUPS * EXPERTS_PER_GROUP
EXPERT_FF = 256
NORM_EPS = 1e-6
NEG_INF = -1e30
FORCE_SCORE = 1e9

LANES = 128
LOG2E = 1.4426950408889634
SUM_ROWS = 16
NSA_Q_W = NSA_HEADS * HEAD_DIM
NSA_KV_W = NSA_KV_HEADS * HEAD_DIM
NSA_GATE_W = 3 * NSA_HEADS
FOX_W = FOX_HEADS * HEAD_DIM
C_QN = 0
C_KC, C_VC, C_KS, C_VS, C_KW, C_VW = (NSA_Q_W + i * NSA_KV_W for i in range(6))
C_QF = NSA_Q_W + 6 * NSA_KV_W
C_KF = C_QF + FOX_W
C_VF = C_KF + FOX_W
C_MISC = C_VF + FOX_W
EVEN_W = C_MISC + LANES
MISC_F = NSA_GATE_W
C_AB = 4 * GDN_WIDTH
ODD_W = C_AB + LANES

VMEM_LIMIT = 56 * 1024 * 1024


def _params(*sem):
    return pltpu.CompilerParams(dimension_semantics=sem, vmem_limit_bytes=VMEM_LIMIT)


def _dot(a, b):
    return jnp.dot(a, b, preferred_element_type=F32)


def _dot_nt(a, b):
    return lax.dot_general(a, b, (((1,), (1,)), ((), ())), preferred_element_type=F32)


def _dot_tn(a, b):
    return lax.dot_general(a, b, (((0,), (0,)), ((), ())), preferred_element_type=F32)


def _split2(x):
    hi = x.astype(BF16)
    return hi, (x - hi.astype(F32)).astype(BF16)


def _split3(x):
    hi = x.astype(BF16)
    r = x - hi.astype(F32)
    mid = r.astype(BF16)
    return hi, mid, (r - mid.astype(F32)).astype(BF16)


def _sigmoid(z):
    return 1.0 / (1.0 + jnp.exp(-z))


def _silu(z):
    return z * _sigmoid(z)


def _full(shape):
    nd = len(shape)
    return pl.BlockSpec(shape, lambda *_: (0,) * nd)


def _norm_matmul_body(x_ref, g_ref, w_ref, o_ref, wbf_ref):
    @pl.when(pl.program_id(0) == 0)
    def _():
        n_in = w_ref.shape[1]
        main = n_in // LANES * LANES
        wbf_ref[:, :main] = w_ref[:, :main].astype(BF16)
        if main < wbf_ref.shape[1]:
            wbf_ref[:, main:] = jnp.zeros((w_ref.shape[0], wbf_ref.shape[1] - main), BF16)
            wbf_ref[:, main:n_in] = w_ref[:, main:n_in].astype(BF16)

    x = x_ref[...]
    ms = jnp.mean(x * x, axis=-1, keepdims=True)
    h = (x * lax.rsqrt(ms + NORM_EPS) * g_ref[...]).astype(BF16)
    o_ref[...] = _dot(h, wbf_ref[...])


def norm_matmul(x2, gain, w, tm=512):
    n, d = x2.shape
    wp = -(-w.shape[1] // LANES) * LANES
    return pl.pallas_call(
        _norm_matmul_body,
        grid=(n // tm,),
        in_specs=[pl.BlockSpec((tm, d), lambda i: (i, 0)), _full((1, d)),
                  pl.BlockSpec(w.shape, lambda i: (0, 0), pipeline_mode=pl.Buffered(1))],
        out_specs=pl.BlockSpec((tm, wp), lambda i: (i, 0)),
        out_shape=jax.ShapeDtypeStruct((n, wp), F32),
        scratch_shapes=[pltpu.VMEM((d, wp), BF16)],
        compiler_params=_params("arbitrary"),
    )(x2, gain.reshape(1, d), w)


def _head_rms(x, bd, gain):
    hi, lo = _split2(x * x)
    ones2 = bd[:LANES, :LANES]
    ssum = jnp.concatenate([_dot(hi[:, c:c + LANES], ones2) + _dot(lo[:, c:c + LANES], ones2)
                            for c in range(0, x.shape[1], LANES)], axis=1)
    return x * lax.rsqrt(ssum * (1.0 / HEAD_DIM) + NORM_EPS) * gain


def _rope(x, cos, sin_signed, first_half):
    fwd = pltpu.roll(x, LANES - HEAD_DIM // 2, 1)
    bwd = pltpu.roll(x, HEAD_DIM // 2, 1)
    return x * cos + jnp.where(first_half, fwd, bwd) * sin_signed


def _even_prep_body(p_ref, cos_ref, sin_ref, gq_ref, gks_ref, gkw_ref, gfq_ref, gfk_ref, bias_ref, bd_ref,
                    qa_ref, ks_ref, kw_ref, vs_ref, vw_ref, kc_ref, vc_ref, qb_ref, kb_ref, vf_ref,
                    gate_ref, cum_ref, carry_ref, stage_ref):
    tr = p_ref.shape[1]
    bd = bd_ref[...]
    cos = cos_ref[...]
    sin = sin_ref[...]
    lane = lax.broadcasted_iota(jnp.int32, (1, LANES), 1)
    first_half = (lane % HEAD_DIM) < (HEAD_DIM // 2)
    scale = HEAD_DIM ** -0.5 * LOG2E

    qn = _head_rms(p_ref[0, :, C_QN:C_QN + NSA_Q_W], bd, gq_ref[...])
    for c in range(NSA_Q_W // LANES):
        sl = slice(c * LANES, (c + 1) * LANES)
        qa_ref[0, sl, :] = (_rope(qn[:, sl], cos, sin, first_half) * scale).T.astype(BF16)
    ks = _head_rms(p_ref[0, :, C_KS:C_KS + NSA_KV_W], bd, gks_ref[...])
    ks_ref[0] = _rope(ks, cos, sin, first_half).astype(BF16)
    kw = _head_rms(p_ref[0, :, C_KW:C_KW + NSA_KV_W], bd, gkw_ref[...])
    kw_ref[0] = _rope(kw, cos, sin, first_half).astype(BF16)
    vs_ref[0] = p_ref[0, :, C_VS:C_VS + NSA_KV_W].T.astype(BF16)
    vw_ref[0] = p_ref[0, :, C_VW:C_VW + NSA_KV_W].T.astype(BF16)
    stage_ref[0] = p_ref[0, :, C_KC:C_KC + NSA_KV_W]
    stage_ref[1] = p_ref[0, :, C_VC:C_VC + NSA_KV_W]
    for l in range(CMP_STRIDE):
        rows = pl.ds(l, tr // CMP_STRIDE, stride=CMP_STRIDE)
        kc_ref[0, :, l * NSA_KV_W:(l + 1) * NSA_KV_W] = stage_ref[0, rows, :]
        vc_ref[0, :, l * NSA_KV_W:(l + 1) * NSA_KV_W] = stage_ref[1, rows, :]

    qb = _head_rms(p_ref[0, :, C_QF:C_QF + FOX_W], bd, gfq_ref[...]) * scale
    kb_ref[0] = _head_rms(p_ref[0, :, C_KF:C_KF + FOX_W], bd, gfk_ref[...]).astype(BF16)
    for c in range(FOX_W // LANES):
        sl = slice(c * LANES, (c + 1) * LANES)
        qb_ref[0, sl, :] = qb[:, sl].T.astype(BF16)
        vf_ref[0, sl, :] = p_ref[0, :, C_VF + c * LANES:C_VF + (c + 1) * LANES].T.astype(BF16)

    z = p_ref[0, :, C_MISC:C_MISC + LANES] + bias_ref[...]
    gate_ref[0] = _sigmoid(z).T
    logf = jnp.minimum(z, 0.0) - jnp.log1p(jnp.exp(-jnp.abs(z)))

    @pl.when(pl.program_id(1) == 0)
    def _():
        carry_ref[...] = jnp.zeros_like(carry_ref)

    row = lax.broadcasted_iota(jnp.int32, (tr, tr), 0)
    col = lax.broadcasted_iota(jnp.int32, (tr, tr), 1)
    tril = jnp.where(row >= col, 1.0, 0.0).astype(BF16)
    hi, mid, lo = _split3(logf)
    cum = _dot(tril, hi) + _dot(tril, mid) + _dot(tril, lo) + carry_ref[...]
    cum_ref[0] = cum
    carry_ref[...] = cum[tr - 1:tr, :]


def _even_in_body(x_ref, g_ref, w_ref, *rest):
    proj_ref, wbf_ref = rest[-2:]

    @pl.when((pl.program_id(0) == 0) & (pl.program_id(1) == 0))
    def _():
        o_fox = C_QF + NSA_GATE_W
        n_fox = 3 * FOX_W
        wbf_ref[:, :C_QF] = w_ref[:, :C_QF].astype(BF16)
        wbf_ref[:, C_QF:C_MISC] = w_ref[:, o_fox:o_fox + n_fox].astype(BF16)
        wbf_ref[:, C_MISC:] = jnp.zeros((w_ref.shape[0], LANES), BF16)
        wbf_ref[:, C_MISC:C_MISC + NSA_GATE_W] = w_ref[:, C_QF:o_fox].astype(BF16)
        wbf_ref[:, C_MISC + NSA_GATE_W:C_MISC + NSA_GATE_W + FOX_HEADS] = w_ref[:, o_fox + n_fox:].astype(BF16)

    x = x_ref[0]
    h = (x * lax.rsqrt(jnp.mean(x * x, axis=-1, keepdims=True) + NORM_EPS) * g_ref[...]).astype(BF16)
    proj_ref[0] = _dot(h, wbf_ref[...])
    _even_prep_body(proj_ref, *rest[:-2])


def even_in_prep(x3, gain, w, cos, sin, gq, gks, gkw, gfq, gfk, bias, bd, tr=512):
    b, t, d = x3.shape
    row = lambda w: pl.BlockSpec((1, tr, w), lambda i, j: (i, j, 0))
    tab = pl.BlockSpec((tr, LANES), lambda i, j: (j, 0))
    shp = lambda w, dt: jax.ShapeDtypeStruct((b, t, w), dt)
    col = lambda w: pl.BlockSpec((1, w, tr), lambda i, j: (i, 0, j))
    shp_t = lambda w, dt: jax.ShapeDtypeStruct((b, w, t), dt)
    strd = pl.BlockSpec((1, tr // CMP_STRIDE, CMP_STRIDE * NSA_KV_W), lambda i, j: (i, j, 0))
    strd_shape = jax.ShapeDtypeStruct((b, t // CMP_STRIDE, CMP_STRIDE * NSA_KV_W), F32)
    return pl.pallas_call(
        _even_in_body,
        grid=(b, t // tr),
        in_specs=[row(d), _full((1, d)), pl.BlockSpec(w.shape, lambda i, j: (0, 0), pipeline_mode=pl.Buffered(1)),
                  tab, tab, _full((1, NSA_Q_W)), _full((1, LANES)), _full((1, LANES)),
                  _full((1, FOX_W)), _full((1, FOX_W)), _full((1, LANES)), _full((FOX_W, FOX_W))],
        out_specs=[col(NSA_Q_W), row(LANES), row(LANES), col(LANES), col(LANES), strd, strd,
                   col(FOX_W), row(FOX_W), col(FOX_W), col(LANES), row(LANES)],
        out_shape=[shp_t(NSA_Q_W, BF16), shp(LANES, BF16), shp(LANES, BF16), shp_t(LANES, BF16), shp_t(LANES, BF16),
                   strd_shape, strd_shape, shp_t(FOX_W, BF16), shp(FOX_W, BF16), shp_t(FOX_W, BF16),
                   shp_t(LANES, F32), shp(LANES, F32)],
        scratch_shapes=[pltpu.VMEM((1, LANES), F32), pltpu.VMEM((2, tr, LANES), F32), pltpu.VMEM((1, tr, EVEN_W), F32),
                        pltpu.VMEM((d, EVEN_W), BF16)],
        compiler_params=_params("arbitrary", "arbitrary"),
    )(x3, gain.reshape(1, d), w, cos, sin, gq, gks, gkw, gfq, gfk, bias, bd)


def _gelu_tanh(x):
    return 0.5 * x * (1.0 + jnp.tanh(np.sqrt(2.0 / np.pi).astype(np.float32) * (x + 0.044715 * (x * x * x))))


def _compress_body(xk_ref, xv_ref, pe_ref, w1_ref, w2_ref, gk_ref, cos_ref, sin_ref, bd_ref, kc_ref, vc_ref):
    n = xk_ref.shape[1]
    lane = lax.broadcasted_iota(jnp.int32, (1, LANES), 1)
    first_half = (lane % HEAD_DIM) < (HEAD_DIM // 2)

    def mlp(x_ref, i):
        x = x_ref[0]
        nxt = pltpu.roll(x, n - 1, 0)
        xa = (x + pe_ref[i, 0]).astype(BF16)
        xb = (nxt + pe_ref[i, 1]).astype(BF16)
        h = _dot(xa, w1_ref[i, 0]) + _dot(xb, w1_ref[i, 1])
        return _dot(_gelu_tanh(h).astype(BF16), w2_ref[i])

    kc = _head_rms(mlp(xk_ref, 0), bd_ref[...], gk_ref[...])
    kc_ref[0] = _rope(kc, cos_ref[...], sin_ref[...], first_half).astype(BF16)
    vc_ref[0] = mlp(xv_ref, 1).T.astype(BF16)


def compress(xk, xv, pe, w1, w2, gk, cos_c, sin_c, bd):
    b, n, w = xk.shape
    blk = pl.BlockSpec((1, n, w), lambda i: (i, 0, 0))
    out = pl.BlockSpec((1, n, LANES), lambda i: (i, 0, 0))
    return pl.pallas_call(
        _compress_body,
        grid=(b,),
        in_specs=[blk, blk, _full(pe.shape), _full(w1.shape), _full(w2.shape), _full((1, LANES)),
                  _full((n, LANES)), _full((n, LANES)), _full((LANES, LANES))],
        out_specs=[out, pl.BlockSpec((1, LANES, n), lambda i: (i, 0, 0))],
        out_shape=[jax.ShapeDtypeStruct((b, n, LANES), BF16), jax.ShapeDtypeStruct((b, LANES, n), BF16)],
        compiler_params=_params("parallel"),
    )(xk, xv, pe, w1, w2, gk, cos_c, sin_c, bd)


def _flash_step(s_ref, p_ref, acc_ref, v_blk, m_i, l_i, adjust, first=False):
    n_ch = acc_ref.shape[0]
    al, ms = [], []
    for cg in range(s_ref.shape[1] // LANES):
        sl = slice(cg * LANES, (cg + 1) * LANES)
        s = adjust(s_ref[:, sl], cg)
        m_new = jnp.maximum(m_i[:, sl], jnp.max(s, axis=0, keepdims=True))
        p_ref[:, sl] = jnp.exp2(s - m_new).astype(BF16)
        al.append(jnp.exp2(m_i[:, sl] - m_new))
        ms.append(m_new)
    cat = lambda xs: jnp.concatenate(xs, axis=1)
    alpha = cat(al)
    pv = _dot(v_blk, p_ref[...])
    acc_ref[...] = pv[:n_ch] if first else alpha * acc_ref[...] + pv[:n_ch]
    return cat(ms), alpha * l_i + pv[n_ch:n_ch + 1]


NSA_KEYS_PER_QUERY_BLOCK = 2


def _nsa_body(q_ref, kc_ref, vc_ref, ks_ref, vs_ref, kw_ref, vw_ref, gate_ref, ovt_ref, o_ref,
              sel_ref, s0_ref, s1_ref, p_ref, acc_ref, *, k_top):
    tq = q_ref.shape[2]
    t_all = ks_ref.shape[1]
    n_cmp = kc_ref.shape[1]
    n_sel = ovt_ref.shape[0]
    g_n = NSA_GROUP
    c = pl.program_id(1)
    t0 = c * tq
    chan = lax.broadcasted_iota(jnp.int32, (LANES, 1), 0)
    tlane = t0 + lax.broadcasted_iota(jnp.int32, (1, tq), 1)
    gates = gate_ref[0]

    nrow = lax.broadcasted_iota(jnp.int32, (n_cmp, 1), 0)
    valid_c = (nrow * CMP_STRIDE + (CMP_BLOCK - 1)) <= tlane
    jrow = lax.broadcasted_iota(jnp.int32, (n_sel, tq), 0)
    jrow_f = jrow.astype(F32)
    cur = tlane // SEL_BLOCK
    forced = (jrow == 0) | (jrow == cur) | (jrow == cur - 1)
    future = jrow * SEL_BLOCK > tlane
    tk = NSA_KEYS_PER_QUERY_BLOCK * tq
    krow = lax.broadcasted_iota(jnp.int32, (tk, 1), 0)
    per_blk = tk // SEL_BLOCK
    w_len = tq + WINDOW
    w_start = pl.multiple_of(jnp.clip(t0 - WINDOW, 0, t_all - w_len), LANES)
    wrow = w_start + lax.broadcasted_iota(jnp.int32, (w_len, 1), 0)
    valid_w = (wrow <= tlane) & (wrow > tlane - WINDOW)

    heads = [(kvh, g) for kvh in range(NSA_KV_HEADS) for g in range(g_n)]
    zero_half = jnp.zeros((HEAD_DIM, tq), BF16)

    def on_kv_rows(h, kvh):
        blk = q_ref[0, h * HEAD_DIM:(h + 1) * HEAD_DIM, :]
        return jnp.concatenate([blk, zero_half] if kvh == 0 else [zero_half, blk], axis=0)

    qst = jnp.concatenate([on_kv_rows(h, kvh) for h, (kvh, _) in enumerate(heads)], axis=1)
    n_col = len(heads) * tq

    def softmax_cols(s, ok, guard):
        outs = []
        for cg in range(len(heads)):
            sc = jnp.where(ok, s[:, cg * tq:(cg + 1) * tq], NEG_INF)
            e = jnp.exp2(sc - jnp.max(sc, axis=0, keepdims=True))
            if guard:
                e = jnp.where(ok, e, 0.0)
            den = jnp.sum(e, axis=0, keepdims=True)
            outs.append(e * (1.0 / (jnp.where(den > 0.0, den, 1.0) if guard else den)))
        return outs

    p_c = softmax_cols(_dot(kc_ref[0], qst), valid_c, guard=True)
    o_cmp = _dot(vc_ref[0], jnp.concatenate(p_c, axis=1).astype(BF16))

    for kvh in range(NSA_KV_HEADS):
        p_sum = p_c[kvh * g_n]
        for g in range(1, g_n):
            p_sum = p_sum + p_c[kvh * g_n + g]
        p_hi, p_lo = _split2(p_sum)
        imp_t = _dot(ovt_ref[...], p_hi) + _dot(ovt_ref[...], p_lo)
        val = jnp.where(forced, FORCE_SCORE, jnp.where(future, NEG_INF, imp_t))
        sel_t = jnp.zeros((n_sel, tq), F32)
        for _ in range(k_top):
            m = jnp.max(val, axis=0, keepdims=True)
            first = jnp.min(jnp.where(val == m, jrow_f, float(n_sel)), axis=0, keepdims=True)
            pick = jrow_f == first
            sel_t = jnp.where(pick, 1.0, sel_t)
            val = jnp.where(pick, -jnp.inf, val)
        sel_ref[kvh] = sel_t

    p_w = softmax_cols(_dot(kw_ref[0, pl.ds(w_start, w_len), :], qst), valid_w, guard=False)
    o_win = _dot(vw_ref[0, :, pl.ds(w_start, w_len)], jnp.concatenate(p_w, axis=1).astype(BF16))

    def put_scores(buf, kb):
        k0 = pl.multiple_of(jnp.minimum(kb * tk, t_all - tk), tk)
        buf[...] = _dot(ks_ref[0, pl.ds(k0, tk), :], qst)

    def half_step(buf, kb, m_i, l_i):
        k0 = pl.multiple_of(jnp.minimum(kb * tk, t_all - tk), tk)
        causal = (kb * tk + krow) <= tlane
        ok = [causal & (jnp.concatenate([jnp.broadcast_to(sel_ref[kvh, pl.ds(k0 // SEL_BLOCK + r, 1), :],
                                                          (SEL_BLOCK, tq)) for r in range(per_blk)], axis=0) > 0.5)
              for kvh in range(NSA_KV_HEADS)]
        adjust = lambda s_cols, cg: jnp.where(ok[cg // g_n], s_cols, NEG_INF)
        v_blk = jnp.concatenate([vs_ref[0, :, pl.ds(k0, tk)], jnp.ones((SUM_ROWS, tk), BF16)], axis=0)
        return _flash_step(buf, p_ref, acc_ref, v_blk, m_i, l_i, adjust)

    def sel_trip(j, carry):
        put_scores(s1_ref, 2 * j + 1)
        carry = half_step(s0_ref, 2 * j, *carry)
        put_scores(s0_ref, 2 * j + 2)
        return half_step(s1_ref, 2 * j + 1, *carry)

    put_scores(s0_ref, 0)
    acc_ref[...] = jnp.zeros_like(acc_ref)
    init = (jnp.full((1, n_col), NEG_INF, F32), jnp.zeros((1, n_col), F32))
    n_blocks = (t0 + tq + tk - 1) // tk
    pairs = n_blocks // 2
    carry = lax.fori_loop(0, pairs, sel_trip, init)
    _, l_s = lax.cond(n_blocks % 2 == 1, lambda m, l: half_step(s0_ref, 2 * pairs, m, l), lambda m, l: (m, l), *carry)
    o_slc = acc_ref[...] * (1.0 / l_s)

    gated = []
    for h, (kvh, _) in enumerate(heads):
        cols = slice(h * tq, (h + 1) * tq)
        rows = slice(kvh * HEAD_DIM, (kvh + 1) * HEAD_DIM)
        gated.append(gates[3 * h:3 * h + 1] * o_cmp[rows, cols] + gates[3 * h + 1:3 * h + 2] * o_slc[rows, cols]
                     + gates[3 * h + 2:3 * h + 3] * o_win[rows, cols])
    for j in range(NSA_HEADS * HEAD_DIM // LANES):
        pair = jnp.concatenate(gated[2 * j:2 * j + 2], axis=0)
        o_ref[0, :, j * LANES:(j + 1) * LANES] = pair.T.astype(BF16)


def nsa_attention(qa_t, kc, vc_t, ks, vs_t, kw, vw_t, gates_t, ovt, tq=LANES):
    b, _, t = qa_t.shape
    n_cmp = kc.shape[1]
    n_sel = ovt.shape[0]
    k_top = min(SEL_TOPK, n_sel)
    tk = NSA_KEYS_PER_QUERY_BLOCK * tq
    tok = lambda n: pl.BlockSpec((1, n, LANES), lambda i, j: (i, 0, 0))
    chn = lambda n: pl.BlockSpec((1, LANES, n), lambda i, j: (i, 0, 0))
    return pl.pallas_call(
        functools.partial(_nsa_body, k_top=k_top),
        grid=(b, t // tq),
        in_specs=[pl.BlockSpec((1, NSA_Q_W, tq), lambda i, j: (i, 0, j)), tok(n_cmp), chn(n_cmp), tok(t), chn(t),
                  tok(t), chn(t), pl.BlockSpec((1, LANES, tq), lambda i, j: (i, 0, j)), _full(ovt.shape)],
        out_specs=pl.BlockSpec((1, tq, NSA_Q_W), lambda i, j: (i, j, 0)),
        out_shape=jax.ShapeDtypeStruct((b, t, NSA_Q_W), BF16),
        scratch_shapes=[pltpu.VMEM((NSA_KV_HEADS, n_sel, tq), F32), pltpu.VMEM((tk, NSA_HEADS * tq), F32),
                        pltpu.VMEM((tk, NSA_HEADS * tq), F32), pltpu.VMEM((tk, NSA_HEADS * tq), BF16),
                        pltpu.VMEM((LANES, NSA_HEADS * tq), F32)],
        compiler_params=_params("parallel", "arbitrary"),
    )(qa_t, kc, vc_t, ks, vs_t, kw, vw_t, gates_t, ovt)


FOX_KEYS_PER_QUERY_BLOCK = 2


def _fox_body(q_ref, k_ref, v_ref, cum_ref, o_ref, ck_ref, s0_ref, s1_ref, p_ref, acc_ref, *, tq):
    t = k_ref.shape[1]
    tk = FOX_KEYS_PER_QUERY_BLOCK * tq
    pair = pl.program_id(1)

    cum = cum_ref[0]
    lanes = lax.broadcasted_iota(jnp.int32, cum.shape, 1)
    for h in range(2):
        col = jnp.sum(jnp.where(lanes == MISC_F + 2 * pair + h, cum, 0.0), axis=-1, keepdims=True)
        ck_ref[h] = jnp.broadcast_to(col * LOG2E, cum.shape)

    chan = lax.broadcasted_iota(jnp.int32, (LANES, 1), 0)
    first_head = chan < HEAD_DIM
    krow = lax.broadcasted_iota(jnp.int32, (tk, 1), 0)
    qlane = lax.broadcasted_iota(jnp.int32, (1, tq), 1)
    reps = tq // LANES
    bufs = (s0_ref, s1_ref)
    blocks = [(i, kb) for i in range(t // tq) for kb in range((i * tq) // tk + 1)]
    q_cache = {}

    def q_pair(i):
        if i not in q_cache:
            q = q_ref[0, :, i * tq:(i + 1) * tq]
            q_cache[i] = jnp.concatenate([jnp.where(first_head, q, 0), jnp.where(first_head, 0, q)], axis=1)
        return q_cache[i]

    def put_scores(n):
        i, kb = blocks[n]
        bufs[n % 2][...] = _dot(k_ref[0, kb * tk:(kb + 1) * tk, :], q_pair(i))

    put_scores(0)
    m_i = l_i = None
    for n, (i, kb) in enumerate(blocks):
        if n + 1 < len(blocks):
            put_scores(n + 1)
        last = kb == (i * tq) // tk
        ok = ((kb * tk + krow) <= (i * tq + qlane)) if last else None

        def adjust(s_cols, cg, kb=kb, last=last, ok=ok):
            s_cols = s_cols - ck_ref[cg // reps, kb * tk:(kb + 1) * tk, :]
            return jnp.where(ok[:, (cg % reps) * LANES:(cg % reps + 1) * LANES], s_cols, NEG_INF) if last else s_cols

        if kb == 0:
            m_i = jnp.full((1, 2 * tq), NEG_INF, F32)
            l_i = jnp.zeros((1, 2 * tq), F32)
        v_blk = jnp.concatenate([v_ref[0, :, kb * tk:(kb + 1) * tk], jnp.ones((SUM_ROWS, tk), BF16)], axis=0)
        m_i, l_i = _flash_step(bufs[n % 2], p_ref, acc_ref, v_blk, m_i, l_i, adjust, first=kb == 0)
        if last:
            o = acc_ref[...] * (1.0 / l_i)
            o_ref[0, i * tq:(i + 1) * tq, :] = jnp.where(first_head, o[:, :tq], o[:, tq:]).T.astype(BF16)


def fox_attention(qb_t, kb, vf_t, cum, tq=256):
    b, w, t = qb_t.shape
    pairs = w // LANES
    tk = FOX_KEYS_PER_QUERY_BLOCK * tq
    return pl.pallas_call(
        functools.partial(_fox_body, tq=tq),
        grid=(b, pairs),
        in_specs=[pl.BlockSpec((1, LANES, t), lambda i, p: (i, p, 0)),
                  pl.BlockSpec((1, t, LANES), lambda i, p: (i, 0, p)),
                  pl.BlockSpec((1, LANES, t), lambda i, p: (i, p, 0)),
                  pl.BlockSpec((1, t, LANES), lambda i, p: (i, 0, 0))],
        out_specs=pl.BlockSpec((1, t, LANES), lambda i, p: (i, 0, p)),
        out_shape=jax.ShapeDtypeStruct((b, t, w), BF16),
        scratch_shapes=[pltpu.VMEM((2, t, LANES), F32), pltpu.VMEM((tk, 2 * tq), F32), pltpu.VMEM((tk, 2 * tq), F32),
                        pltpu.VMEM((tk, 2 * tq), BF16), pltpu.VMEM((LANES, 2 * tq), F32)],
        compiler_params=_params("parallel", "arbitrary"),
    )(qb_t, kb, vf_t, cum)


R_GROUP = 0
R_EXPERT = N_GROUPS
MOE_PROLOGUE_ROWS = 512


def _moe_body(x_ref, *refs, mixer_pending):
    if mixer_pending:
        oa_ref, ob_ref, wo_ref, *refs = refs
    g_ref, wr_hi_ref, wr_lo_ref, br_ref, win_ref, wout_ref, o_ref, h_ref, gate_ref = refs
    e = pl.program_id(1)

    @pl.when(e == 0)
    def _():
        for r0 in range(0, x_ref.shape[0], MOE_PROLOGUE_ROWS):
            rows = slice(r0, r0 + MOE_PROLOGUE_ROWS)
            x = x_ref[rows, :]
            if mixer_pending:
                wa = oa_ref.shape[1]
                x = x + _dot(oa_ref[rows, :], wo_ref[:wa, :]) + _dot(ob_ref[rows, :], wo_ref[wa:, :])
            h = x * lax.rsqrt(jnp.mean(x * x, axis=-1, keepdims=True) + NORM_EPS) * g_ref[...]
            h_ref[rows, :] = h.astype(BF16)
            h_hi, h_lo = _split2(h)
            logit = (_dot(h_hi, wr_hi_ref[...]) + _dot(h_lo, wr_hi_ref[...]) + _dot(h_hi, wr_lo_ref[...])
                     + br_ref[...])
            lane_i = lax.broadcasted_iota(jnp.int32, logit.shape, 1)
            lane = lane_i.astype(F32)
            is_g = lane_i < N_GROUPS
            g_max = jnp.max(jnp.where(is_g, logit, -jnp.inf), axis=-1, keepdims=True)
            g_sel = jnp.min(jnp.where(is_g & (logit == g_max), lane, float(LANES)), axis=-1, keepdims=True)
            p_group = 1.0 / jnp.sum(jnp.where(is_g, jnp.exp(logit - g_max), 0.0), axis=-1, keepdims=True)
            group_of = ((lane_i - R_EXPERT) // EXPERTS_PER_GROUP).astype(F32)
            mine = (lane_i >= R_EXPERT) & (lane_i < R_EXPERT + N_EXPERTS) & (group_of == g_sel)
            v1 = jnp.max(jnp.where(mine, logit, -jnp.inf), axis=-1, keepdims=True)
            i1 = jnp.min(jnp.where(mine & (logit == v1), lane, float(LANES)), axis=-1, keepdims=True)
            rest = mine & (lane != i1)
            v2 = jnp.max(jnp.where(rest, logit, -jnp.inf), axis=-1, keepdims=True)
            i2 = jnp.min(jnp.where(rest & (logit == v2), lane, float(LANES)), axis=-1, keepdims=True)
            e2 = jnp.exp(v2 - v1)
            w1 = p_group / (1.0 + e2)
            w2 = p_group * e2 / (1.0 + e2)
            gate_ref[rows, :] = jnp.where(lane == i1, w1, 0.0) + jnp.where(lane == i2, w2, 0.0)
            o_ref[rows, :] = x

    gates = gate_ref[...]
    lane = lax.broadcasted_iota(jnp.int32, gates.shape, 1)
    acts = []
    for j in range(EXPERTS_PER_GROUP):
        gate_e = jnp.sum(jnp.where(lane == R_EXPERT + e * EXPERTS_PER_GROUP + j, gates, 0.0), axis=-1, keepdims=True)
        gu = _dot(h_ref[...], win_ref[j])
        acts.append((_silu(gu[:, :EXPERT_FF]) * gu[:, EXPERT_FF:] * gate_e).astype(BF16))
    o_ref[...] += _dot(jnp.concatenate(acts, axis=1), wout_ref[0])


def moe(x2, gain, wr_hi, wr_lo, br, win_bf, wout_bf, layer, mixer_pending=(), tm=1024):
    n, d = x2.shape
    row = lambda w: pl.BlockSpec((tm, w), lambda i, e: (i, 0))
    win_g = win_bf.reshape(-1, d, 2 * EXPERT_FF)
    wout_g = wout_bf.reshape(-1, EXPERTS_PER_GROUP * EXPERT_FF, d)
    pending_specs = [row(mixer_pending[0].shape[1]), row(mixer_pending[1].shape[1]),
                     _full(mixer_pending[2].shape)] if mixer_pending else []
    return pl.pallas_call(
        functools.partial(_moe_body, mixer_pending=bool(mixer_pending)),
        grid=(n // tm, N_GROUPS),
        in_specs=[row(d)] + pending_specs
                 + [_full((1, d)), _full((d, LANES)), _full((d, LANES)), _full((1, LANES)),
                    pl.BlockSpec((EXPERTS_PER_GROUP, d, 2 * EXPERT_FF), lambda i, e: (layer * N_GROUPS + e, 0, 0)),
                    pl.BlockSpec((1, EXPERTS_PER_GROUP * EXPERT_FF, d), lambda i, e: (layer * N_GROUPS + e, 0, 0))],
        out_specs=row(d),
        out_shape=jax.ShapeDtypeStruct((n, d), F32),
        scratch_shapes=[pltpu.VMEM((tm, d), BF16), pltpu.VMEM((tm, LANES), F32)],
        compiler_params=_params("parallel", "arbitrary"),
    )(x2, *mixer_pending, gain.reshape(1, d), wr_hi, wr_lo, br, win_g, wout_g)


G_CUM, G_BETA, G_LAST = 0, GDN_HEADS, 2 * GDN_HEADS


def _gdn_gates_body(ab_ref, alog_ref, dtb_ref, gb_ref, grow_ref):
    t = ab_ref.shape[1]
    ab = ab_ref[0]
    sp_in = ab + dtb_ref[...]
    softplus = jnp.maximum(sp_in, 0.0) + jnp.log1p(jnp.exp(-jnp.abs(sp_in)))
    lane_row = lax.broadcasted_iota(jnp.int32, (1, LANES), 1)
    g = jnp.where(lane_row < GDN_HEADS, -jnp.exp(alog_ref[...]) * softplus, 0.0)
    blk = 4 * GDN_CHUNK
    r = lax.broadcasted_iota(jnp.int32, (blk, blk), 0)
    c = lax.broadcasted_iota(jnp.int32, (blk, blk), 1)
    same = r // GDN_CHUNK == c // GDN_CHUNK
    tri = jnp.where(same & (r >= c), 1.0, 0.0).astype(BF16)
    tot = jnp.where(same, 1.0, 0.0).astype(BF16)
    lane = lax.broadcasted_iota(jnp.int32, (blk, LANES), 1)
    for s in range(t // blk):
        rs = slice(s * blk, (s + 1) * blk)
        hi, mid, lo = _split3(g[rs])
        gc = _dot(tri, hi) + _dot(tri, mid) + _dot(tri, lo)
        gl = _dot(tot, hi) + _dot(tot, mid) + _dot(tot, lo)
        gl = pltpu.roll(gl, G_LAST, 1)
        gb_ref[0, rs, :] = jnp.where(lane < G_BETA, gc, jnp.where(lane < G_LAST, _sigmoid(ab[rs]), gl))
        gc_t = gc.T
        for n in range(blk // GDN_CHUNK):
            grow_ref[0, :, s * (blk // GDN_CHUNK) + n, :] = gc_t[G_CUM:G_CUM + GDN_HEADS,
                                                                 n * GDN_CHUNK:(n + 1) * GDN_CHUNK]


def gdn_gates(proj, alog_row, dtb_row):
    b, t, _ = proj.shape
    n_chunks = t // GDN_CHUNK
    return pl.pallas_call(
        _gdn_gates_body,
        grid=(b,),
        in_specs=[pl.BlockSpec((1, t, LANES), lambda i: (i, 0, C_AB // LANES)), _full((1, LANES)), _full((1, LANES))],
        out_specs=[pl.BlockSpec((1, t, LANES), lambda i: (i, 0, 0)),
                   pl.BlockSpec((1, GDN_HEADS, n_chunks, GDN_CHUNK), lambda i: (i, 0, 0, 0))],
        out_shape=[jax.ShapeDtypeStruct((b, t, LANES), F32),
                   jax.ShapeDtypeStruct((b, GDN_HEADS, n_chunks, GDN_CHUNK), F32)],
        compiler_params=_params("parallel"),
    )(proj, alog_row, dtb_row)


GDN_HEADS_PER_STEP = 8
GDN_CHUNKS_PER_TRIP = 4
GDN_SEGMENTS = 4


def _dot3(a, b):
    a_hi, a_lo = _split2(a)
    b_hi, b_lo = _split2(b)
    return _dot(a_hi, b_hi) + _dot(a_hi, b_lo) + _dot(a_lo, b_hi)


def _dot1(a, b):
    return _dot(a.astype(BF16), b.astype(BF16))


CONV_HALO = 8


def _conv_silu(ext_ref, cw_ref, lanes):
    t = ext_ref.shape[0] - CONV_HALO
    y = ext_ref[CONV_HALO:, lanes] * cw_ref[CONV_WIDTH - 1:CONV_WIDTH, lanes]
    for d in range(1, CONV_WIDTH):
        y = y + ext_ref[CONV_HALO - d:CONV_HALO - d + t, lanes] * cw_ref[CONV_WIDTH - 1 - d:CONV_WIDTH - d, lanes]
    return _silu(y)


def _l2norm(y):
    return y * lax.rsqrt(jnp.sum(y * y, axis=-1, keepdims=True) + NORM_EPS)


def _gdn_body(q_ref, k_ref, v_ref, cq_ref, ck_ref, cv_ref, gb_ref, grow_ref, z_ref, x_ref, gain_ref, wout_ref, o_ref,
              ext_ref, state_ref, gl_ref, gc_ref, kb_ref, k_ref_s, kbg_ref, vb_ref, qs_ref, qg_ref, kd_ref,
              u_ref, w_ref, a_ref, mix_ref):
    t = q_ref.shape[1]
    cs = GDN_CHUNK
    dk = GDN_HEAD_DIM
    nh = GDN_HEADS_PER_STEP
    seg = pl.program_id(1)
    gb_hi, gb_mid, gb_lo = _split3(gb_ref[0])
    pick_row = lax.broadcasted_iota(jnp.int32, (LANES, LANES), 0)

    @pl.when(seg == 0)
    def _():
        ext_ref[:, 0:CONV_HALO, :] = jnp.zeros((3, CONV_HALO, ext_ref.shape[2]), F32)
        state_ref[...] = jnp.zeros_like(state_ref)

    @pl.when(seg > 0)
    def _():
        ext_ref[:, 0:CONV_HALO, :] = ext_ref[:, t:t + CONV_HALO, :]

    for i, ref in enumerate((q_ref, k_ref, v_ref)):
        ext_ref[i, CONV_HALO:, :] = ref[0]

    def column(idx):
        sel = jnp.where(pick_row == idx, 1.0, 0.0).astype(BF16)
        return _dot(gb_hi, sel) + _dot(gb_mid, sel) + _dot(gb_lo, sel)

    for s in range(nh):
        lanes = slice(s * dk, (s + 1) * dk)
        gcol = column(G_CUM + s)
        bcol = column(G_BETA + s)
        glast = column(G_LAST + s)
        eg = jnp.exp(gcol)
        k = _l2norm(_conv_silu(ext_ref.at[1], ck_ref, lanes))
        kb = k * bcol
        k_ref_s[s] = k.astype(BF16)
        kb_ref[s] = kb.astype(BF16)
        kbg_ref[s] = (kb * eg).astype(BF16)
        kd_ref[s] = (k * jnp.exp(glast - gcol)).astype(BF16)
        q = _l2norm(_conv_silu(ext_ref.at[0], cq_ref, lanes)) * (dk ** -0.5)
        qs_ref[s] = q.astype(BF16)
        qg_ref[s] = (q * eg).astype(BF16)
        vb_ref[s] = (_conv_silu(ext_ref.at[2], cv_ref, lanes) * bcol).astype(BF16)
        gl_ref[s] = glast
        gc_ref[s] = gcol

    r = lax.broadcasted_iota(jnp.int32, (cs, cs), 0)
    c = lax.broadcasted_iota(jnp.int32, (cs, cs), 1)
    tril = r >= c
    strict = r > c
    eye = jnp.where(r == c, 1.0, 0.0)

    def prep(trip, _):
        probs = [(s, trip * GDN_CHUNKS_PER_TRIP + j) for j in range(GDN_CHUNKS_PER_TRIP) for s in range(nh)]
        rows = [pl.ds(pl.multiple_of(n * cs, cs), cs) for _, n in probs]
        decay, lmat = [], []
        for (s, n), rw in zip(probs, rows):
            gr = grow_ref[0, s, pl.ds(n, 1), :]
            gc = gc_ref[s, rw, :cs]
            decay.append(jnp.where(tril, jnp.exp(jnp.where(tril, gc - gr, 0.0)), 0.0))
        for i, ((s, _), rw) in enumerate(zip(probs, rows)):
            lmat.append(jnp.where(strict, _dot_nt(kb_ref[s, rw, :], k_ref_s[s, rw, :]) * decay[i], 0.0))
        inv = [eye - m for m in lmat]
        pw = [_dot3(m, m) for m in lmat]
        span = 2
        while span < cs:
            mm = _dot3 if span == 2 else _dot1
            inv = [x + mm(x, p) for x, p in zip(inv, pw)]
            span *= 2
            if span < cs:
                pw = [_dot1(p, p) for p in pw]
        inv_bf = [x.astype(BF16) for x in inv]
        for i, ((s, _), rw) in enumerate(zip(probs, rows)):
            u_ref[s, rw, :] = _dot(inv_bf[i], vb_ref[s, rw, :])
            w_ref[s, rw, :] = _dot(inv_bf[i], kbg_ref[s, rw, :]).astype(BF16)
            a_ref[s, rw, :] = jnp.where(tril, _dot_nt(qs_ref[s, rw, :], k_ref_s[s, rw, :]) * decay[i], 0.0).astype(BF16)
        return 0

    lax.fori_loop(0, t // (cs * GDN_CHUNKS_PER_TRIP), prep, 0)

    def scan(n, _):
        r0 = pl.multiple_of(n * cs, cs)
        rows = pl.ds(r0, cs)
        s_bf = [state_ref[s].astype(BF16) for s in range(nh)]
        v_bf = [(u_ref[s, rows, :] - _dot(w_ref[s, rows, :], s_bf[s])).astype(BF16) for s in range(nh)]
        for s in range(nh):
            state_ref[s] = state_ref[s] * jnp.exp(gl_ref[s, pl.ds(r0, 1), :]) + _dot_tn(kd_ref[s, rows, :], v_bf[s])
        for s in range(nh):
            mix_ref[rows, s * dk:(s + 1) * dk] = _dot(qg_ref[s, rows, :], s_bf[s]) + _dot(a_ref[s, rows, :], v_bf[s])
        return 0

    lax.fori_loop(0, t // cs, scan, 0)

    gain = gain_ref[...]
    parts = []
    for s in range(nh):
        lanes = slice(s * dk, (s + 1) * dk)
        o = mix_ref[:, lanes]
        y = o * lax.rsqrt(jnp.mean(o * o, axis=-1, keepdims=True) + NORM_EPS) * gain
        parts.append((y * _silu(z_ref[0, :, lanes])).astype(BF16))
    o_ref[0] = x_ref[0] + _dot(jnp.concatenate(parts, axis=1), wout_ref[...])


def gdn_mixer(x3, proj, conv_w, gb, g_rows, gain, wout_bf):
    b, t, d = x3.shape
    ts = t // GDN_SEGMENTS if t % (GDN_SEGMENTS * GDN_CHUNK * GDN_CHUNKS_PER_TRIP) == 0 else t
    seg_chunks = ts // GDN_CHUNK
    nh = GDN_HEADS_PER_STEP
    assert nh == GDN_HEADS, "the fused output projection needs every head of a row in one grid step"
    sect = lambda k: pl.BlockSpec((1, ts, GDN_WIDTH), lambda i, s: (i, s, k))
    taps = lambda k: pl.BlockSpec((CONV_WIDTH, GDN_WIDTH), lambda i, s: (0, k))
    rows = lambda w: pl.BlockSpec((1, ts, w), lambda i, s: (i, s, 0))
    bf = lambda w: pltpu.VMEM((nh, ts, w), BF16)
    return pl.pallas_call(
        _gdn_body,
        grid=(b, t // ts),
        in_specs=[sect(0), sect(1), sect(2), taps(0), taps(1), taps(2), rows(LANES),
                  pl.BlockSpec((1, nh, seg_chunks, GDN_CHUNK), lambda i, s: (i, 0, s, 0)),
                  sect(3), rows(d), _full((1, GDN_HEAD_DIM)), _full(wout_bf.shape)],
        out_specs=rows(d),
        out_shape=jax.ShapeDtypeStruct((b, t, d), F32),
        scratch_shapes=[pltpu.VMEM((3, CONV_HALO + ts, GDN_WIDTH), F32),
                        pltpu.VMEM((nh, GDN_HEAD_DIM, GDN_HEAD_DIM), F32)]
                       + [pltpu.VMEM((nh, ts, LANES), F32)] * 2 + [bf(GDN_HEAD_DIM)] * 7
                       + [pltpu.VMEM((nh, ts, GDN_HEAD_DIM), F32), bf(GDN_HEAD_DIM), bf(GDN_CHUNK),
                          pltpu.VMEM((ts, GDN_WIDTH), F32)],
        compiler_params=_params("parallel", "arbitrary"),
    )(proj, proj, proj, conv_w, conv_w, conv_w, gb, g_rows, proj, x3, gain, wout_bf)


def _rope_tables(pos):
    half = HEAD_DIM // 2
    inv_freq = ROPE_THETA ** (-jnp.arange(half, dtype=F32) / half)
    ang = pos.astype(F32)[:, None] * inv_freq
    cos = jnp.cos(ang)
    sin = jnp.sin(ang)
    cos_t = jnp.tile(jnp.concatenate([cos, cos], axis=-1), (1, LANES // HEAD_DIM))
    sin_t = jnp.tile(jnp.concatenate([-sin, sin], axis=-1), (1, LANES // HEAD_DIM))
    return cos_t, sin_t


def _block_diag_ones(width, seg):
    idx = np.arange(width) // seg
    return jnp.asarray((idx[:, None] == idx[None, :]).astype(np.float32), dtype=BF16)


def _pad_cols(w, width):
    return jnp.pad(w, ((0, 0), (0, width - w.shape[1])))


def _even_layer(x2, b, t, norm_gain, w_in, b_gate, b_forget, cmp_pe, cmp_w1, cmp_w2, nsa_gain, fox_gain, w_out):
    d = x2.shape[1]
    cos, sin = _rope_tables(jnp.arange(t))
    tile = lambda g, n: jnp.tile(g, n).reshape(1, -1)
    bias = jnp.pad(jnp.concatenate([b_gate, b_forget]), (0, LANES - NSA_GATE_W - FOX_HEADS)).reshape(1, LANES)
    bd = _block_diag_ones(FOX_W, HEAD_DIM)
    (qa, ks, kw, vs, vw, kc_raw, vc_raw, qb, kb, vf, gates, cum) = even_in_prep(
        x2.reshape(b, t, d), norm_gain, w_in, cos, sin, tile(nsa_gain[0], NSA_HEADS), tile(nsa_gain[2], NSA_KV_HEADS), tile(nsa_gain[3], NSA_KV_HEADS),
        tile(fox_gain[0], FOX_HEADS), tile(fox_gain[1], FOX_HEADS), bias, bd)

    n_str = t // CMP_STRIDE
    half = CMP_BLOCK // 2
    eye2 = jnp.eye(NSA_KV_HEADS, dtype=F32)
    pe = jnp.tile(cmp_pe[:, :, None, :], (1, 1, NSA_KV_HEADS, 1)).reshape(2, 2, 1, half * NSA_KV_W)
    w1 = jnp.einsum('ilde,hg->ilhdge', cmp_w1, eye2).reshape(2, 2, half * NSA_KV_W, NSA_KV_W).astype(BF16)
    w2 = jnp.einsum('ide,hg->ihdge', cmp_w2, eye2).reshape(2, NSA_KV_W, NSA_KV_W).astype(BF16)
    cos_c, sin_c = _rope_tables(jnp.arange(n_str) * CMP_STRIDE + (CMP_BLOCK - 1))
    kc, vc = compress(kc_raw, vc_raw, pe, w1, w2, tile(nsa_gain[1], NSA_KV_HEADS), cos_c, sin_c, _block_diag_ones(LANES, HEAD_DIM))

    n_sel = t // SEL_BLOCK
    cs = np.arange(n_str)[:, None] * CMP_STRIDE
    ss = np.arange(n_sel)[None, :] * SEL_BLOCK
    overlap = np.clip(np.minimum(cs + CMP_BLOCK, ss + SEL_BLOCK) - np.maximum(cs, ss), 0, None) / CMP_BLOCK
    overlap[(t - CMP_BLOCK) // CMP_STRIDE + 1:] = 0.0
    ovt = jnp.asarray(overlap.T.astype(np.float32), dtype=BF16)
    o_a = nsa_attention(qa, kc, vc, ks, vs, kw, vw, gates, ovt)
    o_b = fox_attention(qb, kb, vf, cum)

    return o_a.reshape(b * t, NSA_Q_W), o_b.reshape(b * t, FOX_W), w_out.astype(BF16)


def _odd_layer(x2, b, t, norm_gain, w_in, conv_w, a_log, dt_bias, gdn_gain, w_out):
    proj = norm_matmul(x2, norm_gain, w_in).reshape(b, t, ODD_W)
    pad8 = lambda v: jnp.pad(v, (0, LANES - GDN_HEADS)).reshape(1, LANES)
    gb, g_rows = gdn_gates(proj, pad8(a_log), pad8(dt_bias))
    out = gdn_mixer(x2.reshape(b, t, -1), proj, conv_w, gb, g_rows, gdn_gain.reshape(1, GDN_HEAD_DIM),
                    w_out.astype(BF16))
    return out.reshape(b * t, -1)


def _moe_layer(x2, gain, w_rg, b_rg, w_re, b_re, w_ein_bf, w_eout_bf, layer, mixer_pending=()):
    d = x2.shape[1]
    wr = _pad_cols(jnp.concatenate([w_rg, w_re], axis=1), LANES)
    wr_hi = wr.astype(BF16)
    wr_lo = (wr - wr_hi.astype(F32)).astype(BF16)
    br = jnp.pad(jnp.concatenate([b_rg, b_re]), (0, LANES - N_GROUPS - N_EXPERTS)).reshape(1, LANES)
    return moe(x2, gain, wr_hi, wr_lo, br, w_ein_bf, w_eout_bf, layer, mixer_pending)


def kernel(x, norm_mix, norm_ffn, w_in_even, b_nsa_gate, b_forget, cmp_pe, cmp_w1, cmp_w2, nsa_qk_gain, fox_qk_gain,
           w_out_even, w_in_odd, conv_w, a_log, dt_bias, gdn_norm_gain, w_out_odd, w_router_group, b_router_group,
           w_router_expert, b_router_expert, w_expert_in, w_expert_out):
    b, t, d = x.shape
    x2 = x.reshape(b * t, d)
    w_ein_bf = w_expert_in.astype(BF16)
    w_eout_bf = w_expert_out.astype(BF16)
    for layer in range(norm_mix.shape[0]):
        i = layer // 2
        mixer_pending = ()
        if layer % 2 == 0:
            mixer_pending = _even_layer(x2, b, t, norm_mix[layer], w_in_even[i], b_nsa_gate[i], b_forget[i], cmp_pe[i],
                                        cmp_w1[i], cmp_w2[i], nsa_qk_gain[i], fox_qk_gain[i], w_out_even[i])
        else:
            x2 = _odd_layer(x2, b, t, norm_mix[layer], w_in_odd[i], conv_w[i], a_log[i], dt_bias[i], gdn_norm_gain[i],
                            w_out_odd[i])
        x2 = _moe_layer(x2, norm_ffn[layer], w_router_group[layer], b_router_group[layer], w_router_expert[layer],
                        b_router_expert[layer], w_ein_bf, w_eout_bf, layer, mixer_pending)
    return x2.reshape(b, t, d)
```
